```python
import math
import jax, jax.numpy as jnp
from jax import lax
import numpy as np

D_MODEL = 1024
BATCH = 1
SEQ = 16384
DEPTH = 1

HEAD_DIM = 64
A_Q_HEADS = 8
A_KV_HEADS = 2
A_GROUP = A_Q_HEADS // A_KV_HEADS
WINDOW = 128
BLOCK = 128
B_HEADS = 4
B_V_DIM = 2 * HEAD_DIM
A_WIDTH = A_Q_HEADS * HEAD_DIM
B_WIDTH = B_HEADS * B_V_DIM
MIX_WIDTH = A_WIDTH + B_WIDTH
A_Q_COLS = A_Q_HEADS * HEAD_DIM
A_KV_COLS = A_KV_HEADS * HEAD_DIM
B_QK_COLS = B_HEADS * 2 * HEAD_DIM
B_V_COLS = B_HEADS * B_V_DIM
IN_COLS = A_Q_COLS + 2 * A_KV_COLS + 2 * B_QK_COLS + B_V_COLS
NUM_BUCKETS = 32
MAX_DISTANCE = 128
N_BIAS_HEADS = A_Q_HEADS + B_HEADS
D_FF = -(-8 * D_MODEL // (3 * 256)) * 256
EPS = 1e-6

kernel_name = "hymba_window_sink_diffattn_t5bias_sandwich_swiglu"


def rmsnorm(x, g):
    xf = x.astype(jnp.float32)
    y = xf * lax.rsqrt(jnp.mean(xf * xf, axis=-1, keepdims=True) + EPS)
    return (y * g.astype(jnp.float32)).astype(x.dtype)


def t5_bucket(rel):
    nb = NUM_BUCKETS // 2
    max_exact = nb // 2
    ret = jnp.where(rel > 0, nb, 0)
    n = jnp.abs(rel)
    nf = jnp.maximum(n, 1).astype(jnp.float32)
    large = max_exact + (jnp.log(nf / max_exact) / math.log(MAX_DISTANCE / max_exact)
                         * (nb - max_exact)).astype(jnp.int32)
    large = jnp.minimum(large, nb - 1)
    return ret + jnp.where(n < max_exact, n, large)


def window_gqa_sink(q, k, v, sink, bias_tab):
    b, s = q.shape[0], q.shape[1]
    nb = s // BLOCK
    scale = HEAD_DIM ** -0.5
    qb = q.reshape(b, nb, BLOCK, A_KV_HEADS, A_GROUP, HEAD_DIM)
    pad = ((0, 0), (BLOCK, BLOCK), (0, 0), (0, 0))
    kp = jnp.pad(k, pad).reshape(b, nb + 2, BLOCK, A_KV_HEADS, HEAD_DIM)
    vp = jnp.pad(v, pad).reshape(b, nb + 2, BLOCK, A_KV_HEADS, HEAD_DIM)
    kw = jnp.concatenate([kp[:, :-2], kp[:, 1:-1], kp[:, 2:]], axis=2)
    vw = jnp.concatenate([vp[:, :-2], vp[:, 1:-1], vp[:, 2:]], axis=2)
    qi = jnp.arange(BLOCK)[:, None]
    kj = jnp.arange(3 * BLOCK)[None, :]
    rel = kj - BLOCK - qi
    bias = bias_tab[t5_bucket(rel)].astype(jnp.float32)
    bias = bias.transpose(2, 0, 1).reshape(A_KV_HEADS, A_GROUP, BLOCK, 3 * BLOCK)
    kpos = (jnp.arange(nb)[:, None] - 1) * BLOCK + kj
    valid = (jnp.abs(rel) <= WINDOW)[None] & ((kpos >= 0) & (kpos < s))[:, None, :]
    sc = jnp.einsum('bnqhgd,bnkhd->bnhgqk', qb, kw).astype(jnp.float32) * scale + bias[None, None]
    sc = jnp.where(valid[None, :, None, None], sc, -jnp.inf)
    snk = sink.astype(jnp.float32).reshape(A_KV_HEADS, A_GROUP)[None, None, :, :, None, None]
    m = jnp.maximum(jnp.max(sc, axis=-1, keepdims=True), snk)
    p = jnp.exp(sc - m)
    w = p / (jnp.sum(p, axis=-1, keepdims=True) + jnp.exp(snk - m))
    o = jnp.einsum('bnhgqk,bnkhd->bnqhgd', w.astype(v.dtype), vw)
    return o.reshape(b, s, A_Q_HEADS * HEAD_DIM)


def diff_attention(q, k, v, lam, lambda_init, subln_g, bias_tab):
    b, s = q.shape[0], q.shape[1]
    nb = s // BLOCK
    scale = HEAD_DIM ** -0.5
    qb = q.reshape(b, nb, BLOCK, B_HEADS, 2, HEAD_DIM).transpose(1, 0, 2, 3, 4, 5)
    kpos = jnp.arange(s)

    def block_fn(args):
        qblk, start = args
        sc = jnp.einsum('bqhcd,bkhcd->bhcqk', qblk, k).astype(jnp.float32) * scale
        rel = kpos[None, :] - (start + jnp.arange(BLOCK))[:, None]
        bias = bias_tab[t5_bucket(rel)].astype(jnp.float32).transpose(2, 0, 1)
        p = jax.nn.softmax(sc + bias[None, :, None], axis=-1)
        a = p[:, :, 0] - lam * p[:, :, 1]
        return jnp.einsum('bhqk,bkhe->bqhe', a.astype(v.dtype), v)

    starts = jnp.arange(nb) * BLOCK
    o = lax.map(block_fn, (qb, starts))
    o = o.transpose(1, 0, 2, 3, 4).reshape(b, s, B_HEADS, B_V_DIM)
    o = rmsnorm(o, subln_g) * (1.0 - lambda_init)
    return o.reshape(b, s, B_WIDTH)


def setup_inputs(seed: int = 0) -> dict:
    key = jax.random.key(seed)
    ks = jax.random.split(key, 20)
    f32 = jnp.float32

    def nrm(k, shape, scale):
        return jax.random.normal(k, shape, f32) * scale

    def gain(k, shape):
        return 1.0 + 0.01 * jax.random.normal(k, shape, f32)

    return {
        "x": jax.random.normal(ks[0], (BATCH, SEQ, D_MODEL), f32),
        "attn_pre_g": gain(ks[1], (DEPTH, D_MODEL)),
        "w_in": nrm(ks[2], (DEPTH, D_MODEL, IN_COLS), D_MODEL ** -0.5),
        "a_sink": nrm(ks[3], (DEPTH, A_Q_HEADS), 0.5),
        "lambda_q1": nrm(ks[4], (DEPTH, HEAD_DIM), 0.1),
        "lambda_k1": nrm(ks[5], (DEPTH, HEAD_DIM), 0.1),
        "lambda_q2": nrm(ks[6], (DEPTH, HEAD_DIM), 0.1),
        "lambda_k2": nrm(ks[7], (DEPTH, HEAD_DIM), 0.1),
        "diff_subln_g": gain(ks[8], (DEPTH, B_V_DIM)),
        "rel_bias": nrm(ks[9], (NUM_BUCKETS, N_BIAS_HEADS), 0.5),
        "w_out": nrm(ks[10], (DEPTH, MIX_WIDTH, D_MODEL), MIX_WIDTH ** -0.5),
        "attn_post_g": gain(ks[11], (DEPTH, D_MODEL)),
        "ffn_pre_g": gain(ks[12], (DEPTH, D_MODEL)),
        "w_gate": nrm(ks[13], (DEPTH, D_MODEL, D_FF), D_MODEL ** -0.5),
        "w_up": nrm(ks[14], (DEPTH, D_MODEL, D_FF), D_MODEL ** -0.5),
        "w_down": nrm(ks[15], (DEPTH, D_FF, D_MODEL), D_FF ** -0.5),
        "ffn_post_g": gain(ks[16], (DEPTH, D_MODEL)),
    }


def reference(x, attn_pre_g, w_in, a_sink, lambda_q1, lambda_k1, lambda_q2, lambda_k2,
              diff_subln_g, rel_bias, w_out, attn_post_g, ffn_pre_g, w_gate, w_up, w_down,
              ffn_post_g):
    b, s = x.shape[0], x.shape[1]
    bias_a = rel_bias[:, :A_Q_HEADS]
    bias_b = rel_bias[:, A_Q_HEADS:]
    splits = [A_Q_COLS, A_KV_COLS, A_KV_COLS, B_QK_COLS, B_QK_COLS, B_V_COLS]
    cuts = [int(c) for c in np.cumsum(splits)[:-1]]
    h = x
    for l in range(DEPTH):
        lambda_init = 0.8 - 0.6 * math.exp(-0.3 * l)
        u = rmsnorm(h, attn_pre_g[l])
        proj = u @ w_in[l]
        qa, ka, va, qb, kb, vb = jnp.split(proj, cuts, axis=-1)
        qa = qa.reshape(b, s, A_Q_HEADS, HEAD_DIM)
        ka = ka.reshape(b, s, A_KV_HEADS, HEAD_DIM)
        va = va.reshape(b, s, A_KV_HEADS, HEAD_DIM)
        qb = qb.reshape(b, s, B_HEADS, 2, HEAD_DIM)
        kb = kb.reshape(b, s, B_HEADS, 2, HEAD_DIM)
        vb = vb.reshape(b, s, B_HEADS, B_V_DIM)
        lam = (jnp.exp(jnp.sum(lambda_q1[l].astype(jnp.float32) * lambda_k1[l].astype(jnp.float32)))
               - jnp.exp(jnp.sum(lambda_q2[l].astype(jnp.float32) * lambda_k2[l].astype(jnp.float32)))
               + lambda_init)
        ya = window_gqa_sink(qa, ka, va, a_sink[l], bias_a)
        yb = diff_attention(qb, kb, vb, lam, lambda_init, diff_subln_g[l], bias_b)
        y = jnp.concatenate([ya, yb], axis=-1) @ w_out[l]
        h = h + rmsnorm(y, attn_post_g[l])
        u = rmsnorm(h, ffn_pre_g[l])
        f = (jax.nn.silu(u @ w_gate[l]) * (u @ w_up[l])) @ w_down[l]
        h = h + rmsnorm(f, ffn_post_g[l])
    return h
```

```python
import functools
import math

import numpy as np
import jax
import jax.numpy as jnp
from jax import lax
from jax.experimental import pallas as pl
from jax.experimental.pallas import tpu as pltpu

HEAD_DIM = 64
A_Q_HEADS = 8
A_KV_HEADS = 2
A_BLOCK = 128
B_HEADS = 4
B_V_DIM = 2 * HEAD_DIM
NUM_BUCKETS = 32
MAX_DISTANCE = 128
EPS = 1e-6
MASK_VALUE = -1e30

LANES = 128
BF16_SUBLANES = 16
ROW_TILE = 256
B_TILE = 256
B_ACC_ROWS = B_V_DIM + BF16_SUBLANES
VMEM_LIMIT = 56 * 1024 * 1024

_NT = (((1,), (1,)), ((), ()))


def _t5_bucket_np(rel):
    nb = NUM_BUCKETS // 2
    max_exact = nb // 2
    ret = np.where(rel > 0, nb, 0)
    n = np.abs(rel)
    nf = np.maximum(n, 1).astype(np.float32)
    large = max_exact + (np.log(nf / np.float32(max_exact)) / np.float32(math.log(MAX_DISTANCE / max_exact))
                         * np.float32(nb - max_exact)).astype(np.int32)
    large = np.minimum(large, nb - 1)
    return (ret + np.where(n < max_exact, n, large)).astype(np.int32)


def _rms(xf, g):
    return xf * lax.rsqrt(jnp.mean(xf * xf, axis=-1, keepdims=True) + EPS) * g


def _in_proj_kernel(x_ref, g_ref, w_ref, qa_ref, ka_ref, va_ref, qb_ref, kb_ref, vt_ref):
    u = _rms(x_ref[...], g_ref[...]).astype(jnp.bfloat16)
    proj = jnp.dot(u, w_ref[...], preferred_element_type=jnp.float32)
    qa_ref[...] = proj[:, 0:512].astype(jnp.bfloat16)
    ka_ref[...] = proj[:, 512:768].astype(jnp.bfloat16)
    va_ref[...] = proj[:, 768:1024].astype(jnp.bfloat16)
    qb_ref[...] = proj[:, 1024:1536].astype(jnp.bfloat16)
    kb_ref[...] = proj[:, 1536:2048].astype(jnp.bfloat16)
    rows = proj.shape[0]
    for h in range(B_HEADS):
        v = proj[:, 2048 + h * B_V_DIM:2048 + (h + 1) * B_V_DIM]
        vt_ref[h, 0, 0:B_V_DIM, :] = v.T.astype(jnp.bfloat16)
        vt_ref[h, 0, B_V_DIM:B_ACC_ROWS, :] = jnp.ones((BF16_SUBLANES, rows), jnp.bfloat16)


def _win_attn_kernel(sink_ref, q_ref, k_ref, v_ref, bias_ref, o_ref, *, nblocks):
    n = pl.program_id(0)
    variant = jnp.where(n == 0, 0, jnp.where(n == nblocks - 1, 2, 1))
    start = pl.multiple_of(n * A_BLOCK, A_BLOCK)
    kw = k_ref[pl.ds(start, 3 * A_BLOCK), :]
    vw = v_ref[pl.ds(start, 3 * A_BLOCK), :]
    q = q_ref[...]
    lane = lax.broadcasted_iota(jnp.int32, (A_BLOCK, LANES), 1)
    low = lane < HEAD_DIM
    for pair in range(A_Q_HEADS // 2):
        q2 = q[:, pair * LANES:(pair + 1) * LANES]
        kvh = pair // 2
        kg = kw[:, kvh * LANES:(kvh + 1) * LANES]
        vg = vw[:, kvh * LANES:(kvh + 1) * LANES]
        res = []
        for e in range(2):
            hq = 2 * pair + e
            qm = jnp.where(low if e == 0 else jnp.logical_not(low), q2, jnp.zeros_like(q2))
            s = lax.dot_general(qm, kg, _NT, preferred_element_type=jnp.float32)
            s = s + bias_ref[variant, hq]
            snk = sink_ref[hq]
            m = jnp.maximum(jnp.max(s, axis=-1, keepdims=True), snk)
            p = jnp.exp(s - m)
            den = jnp.sum(p, axis=-1, keepdims=True) + jnp.exp(snk - m)
            o = jnp.dot(p.astype(jnp.bfloat16), vg, preferred_element_type=jnp.float32)
            res.append(o / den)
        o_ref[:, pair * LANES:(pair + 1) * LANES] = jnp.where(low, res[0], res[1]).astype(o_ref.dtype)


def _diff_attn_kernel(c_ref, lamv_ref, g_ref, q_ref, k_ref, vt_ref, bias_ref, o_ref,
                      m_ref, acc_ref, qp_ref, *, nchunks, lambda_init):
    T = B_TILE
    h = pl.program_id(0)
    i = pl.program_id(1)

    q = q_ref[...]
    lane = lax.broadcasted_iota(jnp.int32, q.shape, 1)
    zero = jnp.zeros_like(q)
    qp_ref[0] = jnp.where(lane < HEAD_DIM, q, zero)
    qp_ref[1] = jnp.where(lane >= HEAD_DIM, q, zero)
    m_ref[...] = jnp.full(m_ref.shape, MASK_VALUE, jnp.float32)
    acc_ref[...] = jnp.zeros(acc_ref.shape, jnp.float32)

    def chunk(j, bias_tile, const):
        kc = k_ref[pl.ds(pl.multiple_of(j * T, T), T), :]
        vt = vt_ref[0, j]
        for c in range(2):
            s = lax.dot_general(kc, qp_ref[c], _NT, preferred_element_type=jnp.float32)
            if bias_tile is not None:
                s = s + bias_tile
            tmax = jnp.max(s, axis=0, keepdims=True)
            if const is not None:
                tmax = tmax + const
            m_old = m_ref[c]
            m_new = jnp.maximum(m_old, tmax)
            shift = m_new if const is None else m_new - const
            p = jnp.exp(s - shift).astype(jnp.bfloat16)
            pv = jnp.dot(vt, p, preferred_element_type=jnp.float32)
            acc_ref[c] = acc_ref[c] * jnp.exp(m_old - m_new) + pv
            m_ref[c] = m_new

    near_lo = jnp.maximum(i - 1, 0)
    near_hi = jnp.minimum(i + 2, nchunks)

    def far_left(j, carry):
        chunk(j, None, c_ref[h, 0])
        return carry

    def near(j, carry):
        chunk(j, bias_ref[0, j - i + 1], None)
        return carry

    def far_right(j, carry):
        chunk(j, None, c_ref[h, 1])
        return carry

    lax.fori_loop(0, near_lo, far_left, 0)
    lax.fori_loop(near_lo, near_hi, near, 0)
    lax.fori_loop(near_hi, nchunks, far_right, 0)

    lv = lamv_ref[...]
    lam = (jnp.exp(jnp.sum(lv[0:1] * lv[1:2], axis=-1, keepdims=True))
           - jnp.exp(jnp.sum(lv[2:3] * lv[3:4], axis=-1, keepdims=True)) + lambda_init)
    a1 = acc_ref[0]
    a2 = acc_ref[1]
    ot = (a1[0:B_V_DIM] / a1[B_V_DIM:B_V_DIM + 1]
          - lam * (a2[0:B_V_DIM] / a2[B_V_DIM:B_V_DIM + 1]))
    o = ot.T
    o_ref[...] = (_rms(o, g_ref[...]) * (1.0 - lambda_init)).astype(o_ref.dtype)


def _out_ffn_kernel(ya_ref, yb_ref, x_ref, woa_ref, wob_ref, gpost_ref, gpre_ref,
                    wg_ref, wu_ref, wd_ref, gfpost_ref, o_ref):
    y = (jnp.dot(ya_ref[...], woa_ref[...], preferred_element_type=jnp.float32)
         + jnp.dot(yb_ref[...], wob_ref[...], preferred_element_type=jnp.float32))
    h1 = x_ref[...] + _rms(y, gpost_ref[...])
    u = _rms(h1, gpre_ref[...]).astype(jnp.bfloat16)
    gate = jnp.dot(u, wg_ref[...], preferred_element_type=jnp.float32)
    up = jnp.dot(u, wu_ref[...], preferred_element_type=jnp.float32)
    act = (gate * jax.nn.sigmoid(gate) * up).astype(jnp.bfloat16)
    f = jnp.dot(act, wd_ref[...], preferred_element_type=jnp.float32)
    o_ref[...] = h1 + _rms(f, gfpost_ref[...])


def _resident(shape):
    zeros = (0,) * len(shape)
    return pl.BlockSpec(shape, lambda *_: zeros, pipeline_mode=pl.Buffered(1))


def _layer(h, l, p, bias_a, bias_bt, c_b):
    S, D = h.shape
    bf16 = jnp.bfloat16
    lambda_init = 0.8 - 0.6 * math.exp(-0.3 * l)
    scale = HEAD_DIM ** -0.5

    w = p["w_in"]
    a_q, a_kv, b_qk = A_Q_HEADS * HEAD_DIM, A_KV_HEADS * HEAD_DIM, B_HEADS * 2 * HEAD_DIM
    c0 = a_q
    c1 = c0 + a_kv
    c2 = c1 + a_kv
    c3 = c2 + b_qk
    c4 = c3 + b_qk

    def dup(cols):
        parts = []
        for g in range(A_KV_HEADS):
            blk = cols[:, g * HEAD_DIM:(g + 1) * HEAD_DIM]
            parts += [blk, blk]
        return jnp.concatenate(parts, axis=1)

    w_cat = jnp.concatenate([w[:, :c0] * scale, dup(w[:, c0:c1]), dup(w[:, c1:c2]),
                             w[:, c2:c3] * scale, w[:, c3:c4], w[:, c4:]], axis=1).astype(bf16)
    ncols = w_cat.shape[1]
    nrow = S // ROW_TILE
    nchunks = S // B_TILE
    assert ROW_TILE == B_TILE

    qa, ka, va, qb, kb, vt = pl.pallas_call(
        _in_proj_kernel,
        grid=(nrow,),
        in_specs=[pl.BlockSpec((ROW_TILE, D), lambda i: (i, 0)),
                  _resident((1, D)),
                  _resident((D, ncols))],
        out_specs=[pl.BlockSpec((ROW_TILE, 512), lambda i: (i, 0)),
                   pl.BlockSpec((ROW_TILE, 256), lambda i: (i, 0)),
                   pl.BlockSpec((ROW_TILE, 256), lambda i: (i, 0)),
                   pl.BlockSpec((ROW_TILE, 512), lambda i: (i, 0)),
                   pl.BlockSpec((ROW_TILE, 512), lambda i: (i, 0)),
                   pl.BlockSpec((B_HEADS, 1, B_ACC_ROWS, ROW_TILE), lambda i: (0, i, 0, 0))],
        out_shape=[jax.ShapeDtypeStruct((S, 512), bf16),
                   jax.ShapeDtypeStruct((S, 256), bf16),
                   jax.ShapeDtypeStruct((S, 256), bf16),
                   jax.ShapeDtypeStruct((S, 512), bf16),
                   jax.ShapeDtypeStruct((S, 512), bf16),
                   jax.ShapeDtypeStruct((B_HEADS, nchunks, B_ACC_ROWS, B_TILE), bf16)],
        compiler_params=pltpu.CompilerParams(dimension_semantics=("arbitrary",),
                                             vmem_limit_bytes=VMEM_LIMIT),
    )(h, p["attn_pre_g"].reshape(1, D), w_cat)

    nblocks = S // A_BLOCK
    pad = ((A_BLOCK, A_BLOCK), (0, 0))
    kp = jnp.pad(ka, pad)
    vp = jnp.pad(va, pad)
    ya = pl.pallas_call(
        functools.partial(_win_attn_kernel, nblocks=nblocks),
        grid=(nblocks,),
        in_specs=[pl.BlockSpec(memory_space=pltpu.SMEM),
                  pl.BlockSpec((A_BLOCK, 512), lambda n: (n, 0)),
                  _resident(kp.shape),
                  _resident(vp.shape),
                  _resident(bias_a.shape)],
        out_specs=pl.BlockSpec((A_BLOCK, 512), lambda n: (n, 0)),
        out_shape=jax.ShapeDtypeStruct((S, 512), bf16),
        compiler_params=pltpu.CompilerParams(dimension_semantics=("arbitrary",),
                                             vmem_limit_bytes=VMEM_LIMIT),
    )(p["a_sink"], qa, kp, vp, bias_a)

    lamv = jnp.stack([p["lambda_q1"], p["lambda_k1"], p["lambda_q2"], p["lambda_k2"]])
    yb = pl.pallas_call(
        functools.partial(_diff_attn_kernel, nchunks=nchunks, lambda_init=lambda_init),
        grid=(B_HEADS, nchunks),
        in_specs=[pl.BlockSpec(memory_space=pltpu.SMEM),
                  pl.BlockSpec((4, HEAD_DIM), lambda hh, i: (0, 0)),
                  pl.BlockSpec((1, B_V_DIM), lambda hh, i: (0, 0)),
                  pl.BlockSpec((B_TILE, LANES), lambda hh, i: (i, hh)),
                  pl.BlockSpec((S, LANES), lambda hh, i: (0, hh)),
                  pl.BlockSpec((1, nchunks, B_ACC_ROWS, B_TILE), lambda hh, i: (hh, 0, 0, 0)),
                  pl.BlockSpec((1, 3, B_TILE, B_TILE), lambda hh, i: (hh, 0, 0, 0))],
        out_specs=pl.BlockSpec((B_TILE, LANES), lambda hh, i: (i, hh)),
        out_shape=jax.ShapeDtypeStruct((S, B_HEADS * B_V_DIM), bf16),
        scratch_shapes=[pltpu.VMEM((2, 1, B_TILE), jnp.float32),
                        pltpu.VMEM((2, B_ACC_ROWS, B_TILE), jnp.float32),
                        pltpu.VMEM((2, B_TILE, LANES), bf16)],
        compiler_params=pltpu.CompilerParams(dimension_semantics=("arbitrary", "arbitrary"),
                                             vmem_limit_bytes=VMEM_LIMIT),
    )(c_b, lamv, p["diff_subln_g"].reshape(1, B_V_DIM), qb, kb, vt, bias_bt)

    w_out = p["w_out"].astype(bf16)
    a_width = A_Q_HEADS * HEAD_DIM
    d_ff = p["w_gate"].shape[1]
    out = pl.pallas_call(
        _out_ffn_kernel,
        grid=(nrow,),
        in_specs=[pl.BlockSpec((ROW_TILE, 512), lambda i: (i, 0)),
                  pl.BlockSpec((ROW_TILE, 512), lambda i: (i, 0)),
                  pl.BlockSpec((ROW_TILE, D), lambda i: (i, 0)),
                  _resident((a_width, D)),
                  _resident((w_out.shape[0] - a_width, D)),
                  _resident((1, D)),
                  _resident((1, D)),
                  _resident((D, d_ff)),
                  _resident((D, d_ff)),
                  _resident((d_ff, D)),
                  _resident((1, D))],
        out_specs=pl.BlockSpec((ROW_TILE, D), lambda i: (i, 0)),
        out_shape=jax.ShapeDtypeStruct((S, D), jnp.float32),
        compiler_params=pltpu.CompilerParams(dimension_semantics=("arbitrary",),
                                             vmem_limit_bytes=VMEM_LIMIT),
    )(ya, yb, h, w_out[:a_width], w_out[a_width:], p["attn_post_g"].reshape(1, D),
      p["ffn_pre_g"].reshape(1, D), p["w_gate"].astype(bf16), p["w_up"].astype(bf16),
      p["w_down"].astype(bf16), p["ffn_post_g"].reshape(1, D))
    return out


def kernel(x, attn_pre_g, w_in, a_sink, lambda_q1, lambda_k1, lambda_q2, lambda_k2, diff_subln_g,
           rel_bias, w_out, attn_post_g, ffn_pre_g, w_gate, w_up, w_down, ffn_post_g):
    batch, S, D = x.shape
    depth = w_in.shape[0]
    assert S % B_TILE == 0 and S % A_BLOCK == 0 and S // A_BLOCK >= 2

    qi = np.arange(A_BLOCK)[:, None]
    kj = np.arange(3 * A_BLOCK)[None, :]
    rel_a = kj - A_BLOCK - qi
    tab_a = rel_bias[:, :A_Q_HEADS][_t5_bucket_np(rel_a)].astype(jnp.float32).transpose(2, 0, 1)
    in_window = np.abs(rel_a) <= A_BLOCK
    valid = np.stack([in_window & (kj >= A_BLOCK), in_window, in_window & (kj < 2 * A_BLOCK)])
    bias_a = jnp.where(valid[:, None], tab_a[None], MASK_VALUE)

    kk = np.arange(B_TILE)[:, None]
    qq = np.arange(B_TILE)[None, :]
    rel_b = np.stack([(d - 1) * B_TILE + kk - qq for d in range(3)])
    bias_bt = rel_bias[:, A_Q_HEADS:][_t5_bucket_np(rel_b)].astype(jnp.float32).transpose(3, 0, 1, 2)
    far = _t5_bucket_np(np.array([-2 * B_TILE, 2 * B_TILE]))
    assert B_TILE >= MAX_DISTANCE
    c_b = rel_bias[:, A_Q_HEADS:][far].astype(jnp.float32).T

    outs = []
    for b in range(batch):
        h = x[b]
        for l in range(depth):
            p = dict(attn_pre_g=attn_pre_g[l], w_in=w_in[l], a_sink=a_sink[l], lambda_q1=lambda_q1[l],
                     lambda_k1=lambda_k1[l], lambda_q2=lambda_q2[l], lambda_k2=lambda_k2[l],
                     diff_subln_g=diff_subln_g[l], w_out=w_out[l], attn_post_g=attn_post_g[l],
                     ffn_pre_g=ffn_pre_g[l], w_gate=w_gate[l], w_up=w_up[l], w_down=w_down[l],
                     ffn_post_g=ffn_post_g[l])
            h = _layer(h, l, p, bias_a, bias_bt, c_b)
        outs.append(h)
    return jnp.stack(outs)
```

```python
import functools
import math

import numpy as np
import jax
import jax.numpy as jnp
from jax import lax
from jax.experimental import pallas as pl
from jax.experimental.pallas import tpu as pltpu

HEAD_DIM = 64
A_Q_HEADS = 8
A_KV_HEADS = 2
A_BLOCK = 128
B_HEADS = 4
B_V_DIM = 2 * HEAD_DIM
NUM_BUCKETS = 32
MAX_DISTANCE = 128
EPS = 1e-6
MASK_VALUE = -1e30
LOG2E = math.log2(math.e)

LANES = 128
BF16_SUBLANES = 16
ROW_TILE = 512
FFN_ROW_TILE = 256
B_TILE = 512
B_ACC_ROWS = B_V_DIM + BF16_SUBLANES
VMEM_LIMIT = 56 * 1024 * 1024

_NT = (((1,), (1,)), ((), ()))


def _t5_bucket_np(rel):
    nb = NUM_BUCKETS // 2
    max_exact = nb // 2
    ret = np.where(rel > 0, nb, 0)
    n = np.abs(rel)
    nf = np.maximum(n, 1).astype(np.float32)
    large = max_exact + (np.log(nf / np.float32(max_exact)) / np.float32(math.log(MAX_DISTANCE / max_exact))
                         * np.float32(nb - max_exact)).astype(np.int32)
    large = np.minimum(large, nb - 1)
    return (ret + np.where(n < max_exact, n, large)).astype(np.int32)


def _toeplitz_bias(table, rows, cols, rel_of):
    length = rows + cols
    u = np.arange(length)
    t = np.where(u < cols, -u, length - u)
    w = table[_t5_bucket_np(rel_of(t))].astype(jnp.float32).T
    x = jnp.tile(w, (1, rows))[:, :rows * (length - 1)].reshape(w.shape[0], rows, length - 1)
    return x[:, :, :cols]


def _rms(xf, g):
    return xf * lax.rsqrt(jnp.mean(xf * xf, axis=-1, keepdims=True) + EPS) * g


def _in_proj_kernel(x_ref, g_ref, w_ref, qa_ref, ka_ref, va_ref, qb_ref, kb_ref, vt_ref):
    u = _rms(x_ref[...], g_ref[...]).astype(jnp.bfloat16)
    proj = jnp.dot(u, w_ref[...], preferred_element_type=jnp.float32)
    qa_ref[...] = proj[:, 0:512].astype(jnp.bfloat16)
    ka_ref[...] = proj[:, 512:768].astype(jnp.bfloat16)
    va_ref[...] = proj[:, 768:1024].astype(jnp.bfloat16)
    qb_ref[...] = proj[:, 1024:1536].astype(jnp.bfloat16)
    kb_ref[...] = proj[:, 1536:2048].astype(jnp.bfloat16)
    rows = proj.shape[0]
    for h in range(B_HEADS):
        v = proj[:, 2048 + h * B_V_DIM:2048 + (h + 1) * B_V_DIM]
        vt_ref[h, 0, 0:B_V_DIM, :] = v.T.astype(jnp.bfloat16)
        vt_ref[h, 0, B_V_DIM:B_ACC_ROWS, :] = jnp.ones((BF16_SUBLANES, rows), jnp.bfloat16)


def _win_attn_kernel(sink_ref, q_ref, k_ref, v_ref, bias_ref, o_ref, *, nblocks):
    n = pl.program_id(0)
    variant = jnp.where(n == 0, 0, jnp.where(n == nblocks - 1, 2, 1))
    start = pl.multiple_of(n * A_BLOCK, A_BLOCK)
    kw = k_ref[pl.ds(start, 3 * A_BLOCK), :]
    vw = v_ref[pl.ds(start, 3 * A_BLOCK), :]
    q = q_ref[...]
    lane = lax.broadcasted_iota(jnp.int32, (A_BLOCK, LANES), 1)
    low = lane < HEAD_DIM
    for pair in range(A_Q_HEADS // 2):
        q2 = q[:, pair * LANES:(pair + 1) * LANES]
        kvh = pair // 2
        kg = kw[:, kvh * LANES:(kvh + 1) * LANES]
        vg = vw[:, kvh * LANES:(kvh + 1) * LANES]
        res = []
        for e in range(2):
            hq = 2 * pair + e
            qm = jnp.where(low if e == 0 else jnp.logical_not(low), q2, jnp.zeros_like(q2))
            s = lax.dot_general(qm, kg, _NT, preferred_element_type=jnp.float32)
            s = s + bias_ref[variant, hq]
            snk = sink_ref[hq]
            m = jnp.maximum(jnp.max(s, axis=-1, keepdims=True), snk)
            p = jnp.exp(s - m)
            den = jnp.sum(p, axis=-1, keepdims=True) + jnp.exp(snk - m)
            o = jnp.dot(p.astype(jnp.bfloat16), vg, preferred_element_type=jnp.float32)
            res.append(o / den)
        o_ref[:, pair * LANES:(pair + 1) * LANES] = jnp.where(low, res[0], res[1]).astype(o_ref.dtype)


def _diff_attn_kernel(lamv_ref, g_ref, q_ref, k_ref, vt_ref, bias_ref, o_ref,
                      qp_ref, s_ref, tmax_ref, p_ref, alpha_ref, m_ref, acc_ref, *, nchunks, lambda_init):
    T = B_TILE
    i = pl.program_id(1)

    q = q_ref[...]
    lane = lax.broadcasted_iota(jnp.int32, q.shape, 1)
    zero = jnp.zeros_like(q)
    qp_ref[0] = jnp.where(lane < HEAD_DIM, q, zero)
    qp_ref[1] = jnp.where(lane >= HEAD_DIM, q, zero)
    m_ref[...] = jnp.full(m_ref.shape, MASK_VALUE, jnp.float32)
    acc_ref[...] = jnp.zeros(acc_ref.shape, jnp.float32)
    p_ref[1] = jnp.zeros(p_ref.shape[1:], p_ref.dtype)
    alpha_ref[1] = jnp.ones(alpha_ref.shape[1:], jnp.float32)

    def stage_a(j, slot):
        kc = k_ref[pl.ds(pl.multiple_of(j * T, T), T), :]
        bias = bias_ref[0, jnp.clip(j - i, -2, 2) + 2]
        for c in range(2):
            s = lax.dot_general(kc, qp_ref[c], _NT, preferred_element_type=jnp.float32) + bias
            s_ref[slot, c] = s
            tmax_ref[slot, c] = jnp.max(s, axis=0, keepdims=True)

    def stage_b(slot):
        for c in range(2):
            m_old = m_ref[c]
            m_new = jnp.maximum(m_old, tmax_ref[slot, c])
            alpha_ref[slot, c] = jnp.exp2(m_old - m_new)
            m_ref[c] = m_new
            p_ref[slot, c] = jnp.exp2(s_ref[slot, c] - m_new).astype(p_ref.dtype)

    def stage_c(j, slot):
        vt = vt_ref[0, j]
        for c in range(2):
            pv = jnp.dot(vt, p_ref[slot, c], preferred_element_type=jnp.float32)
            acc_ref[c] = acc_ref[c] * alpha_ref[slot, c] + pv

    last = nchunks - 1
    stage_a(0, 0)

    def pair_step(t, carry):
        j = 2 * t
        stage_a(j + 1, 1)
        stage_b(0)
        stage_c(jnp.maximum(j - 1, 0), 1)
        stage_a(jnp.minimum(j + 2, last), 0)
        stage_b(1)
        stage_c(j, 0)
        return carry

    lax.fori_loop(0, nchunks // 2, pair_step, 0)
    stage_c(last, 1)

    lv = lamv_ref[...]
    lam = (jnp.exp(jnp.sum(lv[0:1] * lv[1:2], axis=-1, keepdims=True))
           - jnp.exp(jnp.sum(lv[2:3] * lv[3:4], axis=-1, keepdims=True)) + lambda_init)
    a1 = acc_ref[0]
    a2 = acc_ref[1]
    ot = (a1[0:B_V_DIM] / a1[B_V_DIM:B_V_DIM + 1]
          - lam * (a2[0:B_V_DIM] / a2[B_V_DIM:B_V_DIM + 1]))
    o = ot.T
    o_ref[...] = (_rms(o, g_ref[...]) * (1.0 - lambda_init)).astype(o_ref.dtype)


def _out_ffn_kernel(ya_ref, yb_ref, x_ref, woa_ref, wob_ref, gpost_ref, gpre_ref,
                    wg_ref, wu_ref, wd_ref, gfpost_ref, o_ref):
    y = (jnp.dot(ya_ref[...], woa_ref[...], preferred_element_type=jnp.float32)
         + jnp.dot(yb_ref[...], wob_ref[...], preferred_element_type=jnp.float32))
    h1 = x_ref[...] + _rms(y, gpost_ref[...])
    u = _rms(h1, gpre_ref[...]).astype(jnp.bfloat16)
    gate = jnp.dot(u, wg_ref[...], preferred_element_type=jnp.float32)
    up = jnp.dot(u, wu_ref[...], preferred_element_type=jnp.float32)
    act = (gate * jax.nn.sigmoid(gate) * up).astype(jnp.bfloat16)
    f = jnp.dot(act, wd_ref[...], preferred_element_type=jnp.float32)
    o_ref[...] = h1 + _rms(f, gfpost_ref[...])


def _resident(shape):
    zeros = (0,) * len(shape)
    return pl.BlockSpec(shape, lambda *_: zeros, pipeline_mode=pl.Buffered(1))


def _layer(h, l, p, bias_a, bias_b):
    S, D = h.shape
    bf16 = jnp.bfloat16
    lambda_init = 0.8 - 0.6 * math.exp(-0.3 * l)
    scale = HEAD_DIM ** -0.5

    w = p["w_in"]
    a_q, a_kv, b_qk = A_Q_HEADS * HEAD_DIM, A_KV_HEADS * HEAD_DIM, B_HEADS * 2 * HEAD_DIM
    c0 = a_q
    c1 = c0 + a_kv
    c2 = c1 + a_kv
    c3 = c2 + b_qk
    c4 = c3 + b_qk

    def dup(cols):
        parts = []
        for g in range(A_KV_HEADS):
            blk = cols[:, g * HEAD_DIM:(g + 1) * HEAD_DIM]
            parts += [blk, blk]
        return jnp.concatenate(parts, axis=1)

    w_cat = jnp.concatenate([w[:, :c0] * scale, dup(w[:, c0:c1]), dup(w[:, c1:c2]),
                             w[:, c2:c3] * (scale * LOG2E), w[:, c3:c4], w[:, c4:]], axis=1).astype(bf16)
    ncols = w_cat.shape[1]
    nrow = S // ROW_TILE
    nchunks = S // B_TILE
    assert ROW_TILE == B_TILE and nchunks % 2 == 0

    qa, ka, va, qb, kb, vt = pl.pallas_call(
        _in_proj_kernel,
        grid=(nrow,),
        in_specs=[pl.BlockSpec((ROW_TILE, D), lambda i: (i, 0)),
                  _resident((1, D)),
                  _resident((D, ncols))],
        out_specs=[pl.BlockSpec((ROW_TILE, 512), lambda i: (i, 0)),
                   pl.BlockSpec((ROW_TILE, 256), lambda i: (i, 0)),
                   pl.BlockSpec((ROW_TILE, 256), lambda i: (i, 0)),
                   pl.BlockSpec((ROW_TILE, 512), lambda i: (i, 0)),
                   pl.BlockSpec((ROW_TILE, 512), lambda i: (i, 0)),
                   pl.BlockSpec((B_HEADS, 1, B_ACC_ROWS, ROW_TILE), lambda i: (0, i, 0, 0))],
        out_shape=[jax.ShapeDtypeStruct((S, 512), bf16),
                   jax.ShapeDtypeStruct((S, 256), bf16),
                   jax.ShapeDtypeStruct((S, 256), bf16),
                   jax.ShapeDtypeStruct((S, 512), bf16),
                   jax.ShapeDtypeStruct((S, 512), bf16),
                   jax.ShapeDtypeStruct((B_HEADS, nchunks, B_ACC_ROWS, B_TILE), bf16)],
        compiler_params=pltpu.CompilerParams(dimension_semantics=("arbitrary",),
                                             vmem_limit_bytes=VMEM_LIMIT),
        name="in_proj",
    )(h, p["attn_pre_g"].reshape(1, D), w_cat)

    nblocks = S // A_BLOCK
    pad = ((A_BLOCK, A_BLOCK), (0, 0))
    kp = jnp.pad(ka, pad)
    vp = jnp.pad(va, pad)
    ya = pl.pallas_call(
        functools.partial(_win_attn_kernel, nblocks=nblocks),
        grid=(nblocks,),
        in_specs=[pl.BlockSpec(memory_space=pltpu.SMEM),
                  pl.BlockSpec((A_BLOCK, 512), lambda n: (n, 0)),
                  _resident(kp.shape),
                  _resident(vp.shape),
                  _resident(bias_a.shape)],
        out_specs=pl.BlockSpec((A_BLOCK, 512), lambda n: (n, 0)),
        out_shape=jax.ShapeDtypeStruct((S, 512), bf16),
        compiler_params=pltpu.CompilerParams(dimension_semantics=("arbitrary",),
                                             vmem_limit_bytes=VMEM_LIMIT),
        name="win_attn",
    )(p["a_sink"], qa, kp, vp, bias_a)

    lamv = jnp.stack([p["lambda_q1"], p["lambda_k1"], p["lambda_q2"], p["lambda_k2"]])
    T = B_TILE
    yb = pl.pallas_call(
        functools.partial(_diff_attn_kernel, nchunks=nchunks, lambda_init=lambda_init),
        grid=(B_HEADS, nchunks),
        in_specs=[pl.BlockSpec((4, HEAD_DIM), lambda hh, i: (0, 0)),
                  pl.BlockSpec((1, B_V_DIM), lambda hh, i: (0, 0)),
                  pl.BlockSpec((T, LANES), lambda hh, i: (i, hh)),
                  pl.BlockSpec((S, LANES), lambda hh, i: (0, hh)),
                  pl.BlockSpec((1, nchunks, B_ACC_ROWS, T), lambda hh, i: (hh, 0, 0, 0)),
                  pl.BlockSpec((1, 5, T, T), lambda hh, i: (hh, 0, 0, 0))],
        out_specs=pl.BlockSpec((T, LANES), lambda hh, i: (i, hh)),
        out_shape=jax.ShapeDtypeStruct((S, B_HEADS * B_V_DIM), bf16),
        scratch_shapes=[pltpu.VMEM((2, T, LANES), bf16),
                        pltpu.VMEM((2, 2, T, T), jnp.float32),
                        pltpu.VMEM((2, 2, 1, T), jnp.float32),
                        pltpu.VMEM((2, 2, T, T), bf16),
                        pltpu.VMEM((2, 2, 1, T), jnp.float32),
                        pltpu.VMEM((2, 1, T), jnp.float32),
                        pltpu.VMEM((2, B_ACC_ROWS, T), jnp.float32)],
        compiler_params=pltpu.CompilerParams(dimension_semantics=("arbitrary", "arbitrary"),
                                             vmem_limit_bytes=VMEM_LIMIT),
        name="diff_attn",
    )(lamv, p["diff_subln_g"].reshape(1, B_V_DIM), qb, kb, vt, bias_b)

    w_out = p["w_out"].astype(bf16)
    a_width = A_Q_HEADS * HEAD_DIM
    d_ff = p["w_gate"].shape[1]
    R = FFN_ROW_TILE
    out = pl.pallas_call(
        _out_ffn_kernel,
        grid=(S // R,),
        in_specs=[pl.BlockSpec((R, 512), lambda i: (i, 0)),
                  pl.BlockSpec((R, 512), lambda i: (i, 0)),
                  pl.BlockSpec((R, D), lambda i: (i, 0)),
                  _resident((a_width, D)),
                  _resident((w_out.shape[0] - a_width, D)),
                  _resident((1, D)),
                  _resident((1, D)),
                  _resident((D, d_ff)),
                  _resident((D, d_ff)),
                  _resident((d_ff, D)),
                  _resident((1, D))],
        out_specs=pl.BlockSpec((R, D), lambda i: (i, 0)),
        out_shape=jax.ShapeDtypeStruct((S, D), jnp.float32),
        compiler_params=pltpu.CompilerParams(dimension_semantics=("arbitrary",),
                                             vmem_limit_bytes=VMEM_LIMIT),
        name="out_ffn",
    )(ya, yb, h, w_out[:a_width], w_out[a_width:], p["attn_post_g"].reshape(1, D),
      p["ffn_pre_g"].reshape(1, D), p["w_gate"].astype(bf16), p["w_up"].astype(bf16),
      p["w_down"].astype(bf16), p["ffn_post_g"].reshape(1, D))
    return out


def kernel(x, attn_pre_g, w_in, a_sink, lambda_q1, lambda_k1, lambda_q2, lambda_k2, diff_subln_g,
           rel_bias, w_out, attn_post_g, ffn_pre_g, w_gate, w_up, w_down, ffn_post_g):
    batch, S, D = x.shape
    depth = w_in.shape[0]
    assert S % B_TILE == 0 and S % A_BLOCK == 0 and S // A_BLOCK >= 2

    tab_a = _toeplitz_bias(rel_bias[:, :A_Q_HEADS], A_BLOCK, 3 * A_BLOCK, lambda t: -t - A_BLOCK)
    qi = np.arange(A_BLOCK)[:, None]
    kj = np.arange(3 * A_BLOCK)[None, :]
    in_window = np.abs(kj - A_BLOCK - qi) <= A_BLOCK
    valid = np.stack([in_window & (kj >= A_BLOCK), in_window, in_window & (kj < 2 * A_BLOCK)])
    bias_a = jnp.where(valid[:, None], tab_a[None], MASK_VALUE)

    T = B_TILE
    assert T >= MAX_DISTANCE
    tab_b = rel_bias[:, A_Q_HEADS:]
    near = [_toeplitz_bias(tab_b, T, T, lambda t, d=d: d * T + t) for d in (-1, 0, 1)]
    far = tab_b[_t5_bucket_np(np.array([-2 * T, 2 * T]))].astype(jnp.float32)
    const = [jnp.broadcast_to(far[side][:, None, None], (B_HEADS, T, T)) for side in (0, 1)]
    bias_b = jnp.stack([const[0]] + near + [const[1]], axis=1) * LOG2E

    outs = []
    for b in range(batch):
        h = x[b]
        for l in range(depth):
            p = dict(attn_pre_g=attn_pre_g[l], w_in=w_in[l], a_sink=a_sink[l], lambda_q1=lambda_q1[l],
                     lambda_k1=lambda_k1[l], lambda_q2=lambda_q2[l], lambda_k2=lambda_k2[l],
                     diff_subln_g=diff_subln_g[l], w_out=w_out[l], attn_post_g=attn_post_g[l],
                     ffn_pre_g=ffn_pre_g[l], w_gate=w_gate[l], w_up=w_up[l], w_down=w_down[l],
                     ffn_post_g=ffn_post_g[l])
            h = _layer(h, l, p, bias_a, bias_b)
        outs.append(h)
    return jnp.stack(outs)
```

```python
import functools
import math

import numpy as np
import jax
import jax.numpy as jnp
from jax import lax
from jax.experimental import pallas as pl
from jax.experimental.pallas import tpu as pltpu

HEAD_DIM = 64
A_Q_HEADS = 8
A_KV_HEADS = 2
A_BLOCK = 128
B_HEADS = 4
B_V_DIM = 2 * HEAD_DIM
NUM_BUCKETS = 32
MAX_DISTANCE = 128
EPS = 1e-6
MASK_VALUE = -1e30
LOG2E = math.log2(math.e)
ROW_SUM_LIMIT = 2.0 ** 40

LANES = 128
BF16_SUBLANES = 16
ROW_TILE = 512
FFN_ROW_TILE = 256
B_TILE = 512
B_ACC_ROWS = B_V_DIM + BF16_SUBLANES
B_REF_LANES = 3
B_UNROLL = 4
VMEM_LIMIT = 56 * 1024 * 1024

_NT = (((1,), (1,)), ((), ()))


def _t5_bucket_np(rel):
    nb = NUM_BUCKETS // 2
    max_exact = nb // 2
    ret = np.where(rel > 0, nb, 0)
    n = np.abs(rel)
    nf = np.maximum(n, 1).astype(np.float32)
    large = max_exact + (np.log(nf / np.float32(max_exact)) / np.float32(math.log(MAX_DISTANCE / max_exact))
                         * np.float32(nb - max_exact)).astype(np.int32)
    large = np.minimum(large, nb - 1)
    return (ret + np.where(n < max_exact, n, large)).astype(np.int32)


def _toeplitz_bias(table, rows, cols, rel_of):
    length = rows + cols
    u = np.arange(length)
    t = np.where(u < cols, -u, length - u)
    w = table[_t5_bucket_np(rel_of(t))].astype(jnp.float32).T
    x = jnp.tile(w, (1, rows))[:, :rows * (length - 1)].reshape(w.shape[0], rows, length - 1)
    return x[:, :, :cols]


def _rms(xf, g):
    return xf * lax.rsqrt(jnp.mean(xf * xf, axis=-1, keepdims=True) + EPS) * g


def _in_proj_kernel(x_ref, g_ref, w_ref, kones_ref, qa_ref, ka_ref, va_ref, qb_ref, kb_ref, vt_ref):
    u = _rms(x_ref[...], g_ref[...]).astype(jnp.bfloat16)
    proj = jnp.dot(u, w_ref[...], preferred_element_type=jnp.float32)
    qa_ref[...] = proj[:, 0:512].astype(jnp.bfloat16)
    ka_ref[...] = proj[:, 512:768].astype(jnp.bfloat16)
    va_ref[...] = proj[:, 768:1024].astype(jnp.bfloat16)
    qb_ref[...] = proj[:, 1024:1536].astype(jnp.bfloat16)
    kb_ref[...] = (proj[:, 1536:2560] + kones_ref[...]).astype(jnp.bfloat16)
    rows = proj.shape[0]
    for h in range(B_HEADS):
        v = proj[:, 2560 + h * B_V_DIM:2560 + (h + 1) * B_V_DIM]
        vt_ref[h, 0, 0:B_V_DIM, :] = v.T.astype(jnp.bfloat16)
        vt_ref[h, 0, B_V_DIM:B_ACC_ROWS, :] = jnp.ones((BF16_SUBLANES, rows), jnp.bfloat16)


def _win_attn_kernel(sink_ref, q_ref, k_ref, v_ref, bias_ref, o_ref, *, nblocks):
    n = pl.program_id(0)
    variant = jnp.where(n == 0, 0, jnp.where(n == nblocks - 1, 2, 1))
    start = pl.multiple_of(n * A_BLOCK, A_BLOCK)
    kw = k_ref[pl.ds(start, 3 * A_BLOCK), :]
    vw = v_ref[pl.ds(start, 3 * A_BLOCK), :]
    q = q_ref[...]
    lane = lax.broadcasted_iota(jnp.int32, (A_BLOCK, LANES), 1)
    low = lane < HEAD_DIM
    for pair in range(A_Q_HEADS // 2):
        q2 = q[:, pair * LANES:(pair + 1) * LANES]
        kvh = pair // 2
        kg = kw[:, kvh * LANES:(kvh + 1) * LANES]
        vg = vw[:, kvh * LANES:(kvh + 1) * LANES]
        res = []
        for e in range(2):
            hq = 2 * pair + e
            qm = jnp.where(low if e == 0 else jnp.logical_not(low), q2, jnp.zeros_like(q2))
            s = lax.dot_general(qm, kg, _NT, preferred_element_type=jnp.float32)
            s = s + bias_ref[variant, hq]
            snk = sink_ref[hq]
            m = jnp.maximum(jnp.max(s, axis=-1, keepdims=True), snk)
            p = jnp.exp(s - m)
            den = jnp.sum(p, axis=-1, keepdims=True) + jnp.exp(snk - m)
            o = jnp.dot(p.astype(jnp.bfloat16), vg, preferred_element_type=jnp.float32)
            res.append(o / den)
        o_ref[:, pair * LANES:(pair + 1) * LANES] = jnp.where(low, res[0], res[1]).astype(o_ref.dtype)


def _diff_attn_kernel(lamv_ref, g_ref, q_ref, k_ref, vt_ref, bias_ref, o_ref,
                      qz_ref, qp_ref, p_ref, r_ref, flag_ref, any_ref, acc_ref, *, nchunks, lambda_init):
    T = B_TILE
    i = pl.program_id(1)
    f32 = jnp.float32
    bf16 = jnp.bfloat16

    lane = lax.broadcasted_iota(jnp.int32, (T, LANES), 1)
    is_q = (lane < HEAD_DIM, lane >= HEAD_DIM)
    ref_lane = (HEAD_DIM, 0)
    zero = jnp.zeros((T, LANES), bf16)

    def bias_tile(j):
        return bias_ref[0, jnp.clip(j - i, -2, 2) + 2]

    def k_chunk(j, c):
        return k_ref[pl.ds(pl.multiple_of(j * T, T), T), c * LANES:(c + 1) * LANES]

    def set_reference(c, r_row):
        rt =jnp.broadcast_to(r_row, (LANES, T)).T
        hi = rt.astype(bf16)
        rem = rt - hi.astype(f32)
        mid = rem.astype(bf16)
        low = (rem - mid.astype(f32)).astype(bf16)
        first = ref_lane[c]
        ext = jnp.where(lane == first, -hi,
                        jnp.where(lane == first + 1, -mid, jnp.where(lane == first + 2, -low, zero)))
        qp_ref[c] = jnp.where(is_q[c], q_ref[...], ext)

    def exact_chunk(j, c, flagged):
        s = lax.dot_general(k_chunk(j, c), qz_ref[c], _NT, preferred_element_type=f32) + bias_tile(j)
        r_old = r_ref[c]
        r_new = jnp.maximum(r_old, jnp.max(s, axis=0, keepdims=True))
        if flagged is not None:
            r_new = jnp.where(flagged, r_new, r_old)
        p = jnp.exp2(s - r_new)
        if flagged is not None:
            p = jnp.where(flagged, p, 0.0)
        pv = jnp.dot(vt_ref[0, j], p.astype(bf16), preferred_element_type=f32)
        acc_ref[c] = acc_ref[c] * jnp.exp2(r_old - r_new) + pv
        r_ref[c] = r_new

    def stage_a(j, slot):
        bias = bias_tile(j)
        for c in range(2):
            s = lax.dot_general(k_chunk(j, c), qp_ref[c], _NT, preferred_element_type=f32) + bias
            p_ref[slot, c] = jnp.exp2(s).astype(bf16)

    def stage_c(j, slot):
        vt = vt_ref[0, j]
        for c in range(2):
            pv = jnp.dot(vt, p_ref[slot, c], preferred_element_type=f32)
            ok = pv[B_V_DIM:B_V_DIM + 1] <= ROW_SUM_LIMIT
            acc_ref[c] = acc_ref[c] + jnp.where(ok, pv, 0.0)
            flag = jnp.where(ok, 0.0, 1.0)
            flag_ref[j, c] = flag
            any_ref[c] = jnp.maximum(any_ref[c], flag)

    def other(t):
        return jnp.where(t >= i, t + 1, t)

    q = q_ref[...]
    for c in range(2):
        qz_ref[c] = jnp.where(is_q[c], q, zero)
    r_ref[...] = jnp.full(r_ref.shape, MASK_VALUE, f32)
    acc_ref[...] = jnp.zeros(acc_ref.shape, f32)
    flag_ref[...] = jnp.zeros(flag_ref.shape, f32)
    any_ref[...] = jnp.zeros(any_ref.shape, f32)
    for c in range(2):
        exact_chunk(i, c, None)
        set_reference(c, r_ref[c])

    def run(t, count, has_next):
        for u in range(count):
            stage_c(other(t + u), u % 2)
            if u + 1 < count or has_next:
                stage_a(other(t + u + 1), (u + 1) % 2)

    nmain = nchunks - 1
    nbody = nmain // B_UNROLL
    stage_a(other(0), 0)

    def body(b, carry):
        run(b * B_UNROLL, B_UNROLL, True)
        return carry

    lax.fori_loop(0, nbody - 1, body, 0)
    run((nbody - 1) * B_UNROLL, nmain - (nbody - 1) * B_UNROLL, False)

    @pl.when(jnp.max(jnp.maximum(any_ref[0], any_ref[1])) > 0.0)
    def _():
        def redo(j, carry):
            @pl.when(jnp.max(jnp.maximum(flag_ref[j, 0], flag_ref[j, 1])) > 0.0)
            def _():
                for c in range(2):
                    exact_chunk(j, c, flag_ref[j, c] > 0.0)
            return carry

        lax.fori_loop(0, nchunks, redo, 0)

    lv = lamv_ref[...]
    lam = (jnp.exp(jnp.sum(lv[0:1] * lv[1:2], axis=-1, keepdims=True))
           - jnp.exp(jnp.sum(lv[2:3] * lv[3:4], axis=-1, keepdims=True)) + lambda_init)
    a1 = acc_ref[0]
    a2 = acc_ref[1]
    ot = (a1[0:B_V_DIM] / a1[B_V_DIM:B_V_DIM + 1]
          - lam * (a2[0:B_V_DIM] / a2[B_V_DIM:B_V_DIM + 1]))
    o = ot.T
    o_ref[...] = (_rms(o, g_ref[...]) * (1.0 - lambda_init)).astype(o_ref.dtype)


def _out_ffn_kernel(ya_ref, yb_ref, x_ref, woa_ref, wob_ref, gpost_ref, gpre_ref,
                    wg_ref, wu_ref, wd_ref, gfpost_ref, o_ref):
    y = (jnp.dot(ya_ref[...], woa_ref[...], preferred_element_type=jnp.float32)
         + jnp.dot(yb_ref[...], wob_ref[...], preferred_element_type=jnp.float32))
    h1 = x_ref[...] + _rms(y, gpost_ref[...])
    u = _rms(h1, gpre_ref[...]).astype(jnp.bfloat16)
    gate = jnp.dot(u, wg_ref[...], preferred_element_type=jnp.float32)
    up = jnp.dot(u, wu_ref[...], preferred_element_type=jnp.float32)
    act = (gate * jax.nn.sigmoid(gate) * up).astype(jnp.bfloat16)
    f = jnp.dot(act, wd_ref[...], preferred_element_type=jnp.float32)
    o_ref[...] = h1 + _rms(f, gfpost_ref[...])


def _resident(shape):
    zeros = (0,) * len(shape)
    return pl.BlockSpec(shape, lambda *_: zeros, pipeline_mode=pl.Buffered(1))


def _layer(h, l, p, bias_a, bias_b):
    S, D = h.shape
    bf16 = jnp.bfloat16
    lambda_init = 0.8 - 0.6 * math.exp(-0.3 * l)
    scale = HEAD_DIM ** -0.5

    w = p["w_in"]
    a_q, a_kv, b_qk = A_Q_HEADS * HEAD_DIM, A_KV_HEADS * HEAD_DIM, B_HEADS * 2 * HEAD_DIM
    c0 = a_q
    c1 = c0 + a_kv
    c2 = c1 + a_kv
    c3 = c2 + b_qk
    c4 = c3 + b_qk

    def dup(cols):
        parts = []
        for g in range(A_KV_HEADS):
            blk = cols[:, g * HEAD_DIM:(g + 1) * HEAD_DIM]
            parts += [blk, blk]
        return jnp.concatenate(parts, axis=1)

    def widen_b_keys(cols):
        wk = cols.reshape(D, B_HEADS, 2, HEAD_DIM)
        z = jnp.zeros_like(wk[:, :, 0])
        return jnp.stack([jnp.concatenate([wk[:, :, 0], z], axis=-1),
                          jnp.concatenate([z, wk[:, :, 1]], axis=-1)], axis=2).reshape(D, B_HEADS * 2 * LANES)

    kones = np.zeros((B_HEADS, 2, LANES), np.float32)
    kones[:, 0, HEAD_DIM:HEAD_DIM + B_REF_LANES] = 1.0
    kones[:, 1, 0:B_REF_LANES] = 1.0
    kones = jnp.asarray(kones.reshape(1, B_HEADS * 2 * LANES))

    w_cat = jnp.concatenate([w[:, :c0] * scale, dup(w[:, c0:c1]), dup(w[:, c1:c2]),
                             w[:, c2:c3] * (scale * LOG2E), widen_b_keys(w[:, c3:c4]), w[:, c4:]],
                            axis=1).astype(bf16)
    ncols = w_cat.shape[1]
    nrow = S // ROW_TILE
    nchunks = S // B_TILE
    assert ROW_TILE == B_TILE and nchunks % 2 == 0 and nchunks >= 4
    kb_cols = B_HEADS * 2 * LANES

    qa, ka, va, qb, kb, vt = pl.pallas_call(
        _in_proj_kernel,
        grid=(nrow,),
        in_specs=[pl.BlockSpec((ROW_TILE, D), lambda i: (i, 0)),
                  _resident((1, D)),
                  _resident((D, ncols)),
                  _resident((1, kb_cols))],
        out_specs=[pl.BlockSpec((ROW_TILE, 512), lambda i: (i, 0)),
                   pl.BlockSpec((ROW_TILE, 256), lambda i: (i, 0)),
                   pl.BlockSpec((ROW_TILE, 256), lambda i: (i, 0)),
                   pl.BlockSpec((ROW_TILE, 512), lambda i: (i, 0)),
                   pl.BlockSpec((ROW_TILE, kb_cols), lambda i: (i, 0)),
                   pl.BlockSpec((B_HEADS, 1, B_ACC_ROWS, ROW_TILE), lambda i: (0, i, 0, 0))],
        out_shape=[jax.ShapeDtypeStruct((S, 512), bf16),
                   jax.ShapeDtypeStruct((S, 256), bf16),
                   jax.ShapeDtypeStruct((S, 256), bf16),
                   jax.ShapeDtypeStruct((S, 512), bf16),
                   jax.ShapeDtypeStruct((S, kb_cols), bf16),
                   jax.ShapeDtypeStruct((B_HEADS, nchunks, B_ACC_ROWS, B_TILE), bf16)],
        compiler_params=pltpu.CompilerParams(dimension_semantics=("arbitrary",),
                                             vmem_limit_bytes=VMEM_LIMIT),
        name="in_proj",
    )(h, p["attn_pre_g"].reshape(1, D), w_cat, kones)

    nblocks = S // A_BLOCK
    pad = ((A_BLOCK, A_BLOCK), (0, 0))
    kp = jnp.pad(ka, pad)
    vp = jnp.pad(va, pad)
    ya = pl.pallas_call(
        functools.partial(_win_attn_kernel, nblocks=nblocks),
        grid=(nblocks,),
        in_specs=[pl.BlockSpec(memory_space=pltpu.SMEM),
                  pl.BlockSpec((A_BLOCK, 512), lambda n: (n, 0)),
                  _resident(kp.shape),
                  _resident(vp.shape),
                  _resident(bias_a.shape)],
        out_specs=pl.BlockSpec((A_BLOCK, 512), lambda n: (n, 0)),
        out_shape=jax.ShapeDtypeStruct((S, 512), bf16),
        compiler_params=pltpu.CompilerParams(dimension_semantics=("arbitrary",),
                                             vmem_limit_bytes=VMEM_LIMIT),
        name="win_attn",
    )(p["a_sink"], qa, kp, vp, bias_a)

    lamv = jnp.stack([p["lambda_q1"], p["lambda_k1"], p["lambda_q2"], p["lambda_k2"]])
    T = B_TILE
    yb = pl.pallas_call(
        functools.partial(_diff_attn_kernel, nchunks=nchunks, lambda_init=lambda_init),
        grid=(B_HEADS, nchunks),
        in_specs=[pl.BlockSpec((4, HEAD_DIM), lambda hh, i: (0, 0)),
                  pl.BlockSpec((1, B_V_DIM), lambda hh, i: (0, 0)),
                  pl.BlockSpec((T, LANES), lambda hh, i: (i, hh)),
                  pl.BlockSpec((S, 2 * LANES), lambda hh, i: (0, hh)),
                  pl.BlockSpec((1, nchunks, B_ACC_ROWS, T), lambda hh, i: (hh, 0, 0, 0)),
                  pl.BlockSpec((1, 5, T, T), lambda hh, i: (hh, 0, 0, 0))],
        out_specs=pl.BlockSpec((T, LANES), lambda hh, i: (i, hh)),
        out_shape=jax.ShapeDtypeStruct((S, B_HEADS * B_V_DIM), bf16),
        scratch_shapes=[pltpu.VMEM((2, T, LANES), bf16),
                        pltpu.VMEM((2, T, LANES), bf16),
                        pltpu.VMEM((2, 2, T, T), bf16),
                        pltpu.VMEM((2, 1, T), jnp.float32),
                        pltpu.VMEM((nchunks, 2, 1, T), jnp.float32),
                        pltpu.VMEM((2, 1, T), jnp.float32),
                        pltpu.VMEM((2, B_ACC_ROWS, T), jnp.float32)],
        compiler_params=pltpu.CompilerParams(dimension_semantics=("arbitrary", "arbitrary"),
                                             vmem_limit_bytes=VMEM_LIMIT),
        name="diff_attn",
    )(lamv, p["diff_subln_g"].reshape(1, B_V_DIM), qb, kb, vt, bias_b)

    w_out = p["w_out"].astype(bf16)
    a_width = A_Q_HEADS * HEAD_DIM
    d_ff = p["w_gate"].shape[1]
    R = FFN_ROW_TILE
    out = pl.pallas_call(
        _out_ffn_kernel,
        grid=(S // R,),
        in_specs=[pl.BlockSpec((R, 512), lambda i: (i, 0)),
                  pl.BlockSpec((R, 512), lambda i: (i, 0)),
                  pl.BlockSpec((R, D), lambda i: (i, 0)),
                  _resident((a_width, D)),
                  _resident((w_out.shape[0] - a_width, D)),
                  _resident((1, D)),
                  _resident((1, D)),
                  _resident((D, d_ff)),
                  _resident((D, d_ff)),
                  _resident((d_ff, D)),
                  _resident((1, D))],
        out_specs=pl.BlockSpec((R, D), lambda i: (i, 0)),
        out_shape=jax.ShapeDtypeStruct((S, D), jnp.float32),
        compiler_params=pltpu.CompilerParams(dimension_semantics=("arbitrary",),
                                             vmem_limit_bytes=VMEM_LIMIT),
        name="out_ffn",
    )(ya, yb, h, w_out[:a_width], w_out[a_width:], p["attn_post_g"].reshape(1, D),
      p["ffn_pre_g"].reshape(1, D), p["w_gate"].astype(bf16), p["w_up"].astype(bf16),
      p["w_down"].astype(bf16), p["ffn_post_g"].reshape(1, D))
    return out


def kernel(x, attn_pre_g, w_in, a_sink, lambda_q1, lambda_k1, lambda_q2, lambda_k2, diff_subln_g,
           rel_bias, w_out, attn_post_g, ffn_pre_g, w_gate, w_up, w_down, ffn_post_g):
    batch, S, D = x.shape
    depth = w_in.shape[0]
    assert S % B_TILE == 0 and S % A_BLOCK == 0 and S // A_BLOCK >= 2

    tab_a = _toeplitz_bias(rel_bias[:, :A_Q_HEADS], A_BLOCK, 3 * A_BLOCK, lambda t: -t - A_BLOCK)
    qi = np.arange(A_BLOCK)[:, None]
    kj = np.arange(3 * A_BLOCK)[None, :]
    in_window = np.abs(kj - A_BLOCK - qi) <= A_BLOCK
    valid = np.stack([in_window & (kj >= A_BLOCK), in_window, in_window & (kj < 2 * A_BLOCK)])
    bias_a = jnp.where(valid[:, None], tab_a[None], MASK_VALUE)

    T = B_TILE
    assert T >= MAX_DISTANCE
    tab_b = rel_bias[:, A_Q_HEADS:]
    near = [_toeplitz_bias(tab_b, T, T, lambda t, d=d: d * T + t) for d in (-1, 0, 1)]
    far = tab_b[_t5_bucket_np(np.array([-2 * T, 2 * T]))].astype(jnp.float32)
    const = [jnp.broadcast_to(far[side][:, None, None], (B_HEADS, T, T)) for side in (0, 1)]
    bias_b = jnp.stack([const[0]] + near + [const[1]], axis=1) * LOG2E

    outs = []
    for b in range(batch):
        h = x[b]
        for l in range(depth):
            p = dict(attn_pre_g=attn_pre_g[l], w_in=w_in[l], a_sink=a_sink[l], lambda_q1=lambda_q1[l],
                     lambda_k1=lambda_k1[l], lambda_q2=lambda_q2[l], lambda_k2=lambda_k2[l],
                     diff_subln_g=diff_subln_g[l], w_out=w_out[l], attn_post_g=attn_post_g[l],
                     ffn_pre_g=ffn_pre_g[l], w_gate=w_gate[l], w_up=w_up[l], w_down=w_down[l],
                     ffn_post_g=ffn_post_g[l])
            h = _layer(h, l, p, bias_a, bias_b)
        outs.append(h)
    return jnp.stack(outs)
```

```python
import functools
import math

import numpy as np
import jax
import jax.numpy as jnp
from jax import lax
from jax.experimental import pallas as pl
from jax.experimental.pallas import tpu as pltpu

HEAD_DIM = 64
A_Q_HEADS = 8
A_KV_HEADS = 2
A_BLOCK = 128
B_HEADS = 4
B_V_DIM = 2 * HEAD_DIM
NUM_BUCKETS = 32
MAX_DISTANCE = 128
EPS = 1e-6
MASK_VALUE = -1e30
LOG2E = math.log2(math.e)
ROW_SUM_LIMIT = 2.0 ** 40

LANES = 128
BF16_SUBLANES = 16
ROW_TILE = 512
FFN_ROW_TILE = 256
B_TILE = 512
B_ACC_ROWS = B_V_DIM + BF16_SUBLANES
B_REF_LANES = 3
B_UNROLL = 4
B_PROBE = 128
VMEM_LIMIT = 56 * 1024 * 1024

_NT = (((1,), (1,)), ((), ()))


def _t5_bucket_np(rel):
    nb = NUM_BUCKETS // 2
    max_exact = nb // 2
    ret = np.where(rel > 0, nb, 0)
    n = np.abs(rel)
    nf = np.maximum(n, 1).astype(np.float32)
    large = max_exact + (np.log(nf / np.float32(max_exact)) / np.float32(math.log(MAX_DISTANCE / max_exact))
                         * np.float32(nb - max_exact)).astype(np.int32)
    large = np.minimum(large, nb - 1)
    return (ret + np.where(n < max_exact, n, large)).astype(np.int32)


def _toeplitz_bias(table, rows, cols, rel_of):
    length = rows + cols
    u = np.arange(length)
    t = np.where(u < cols, -u, length - u)
    w = table[_t5_bucket_np(rel_of(t))].astype(jnp.float32).T
    x = jnp.tile(w, (1, rows))[:, :rows * (length - 1)].reshape(w.shape[0], rows, length - 1)
    return x[:, :, :cols]


def _rms(xf, g):
    return xf * lax.rsqrt(jnp.mean(xf * xf, axis=-1, keepdims=True) + EPS) * g


def _in_proj_kernel(x_ref, g_ref, w_ref, kones_ref, qa_ref, ka_ref, va_ref, qb_ref, kb_ref, vt_ref):
    u = _rms(x_ref[...], g_ref[...]).astype(jnp.bfloat16)
    proj = jnp.dot(u, w_ref[...], preferred_element_type=jnp.float32)
    qa_ref[...] = proj[:, 0:512].astype(jnp.bfloat16)
    ka_ref[...] = proj[:, 512:768].astype(jnp.bfloat16)
    va_ref[...] = proj[:, 768:1024].astype(jnp.bfloat16)
    qb_ref[...] = proj[:, 1024:1536].astype(jnp.bfloat16)
    kb_ref[...] = (proj[:, 1536:2560] + kones_ref[...]).astype(jnp.bfloat16)
    rows = proj.shape[0]
    for h in range(B_HEADS):
        v = proj[:, 2560 + h * B_V_DIM:2560 + (h + 1) * B_V_DIM]
        vt_ref[h, 0, 0:B_V_DIM, :] = v.T.astype(jnp.bfloat16)
        vt_ref[h, 0, B_V_DIM:B_ACC_ROWS, :] = jnp.ones((BF16_SUBLANES, rows), jnp.bfloat16)


def _win_attn_kernel(sink_ref, q_ref, k_ref, v_ref, bias_ref, o_ref, *, nblocks):
    n = pl.program_id(0)
    variant = jnp.where(n == 0, 0, jnp.where(n == nblocks - 1, 2, 1))
    start = pl.multiple_of(n * A_BLOCK, A_BLOCK)
    kw = k_ref[pl.ds(start, 3 * A_BLOCK), :]
    vw = v_ref[pl.ds(start, 3 * A_BLOCK), :]
    q = q_ref[...]
    lane = lax.broadcasted_iota(jnp.int32, (A_BLOCK, LANES), 1)
    low = lane < HEAD_DIM
    for pair in range(A_Q_HEADS // 2):
        q2 = q[:, pair * LANES:(pair + 1) * LANES]
        kvh = pair // 2
        kg = kw[:, kvh * LANES:(kvh + 1) * LANES]
        vg = vw[:, kvh * LANES:(kvh + 1) * LANES]
        res = []
        for e in range(2):
            hq = 2 * pair + e
            qm = jnp.where(low if e == 0 else jnp.logical_not(low), q2, jnp.zeros_like(q2))
            s = lax.dot_general(qm, kg, _NT, preferred_element_type=jnp.float32)
            s = s + bias_ref[variant, hq]
            snk = sink_ref[hq]
            m = jnp.maximum(jnp.max(s, axis=-1, keepdims=True), snk)
            p = jnp.exp(s - m)
            den = jnp.sum(p, axis=-1, keepdims=True) + jnp.exp(snk - m)
            o = jnp.dot(p.astype(jnp.bfloat16), vg, preferred_element_type=jnp.float32)
            res.append(o / den)
        o_ref[:, pair * LANES:(pair + 1) * LANES] = jnp.where(low, res[0], res[1]).astype(o_ref.dtype)


def _diff_attn_kernel(lamv_ref, g_ref, q_ref, k_ref, vt_ref, bias_ref, o_ref,
                      qz_ref, qp_ref, p_ref, r_ref, flag_ref, any_ref, acc_ref, *, nchunks, lambda_init):
    T = B_TILE
    i = pl.program_id(1)
    f32 = jnp.float32
    bf16 = jnp.bfloat16

    qt = q_ref[...].astype(f32).T
    row = lax.broadcasted_iota(jnp.int32, (LANES, T), 0)
    is_q = (row < HEAD_DIM, row >= HEAD_DIM)
    ref_row = (HEAD_DIM, 0)

    def bias_tile(j):
        return bias_ref[0, jnp.clip(j - i, -2, 2) + 2]

    def k_chunk(j, c):
        return k_ref[pl.ds(pl.multiple_of(j * T, T), T), c * LANES:(c + 1) * LANES]

    def set_reference(c, r_row):
        hi = r_row.astype(bf16).astype(f32)
        rem = r_row - hi
        mid = rem.astype(bf16).astype(f32)
        low = (rem - mid).astype(bf16).astype(f32)
        first = ref_row[c]
        ext = jnp.where(row == first, -hi, jnp.where(row == first + 1, -mid,
                                                     jnp.where(row == first + 2, -low, 0.0)))
        qp_ref[c] = jnp.where(is_q[c], qt, ext).astype(bf16)

    def exact_chunk(j, c, flagged):
        s = jnp.dot(k_chunk(j, c), qz_ref[c], preferred_element_type=f32) + bias_tile(j)
        r_old = r_ref[c]
        r_new = jnp.where(flagged, jnp.maximum(r_old, jnp.max(s, axis=0, keepdims=True)), r_old)
        p = jnp.where(flagged, jnp.exp2(s - r_new), 0.0)
        pv = jnp.dot(vt_ref[0, j], p.astype(bf16), preferred_element_type=f32)
        acc_ref[c] = acc_ref[c] * jnp.exp2(r_old - r_new) + pv
        r_ref[c] = r_new

    def stage_a(j, slot):
        bias = bias_tile(j)
        for c in range(2):
            s = jnp.dot(k_chunk(j, c), qp_ref[c], preferred_element_type=f32) + bias
            p_ref[slot, c] = jnp.exp2(s).astype(bf16)

    def stage_c(j, slot):
        vt = vt_ref[0, j]
        for c in range(2):
            pv = jnp.dot(vt, p_ref[slot, c], preferred_element_type=f32)
            ok = pv[B_V_DIM:B_V_DIM + 1] <= ROW_SUM_LIMIT
            acc_ref[c] = acc_ref[c] + jnp.where(ok, pv, 0.0)
            flag = jnp.where(ok, 0.0, 1.0)
            flag_ref[j, c] = flag
            any_ref[c] = jnp.maximum(any_ref[c], flag)

    for c in range(2):
        qz_ref[c] = jnp.where(is_q[c], qt, 0.0).astype(bf16)
    acc_ref[...] = jnp.zeros(acc_ref.shape, f32)
    any_ref[...] = jnp.zeros(any_ref.shape, f32)
    probe_bias = bias_ref[0, 2, 0:B_PROBE, :]
    for c in range(2):
        kp = k_ref[pl.ds(pl.multiple_of(i * T, T), B_PROBE), c * LANES:(c + 1) * LANES]
        s = jnp.dot(kp, qz_ref[c], preferred_element_type=f32) + probe_bias
        r = jnp.max(s, axis=0, keepdims=True)
        r_ref[c] = r
        set_reference(c, r)

    def run(j0, has_next):
        for u in range(B_UNROLL):
            stage_c(j0 + u, u % 2)
            if u + 1 < B_UNROLL or has_next:
                stage_a(j0 + u + 1, (u + 1) % 2)

    nbody = nchunks // B_UNROLL
    stage_a(0, 0)

    def body(b, carry):
        run(b * B_UNROLL, True)
        return carry

    lax.fori_loop(0, nbody - 1, body, 0)
    run((nbody - 1) * B_UNROLL, False)

    @pl.when(jnp.max(jnp.maximum(any_ref[0], any_ref[1])) > 0.0)
    def _():
        def redo(j, carry):
            @pl.when(jnp.max(jnp.maximum(flag_ref[j, 0], flag_ref[j, 1])) > 0.0)
            def _():
                for c in range(2):
                    exact_chunk(j, c, flag_ref[j, c] > 0.0)
            return carry

        lax.fori_loop(0, nchunks, redo, 0)

    lv = lamv_ref[...]
    lam = (jnp.exp(jnp.sum(lv[0:1] * lv[1:2], axis=-1, keepdims=True))
           - jnp.exp(jnp.sum(lv[2:3] * lv[3:4], axis=-1, keepdims=True)) + lambda_init)
    a1 = acc_ref[0]
    a2 = acc_ref[1]
    ot = (a1[0:B_V_DIM] / a1[B_V_DIM:B_V_DIM + 1]
          - lam * (a2[0:B_V_DIM] / a2[B_V_DIM:B_V_DIM + 1]))
    o = ot.T
    o_ref[...] = (_rms(o, g_ref[...]) * (1.0 - lambda_init)).astype(o_ref.dtype)


def _out_ffn_kernel(ya_ref, yb_ref, x_ref, woa_ref, wob_ref, gpost_ref, gpre_ref,
                    wg_ref, wu_ref, wd_ref, gfpost_ref, o_ref):
    y = (jnp.dot(ya_ref[...], woa_ref[...], preferred_element_type=jnp.float32)
         + jnp.dot(yb_ref[...], wob_ref[...], preferred_element_type=jnp.float32))
    h1 = x_ref[...] + _rms(y, gpost_ref[...])
    u = _rms(h1, gpre_ref[...]).astype(jnp.bfloat16)
    gate = jnp.dot(u, wg_ref[...], preferred_element_type=jnp.float32)
    up = jnp.dot(u, wu_ref[...], preferred_element_type=jnp.float32)
    act = (gate * jax.nn.sigmoid(gate) * up).astype(jnp.bfloat16)
    f = jnp.dot(act, wd_ref[...], preferred_element_type=jnp.float32)
    o_ref[...] = h1 + _rms(f, gfpost_ref[...])


def _resident(shape):
    zeros = (0,) * len(shape)
    return pl.BlockSpec(shape, lambda *_: zeros, pipeline_mode=pl.Buffered(1))


def _layer(h, l, p, bias_a, bias_b):
    S, D = h.shape
    bf16 = jnp.bfloat16
    lambda_init = 0.8 - 0.6 * math.exp(-0.3 * l)
    scale = HEAD_DIM ** -0.5

    w = p["w_in"]
    a_q, a_kv, b_qk = A_Q_HEADS * HEAD_DIM, A_KV_HEADS * HEAD_DIM, B_HEADS * 2 * HEAD_DIM
    c0 = a_q
    c1 = c0 + a_kv
    c2 = c1 + a_kv
    c3 = c2 + b_qk
    c4 = c3 + b_qk

    def dup(cols):
        parts = []
        for g in range(A_KV_HEADS):
            blk = cols[:, g * HEAD_DIM:(g + 1) * HEAD_DIM]
            parts += [blk, blk]
        return jnp.concatenate(parts, axis=1)

    def widen_b_keys(cols):
        wk = cols.reshape(D, B_HEADS, 2, HEAD_DIM)
        z = jnp.zeros_like(wk[:, :, 0])
        return jnp.stack([jnp.concatenate([wk[:, :, 0], z], axis=-1),
                          jnp.concatenate([z, wk[:, :, 1]], axis=-1)], axis=2).reshape(D, B_HEADS * 2 * LANES)

    kones = np.zeros((B_HEADS, 2, LANES), np.float32)
    kones[:, 0, HEAD_DIM:HEAD_DIM + B_REF_LANES] = 1.0
    kones[:, 1, 0:B_REF_LANES] = 1.0
    kones = jnp.asarray(kones.reshape(1, B_HEADS * 2 * LANES))

    w_cat = jnp.concatenate([w[:, :c0] * scale, dup(w[:, c0:c1]), dup(w[:, c1:c2]),
                             w[:, c2:c3] * (scale * LOG2E), widen_b_keys(w[:, c3:c4]), w[:, c4:]],
                            axis=1).astype(bf16)
    ncols = w_cat.shape[1]
    nrow = S // ROW_TILE
    nchunks = S // B_TILE
    assert ROW_TILE == B_TILE and B_UNROLL % 2 == 0 and nchunks % B_UNROLL == 0
    kb_cols = B_HEADS * 2 * LANES

    qa, ka, va, qb, kb, vt = pl.pallas_call(
        _in_proj_kernel,
        grid=(nrow,),
        in_specs=[pl.BlockSpec((ROW_TILE, D), lambda i: (i, 0)),
                  _resident((1, D)),
                  _resident((D, ncols)),
                  _resident((1, kb_cols))],
        out_specs=[pl.BlockSpec((ROW_TILE, 512), lambda i: (i, 0)),
                   pl.BlockSpec((ROW_TILE, 256), lambda i: (i, 0)),
                   pl.BlockSpec((ROW_TILE, 256), lambda i: (i, 0)),
                   pl.BlockSpec((ROW_TILE, 512), lambda i: (i, 0)),
                   pl.BlockSpec((ROW_TILE, kb_cols), lambda i: (i, 0)),
                   pl.BlockSpec((B_HEADS, 1, B_ACC_ROWS, ROW_TILE), lambda i: (0, i, 0, 0))],
        out_shape=[jax.ShapeDtypeStruct((S, 512), bf16),
                   jax.ShapeDtypeStruct((S, 256), bf16),
                   jax.ShapeDtypeStruct((S, 256), bf16),
                   jax.ShapeDtypeStruct((S, 512), bf16),
                   jax.ShapeDtypeStruct((S, kb_cols), bf16),
                   jax.ShapeDtypeStruct((B_HEADS, nchunks, B_ACC_ROWS, B_TILE), bf16)],
        compiler_params=pltpu.CompilerParams(dimension_semantics=("arbitrary",),
                                             vmem_limit_bytes=VMEM_LIMIT),
        name="in_proj",
    )(h, p["attn_pre_g"].reshape(1, D), w_cat, kones)

    nblocks = S // A_BLOCK
    pad = ((A_BLOCK, A_BLOCK), (0, 0))
    kp = jnp.pad(ka, pad)
    vp = jnp.pad(va, pad)
    ya = pl.pallas_call(
        functools.partial(_win_attn_kernel, nblocks=nblocks),
        grid=(nblocks,),
        in_specs=[pl.BlockSpec(memory_space=pltpu.SMEM),
                  pl.BlockSpec((A_BLOCK, 512), lambda n: (n, 0)),
                  _resident(kp.shape),
                  _resident(vp.shape),
                  _resident(bias_a.shape)],
        out_specs=pl.BlockSpec((A_BLOCK, 512), lambda n: (n, 0)),
        out_shape=jax.ShapeDtypeStruct((S, 512), bf16),
        compiler_params=pltpu.CompilerParams(dimension_semantics=("arbitrary",),
                                             vmem_limit_bytes=VMEM_LIMIT),
        name="win_attn",
    )(p["a_sink"], qa, kp, vp, bias_a)

    lamv = jnp.stack([p["lambda_q1"], p["lambda_k1"], p["lambda_q2"], p["lambda_k2"]])
    T = B_TILE
    yb = pl.pallas_call(
        functools.partial(_diff_attn_kernel, nchunks=nchunks, lambda_init=lambda_init),
        grid=(B_HEADS, nchunks),
        in_specs=[pl.BlockSpec((4, HEAD_DIM), lambda hh, i: (0, 0)),
                  pl.BlockSpec((1, B_V_DIM), lambda hh, i: (0, 0)),
                  pl.BlockSpec((T, LANES), lambda hh, i: (i, hh)),
                  pl.BlockSpec((S, 2 * LANES), lambda hh, i: (0, hh)),
                  pl.BlockSpec((1, nchunks, B_ACC_ROWS, T), lambda hh, i: (hh, 0, 0, 0)),
                  pl.BlockSpec((1, 5, T, T), lambda hh, i: (hh, 0, 0, 0))],
        out_specs=pl.BlockSpec((T, LANES), lambda hh, i: (i, hh)),
        out_shape=jax.ShapeDtypeStruct((S, B_HEADS * B_V_DIM), bf16),
        scratch_shapes=[pltpu.VMEM((2, LANES, T), bf16),
                        pltpu.VMEM((2, LANES, T), bf16),
                        pltpu.VMEM((2, 2, T, T), bf16),
                        pltpu.VMEM((2, 1, T), jnp.float32),
                        pltpu.VMEM((nchunks, 2, 1, T), jnp.float32),
                        pltpu.VMEM((2, 1, T), jnp.float32),
                        pltpu.VMEM((2, B_ACC_ROWS, T), jnp.float32)],
        compiler_params=pltpu.CompilerParams(dimension_semantics=("arbitrary", "arbitrary"),
                                             vmem_limit_bytes=VMEM_LIMIT),
        name="diff_attn",
    )(lamv, p["diff_subln_g"].reshape(1, B_V_DIM), qb, kb, vt, bias_b)

    w_out = p["w_out"].astype(bf16)
    a_width = A_Q_HEADS * HEAD_DIM
    d_ff = p["w_gate"].shape[1]
    R = FFN_ROW_TILE
    out = pl.pallas_call(
        _out_ffn_kernel,
        grid=(S // R,),
        in_specs=[pl.BlockSpec((R, 512), lambda i: (i, 0)),
                  pl.BlockSpec((R, 512), lambda i: (i, 0)),
                  pl.BlockSpec((R, D), lambda i: (i, 0)),
                  _resident((a_width, D)),
                  _resident((w_out.shape[0] - a_width, D)),
                  _resident((1, D)),
                  _resident((1, D)),
                  _resident((D, d_ff)),
                  _resident((D, d_ff)),
                  _resident((d_ff, D)),
                  _resident((1, D))],
        out_specs=pl.BlockSpec((R, D), lambda i: (i, 0)),
        out_shape=jax.ShapeDtypeStruct((S, D), jnp.float32),
        compiler_params=pltpu.CompilerParams(dimension_semantics=("arbitrary",),
                                             vmem_limit_bytes=VMEM_LIMIT),
        name="out_ffn",
    )(ya, yb, h, w_out[:a_width], w_out[a_width:], p["attn_post_g"].reshape(1, D),
      p["ffn_pre_g"].reshape(1, D), p["w_gate"].astype(bf16), p["w_up"].astype(bf16),
      p["w_down"].astype(bf16), p["ffn_post_g"].reshape(1, D))
    return out


def kernel(x, attn_pre_g, w_in, a_sink, lambda_q1, lambda_k1, lambda_q2, lambda_k2, diff_subln_g,
           rel_bias, w_out, attn_post_g, ffn_pre_g, w_gate, w_up, w_down, ffn_post_g):
    batch, S, D = x.shape
    depth = w_in.shape[0]
    assert S % B_TILE == 0 and S % A_BLOCK == 0 and S // A_BLOCK >= 2

    tab_a = _toeplitz_bias(rel_bias[:, :A_Q_HEADS], A_BLOCK, 3 * A_BLOCK, lambda t: -t - A_BLOCK)
    qi = np.arange(A_BLOCK)[:, None]
    kj = np.arange(3 * A_BLOCK)[None, :]
    in_window = np.abs(kj - A_BLOCK - qi) <= A_BLOCK
    valid = np.stack([in_window & (kj >= A_BLOCK), in_window, in_window & (kj < 2 * A_BLOCK)])
    bias_a = jnp.where(valid[:, None], tab_a[None], MASK_VALUE)

    T = B_TILE
    assert T >= MAX_DISTANCE
    tab_b = rel_bias[:, A_Q_HEADS:]
    near = [_toeplitz_bias(tab_b, T, T, lambda t, d=d: d * T + t) for d in (-1, 0, 1)]
    far = tab_b[_t5_bucket_np(np.array([-2 * T, 2 * T]))].astype(jnp.float32)
    const = [jnp.broadcast_to(far[side][:, None, None], (B_HEADS, T, T)) for side in (0, 1)]
    bias_b = jnp.stack([const[0]] + near + [const[1]], axis=1) * LOG2E

    outs = []
    for b in range(batch):
        h = x[b]
        for l in range(depth):
            p = dict(attn_pre_g=attn_pre_g[l], w_in=w_in[l], a_sink=a_sink[l], lambda_q1=lambda_q1[l],
                     lambda_k1=lambda_k1[l], lambda_q2=lambda_q2[l], lambda_k2=lambda_k2[l],
                     diff_subln_g=diff_subln_g[l], w_out=w_out[l], attn_post_g=attn_post_g[l],
                     ffn_pre_g=ffn_pre_g[l], w_gate=w_gate[l], w_up=w_up[l], w_down=w_down[l],
                     ffn_post_g=ffn_post_g[l])
            h = _layer(h, l, p, bias_a, bias_b)
        outs.append(h)
    return jnp.stack(outs)
```

```python
import functools
import math

import numpy as np
import jax
import jax.numpy as jnp
from jax import lax
from jax.experimental import pallas as pl
from jax.experimental.pallas import tpu as pltpu

HEAD_DIM = 64
A_Q_HEADS = 8
A_KV_HEADS = 2
A_BLOCK = 128
B_HEADS = 4
B_V_DIM = 2 * HEAD_DIM
NUM_BUCKETS = 32
MAX_DISTANCE = 128
EPS = 1e-6
MASK_VALUE = -1e30
LOG2E = math.log2(math.e)
ROW_SUM_LIMIT = 2.0 ** 40

LANES = 128
BF16_SUBLANES = 16
ROW_TILE = 512
FFN_ROW_TILE = 256
B_TILE = 512
B_ACC_ROWS = B_V_DIM + BF16_SUBLANES
B_REF_LANES = 3
B_UNROLL = 8
B_AHEAD = 2
B_SLOTS = 4
B_PROBE = 128
VMEM_LIMIT = 56 * 1024 * 1024

_NT = (((1,), (1,)), ((), ()))


def _t5_bucket_np(rel):
    nb = NUM_BUCKETS // 2
    max_exact = nb // 2
    ret = np.where(rel > 0, nb, 0)
    n = np.abs(rel)
    nf = np.maximum(n, 1).astype(np.float32)
    large = max_exact + (np.log(nf / np.float32(max_exact)) / np.float32(math.log(MAX_DISTANCE / max_exact))
                         * np.float32(nb - max_exact)).astype(np.int32)
    large = np.minimum(large, nb - 1)
    return (ret + np.where(n < max_exact, n, large)).astype(np.int32)


def _toeplitz_bias(table, rows, cols, rel_of):
    length = rows + cols
    u = np.arange(length)
    t = np.where(u < cols, -u, length - u)
    w = table[_t5_bucket_np(rel_of(t))].astype(jnp.float32).T
    x = jnp.tile(w, (1, rows))[:, :rows * (length - 1)].reshape(w.shape[0], rows, length - 1)
    return x[:, :, :cols]


def _rms(xf, g):
    return xf * lax.rsqrt(jnp.mean(xf * xf, axis=-1, keepdims=True) + EPS) * g


def _in_proj_kernel(x_ref, g_ref, w_ref, kones_ref, qa_ref, ka_ref, va_ref, qb_ref, kb_ref, vt_ref):
    u = _rms(x_ref[...], g_ref[...]).astype(jnp.bfloat16)
    proj = jnp.dot(u, w_ref[...], preferred_element_type=jnp.float32)
    qa_ref[...] = proj[:, 0:512].astype(jnp.bfloat16)
    ka_ref[...] = proj[:, 512:768].astype(jnp.bfloat16)
    va_ref[...] = proj[:, 768:1024].astype(jnp.bfloat16)
    qb_ref[...] = proj[:, 1024:1536].astype(jnp.bfloat16)
    kb_ref[...] = (proj[:, 1536:2560] + kones_ref[...]).astype(jnp.bfloat16)
    rows = proj.shape[0]
    for h in range(B_HEADS):
        v = proj[:, 2560 + h * B_V_DIM:2560 + (h + 1) * B_V_DIM]
        vt_ref[h, 0, 0:B_V_DIM, :] = v.T.astype(jnp.bfloat16)
        vt_ref[h, 0, B_V_DIM:B_ACC_ROWS, :] = jnp.ones((BF16_SUBLANES, rows), jnp.bfloat16)


def _win_attn_kernel(sink_ref, q_ref, k_ref, v_ref, bias_ref, o_ref, *, nblocks):
    n = pl.program_id(0)
    variant = jnp.where(n == 0, 0, jnp.where(n == nblocks - 1, 2, 1))
    start = pl.multiple_of(n * A_BLOCK, A_BLOCK)
    kw = k_ref[pl.ds(start, 3 * A_BLOCK), :]
    vw = v_ref[pl.ds(start, 3 * A_BLOCK), :]
    q = q_ref[...]
    lane = lax.broadcasted_iota(jnp.int32, (A_BLOCK, LANES), 1)
    low = lane < HEAD_DIM
    for pair in range(A_Q_HEADS // 2):
        q2 = q[:, pair * LANES:(pair + 1) * LANES]
        kvh = pair // 2
        kg = kw[:, kvh * LANES:(kvh + 1) * LANES]
        vg = vw[:, kvh * LANES:(kvh + 1) * LANES]
        res = []
        for e in range(2):
            hq = 2 * pair + e
            qm = jnp.where(low if e == 0 else jnp.logical_not(low), q2, jnp.zeros_like(q2))
            s = lax.dot_general(qm, kg, _NT, preferred_element_type=jnp.float32)
            s = s + bias_ref[variant, hq]
            snk = sink_ref[hq]
            m = jnp.maximum(jnp.max(s, axis=-1, keepdims=True), snk)
            p = jnp.exp(s - m)
            den = jnp.sum(p, axis=-1, keepdims=True) + jnp.exp(snk - m)
            o = jnp.dot(p.astype(jnp.bfloat16), vg, preferred_element_type=jnp.float32)
            res.append(o / den)
        o_ref[:, pair * LANES:(pair + 1) * LANES] = jnp.where(low, res[0], res[1]).astype(o_ref.dtype)


def _diff_attn_kernel(lamv_ref, g_ref, q_ref, k_ref, vt_ref, bias_ref, o_ref,
                      qz_ref, qp_ref, p_ref, r_ref, flag_ref, any_ref, acc_ref, *, nchunks, lambda_init):
    T = B_TILE
    i = pl.program_id(1)
    f32 = jnp.float32
    bf16 = jnp.bfloat16

    qt = q_ref[...].astype(f32).T
    row = lax.broadcasted_iota(jnp.int32, (LANES, T), 0)
    is_q = (row < HEAD_DIM, row >= HEAD_DIM)
    ref_row = (HEAD_DIM, 0)

    def bias_tile(j):
        return bias_ref[0, jnp.clip(j - i, -2, 2) + 2]

    def k_chunk(j, c):
        return k_ref[pl.ds(pl.multiple_of(j * T, T), T), c * LANES:(c + 1) * LANES]

    def set_reference(c, r_row):
        hi = r_row.astype(bf16).astype(f32)
        rem = r_row - hi
        mid = rem.astype(bf16).astype(f32)
        low = (rem - mid).astype(bf16).astype(f32)
        first = ref_row[c]
        ext = jnp.where(row == first, -hi, jnp.where(row == first + 1, -mid,
                                                     jnp.where(row == first + 2, -low, 0.0)))
        qp_ref[c] = jnp.where(is_q[c], qt, ext).astype(bf16)

    def exact_chunk(j, c, flagged):
        s = jnp.dot(k_chunk(j, c), qz_ref[c], preferred_element_type=f32) + bias_tile(j)
        r_old = r_ref[c]
        r_new = jnp.where(flagged, jnp.maximum(r_old, jnp.max(s, axis=0, keepdims=True)), r_old)
        p = jnp.where(flagged, jnp.exp2(s - r_new), 0.0)
        pv = jnp.dot(vt_ref[0, j], p.astype(bf16), preferred_element_type=f32)
        acc_ref[c] = acc_ref[c] * jnp.exp2(r_old - r_new) + pv
        r_ref[c] = r_new

    def stage_a(j, slot):
        bias = bias_tile(j)
        for c in range(2):
            s = jnp.dot(k_chunk(j, c), qp_ref[c], preferred_element_type=f32) + bias
            p_ref[slot, c] = jnp.exp2(s).astype(bf16)

    def stage_c(j, slot):
        vt = vt_ref[0, j]
        for c in range(2):
            pv = jnp.dot(vt, p_ref[slot, c], preferred_element_type=f32)
            ok = pv[B_V_DIM:B_V_DIM + 1] <= ROW_SUM_LIMIT
            acc_ref[c] = acc_ref[c] + jnp.where(ok, pv, 0.0)
            flag = jnp.where(ok, 0.0, 1.0)
            flag_ref[j, c] = flag
            any_ref[c] = jnp.maximum(any_ref[c], flag)

    for c in range(2):
        qz_ref[c] = jnp.where(is_q[c], qt, 0.0).astype(bf16)
    acc_ref[...] = jnp.zeros(acc_ref.shape, f32)
    any_ref[...] = jnp.zeros(any_ref.shape, f32)
    probe_bias = bias_ref[0, 2, 0:B_PROBE, :]
    for c in range(2):
        kp = k_ref[pl.ds(pl.multiple_of(i * T, T), B_PROBE), c * LANES:(c + 1) * LANES]
        s = jnp.dot(kp, qz_ref[c], preferred_element_type=f32) + probe_bias
        r = jnp.max(s, axis=0, keepdims=True)
        r_ref[c] = r
        set_reference(c, r)

    def run(j0, has_next):
        for u in range(B_UNROLL):
            stage_c(j0 + u, u % B_SLOTS)
            if u + B_AHEAD < B_UNROLL or has_next:
                stage_a(j0 + u + B_AHEAD, (u + B_AHEAD) % B_SLOTS)

    nbody = nchunks // B_UNROLL
    for u in range(B_AHEAD):
        stage_a(u, u)

    def body(b, carry):
        run(b * B_UNROLL, True)
        return carry

    lax.fori_loop(0, nbody - 1, body, 0)
    run((nbody - 1) * B_UNROLL, False)

    @pl.when(jnp.max(jnp.maximum(any_ref[0], any_ref[1])) > 0.0)
    def _():
        def redo(j, carry):
            @pl.when(jnp.max(jnp.maximum(flag_ref[j, 0], flag_ref[j, 1])) > 0.0)
            def _():
                for c in range(2):
                    exact_chunk(j, c, flag_ref[j, c] > 0.0)
            return carry

        lax.fori_loop(0, nchunks, redo, 0)

    lv = lamv_ref[...]
    lam = (jnp.exp(jnp.sum(lv[0:1] * lv[1:2], axis=-1, keepdims=True))
           - jnp.exp(jnp.sum(lv[2:3] * lv[3:4], axis=-1, keepdims=True)) + lambda_init)
    a1 = acc_ref[0]
    a2 = acc_ref[1]
    ot = (a1[0:B_V_DIM] / a1[B_V_DIM:B_V_DIM + 1]
          - lam * (a2[0:B_V_DIM] / a2[B_V_DIM:B_V_DIM + 1]))
    o = ot.T
    o_ref[...] = (_rms(o, g_ref[...]) * (1.0 - lambda_init)).astype(o_ref.dtype)


def _out_ffn_kernel(ya_ref, yb_ref, x_ref, woa_ref, wob_ref, gpost_ref, gpre_ref,
                    wg_ref, wu_ref, wd_ref, gfpost_ref, o_ref):
    y = (jnp.dot(ya_ref[...], woa_ref[...], preferred_element_type=jnp.float32)
         + jnp.dot(yb_ref[...], wob_ref[...], preferred_element_type=jnp.float32))
    h1 = x_ref[...] + _rms(y, gpost_ref[...])
    u = _rms(h1, gpre_ref[...]).astype(jnp.bfloat16)
    gate = jnp.dot(u, wg_ref[...], preferred_element_type=jnp.float32)
    up = jnp.dot(u, wu_ref[...], preferred_element_type=jnp.float32)
    act = (gate * jax.nn.sigmoid(gate) * up).astype(jnp.bfloat16)
    f = jnp.dot(act, wd_ref[...], preferred_element_type=jnp.float32)
    o_ref[...] = h1 + _rms(f, gfpost_ref[...])


def _resident(shape):
    zeros = (0,) * len(shape)
    return pl.BlockSpec(shape, lambda *_: zeros, pipeline_mode=pl.Buffered(1))


def _layer(h, l, p, bias_a, bias_b):
    S, D = h.shape
    bf16 = jnp.bfloat16
    lambda_init = 0.8 - 0.6 * math.exp(-0.3 * l)
    scale = HEAD_DIM ** -0.5

    w = p["w_in"]
    a_q, a_kv, b_qk = A_Q_HEADS * HEAD_DIM, A_KV_HEADS * HEAD_DIM, B_HEADS * 2 * HEAD_DIM
    c0 = a_q
    c1 = c0 + a_kv
    c2 = c1 + a_kv
    c3 = c2 + b_qk
    c4 = c3 + b_qk

    def dup(cols):
        parts = []
        for g in range(A_KV_HEADS):
            blk = cols[:, g * HEAD_DIM:(g + 1) * HEAD_DIM]
            parts += [blk, blk]
        return jnp.concatenate(parts, axis=1)

    def widen_b_keys(cols):
        wk = cols.reshape(D, B_HEADS, 2, HEAD_DIM)
        z = jnp.zeros_like(wk[:, :, 0])
        return jnp.stack([jnp.concatenate([wk[:, :, 0], z], axis=-1),
                          jnp.concatenate([z, wk[:, :, 1]], axis=-1)], axis=2).reshape(D, B_HEADS * 2 * LANES)

    kones = np.zeros((B_HEADS, 2, LANES), np.float32)
    kones[:, 0, HEAD_DIM:HEAD_DIM + B_REF_LANES] = 1.0
    kones[:, 1, 0:B_REF_LANES] = 1.0
    kones = jnp.asarray(kones.reshape(1, B_HEADS * 2 * LANES))

    w_cat = jnp.concatenate([w[:, :c0] * scale, dup(w[:, c0:c1]), dup(w[:, c1:c2]),
                             w[:, c2:c3] * (scale * LOG2E), widen_b_keys(w[:, c3:c4]), w[:, c4:]],
                            axis=1).astype(bf16)
    ncols = w_cat.shape[1]
    nrow = S // ROW_TILE
    nchunks = S // B_TILE
    assert ROW_TILE == B_TILE and nchunks % B_UNROLL == 0
    assert B_UNROLL % B_SLOTS == 0 and B_AHEAD < B_SLOTS and B_AHEAD <= B_UNROLL
    kb_cols = B_HEADS * 2 * LANES

    qa, ka, va, qb, kb, vt = pl.pallas_call(
        _in_proj_kernel,
        grid=(nrow,),
        in_specs=[pl.BlockSpec((ROW_TILE, D), lambda i: (i, 0)),
                  _resident((1, D)),
                  _resident((D, ncols)),
                  _resident((1, kb_cols))],
        out_specs=[pl.BlockSpec((ROW_TILE, 512), lambda i: (i, 0)),
                   pl.BlockSpec((ROW_TILE, 256), lambda i: (i, 0)),
                   pl.BlockSpec((ROW_TILE, 256), lambda i: (i, 0)),
                   pl.BlockSpec((ROW_TILE, 512), lambda i: (i, 0)),
                   pl.BlockSpec((ROW_TILE, kb_cols), lambda i: (i, 0)),
                   pl.BlockSpec((B_HEADS, 1, B_ACC_ROWS, ROW_TILE), lambda i: (0, i, 0, 0))],
        out_shape=[jax.ShapeDtypeStruct((S, 512), bf16),
                   jax.ShapeDtypeStruct((S, 256), bf16),
                   jax.ShapeDtypeStruct((S, 256), bf16),
                   jax.ShapeDtypeStruct((S, 512), bf16),
                   jax.ShapeDtypeStruct((S, kb_cols), bf16),
                   jax.ShapeDtypeStruct((B_HEADS, nchunks, B_ACC_ROWS, B_TILE), bf16)],
        compiler_params=pltpu.CompilerParams(dimension_semantics=("arbitrary",),
                                             vmem_limit_bytes=VMEM_LIMIT),
        name="in_proj",
    )(h, p["attn_pre_g"].reshape(1, D), w_cat, kones)

    nblocks = S // A_BLOCK
    pad = ((A_BLOCK, A_BLOCK), (0, 0))
    kp = jnp.pad(ka, pad)
    vp = jnp.pad(va, pad)
    ya = pl.pallas_call(
        functools.partial(_win_attn_kernel, nblocks=nblocks),
        grid=(nblocks,),
        in_specs=[pl.BlockSpec(memory_space=pltpu.SMEM),
                  pl.BlockSpec((A_BLOCK, 512), lambda n: (n, 0)),
                  _resident(kp.shape),
                  _resident(vp.shape),
                  _resident(bias_a.shape)],
        out_specs=pl.BlockSpec((A_BLOCK, 512), lambda n: (n, 0)),
        out_shape=jax.ShapeDtypeStruct((S, 512), bf16),
        compiler_params=pltpu.CompilerParams(dimension_semantics=("arbitrary",),
                                             vmem_limit_bytes=VMEM_LIMIT),
        name="win_attn",
    )(p["a_sink"], qa, kp, vp, bias_a)

    lamv = jnp.stack([p["lambda_q1"], p["lambda_k1"], p["lambda_q2"], p["lambda_k2"]])
    T = B_TILE
    yb = pl.pallas_call(
        functools.partial(_diff_attn_kernel, nchunks=nchunks, lambda_init=lambda_init),
        grid=(B_HEADS, nchunks),
        in_specs=[pl.BlockSpec((4, HEAD_DIM), lambda hh, i: (0, 0)),
                  pl.BlockSpec((1, B_V_DIM), lambda hh, i: (0, 0)),
                  pl.BlockSpec((T, LANES), lambda hh, i: (i, hh)),
                  pl.BlockSpec((S, 2 * LANES), lambda hh, i: (0, hh)),
                  pl.BlockSpec((1, nchunks, B_ACC_ROWS, T), lambda hh, i: (hh, 0, 0, 0)),
                  pl.BlockSpec((1, 5, T, T), lambda hh, i: (hh, 0, 0, 0))],
        out_specs=pl.BlockSpec((T, LANES), lambda hh, i: (i, hh)),
        out_shape=jax.ShapeDtypeStruct((S, B_HEADS * B_V_DIM), bf16),
        scratch_shapes=[pltpu.VMEM((2, LANES, T), bf16),
                        pltpu.VMEM((2, LANES, T), bf16),
                        pltpu.VMEM((B_SLOTS, 2, T, T), bf16),
                        pltpu.VMEM((2, 1, T), jnp.float32),
                        pltpu.VMEM((nchunks, 2, 1, T), jnp.float32),
                        pltpu.VMEM((2, 1, T), jnp.float32),
                        pltpu.VMEM((2, B_ACC_ROWS, T), jnp.float32)],
        compiler_params=pltpu.CompilerParams(dimension_semantics=("arbitrary", "arbitrary"),
                                             vmem_limit_bytes=VMEM_LIMIT),
        name="diff_attn",
    )(lamv, p["diff_subln_g"].reshape(1, B_V_DIM), qb, kb, vt, bias_b)

    w_out = p["w_out"].astype(bf16)
    a_width = A_Q_HEADS * HEAD_DIM
    d_ff = p["w_gate"].shape[1]
    R = FFN_ROW_TILE
    out = pl.pallas_call(
        _out_ffn_kernel,
        grid=(S // R,),
        in_specs=[pl.BlockSpec((R, 512), lambda i: (i, 0)),
                  pl.BlockSpec((R, 512), lambda i: (i, 0)),
                  pl.BlockSpec((R, D), lambda i: (i, 0)),
                  _resident((a_width, D)),
                  _resident((w_out.shape[0] - a_width, D)),
                  _resident((1, D)),
                  _resident((1, D)),
                  _resident((D, d_ff)),
                  _resident((D, d_ff)),
                  _resident((d_ff, D)),
                  _resident((1, D))],
        out_specs=pl.BlockSpec((R, D), lambda i: (i, 0)),
        out_shape=jax.ShapeDtypeStruct((S, D), jnp.float32),
        compiler_params=pltpu.CompilerParams(dimension_semantics=("arbitrary",),
                                             vmem_limit_bytes=VMEM_LIMIT),
        name="out_ffn",
    )(ya, yb, h, w_out[:a_width], w_out[a_width:], p["attn_post_g"].reshape(1, D),
      p["ffn_pre_g"].reshape(1, D), p["w_gate"].astype(bf16), p["w_up"].astype(bf16),
      p["w_down"].astype(bf16), p["ffn_post_g"].reshape(1, D))
    return out


def kernel(x, attn_pre_g, w_in, a_sink, lambda_q1, lambda_k1, lambda_q2, lambda_k2, diff_subln_g,
           rel_bias, w_out, attn_post_g, ffn_pre_g, w_gate, w_up, w_down, ffn_post_g):
    batch, S, D = x.shape
    depth = w_in.shape[0]
    assert S % B_TILE == 0 and S % A_BLOCK == 0 and S // A_BLOCK >= 2

    tab_a = _toeplitz_bias(rel_bias[:, :A_Q_HEADS], A_BLOCK, 3 * A_BLOCK, lambda t: -t - A_BLOCK)
    qi = np.arange(A_BLOCK)[:, None]
    kj = np.arange(3 * A_BLOCK)[None, :]
    in_window = np.abs(kj - A_BLOCK - qi) <= A_BLOCK
    valid = np.stack([in_window & (kj >= A_BLOCK), in_window, in_window & (kj < 2 * A_BLOCK)])
    bias_a = jnp.where(valid[:, None], tab_a[None], MASK_VALUE)

    T = B_TILE
    assert T >= MAX_DISTANCE
    tab_b = rel_bias[:, A_Q_HEADS:]
    near = [_toeplitz_bias(tab_b, T, T, lambda t, d=d: d * T + t) for d in (-1, 0, 1)]
    far = tab_b[_t5_bucket_np(np.array([-2 * T, 2 * T]))].astype(jnp.float32)
    const = [jnp.broadcast_to(far[side][:, None, None], (B_HEADS, T, T)) for side in (0, 1)]
    bias_b = jnp.stack([const[0]] + near + [const[1]], axis=1) * LOG2E

    outs = []
    for b in range(batch):
        h = x[b]
        for l in range(depth):
            p = dict(attn_pre_g=attn_pre_g[l], w_in=w_in[l], a_sink=a_sink[l], lambda_q1=lambda_q1[l],
                     lambda_k1=lambda_k1[l], lambda_q2=lambda_q2[l], lambda_k2=lambda_k2[l],
                     diff_subln_g=diff_subln_g[l], w_out=w_out[l], attn_post_g=attn_post_g[l],
                     ffn_pre_g=ffn_pre_g[l], w_gate=w_gate[l], w_up=w_up[l], w_down=w_down[l],
                     ffn_post_g=ffn_post_g[l])
            h = _layer(h, l, p, bias_a, bias_b)
        outs.append(h)
    return jnp.stack(outs)
```

```python
import functools
import math

import numpy as np
import jax
import jax.numpy as jnp
from jax import lax
from jax.experimental import pallas as pl
from jax.experimental.pallas import tpu as pltpu

HEAD_DIM = 64
A_Q_HEADS = 8
A_KV_HEADS = 2
A_BLOCK = 128
B_HEADS = 4
B_V_DIM = 2 * HEAD_DIM
NUM_BUCKETS = 32
MAX_DISTANCE = 128
EPS = 1e-6
MASK_VALUE = -1e30
LOG2E = math.log2(math.e)
ROW_SUM_LIMIT = 2.0 ** 40

LANES = 128
BF16_SUBLANES = 16
ROW_TILE = 512
FFN_ROW_TILE = 256
B_TILE = 512
B_ACC_ROWS = B_V_DIM + BF16_SUBLANES
B_REF_LANES = 3
B_UNROLL = 8
B_AHEAD = 2
B_SLOTS = 4
B_PROBE = 128
B_BIAS_BLOCK = 128
VMEM_LIMIT = 56 * 1024 * 1024

_NT = (((1,), (1,)), ((), ()))


def _t5_bucket_np(rel):
    nb = NUM_BUCKETS // 2
    max_exact = nb // 2
    ret = np.where(rel > 0, nb, 0)
    n = np.abs(rel)
    nf = np.maximum(n, 1).astype(np.float32)
    large = max_exact + (np.log(nf / np.float32(max_exact)) / np.float32(math.log(MAX_DISTANCE / max_exact))
                         * np.float32(nb - max_exact)).astype(np.int32)
    large = np.minimum(large, nb - 1)
    return (ret + np.where(n < max_exact, n, large)).astype(np.int32)


def _toeplitz_bias(table, rows, cols, rel_of):
    length = rows + cols
    u = np.arange(length)
    t = np.where(u < cols, -u, length - u)
    w = table[_t5_bucket_np(rel_of(t))].astype(jnp.float32).T
    x = jnp.tile(w, (1, rows))[:, :rows * (length - 1)].reshape(w.shape[0], rows, length - 1)
    return x[:, :, :cols]


def _rms(xf, g):
    return xf * lax.rsqrt(jnp.mean(xf * xf, axis=-1, keepdims=True) + EPS) * g


def _in_proj_kernel(x_ref, g_ref, w_ref, kones_ref, qa_ref, ka_ref, va_ref, qb_ref, kb_ref, vt_ref):
    u = _rms(x_ref[...], g_ref[...]).astype(jnp.bfloat16)
    proj = jnp.dot(u, w_ref[...], preferred_element_type=jnp.float32)
    qa_ref[...] = proj[:, 0:512].astype(jnp.bfloat16)
    ka_ref[...] = proj[:, 512:768].astype(jnp.bfloat16)
    va_ref[...] = proj[:, 768:1024].astype(jnp.bfloat16)
    qb_ref[...] = proj[:, 1024:1536].astype(jnp.bfloat16)
    kb_ref[...] = (proj[:, 1536:2560] + kones_ref[...]).astype(jnp.bfloat16)
    rows = proj.shape[0]
    for h in range(B_HEADS):
        v = proj[:, 2560 + h * B_V_DIM:2560 + (h + 1) * B_V_DIM]
        vt_ref[h, 0, 0:B_V_DIM, :] = v.T.astype(jnp.bfloat16)
        vt_ref[h, 0, B_V_DIM:B_ACC_ROWS, :] = jnp.ones((BF16_SUBLANES, rows), jnp.bfloat16)


def _win_attn_kernel(sink_ref, q_ref, k_ref, v_ref, bias_ref, o_ref, *, nblocks):
    n = pl.program_id(0)
    variant = jnp.where(n == 0, 0, jnp.where(n == nblocks - 1, 2, 1))
    start = pl.multiple_of(n * A_BLOCK, A_BLOCK)
    kw = k_ref[pl.ds(start, 3 * A_BLOCK), :]
    vw = v_ref[pl.ds(start, 3 * A_BLOCK), :]
    q = q_ref[...]
    lane = lax.broadcasted_iota(jnp.int32, (A_BLOCK, LANES), 1)
    low = lane < HEAD_DIM
    for pair in range(A_Q_HEADS // 2):
        q2 = q[:, pair * LANES:(pair + 1) * LANES]
        kvh = pair // 2
        kg = kw[:, kvh * LANES:(kvh + 1) * LANES]
        vg = vw[:, kvh * LANES:(kvh + 1) * LANES]
        res = []
        for e in range(2):
            hq = 2 * pair + e
            qm = jnp.where(low if e == 0 else jnp.logical_not(low), q2, jnp.zeros_like(q2))
            s = lax.dot_general(qm, kg, _NT, preferred_element_type=jnp.float32)
            s = s + bias_ref[variant, hq]
            snk = sink_ref[hq]
            m = jnp.maximum(jnp.max(s, axis=-1, keepdims=True), snk)
            p = jnp.exp(s - m)
            den = jnp.sum(p, axis=-1, keepdims=True) + jnp.exp(snk - m)
            o = jnp.dot(p.astype(jnp.bfloat16), vg, preferred_element_type=jnp.float32)
            res.append(o / den)
        o_ref[:, pair * LANES:(pair + 1) * LANES] = jnp.where(low, res[0], res[1]).astype(o_ref.dtype)


def _diff_attn_kernel(lamv_ref, g_ref, q_ref, k_ref, vt_ref, bias_ref, o_ref,
                      qz_ref, qp_ref, p_ref, r_ref, flag_ref, any_ref, acc_ref, *, nchunks, lambda_init):
    T = B_TILE
    i = pl.program_id(1)
    f32 = jnp.float32
    bf16 = jnp.bfloat16

    qt = q_ref[...].astype(f32).T
    row = lax.broadcasted_iota(jnp.int32, (LANES, T), 0)
    is_q = (row < HEAD_DIM, row >= HEAD_DIM)
    ref_row = (HEAD_DIM, 0)

    nsub = T // B_BIAS_BLOCK

    def bias_tile(j, rows=nsub):
        by_dist = {d: bias_ref[0, jnp.clip((j - i) * nsub + d, -2, 2) + 2] for d in range(1 - nsub, rows)}
        return jnp.concatenate(
            [jnp.concatenate([by_dist[a - b] for b in range(nsub)], axis=1) for a in range(rows)], axis=0)

    def k_chunk(j, c):
        return k_ref[pl.ds(pl.multiple_of(j * T, T), T), c * LANES:(c + 1) * LANES]

    def set_reference(c, r_row):
        hi = r_row.astype(bf16).astype(f32)
        rem = r_row - hi
        mid = rem.astype(bf16).astype(f32)
        low = (rem - mid).astype(bf16).astype(f32)
        first = ref_row[c]
        ext = jnp.where(row == first, -hi, jnp.where(row == first + 1, -mid,
                                                     jnp.where(row == first + 2, -low, 0.0)))
        qp_ref[c] = jnp.where(is_q[c], qt, ext).astype(bf16)

    def exact_chunk(j, c, flagged):
        s = jnp.dot(k_chunk(j, c), qz_ref[c], preferred_element_type=f32) + bias_tile(j)
        r_old = r_ref[c]
        r_new = jnp.where(flagged, jnp.maximum(r_old, jnp.max(s, axis=0, keepdims=True)), r_old)
        p = jnp.where(flagged, jnp.exp2(s - r_new), 0.0)
        pv = jnp.dot(vt_ref[0, j], p.astype(bf16), preferred_element_type=f32)
        acc_ref[c] = acc_ref[c] * jnp.exp2(r_old - r_new) + pv
        r_ref[c] = r_new

    def stage_a(j, slot):
        bias = bias_tile(j)
        for c in range(2):
            s = jnp.dot(k_chunk(j, c), qp_ref[c], preferred_element_type=f32) + bias
            p_ref[slot, c] = jnp.exp2(s).astype(bf16)

    def stage_c(j, slot):
        vt = vt_ref[0, j]
        for c in range(2):
            pv = jnp.dot(vt, p_ref[slot, c], preferred_element_type=f32)
            ok = pv[B_V_DIM:B_V_DIM + 1] <= ROW_SUM_LIMIT
            acc_ref[c] = acc_ref[c] + jnp.where(ok, pv, 0.0)
            flag = jnp.where(ok, 0.0, 1.0)
            flag_ref[j, c] = flag
            any_ref[c] = jnp.maximum(any_ref[c], flag)

    for c in range(2):
        qz_ref[c] = jnp.where(is_q[c], qt, 0.0).astype(bf16)
    acc_ref[...] = jnp.zeros(acc_ref.shape, f32)
    any_ref[...] = jnp.zeros(any_ref.shape, f32)
    probe_bias = bias_tile(i, rows=B_PROBE // B_BIAS_BLOCK)
    for c in range(2):
        kp = k_ref[pl.ds(pl.multiple_of(i * T, T), B_PROBE), c * LANES:(c + 1) * LANES]
        s = jnp.dot(kp, qz_ref[c], preferred_element_type=f32) + probe_bias
        r = jnp.max(s, axis=0, keepdims=True)
        r_ref[c] = r
        set_reference(c, r)

    def run(j0, has_next):
        for u in range(B_UNROLL):
            stage_c(j0 + u, u % B_SLOTS)
            if u + B_AHEAD < B_UNROLL or has_next:
                stage_a(j0 + u + B_AHEAD, (u + B_AHEAD) % B_SLOTS)

    nbody = nchunks // B_UNROLL
    for u in range(B_AHEAD):
        stage_a(u, u)

    def body(b, carry):
        run(b * B_UNROLL, True)
        return carry

    lax.fori_loop(0, nbody - 1, body, 0)
    run((nbody - 1) * B_UNROLL, False)

    @pl.when(jnp.max(jnp.maximum(any_ref[0], any_ref[1])) > 0.0)
    def _():
        def redo(j, carry):
            @pl.when(jnp.max(jnp.maximum(flag_ref[j, 0], flag_ref[j, 1])) > 0.0)
            def _():
                for c in range(2):
                    exact_chunk(j, c, flag_ref[j, c] > 0.0)
            return carry

        lax.fori_loop(0, nchunks, redo, 0)

    lv = lamv_ref[...]
    lam = (jnp.exp(jnp.sum(lv[0:1] * lv[1:2], axis=-1, keepdims=True))
           - jnp.exp(jnp.sum(lv[2:3] * lv[3:4], axis=-1, keepdims=True)) + lambda_init)
    a1 = acc_ref[0]
    a2 = acc_ref[1]
    ot = (a1[0:B_V_DIM] / a1[B_V_DIM:B_V_DIM + 1]
          - lam * (a2[0:B_V_DIM] / a2[B_V_DIM:B_V_DIM + 1]))
    o = ot.T
    o_ref[...] = (_rms(o, g_ref[...]) * (1.0 - lambda_init)).astype(o_ref.dtype)


def _out_ffn_kernel(ya_ref, yb_ref, x_ref, woa_ref, wob_ref, gpost_ref, gpre_ref,
                    wg_ref, wu_ref, wd_ref, gfpost_ref, o_ref):
    y = (jnp.dot(ya_ref[...], woa_ref[...], preferred_element_type=jnp.float32)
         + jnp.dot(yb_ref[...], wob_ref[...], preferred_element_type=jnp.float32))
    h1 = x_ref[...] + _rms(y, gpost_ref[...])
    u = _rms(h1, gpre_ref[...]).astype(jnp.bfloat16)
    gate = jnp.dot(u, wg_ref[...], preferred_element_type=jnp.float32)
    up = jnp.dot(u, wu_ref[...], preferred_element_type=jnp.float32)
    act = (gate * jax.nn.sigmoid(gate) * up).astype(jnp.bfloat16)
    f = jnp.dot(act, wd_ref[...], preferred_element_type=jnp.float32)
    o_ref[...] = h1 + _rms(f, gfpost_ref[...])


def _resident(shape):
    zeros = (0,) * len(shape)
    return pl.BlockSpec(shape, lambda *_: zeros, pipeline_mode=pl.Buffered(1))


def _layer(h, l, p, bias_a, bias_b):
    S, D = h.shape
    bf16 = jnp.bfloat16
    lambda_init = 0.8 - 0.6 * math.exp(-0.3 * l)
    scale = HEAD_DIM ** -0.5

    w = p["w_in"]
    a_q, a_kv, b_qk = A_Q_HEADS * HEAD_DIM, A_KV_HEADS * HEAD_DIM, B_HEADS * 2 * HEAD_DIM
    c0 = a_q
    c1 = c0 + a_kv
    c2 = c1 + a_kv
    c3 = c2 + b_qk
    c4 = c3 + b_qk

    def dup(cols):
        parts = []
        for g in range(A_KV_HEADS):
            blk = cols[:, g * HEAD_DIM:(g + 1) * HEAD_DIM]
            parts += [blk, blk]
        return jnp.concatenate(parts, axis=1)

    def widen_b_keys(cols):
        wk = cols.reshape(D, B_HEADS, 2, HEAD_DIM)
        z = jnp.zeros_like(wk[:, :, 0])
        return jnp.stack([jnp.concatenate([wk[:, :, 0], z], axis=-1),
                          jnp.concatenate([z, wk[:, :, 1]], axis=-1)], axis=2).reshape(D, B_HEADS * 2 * LANES)

    kones = np.zeros((B_HEADS, 2, LANES), np.float32)
    kones[:, 0, HEAD_DIM:HEAD_DIM + B_REF_LANES] = 1.0
    kones[:, 1, 0:B_REF_LANES] = 1.0
    kones = jnp.asarray(kones.reshape(1, B_HEADS * 2 * LANES))

    w_cat = jnp.concatenate([w[:, :c0] * scale, dup(w[:, c0:c1]), dup(w[:, c1:c2]),
                             w[:, c2:c3] * (scale * LOG2E), widen_b_keys(w[:, c3:c4]), w[:, c4:]],
                            axis=1).astype(bf16)
    ncols = w_cat.shape[1]
    nrow = S // ROW_TILE
    nchunks = S // B_TILE
    assert ROW_TILE == B_TILE and nchunks % B_UNROLL == 0
    assert B_UNROLL % B_SLOTS == 0 and B_AHEAD < B_SLOTS and B_AHEAD <= B_UNROLL
    kb_cols = B_HEADS * 2 * LANES

    qa, ka, va, qb, kb, vt = pl.pallas_call(
        _in_proj_kernel,
        grid=(nrow,),
        in_specs=[pl.BlockSpec((ROW_TILE, D), lambda i: (i, 0)),
                  _resident((1, D)),
                  _resident((D, ncols)),
                  _resident((1, kb_cols))],
        out_specs=[pl.BlockSpec((ROW_TILE, 512), lambda i: (i, 0)),
                   pl.BlockSpec((ROW_TILE, 256), lambda i: (i, 0)),
                   pl.BlockSpec((ROW_TILE, 256), lambda i: (i, 0)),
                   pl.BlockSpec((ROW_TILE, 512), lambda i: (i, 0)),
                   pl.BlockSpec((ROW_TILE, kb_cols), lambda i: (i, 0)),
                   pl.BlockSpec((B_HEADS, 1, B_ACC_ROWS, ROW_TILE), lambda i: (0, i, 0, 0))],
        out_shape=[jax.ShapeDtypeStruct((S, 512), bf16),
                   jax.ShapeDtypeStruct((S, 256), bf16),
                   jax.ShapeDtypeStruct((S, 256), bf16),
                   jax.ShapeDtypeStruct((S, 512), bf16),
                   jax.ShapeDtypeStruct((S, kb_cols), bf16),
                   jax.ShapeDtypeStruct((B_HEADS, nchunks, B_ACC_ROWS, B_TILE), bf16)],
        compiler_params=pltpu.CompilerParams(dimension_semantics=("arbitrary",),
                                             vmem_limit_bytes=VMEM_LIMIT),
        name="in_proj",
    )(h, p["attn_pre_g"].reshape(1, D), w_cat, kones)

    nblocks = S // A_BLOCK
    pad = ((A_BLOCK, A_BLOCK), (0, 0))
    kp = jnp.pad(ka, pad)
    vp = jnp.pad(va, pad)
    ya = pl.pallas_call(
        functools.partial(_win_attn_kernel, nblocks=nblocks),
        grid=(nblocks,),
        in_specs=[pl.BlockSpec(memory_space=pltpu.SMEM),
                  pl.BlockSpec((A_BLOCK, 512), lambda n: (n, 0)),
                  _resident(kp.shape),
                  _resident(vp.shape),
                  _resident(bias_a.shape)],
        out_specs=pl.BlockSpec((A_BLOCK, 512), lambda n: (n, 0)),
        out_shape=jax.ShapeDtypeStruct((S, 512), bf16),
        compiler_params=pltpu.CompilerParams(dimension_semantics=("arbitrary",),
                                             vmem_limit_bytes=VMEM_LIMIT),
        name="win_attn",
    )(p["a_sink"], qa, kp, vp, bias_a)

    lamv = jnp.stack([p["lambda_q1"], p["lambda_k1"], p["lambda_q2"], p["lambda_k2"]])
    T = B_TILE
    yb = pl.pallas_call(
        functools.partial(_diff_attn_kernel, nchunks=nchunks, lambda_init=lambda_init),
        grid=(B_HEADS, nchunks),
        in_specs=[pl.BlockSpec((4, HEAD_DIM), lambda hh, i: (0, 0)),
                  pl.BlockSpec((1, B_V_DIM), lambda hh, i: (0, 0)),
                  pl.BlockSpec((T, LANES), lambda hh, i: (i, hh)),
                  pl.BlockSpec((S, 2 * LANES), lambda hh, i: (0, hh)),
                  pl.BlockSpec((1, nchunks, B_ACC_ROWS, T), lambda hh, i: (hh, 0, 0, 0)),
                  pl.BlockSpec((1, 5, B_BIAS_BLOCK, B_BIAS_BLOCK), lambda hh, i: (hh, 0, 0, 0))],
        out_specs=pl.BlockSpec((T, LANES), lambda hh, i: (i, hh)),
        out_shape=jax.ShapeDtypeStruct((S, B_HEADS * B_V_DIM), bf16),
        scratch_shapes=[pltpu.VMEM((2, LANES, T), bf16),
                        pltpu.VMEM((2, LANES, T), bf16),
                        pltpu.VMEM((B_SLOTS, 2, T, T), bf16),
                        pltpu.VMEM((2, 1, T), jnp.float32),
                        pltpu.VMEM((nchunks, 2, 1, T), jnp.float32),
                        pltpu.VMEM((2, 1, T), jnp.float32),
                        pltpu.VMEM((2, B_ACC_ROWS, T), jnp.float32)],
        compiler_params=pltpu.CompilerParams(dimension_semantics=("arbitrary", "arbitrary"),
                                             vmem_limit_bytes=VMEM_LIMIT),
        name="diff_attn",
    )(lamv, p["diff_subln_g"].reshape(1, B_V_DIM), qb, kb, vt, bias_b)

    w_out = p["w_out"].astype(bf16)
    a_width = A_Q_HEADS * HEAD_DIM
    d_ff = p["w_gate"].shape[1]
    R = FFN_ROW_TILE
    out = pl.pallas_call(
        _out_ffn_kernel,
        grid=(S // R,),
        in_specs=[pl.BlockSpec((R, 512), lambda i: (i, 0)),
                  pl.BlockSpec((R, 512), lambda i: (i, 0)),
                  pl.BlockSpec((R, D), lambda i: (i, 0)),
                  _resident((a_width, D)),
                  _resident((w_out.shape[0] - a_width, D)),
                  _resident((1, D)),
                  _resident((1, D)),
                  _resident((D, d_ff)),
                  _resident((D, d_ff)),
                  _resident((d_ff, D)),
                  _resident((1, D))],
        out_specs=pl.BlockSpec((R, D), lambda i: (i, 0)),
        out_shape=jax.ShapeDtypeStruct((S, D), jnp.float32),
        compiler_params=pltpu.CompilerParams(dimension_semantics=("arbitrary",),
                                             vmem_limit_bytes=VMEM_LIMIT),
        name="out_ffn",
    )(ya, yb, h, w_out[:a_width], w_out[a_width:], p["attn_post_g"].reshape(1, D),
      p["ffn_pre_g"].reshape(1, D), p["w_gate"].astype(bf16), p["w_up"].astype(bf16),
      p["w_down"].astype(bf16), p["ffn_post_g"].reshape(1, D))
    return out


def kernel(x, attn_pre_g, w_in, a_sink, lambda_q1, lambda_k1, lambda_q2, lambda_k2, diff_subln_g,
           rel_bias, w_out, attn_post_g, ffn_pre_g, w_gate, w_up, w_down, ffn_post_g):
    batch, S, D = x.shape
    depth = w_in.shape[0]
    assert S % B_TILE == 0 and S % A_BLOCK == 0 and S // A_BLOCK >= 2

    tab_a = _toeplitz_bias(rel_bias[:, :A_Q_HEADS], A_BLOCK, 3 * A_BLOCK, lambda t: -t - A_BLOCK)
    qi = np.arange(A_BLOCK)[:, None]
    kj = np.arange(3 * A_BLOCK)[None, :]
    in_window = np.abs(kj - A_BLOCK - qi) <= A_BLOCK
    valid = np.stack([in_window & (kj >= A_BLOCK), in_window, in_window & (kj < 2 * A_BLOCK)])
    bias_a = jnp.where(valid[:, None], tab_a[None], MASK_VALUE)

    T = B_TILE
    assert T >= MAX_DISTANCE
    tab_b = rel_bias[:, A_Q_HEADS:]
    nb = B_BIAS_BLOCK
    near = [_toeplitz_bias(tab_b, nb, nb, lambda t, d=d: d * nb + t) for d in (-1, 0, 1)]
    far = tab_b[_t5_bucket_np(np.array([-2 * nb, 2 * nb]))].astype(jnp.float32)
    const = [jnp.broadcast_to(far[side][:, None, None], (B_HEADS, nb, nb)) for side in (0, 1)]
    bias_b = jnp.stack([const[0]] + near + [const[1]], axis=1) * LOG2E

    outs = []
    for b in range(batch):
        h = x[b]
        for l in range(depth):
            p = dict(attn_pre_g=attn_pre_g[l], w_in=w_in[l], a_sink=a_sink[l], lambda_q1=lambda_q1[l],
                     lambda_k1=lambda_k1[l], lambda_q2=lambda_q2[l], lambda_k2=lambda_k2[l],
                     diff_subln_g=diff_subln_g[l], w_out=w_out[l], attn_post_g=attn_post_g[l],
                     ffn_pre_g=ffn_pre_g[l], w_gate=w_gate[l], w_up=w_up[l], w_down=w_down[l],
                     ffn_post_g=ffn_post_g[l])
            h = _layer(h, l, p, bias_a, bias_b)
        outs.append(h)
    return jnp.stack(outs)
```

```python
import functools
import math

import numpy as np
import jax
import jax.numpy as jnp
from jax import lax
from jax.experimental import pallas as pl
from jax.experimental.pallas import tpu as pltpu

HEAD_DIM = 64
A_Q_HEADS = 8
A_KV_HEADS = 2
A_BLOCK = 128
A_STEP_BLOCKS = 4
B_HEADS = 4
B_V_DIM = 2 * HEAD_DIM
NUM_BUCKETS = 32
MAX_DISTANCE = 128
EPS = 1e-6
MASK_VALUE = -1e30
LOG2E = math.log2(math.e)
ROW_SUM_LIMIT = 2.0 ** 40

LANES = 128
BF16_SUBLANES = 16
ROW_TILE = 512
FFN_ROW_TILE = 256
B_TILE = 512
B_ACC_ROWS = B_V_DIM + BF16_SUBLANES
B_REF_LANES = 3
B_UNROLL = 8
B_AHEAD = 2
B_SLOTS = 4
B_PROBE = 128
B_BIAS_BLOCK = 128
VMEM_LIMIT = 56 * 1024 * 1024

_NT = (((1,), (1,)), ((), ()))


def _t5_bucket_np(rel):
    nb = NUM_BUCKETS // 2
    max_exact = nb // 2
    ret = np.where(rel > 0, nb, 0)
    n = np.abs(rel)
    nf = np.maximum(n, 1).astype(np.float32)
    large = max_exact + (np.log(nf / np.float32(max_exact)) / np.float32(math.log(MAX_DISTANCE / max_exact))
                         * np.float32(nb - max_exact)).astype(np.int32)
    large = np.minimum(large, nb - 1)
    return (ret + np.where(n < max_exact, n, large)).astype(np.int32)


def _toeplitz_bias(table, rows, cols, rel_of):
    length = rows + cols
    u = np.arange(length)
    t = np.where(u < cols, -u, length - u)
    w = table[_t5_bucket_np(rel_of(t))].astype(jnp.float32).T
    x = jnp.tile(w, (1, rows))[:, :rows * (length - 1)].reshape(w.shape[0], rows, length - 1)
    return x[:, :, :cols]


def _rms(xf, g):
    return xf * lax.rsqrt(jnp.mean(xf * xf, axis=-1, keepdims=True) + EPS) * g


def _in_proj_kernel(x_ref, g_ref, w_ref, kones_ref, qa_ref, ka_ref, va_ref, qb_ref, kb_ref, vt_ref):
    u = _rms(x_ref[...], g_ref[...]).astype(jnp.bfloat16)
    proj = jnp.dot(u, w_ref[...], preferred_element_type=jnp.float32)
    qa_ref[...] = proj[:, 0:512].astype(jnp.bfloat16)
    ka_ref[...] = proj[:, 512:768].astype(jnp.bfloat16)
    va_ref[...] = proj[:, 768:1024].astype(jnp.bfloat16)
    qb_ref[...] = proj[:, 1024:1536].astype(jnp.bfloat16)
    kb_ref[...] = (proj[:, 1536:2560] + kones_ref[...]).astype(jnp.bfloat16)
    rows = proj.shape[0]
    for h in range(B_HEADS):
        v = proj[:, 2560 + h * B_V_DIM:2560 + (h + 1) * B_V_DIM]
        vt_ref[h, 0, 0:B_V_DIM, :] = v.T.astype(jnp.bfloat16)
        vt_ref[h, 0, B_V_DIM:B_ACC_ROWS, :] = jnp.ones((BF16_SUBLANES, rows), jnp.bfloat16)


def _win_attn_kernel(sink_ref, q_ref, k_ref, v_ref, bias_ref, o_ref, *, nblocks):
    f32 = jnp.float32
    bf16 = jnp.bfloat16
    n = pl.program_id(0)
    rows = A_STEP_BLOCKS * A_BLOCK
    seq = k_ref.shape[0]
    group = A_Q_HEADS // A_KV_HEADS
    start = pl.multiple_of(n * rows, rows)
    prev = pl.multiple_of(jnp.maximum(start - A_BLOCK, 0), A_BLOCK)
    nxt = pl.multiple_of(jnp.minimum(start + rows, seq - A_BLOCK), A_BLOCK)
    kw = jnp.concatenate([k_ref[pl.ds(prev, A_BLOCK), :], k_ref[pl.ds(start, rows), :],
                          k_ref[pl.ds(nxt, A_BLOCK), :]], axis=0)
    vw = jnp.concatenate([v_ref[pl.ds(prev, A_BLOCK), :], v_ref[pl.ds(start, rows), :],
                          v_ref[pl.ds(nxt, A_BLOCK), :]], axis=0)
    lane = lax.broadcasted_iota(jnp.int32, (A_BLOCK, LANES), 1)
    low = lane < HEAD_DIM
    ones = jnp.ones((3 * A_BLOCK, LANES), bf16)
    for b in range(A_STEP_BLOCKS):
        blk = n * A_STEP_BLOCKS + b
        if b == 0:
            variant = jnp.where(blk == 0, 0, 1)
        elif b == A_STEP_BLOCKS - 1:
            variant = jnp.where(blk == nblocks - 1, 2, 1)
        else:
            variant = 1
        q = q_ref[b * A_BLOCK:(b + 1) * A_BLOCK, :]
        for g in range(A_KV_HEADS):
            heads = range(g * group, (g + 1) * group)
            qs = []
            for h in heads:
                q2 = q[:, (h // 2) * LANES:(h // 2 + 1) * LANES]
                qs.append(jnp.where(low if h % 2 == 0 else jnp.logical_not(low), q2, jnp.zeros_like(q2)))
            q4 = jnp.concatenate(qs, axis=0)
            kg = kw[b * A_BLOCK:(b + 3) * A_BLOCK, g * LANES:(g + 1) * LANES]
            vg = jnp.concatenate([vw[b * A_BLOCK:(b + 3) * A_BLOCK, g * LANES:(g + 1) * LANES], ones],
                                 axis=1)
            s = lax.dot_general(q4, kg, _NT, preferred_element_type=f32)
            s = s + bias_ref[variant, g * group:(g + 1) * group].reshape(group * A_BLOCK, 3 * A_BLOCK)
            snk = jnp.concatenate([jnp.full((A_BLOCK, 1), sink_ref[h], f32) for h in heads], axis=0)
            m = jnp.maximum(jnp.max(s, axis=-1, keepdims=True), snk)
            p = jnp.exp(s - m).astype(bf16)
            res = jnp.dot(p, vg, preferred_element_type=f32)
            on = res[:, 0:LANES] / (res[:, LANES:2 * LANES] + jnp.exp(snk - m))
            for t in range(group // 2):
                even = on[(2 * t) * A_BLOCK:(2 * t + 1) * A_BLOCK]
                odd = on[(2 * t + 1) * A_BLOCK:(2 * t + 2) * A_BLOCK]
                pair = (g * group) // 2 + t
                o_ref[b * A_BLOCK:(b + 1) * A_BLOCK, pair * LANES:(pair + 1) * LANES] = (
                    jnp.where(low, even, odd).astype(o_ref.dtype))


def _diff_attn_kernel(lamv_ref, g_ref, q_ref, k_ref, vt_ref, bias_ref, o_ref,
                      qz_ref, qp_ref, p_ref, r_ref, flag_ref, any_ref, acc_ref, *, nchunks, lambda_init):
    T = B_TILE
    i = pl.program_id(1)
    f32 = jnp.float32
    bf16 = jnp.bfloat16

    qt = q_ref[...].astype(f32).T
    row = lax.broadcasted_iota(jnp.int32, (LANES, T), 0)
    is_q = (row < HEAD_DIM, row >= HEAD_DIM)
    ref_row = (HEAD_DIM, 0)

    nsub = T // B_BIAS_BLOCK

    def bias_tile(j, rows=nsub):
        by_dist = {d: bias_ref[0, jnp.clip((j - i) * nsub + d, -2, 2) + 2] for d in range(1 - nsub, rows)}
        return jnp.concatenate(
            [jnp.concatenate([by_dist[a - b] for b in range(nsub)], axis=1) for a in range(rows)], axis=0)

    def k_chunk(j, c):
        return k_ref[pl.ds(pl.multiple_of(j * T, T), T), c * LANES:(c + 1) * LANES]

    def set_reference(c, r_row):
        hi = r_row.astype(bf16).astype(f32)
        rem = r_row - hi
        mid = rem.astype(bf16).astype(f32)
        low = (rem - mid).astype(bf16).astype(f32)
        first = ref_row[c]
        ext = jnp.where(row == first, -hi, jnp.where(row == first + 1, -mid,
                                                     jnp.where(row == first + 2, -low, 0.0)))
        qp_ref[c] = jnp.where(is_q[c], qt, ext).astype(bf16)

    def exact_chunk(j, c, flagged):
        s = jnp.dot(k_chunk(j, c), qz_ref[c], preferred_element_type=f32) + bias_tile(j)
        r_old = r_ref[c]
        r_new = jnp.where(flagged, jnp.maximum(r_old, jnp.max(s, axis=0, keepdims=True)), r_old)
        p = jnp.where(flagged, jnp.exp2(s - r_new), 0.0)
        pv = jnp.dot(vt_ref[0, j], p.astype(bf16), preferred_element_type=f32)
        acc_ref[c] = acc_ref[c] * jnp.exp2(r_old - r_new) + pv
        r_ref[c] = r_new

    def stage_a(j, slot):
        bias = bias_tile(j)
        for c in range(2):
            s = jnp.dot(k_chunk(j, c), qp_ref[c], preferred_element_type=f32) + bias
            p_ref[slot, c] = jnp.exp2(s).astype(bf16)

    def stage_c(j, slot):
        vt = vt_ref[0, j]
        for c in range(2):
            pv = jnp.dot(vt, p_ref[slot, c], preferred_element_type=f32)
            ok = pv[B_V_DIM:B_V_DIM + 1] <= ROW_SUM_LIMIT
            acc_ref[c] = acc_ref[c] + jnp.where(ok, pv, 0.0)
            flag = jnp.where(ok, 0.0, 1.0)
            flag_ref[j, c] = flag
            any_ref[c] = jnp.maximum(any_ref[c], flag)

    for c in range(2):
        qz_ref[c] = jnp.where(is_q[c], qt, 0.0).astype(bf16)
    acc_ref[...] = jnp.zeros(acc_ref.shape, f32)
    any_ref[...] = jnp.zeros(any_ref.shape, f32)
    probe_bias = bias_tile(i, rows=B_PROBE // B_BIAS_BLOCK)
    for c in range(2):
        kp = k_ref[pl.ds(pl.multiple_of(i * T, T), B_PROBE), c * LANES:(c + 1) * LANES]
        s = jnp.dot(kp, qz_ref[c], preferred_element_type=f32) + probe_bias
        r = jnp.max(s, axis=0, keepdims=True)
        r_ref[c] = r
        set_reference(c, r)

    def run(j0, has_next):
        for u in range(B_UNROLL):
            stage_c(j0 + u, u % B_SLOTS)
            if u + B_AHEAD < B_UNROLL or has_next:
                stage_a(j0 + u + B_AHEAD, (u + B_AHEAD) % B_SLOTS)

    nbody = nchunks // B_UNROLL
    for u in range(B_AHEAD):
        stage_a(u, u)

    def body(b, carry):
        run(b * B_UNROLL, True)
        return carry

    lax.fori_loop(0, nbody - 1, body, 0)
    run((nbody - 1) * B_UNROLL, False)

    @pl.when(jnp.max(jnp.maximum(any_ref[0], any_ref[1])) > 0.0)
    def _():
        def redo(j, carry):
            @pl.when(jnp.max(jnp.maximum(flag_ref[j, 0], flag_ref[j, 1])) > 0.0)
            def _():
                for c in range(2):
                    exact_chunk(j, c, flag_ref[j, c] > 0.0)
            return carry

        lax.fori_loop(0, nchunks, redo, 0)

    lv = lamv_ref[...]
    lam = (jnp.exp(jnp.sum(lv[0:1] * lv[1:2], axis=-1, keepdims=True))
           - jnp.exp(jnp.sum(lv[2:3] * lv[3:4], axis=-1, keepdims=True)) + lambda_init)
    a1 = acc_ref[0]
    a2 = acc_ref[1]
    ot = (a1[0:B_V_DIM] / a1[B_V_DIM:B_V_DIM + 1]
          - lam * (a2[0:B_V_DIM] / a2[B_V_DIM:B_V_DIM + 1]))
    o = ot.T
    o_ref[...] = (_rms(o, g_ref[...]) * (1.0 - lambda_init)).astype(o_ref.dtype)


def _out_ffn_kernel(ya_ref, yb_ref, x_ref, woa_ref, wob_ref, gpost_ref, gpre_ref,
                    wg_ref, wu_ref, wd_ref, gfpost_ref, o_ref):
    y = (jnp.dot(ya_ref[...], woa_ref[...], preferred_element_type=jnp.float32)
         + jnp.dot(yb_ref[...], wob_ref[...], preferred_element_type=jnp.float32))
    h1 = x_ref[...] + _rms(y, gpost_ref[...])
    u = _rms(h1, gpre_ref[...]).astype(jnp.bfloat16)
    gate = jnp.dot(u, wg_ref[...], preferred_element_type=jnp.float32)
    up = jnp.dot(u, wu_ref[...], preferred_element_type=jnp.float32)
    act = (gate * jax.nn.sigmoid(gate) * up).astype(jnp.bfloat16)
    f = jnp.dot(act, wd_ref[...], preferred_element_type=jnp.float32)
    o_ref[...] = h1 + _rms(f, gfpost_ref[...])


def _resident(shape):
    zeros = (0,) * len(shape)
    return pl.BlockSpec(shape, lambda *_: zeros, pipeline_mode=pl.Buffered(1))


def _layer(h, l, p, bias_a, bias_b):
    S, D = h.shape
    bf16 = jnp.bfloat16
    lambda_init = 0.8 - 0.6 * math.exp(-0.3 * l)
    scale = HEAD_DIM ** -0.5

    w = p["w_in"]
    a_q, a_kv, b_qk = A_Q_HEADS * HEAD_DIM, A_KV_HEADS * HEAD_DIM, B_HEADS * 2 * HEAD_DIM
    c0 = a_q
    c1 = c0 + a_kv
    c2 = c1 + a_kv
    c3 = c2 + b_qk
    c4 = c3 + b_qk

    def dup(cols):
        parts = []
        for g in range(A_KV_HEADS):
            blk = cols[:, g * HEAD_DIM:(g + 1) * HEAD_DIM]
            parts += [blk, blk]
        return jnp.concatenate(parts, axis=1)

    def widen_b_keys(cols):
        wk = cols.reshape(D, B_HEADS, 2, HEAD_DIM)
        z = jnp.zeros_like(wk[:, :, 0])
        return jnp.stack([jnp.concatenate([wk[:, :, 0], z], axis=-1),
                          jnp.concatenate([z, wk[:, :, 1]], axis=-1)], axis=2).reshape(D, B_HEADS * 2 * LANES)

    kones = np.zeros((B_HEADS, 2, LANES), np.float32)
    kones[:, 0, HEAD_DIM:HEAD_DIM + B_REF_LANES] = 1.0
    kones[:, 1, 0:B_REF_LANES] = 1.0
    kones = jnp.asarray(kones.reshape(1, B_HEADS * 2 * LANES))

    w_cat = jnp.concatenate([w[:, :c0] * scale, dup(w[:, c0:c1]), dup(w[:, c1:c2]),
                             w[:, c2:c3] * (scale * LOG2E), widen_b_keys(w[:, c3:c4]), w[:, c4:]],
                            axis=1).astype(bf16)
    ncols = w_cat.shape[1]
    nrow = S // ROW_TILE
    nchunks = S // B_TILE
    assert ROW_TILE == B_TILE and nchunks % B_UNROLL == 0
    assert B_UNROLL % B_SLOTS == 0 and B_AHEAD < B_SLOTS and B_AHEAD <= B_UNROLL
    kb_cols = B_HEADS * 2 * LANES

    qa, ka, va, qb, kb, vt = pl.pallas_call(
        _in_proj_kernel,
        grid=(nrow,),
        in_specs=[pl.BlockSpec((ROW_TILE, D), lambda i: (i, 0)),
                  _resident((1, D)),
                  _resident((D, ncols)),
                  _resident((1, kb_cols))],
        out_specs=[pl.BlockSpec((ROW_TILE, 512), lambda i: (i, 0)),
                   pl.BlockSpec((ROW_TILE, 256), lambda i: (i, 0)),
                   pl.BlockSpec((ROW_TILE, 256), lambda i: (i, 0)),
                   pl.BlockSpec((ROW_TILE, 512), lambda i: (i, 0)),
                   pl.BlockSpec((ROW_TILE, kb_cols), lambda i: (i, 0)),
                   pl.BlockSpec((B_HEADS, 1, B_ACC_ROWS, ROW_TILE), lambda i: (0, i, 0, 0))],
        out_shape=[jax.ShapeDtypeStruct((S, 512), bf16),
                   jax.ShapeDtypeStruct((S, 256), bf16),
                   jax.ShapeDtypeStruct((S, 256), bf16),
                   jax.ShapeDtypeStruct((S, 512), bf16),
                   jax.ShapeDtypeStruct((S, kb_cols), bf16),
                   jax.ShapeDtypeStruct((B_HEADS, nchunks, B_ACC_ROWS, B_TILE), bf16)],
        compiler_params=pltpu.CompilerParams(dimension_semantics=("arbitrary",),
                                             vmem_limit_bytes=VMEM_LIMIT),
        name="in_proj",
    )(h, p["attn_pre_g"].reshape(1, D), w_cat, kones)

    nblocks = S // A_BLOCK
    a_rows = A_STEP_BLOCKS * A_BLOCK
    assert nblocks % A_STEP_BLOCKS == 0 and A_STEP_BLOCKS >= 2
    ya = pl.pallas_call(
        functools.partial(_win_attn_kernel, nblocks=nblocks),
        grid=(nblocks // A_STEP_BLOCKS,),
        in_specs=[pl.BlockSpec(memory_space=pltpu.SMEM),
                  pl.BlockSpec((a_rows, 512), lambda n: (n, 0)),
                  _resident(ka.shape),
                  _resident(va.shape),
                  _resident(bias_a.shape)],
        out_specs=pl.BlockSpec((a_rows, 512), lambda n: (n, 0)),
        out_shape=jax.ShapeDtypeStruct((S, 512), bf16),
        compiler_params=pltpu.CompilerParams(dimension_semantics=("arbitrary",),
                                             vmem_limit_bytes=VMEM_LIMIT),
        name="win_attn",
    )(p["a_sink"], qa, ka, va, bias_a)

    lamv = jnp.stack([p["lambda_q1"], p["lambda_k1"], p["lambda_q2"], p["lambda_k2"]])
    T = B_TILE
    yb = pl.pallas_call(
        functools.partial(_diff_attn_kernel, nchunks=nchunks, lambda_init=lambda_init),
        grid=(B_HEADS, nchunks),
        in_specs=[pl.BlockSpec((4, HEAD_DIM), lambda hh, i: (0, 0)),
                  pl.BlockSpec((1, B_V_DIM), lambda hh, i: (0, 0)),
                  pl.BlockSpec((T, LANES), lambda hh, i: (i, hh)),
                  pl.BlockSpec((S, 2 * LANES), lambda hh, i: (0, hh)),
                  pl.BlockSpec((1, nchunks, B_ACC_ROWS, T), lambda hh, i: (hh, 0, 0, 0)),
                  pl.BlockSpec((1, 5, B_BIAS_BLOCK, B_BIAS_BLOCK), lambda hh, i: (hh, 0, 0, 0))],
        out_specs=pl.BlockSpec((T, LANES), lambda hh, i: (i, hh)),
        out_shape=jax.ShapeDtypeStruct((S, B_HEADS * B_V_DIM), bf16),
        scratch_shapes=[pltpu.VMEM((2, LANES, T), bf16),
                        pltpu.VMEM((2, LANES, T), bf16),
                        pltpu.VMEM((B_SLOTS, 2, T, T), bf16),
                        pltpu.VMEM((2, 1, T), jnp.float32),
                        pltpu.VMEM((nchunks, 2, 1, T), jnp.float32),
                        pltpu.VMEM((2, 1, T), jnp.float32),
                        pltpu.VMEM((2, B_ACC_ROWS, T), jnp.float32)],
        compiler_params=pltpu.CompilerParams(dimension_semantics=("arbitrary", "arbitrary"),
                                             vmem_limit_bytes=VMEM_LIMIT),
        name="diff_attn",
    )(lamv, p["diff_subln_g"].reshape(1, B_V_DIM), qb, kb, vt, bias_b)

    w_out = p["w_out"].astype(bf16)
    a_width = A_Q_HEADS * HEAD_DIM
    d_ff = p["w_gate"].shape[1]
    R = FFN_ROW_TILE
    out = pl.pallas_call(
        _out_ffn_kernel,
        grid=(S // R,),
        in_specs=[pl.BlockSpec((R, 512), lambda i: (i, 0)),
                  pl.BlockSpec((R, 512), lambda i: (i, 0)),
                  pl.BlockSpec((R, D), lambda i: (i, 0)),
                  _resident((a_width, D)),
                  _resident((w_out.shape[0] - a_width, D)),
                  _resident((1, D)),
                  _resident((1, D)),
                  _resident((D, d_ff)),
                  _resident((D, d_ff)),
                  _resident((d_ff, D)),
                  _resident((1, D))],
        out_specs=pl.BlockSpec((R, D), lambda i: (i, 0)),
        out_shape=jax.ShapeDtypeStruct((S, D), jnp.float32),
        compiler_params=pltpu.CompilerParams(dimension_semantics=("arbitrary",),
                                             vmem_limit_bytes=VMEM_LIMIT),
        name="out_ffn",
    )(ya, yb, h, w_out[:a_width], w_out[a_width:], p["attn_post_g"].reshape(1, D),
      p["ffn_pre_g"].reshape(1, D), p["w_gate"].astype(bf16), p["w_up"].astype(bf16),
      p["w_down"].astype(bf16), p["ffn_post_g"].reshape(1, D))
    return out


def kernel(x, attn_pre_g, w_in, a_sink, lambda_q1, lambda_k1, lambda_q2, lambda_k2, diff_subln_g,
           rel_bias, w_out, attn_post_g, ffn_pre_g, w_gate, w_up, w_down, ffn_post_g):
    batch, S, D = x.shape
    depth = w_in.shape[0]
    assert S % B_TILE == 0 and S % A_BLOCK == 0 and S // A_BLOCK >= 2

    tab_a = _toeplitz_bias(rel_bias[:, :A_Q_HEADS], A_BLOCK, 3 * A_BLOCK, lambda t: -t - A_BLOCK)
    qi = np.arange(A_BLOCK)[:, None]
    kj = np.arange(3 * A_BLOCK)[None, :]
    in_window = np.abs(kj - A_BLOCK - qi) <= A_BLOCK
    valid = np.stack([in_window & (kj >= A_BLOCK), in_window, in_window & (kj < 2 * A_BLOCK)])
    bias_a = jnp.where(valid[:, None], tab_a[None], MASK_VALUE)

    T = B_TILE
    assert T >= MAX_DISTANCE
    tab_b = rel_bias[:, A_Q_HEADS:]
    nb = B_BIAS_BLOCK
    near = [_toeplitz_bias(tab_b, nb, nb, lambda t, d=d: d * nb + t) for d in (-1, 0, 1)]
    far = tab_b[_t5_bucket_np(np.array([-2 * nb, 2 * nb]))].astype(jnp.float32)
    const = [jnp.broadcast_to(far[side][:, None, None], (B_HEADS, nb, nb)) for side in (0, 1)]
    bias_b = jnp.stack([const[0]] + near + [const[1]], axis=1) * LOG2E

    outs = []
    for b in range(batch):
        h = x[b]
        for l in range(depth):
            p = dict(attn_pre_g=attn_pre_g[l], w_in=w_in[l], a_sink=a_sink[l], lambda_q1=lambda_q1[l],
                     lambda_k1=lambda_k1[l], lambda_q2=lambda_q2[l], lambda_k2=lambda_k2[l],
                     diff_subln_g=diff_subln_g[l], w_out=w_out[l], attn_post_g=attn_post_g[l],
                     ffn_pre_g=ffn_pre_g[l], w_gate=w_gate[l], w_up=w_up[l], w_down=w_down[l],
                     ffn_post_g=ffn_post_g[l])
            h = _layer(h, l, p, bias_a, bias_b)
        outs.append(h)
    return jnp.stack(outs)
```

```python
import functools
import math

import numpy as np
import jax
import jax.numpy as jnp
from jax import lax
from jax.experimental import pallas as pl
from jax.experimental.pallas import tpu as pltpu

HEAD_DIM = 64
A_Q_HEADS = 8
A_KV_HEADS = 2
A_BLOCK = 128
A_STEP_BLOCKS = 4
B_HEADS = 4
B_V_DIM = 2 * HEAD_DIM
NUM_BUCKETS = 32
MAX_DISTANCE = 128
EPS = 1e-6
MASK_VALUE = -1e30
LOG2E = math.log2(math.e)
ROW_SUM_LIMIT = 2.0 ** 40

LANES = 128
BF16_SUBLANES = 16
ROW_TILE = 512
FFN_ROW_TILE = 512
B_TILE = 512
B_QTILE = 1024
B_ACC_ROWS = B_V_DIM + BF16_SUBLANES
B_REF_LANES = 3
B_UNROLL = 4
B_AHEAD = 2
B_SLOTS = 4
B_PROBE = 128
B_BIAS_BLOCK = 128
VMEM_LIMIT = 56 * 1024 * 1024

_NT = (((1,), (1,)), ((), ()))


def _t5_bucket_np(rel):
    nb = NUM_BUCKETS // 2
    max_exact = nb // 2
    ret = np.where(rel > 0, nb, 0)
    n = np.abs(rel)
    nf = np.maximum(n, 1).astype(np.float32)
    large = max_exact + (np.log(nf / np.float32(max_exact)) / np.float32(math.log(MAX_DISTANCE / max_exact))
                         * np.float32(nb - max_exact)).astype(np.int32)
    large = np.minimum(large, nb - 1)
    return (ret + np.where(n < max_exact, n, large)).astype(np.int32)


def _toeplitz_bias(table, rows, cols, rel_of):
    length = rows + cols
    u = np.arange(length)
    t = np.where(u < cols, -u, length - u)
    w = table[_t5_bucket_np(rel_of(t))].astype(jnp.float32).T
    x = jnp.tile(w, (1, rows))[:, :rows * (length - 1)].reshape(w.shape[0], rows, length - 1)
    return x[:, :, :cols]


def _rms(xf, g):
    return xf * lax.rsqrt(jnp.mean(xf * xf, axis=-1, keepdims=True) + EPS) * g


def _in_proj_kernel(x_ref, g_ref, w_ref, kones_ref, qa_ref, ka_ref, va_ref, qb_ref, kb_ref, vt_ref):
    u = _rms(x_ref[...], g_ref[...]).astype(jnp.bfloat16)
    proj = jnp.dot(u, w_ref[...], preferred_element_type=jnp.float32)
    qa_ref[...] = proj[:, 0:512].astype(jnp.bfloat16)
    ka_ref[...] = proj[:, 512:768].astype(jnp.bfloat16)
    va_ref[...] = proj[:, 768:1024].astype(jnp.bfloat16)
    qb_ref[...] = proj[:, 1024:1536].astype(jnp.bfloat16)
    kb_ref[...] = (proj[:, 1536:2560] + kones_ref[...]).astype(jnp.bfloat16)
    rows = proj.shape[0]
    for h in range(B_HEADS):
        v = proj[:, 2560 + h * B_V_DIM:2560 + (h + 1) * B_V_DIM]
        vt_ref[h, 0, 0:B_V_DIM, :] = v.T.astype(jnp.bfloat16)
        vt_ref[h, 0, B_V_DIM:B_ACC_ROWS, :] = jnp.ones((BF16_SUBLANES, rows), jnp.bfloat16)


def _win_attn_kernel(sink_ref, q_ref, k_ref, v_ref, bias_ref, o_ref, *, nblocks):
    f32 = jnp.float32
    bf16 = jnp.bfloat16
    n = pl.program_id(0)
    rows = A_STEP_BLOCKS * A_BLOCK
    seq = k_ref.shape[0]
    group = A_Q_HEADS // A_KV_HEADS
    start = pl.multiple_of(n * rows, rows)
    prev = pl.multiple_of(jnp.maximum(start - A_BLOCK, 0), A_BLOCK)
    nxt = pl.multiple_of(jnp.minimum(start + rows, seq - A_BLOCK), A_BLOCK)
    kw = jnp.concatenate([k_ref[pl.ds(prev, A_BLOCK), :], k_ref[pl.ds(start, rows), :],
                          k_ref[pl.ds(nxt, A_BLOCK), :]], axis=0)
    vw = jnp.concatenate([v_ref[pl.ds(prev, A_BLOCK), :], v_ref[pl.ds(start, rows), :],
                          v_ref[pl.ds(nxt, A_BLOCK), :]], axis=0)
    lane = lax.broadcasted_iota(jnp.int32, (A_BLOCK, LANES), 1)
    low = lane < HEAD_DIM
    ones = jnp.ones((3 * A_BLOCK, LANES), bf16)
    for b in range(A_STEP_BLOCKS):
        blk = n * A_STEP_BLOCKS + b
        if b == 0:
            variant = jnp.where(blk == 0, 0, 1)
        elif b == A_STEP_BLOCKS - 1:
            variant = jnp.where(blk == nblocks - 1, 2, 1)
        else:
            variant = 1
        q = q_ref[b * A_BLOCK:(b + 1) * A_BLOCK, :]
        for g in range(A_KV_HEADS):
            heads = range(g * group, (g + 1) * group)
            qs = []
            for h in heads:
                q2 = q[:, (h // 2) * LANES:(h // 2 + 1) * LANES]
                qs.append(jnp.where(low if h % 2 == 0 else jnp.logical_not(low), q2, jnp.zeros_like(q2)))
            q4 = jnp.concatenate(qs, axis=0)
            kg = kw[b * A_BLOCK:(b + 3) * A_BLOCK, g * LANES:(g + 1) * LANES]
            vg = jnp.concatenate([vw[b * A_BLOCK:(b + 3) * A_BLOCK, g * LANES:(g + 1) * LANES], ones],
                                 axis=1)
            s = lax.dot_general(q4, kg, _NT, preferred_element_type=f32)
            s = s + bias_ref[variant, g * group:(g + 1) * group].reshape(group * A_BLOCK, 3 * A_BLOCK)
            snk = jnp.concatenate([jnp.full((A_BLOCK, 1), sink_ref[h], f32) for h in heads], axis=0)
            m = jnp.maximum(jnp.max(s, axis=-1, keepdims=True), snk)
            p = jnp.exp(s - m).astype(bf16)
            res = jnp.dot(p, vg, preferred_element_type=f32)
            on = res[:, 0:LANES] / (res[:, LANES:2 * LANES] + jnp.exp(snk - m))
            for t in range(group // 2):
                even = on[(2 * t) * A_BLOCK:(2 * t + 1) * A_BLOCK]
                odd = on[(2 * t + 1) * A_BLOCK:(2 * t + 2) * A_BLOCK]
                pair = (g * group) // 2 + t
                o_ref[b * A_BLOCK:(b + 1) * A_BLOCK, pair * LANES:(pair + 1) * LANES] = (
                    jnp.where(low, even, odd).astype(o_ref.dtype))


def _diff_attn_kernel(lamv_ref, g_ref, q_ref, k_ref, vt_ref, bias_ref, o_ref,
                      qz_ref, qp_ref, p_ref, r_ref, flag_ref, any_ref, acc_ref, *, nchunks, lambda_init):
    TQ = B_QTILE
    TK = B_TILE
    i = pl.program_id(1)
    f32 = jnp.float32
    bf16 = jnp.bfloat16

    qt = q_ref[...].astype(f32).T
    row = lax.broadcasted_iota(jnp.int32, (LANES, TQ), 0)
    is_q = (row < HEAD_DIM, row >= HEAD_DIM)
    ref_row = (HEAD_DIM, 0)

    ksub = TK // B_BIAS_BLOCK
    qsub = TQ // B_BIAS_BLOCK

    def bias_tile(j, rows=ksub):
        base = j * ksub - i * qsub
        by_dist = {d: bias_ref[0, jnp.clip(base + d, -2, 2) + 2] for d in range(1 - qsub, rows)}
        return jnp.concatenate(
            [jnp.concatenate([by_dist[a - b] for b in range(qsub)], axis=1) for a in range(rows)], axis=0)

    def k_chunk(j, c):
        return k_ref[pl.ds(pl.multiple_of(j * TK, TK), TK), c * LANES:(c + 1) * LANES]

    def set_reference(c, r_row):
        hi = r_row.astype(bf16).astype(f32)
        rem = r_row - hi
        mid = rem.astype(bf16).astype(f32)
        low = (rem - mid).astype(bf16).astype(f32)
        first = ref_row[c]
        ext = jnp.where(row == first, -hi, jnp.where(row == first + 1, -mid,
                                                     jnp.where(row == first + 2, -low, 0.0)))
        qp_ref[c] = jnp.where(is_q[c], qt, ext).astype(bf16)

    def exact_chunk(j, c, flagged):
        s = jnp.dot(k_chunk(j, c), qz_ref[c], preferred_element_type=f32) + bias_tile(j)
        r_old = r_ref[c]
        r_new = jnp.where(flagged, jnp.maximum(r_old, jnp.max(s, axis=0, keepdims=True)), r_old)
        p = jnp.where(flagged, jnp.exp2(s - r_new), 0.0)
        pv = jnp.dot(vt_ref[0, j], p.astype(bf16), preferred_element_type=f32)
        acc_ref[c] = acc_ref[c] * jnp.exp2(r_old - r_new) + pv
        r_ref[c] = r_new

    def stage_a(j, slot):
        bias = bias_tile(j)
        for c in range(2):
            s = jnp.dot(k_chunk(j, c), qp_ref[c], preferred_element_type=f32) + bias
            p_ref[slot, c] = jnp.exp2(s).astype(bf16)

    def stage_c(j, slot):
        vt = vt_ref[0, j]
        for c in range(2):
            pv = jnp.dot(vt, p_ref[slot, c], preferred_element_type=f32)
            ok = pv[B_V_DIM:B_V_DIM + 1] <= ROW_SUM_LIMIT
            acc_ref[c] = acc_ref[c] + jnp.where(ok, pv, 0.0)
            flag = jnp.where(ok, 0.0, 1.0)
            flag_ref[j, c] = flag
            any_ref[c] = jnp.maximum(any_ref[c], flag)

    for c in range(2):
        qz_ref[c] = jnp.where(is_q[c], qt, 0.0).astype(bf16)
    acc_ref[...] = jnp.zeros(acc_ref.shape, f32)
    any_ref[...] = jnp.zeros(any_ref.shape, f32)
    diag = i * (TQ // TK)
    probe_bias = bias_tile(diag, rows=B_PROBE // B_BIAS_BLOCK)
    for c in range(2):
        kp = k_ref[pl.ds(pl.multiple_of(diag * TK, TK), B_PROBE), c * LANES:(c + 1) * LANES]
        s = jnp.dot(kp, qz_ref[c], preferred_element_type=f32) + probe_bias
        r = jnp.max(s, axis=0, keepdims=True)
        r_ref[c] = r
        set_reference(c, r)

    def run(j0, has_next):
        for u in range(B_UNROLL):
            stage_c(j0 + u, u % B_SLOTS)
            if u + B_AHEAD < B_UNROLL or has_next:
                stage_a(j0 + u + B_AHEAD, (u + B_AHEAD) % B_SLOTS)

    nbody = nchunks // B_UNROLL
    for u in range(B_AHEAD):
        stage_a(u, u)

    def body(b, carry):
        run(b * B_UNROLL, True)
        return carry

    lax.fori_loop(0, nbody - 1, body, 0)
    run((nbody - 1) * B_UNROLL, False)

    @pl.when(jnp.max(jnp.maximum(any_ref[0], any_ref[1])) > 0.0)
    def _():
        def redo(j, carry):
            @pl.when(jnp.max(jnp.maximum(flag_ref[j, 0], flag_ref[j, 1])) > 0.0)
            def _():
                for c in range(2):
                    exact_chunk(j, c, flag_ref[j, c] > 0.0)
            return carry

        lax.fori_loop(0, nchunks, redo, 0)

    lv = lamv_ref[...]
    lam = (jnp.exp(jnp.sum(lv[0:1] * lv[1:2], axis=-1, keepdims=True))
           - jnp.exp(jnp.sum(lv[2:3] * lv[3:4], axis=-1, keepdims=True)) + lambda_init)
    a1 = acc_ref[0]
    a2 = acc_ref[1]
    ot = (a1[0:B_V_DIM] / a1[B_V_DIM:B_V_DIM + 1]
          - lam * (a2[0:B_V_DIM] / a2[B_V_DIM:B_V_DIM + 1]))
    o = ot.T
    o_ref[...] = (_rms(o, g_ref[...]) * (1.0 - lambda_init)).astype(o_ref.dtype)


def _out_ffn_kernel(ya_ref, yb_ref, x_ref, woa_ref, wob_ref, gpost_ref, gpre_ref,
                    wg_ref, wu_ref, wd_ref, gfpost_ref, o_ref):
    y = (jnp.dot(ya_ref[...], woa_ref[...], preferred_element_type=jnp.float32)
         + jnp.dot(yb_ref[...], wob_ref[...], preferred_element_type=jnp.float32))
    h1 = x_ref[...] + _rms(y, gpost_ref[...])
    u = _rms(h1, gpre_ref[...]).astype(jnp.bfloat16)
    gate = jnp.dot(u, wg_ref[...], preferred_element_type=jnp.float32)
    up = jnp.dot(u, wu_ref[...], preferred_element_type=jnp.float32)
    act = (gate * jax.nn.sigmoid(gate) * up).astype(jnp.bfloat16)
    f = jnp.dot(act, wd_ref[...], preferred_element_type=jnp.float32)
    o_ref[...] = h1 + _rms(f, gfpost_ref[...])


def _resident(shape):
    zeros = (0,) * len(shape)
    return pl.BlockSpec(shape, lambda *_: zeros, pipeline_mode=pl.Buffered(1))


def _layer(h, l, p, bias_a, bias_b):
    S, D = h.shape
    bf16 = jnp.bfloat16
    lambda_init = 0.8 - 0.6 * math.exp(-0.3 * l)
    scale = HEAD_DIM ** -0.5

    w = p["w_in"]
    a_q, a_kv, b_qk = A_Q_HEADS * HEAD_DIM, A_KV_HEADS * HEAD_DIM, B_HEADS * 2 * HEAD_DIM
    c0 = a_q
    c1 = c0 + a_kv
    c2 = c1 + a_kv
    c3 = c2 + b_qk
    c4 = c3 + b_qk

    def dup(cols):
        parts = []
        for g in range(A_KV_HEADS):
            blk = cols[:, g * HEAD_DIM:(g + 1) * HEAD_DIM]
            parts += [blk, blk]
        return jnp.concatenate(parts, axis=1)

    def widen_b_keys(cols):
        wk = cols.reshape(D, B_HEADS, 2, HEAD_DIM)
        z = jnp.zeros_like(wk[:, :, 0])
        return jnp.stack([jnp.concatenate([wk[:, :, 0], z], axis=-1),
                          jnp.concatenate([z, wk[:, :, 1]], axis=-1)], axis=2).reshape(D, B_HEADS * 2 * LANES)

    kones = np.zeros((B_HEADS, 2, LANES), np.float32)
    kones[:, 0, HEAD_DIM:HEAD_DIM + B_REF_LANES] = 1.0
    kones[:, 1, 0:B_REF_LANES] = 1.0
    kones = jnp.asarray(kones.reshape(1, B_HEADS * 2 * LANES))

    w_cat = jnp.concatenate([w[:, :c0] * scale, dup(w[:, c0:c1]), dup(w[:, c1:c2]),
                             w[:, c2:c3] * (scale * LOG2E), widen_b_keys(w[:, c3:c4]), w[:, c4:]],
                            axis=1).astype(bf16)
    ncols = w_cat.shape[1]
    nrow = S // ROW_TILE
    nchunks = S // B_TILE
    assert ROW_TILE == B_TILE and nchunks % B_UNROLL == 0
    assert B_UNROLL % B_SLOTS == 0 and B_AHEAD < B_SLOTS and B_AHEAD <= B_UNROLL
    kb_cols = B_HEADS * 2 * LANES

    qa, ka, va, qb, kb, vt = pl.pallas_call(
        _in_proj_kernel,
        grid=(nrow,),
        in_specs=[pl.BlockSpec((ROW_TILE, D), lambda i: (i, 0)),
                  _resident((1, D)),
                  _resident((D, ncols)),
                  _resident((1, kb_cols))],
        out_specs=[pl.BlockSpec((ROW_TILE, 512), lambda i: (i, 0)),
                   pl.BlockSpec((ROW_TILE, 256), lambda i: (i, 0)),
                   pl.BlockSpec((ROW_TILE, 256), lambda i: (i, 0)),
                   pl.BlockSpec((ROW_TILE, 512), lambda i: (i, 0)),
                   pl.BlockSpec((ROW_TILE, kb_cols), lambda i: (i, 0)),
                   pl.BlockSpec((B_HEADS, 1, B_ACC_ROWS, ROW_TILE), lambda i: (0, i, 0, 0))],
        out_shape=[jax.ShapeDtypeStruct((S, 512), bf16),
                   jax.ShapeDtypeStruct((S, 256), bf16),
                   jax.ShapeDtypeStruct((S, 256), bf16),
                   jax.ShapeDtypeStruct((S, 512), bf16),
                   jax.ShapeDtypeStruct((S, kb_cols), bf16),
                   jax.ShapeDtypeStruct((B_HEADS, nchunks, B_ACC_ROWS, B_TILE), bf16)],
        compiler_params=pltpu.CompilerParams(dimension_semantics=("arbitrary",),
                                             vmem_limit_bytes=VMEM_LIMIT),
        name="in_proj",
    )(h, p["attn_pre_g"].reshape(1, D), w_cat, kones)

    nblocks = S // A_BLOCK
    a_rows = A_STEP_BLOCKS * A_BLOCK
    assert nblocks % A_STEP_BLOCKS == 0 and A_STEP_BLOCKS >= 2
    ya = pl.pallas_call(
        functools.partial(_win_attn_kernel, nblocks=nblocks),
        grid=(nblocks // A_STEP_BLOCKS,),
        in_specs=[pl.BlockSpec(memory_space=pltpu.SMEM),
                  pl.BlockSpec((a_rows, 512), lambda n: (n, 0)),
                  _resident(ka.shape),
                  _resident(va.shape),
                  _resident(bias_a.shape)],
        out_specs=pl.BlockSpec((a_rows, 512), lambda n: (n, 0)),
        out_shape=jax.ShapeDtypeStruct((S, 512), bf16),
        compiler_params=pltpu.CompilerParams(dimension_semantics=("arbitrary",),
                                             vmem_limit_bytes=VMEM_LIMIT),
        name="win_attn",
    )(p["a_sink"], qa, ka, va, bias_a)

    lamv = jnp.stack([p["lambda_q1"], p["lambda_k1"], p["lambda_q2"], p["lambda_k2"]])
    TQ, TK = B_QTILE, B_TILE
    assert S % TQ == 0 and TQ % TK == 0
    yb = pl.pallas_call(
        functools.partial(_diff_attn_kernel, nchunks=nchunks, lambda_init=lambda_init),
        grid=(B_HEADS, S // TQ),
        in_specs=[pl.BlockSpec((4, HEAD_DIM), lambda hh, i: (0, 0)),
                  pl.BlockSpec((1, B_V_DIM), lambda hh, i: (0, 0)),
                  pl.BlockSpec((TQ, LANES), lambda hh, i: (i, hh)),
                  pl.BlockSpec((S, 2 * LANES), lambda hh, i: (0, hh)),
                  pl.BlockSpec((1, nchunks, B_ACC_ROWS, TK), lambda hh, i: (hh, 0, 0, 0)),
                  pl.BlockSpec((1, 5, B_BIAS_BLOCK, B_BIAS_BLOCK), lambda hh, i: (hh, 0, 0, 0))],
        out_specs=pl.BlockSpec((TQ, LANES), lambda hh, i: (i, hh)),
        out_shape=jax.ShapeDtypeStruct((S, B_HEADS * B_V_DIM), bf16),
        scratch_shapes=[pltpu.VMEM((2, LANES, TQ), bf16),
                        pltpu.VMEM((2, LANES, TQ), bf16),
                        pltpu.VMEM((B_SLOTS, 2, TK, TQ), bf16),
                        pltpu.VMEM((2, 1, TQ), jnp.float32),
                        pltpu.VMEM((nchunks, 2, 1, TQ), jnp.float32),
                        pltpu.VMEM((2, 1, TQ), jnp.float32),
                        pltpu.VMEM((2, B_ACC_ROWS, TQ), jnp.float32)],
        compiler_params=pltpu.CompilerParams(dimension_semantics=("arbitrary", "arbitrary"),
                                             vmem_limit_bytes=VMEM_LIMIT),
        name="diff_attn",
    )(lamv, p["diff_subln_g"].reshape(1, B_V_DIM), qb, kb, vt, bias_b)

    w_out = p["w_out"].astype(bf16)
    a_width = A_Q_HEADS * HEAD_DIM
    d_ff = p["w_gate"].shape[1]
    R = FFN_ROW_TILE
    out = pl.pallas_call(
        _out_ffn_kernel,
        grid=(S // R,),
        in_specs=[pl.BlockSpec((R, 512), lambda i: (i, 0)),
                  pl.BlockSpec((R, 512), lambda i: (i, 0)),
                  pl.BlockSpec((R, D), lambda i: (i, 0)),
                  _resident((a_width, D)),
                  _resident((w_out.shape[0] - a_width, D)),
                  _resident((1, D)),
                  _resident((1, D)),
                  _resident((D, d_ff)),
                  _resident((D, d_ff)),
                  _resident((d_ff, D)),
                  _resident((1, D))],
        out_specs=pl.BlockSpec((R, D), lambda i: (i, 0)),
        out_shape=jax.ShapeDtypeStruct((S, D), jnp.float32),
        compiler_params=pltpu.CompilerParams(dimension_semantics=("arbitrary",),
                                             vmem_limit_bytes=VMEM_LIMIT),
        name="out_ffn",
    )(ya, yb, h, w_out[:a_width], w_out[a_width:], p["attn_post_g"].reshape(1, D),
      p["ffn_pre_g"].reshape(1, D), p["w_gate"].astype(bf16), p["w_up"].astype(bf16),
      p["w_down"].astype(bf16), p["ffn_post_g"].reshape(1, D))
    return out


def kernel(x, attn_pre_g, w_in, a_sink, lambda_q1, lambda_k1, lambda_q2, lambda_k2, diff_subln_g,
           rel_bias, w_out, attn_post_g, ffn_pre_g, w_gate, w_up, w_down, ffn_post_g):
    batch, S, D = x.shape
    depth = w_in.shape[0]
    assert S % B_TILE == 0 and S % A_BLOCK == 0 and S // A_BLOCK >= 2

    tab_a = _toeplitz_bias(rel_bias[:, :A_Q_HEADS], A_BLOCK, 3 * A_BLOCK, lambda t: -t - A_BLOCK)
    qi = np.arange(A_BLOCK)[:, None]
    kj = np.arange(3 * A_BLOCK)[None, :]
    in_window = np.abs(kj - A_BLOCK - qi) <= A_BLOCK
    valid = np.stack([in_window & (kj >= A_BLOCK), in_window, in_window & (kj < 2 * A_BLOCK)])
    bias_a = jnp.where(valid[:, None], tab_a[None], MASK_VALUE)

    nb = B_BIAS_BLOCK
    assert len(set(_t5_bucket_np(np.arange(nb + 1, 4 * nb)))) == 1
    tab_b = rel_bias[:, A_Q_HEADS:]
    near = [_toeplitz_bias(tab_b, nb, nb, lambda t, d=d: d * nb + t) for d in (-1, 0, 1)]
    far = tab_b[_t5_bucket_np(np.array([-2 * nb, 2 * nb]))].astype(jnp.float32)
    const = [jnp.broadcast_to(far[side][:, None, None], (B_HEADS, nb, nb)) for side in (0, 1)]
    bias_b = jnp.stack([const[0]] + near + [const[1]], axis=1) * LOG2E

    outs = []
    for b in range(batch):
        h = x[b]
        for l in range(depth):
            p = dict(attn_pre_g=attn_pre_g[l], w_in=w_in[l], a_sink=a_sink[l], lambda_q1=lambda_q1[l],
                     lambda_k1=lambda_k1[l], lambda_q2=lambda_q2[l], lambda_k2=lambda_k2[l],
                     diff_subln_g=diff_subln_g[l], w_out=w_out[l], attn_post_g=attn_post_g[l],
                     ffn_pre_g=ffn_pre_g[l], w_gate=w_gate[l], w_up=w_up[l], w_down=w_down[l],
                     ffn_post_g=ffn_post_g[l])
            h = _layer(h, l, p, bias_a, bias_b)
        outs.append(h)
    return jnp.stack(outs)
```

```python
import functools
import math

import numpy as np
import jax
import jax.numpy as jnp
from jax import lax
from jax.experimental import pallas as pl
from jax.experimental.pallas import tpu as pltpu

HEAD_DIM = 64
A_Q_HEADS = 8
A_KV_HEADS = 2
A_BLOCK = 128
A_STEP_BLOCKS = 4
B_HEADS = 4
B_V_DIM = 2 * HEAD_DIM
NUM_BUCKETS = 32
MAX_DISTANCE = 128
EPS = 1e-6
MASK_VALUE = -1e30
LOG2E = math.log2(math.e)
ROW_SUM_LIMIT = 2.0 ** 40

LANES = 128
BF16_SUBLANES = 16
ROW_TILE = 512
FFN_ROW_TILE = 512
B_TILE = 512
B_QTILE = 1024
B_ACC_ROWS = B_V_DIM + BF16_SUBLANES
B_REF_LANES = 3
B_UNROLL = 8
B_AHEAD = 2
B_SLOTS = 4
B_PROBE = 128
B_BIAS_BLOCK = 128
VMEM_LIMIT = 56 * 1024 * 1024

_NT = (((1,), (1,)), ((), ()))


def _t5_bucket_np(rel):
    nb = NUM_BUCKETS // 2
    max_exact = nb // 2
    ret = np.where(rel > 0, nb, 0)
    n = np.abs(rel)
    nf = np.maximum(n, 1).astype(np.float32)
    large = max_exact + (np.log(nf / np.float32(max_exact)) / np.float32(math.log(MAX_DISTANCE / max_exact))
                         * np.float32(nb - max_exact)).astype(np.int32)
    large = np.minimum(large, nb - 1)
    return (ret + np.where(n < max_exact, n, large)).astype(np.int32)


def _toeplitz_bias(table, rows, cols, rel_of):
    length = rows + cols
    u = np.arange(length)
    t = np.where(u < cols, -u, length - u)
    w = table[_t5_bucket_np(rel_of(t))].astype(jnp.float32).T
    x = jnp.tile(w, (1, rows))[:, :rows * (length - 1)].reshape(w.shape[0], rows, length - 1)
    return x[:, :, :cols]


def _rms(xf, g):
    return xf * lax.rsqrt(jnp.mean(xf * xf, axis=-1, keepdims=True) + EPS) * g


def _in_proj_kernel(x_ref, g_ref, w_ref, kones_ref, qa_ref, ka_ref, va_ref, qb_ref, kb_ref, vt_ref):
    u = _rms(x_ref[...], g_ref[...]).astype(jnp.bfloat16)
    proj = jnp.dot(u, w_ref[...], preferred_element_type=jnp.float32)
    qa_ref[...] = proj[:, 0:512].astype(jnp.bfloat16)
    ka_ref[...] = proj[:, 512:768].astype(jnp.bfloat16)
    va_ref[...] = proj[:, 768:1024].astype(jnp.bfloat16)
    qb_ref[...] = proj[:, 1024:1536].astype(jnp.bfloat16)
    kb_ref[...] = (proj[:, 1536:2560] + kones_ref[...]).astype(jnp.bfloat16)
    rows = proj.shape[0]
    for h in range(B_HEADS):
        v = proj[:, 2560 + h * B_V_DIM:2560 + (h + 1) * B_V_DIM]
        vt_ref[h, 0, 0:B_V_DIM, :] = v.T.astype(jnp.bfloat16)
        vt_ref[h, 0, B_V_DIM:B_ACC_ROWS, :] = jnp.ones((BF16_SUBLANES, rows), jnp.bfloat16)


def _win_attn_kernel(sink_ref, q_ref, k_ref, v_ref, bias_ref, o_ref, *, nblocks):
    f32 = jnp.float32
    bf16 = jnp.bfloat16
    n = pl.program_id(0)
    rows = A_STEP_BLOCKS * A_BLOCK
    seq = k_ref.shape[0]
    group = A_Q_HEADS // A_KV_HEADS
    start = pl.multiple_of(n * rows, rows)
    prev = pl.multiple_of(jnp.maximum(start - A_BLOCK, 0), A_BLOCK)
    nxt = pl.multiple_of(jnp.minimum(start + rows, seq - A_BLOCK), A_BLOCK)
    kw = jnp.concatenate([k_ref[pl.ds(prev, A_BLOCK), :], k_ref[pl.ds(start, rows), :],
                          k_ref[pl.ds(nxt, A_BLOCK), :]], axis=0)
    vw = jnp.concatenate([v_ref[pl.ds(prev, A_BLOCK), :], v_ref[pl.ds(start, rows), :],
                          v_ref[pl.ds(nxt, A_BLOCK), :]], axis=0)
    lane = lax.broadcasted_iota(jnp.int32, (A_BLOCK, LANES), 1)
    low = lane < HEAD_DIM
    ones = jnp.ones((3 * A_BLOCK, LANES), bf16)
    for b in range(A_STEP_BLOCKS):
        blk = n * A_STEP_BLOCKS + b
        if b == 0:
            variant = jnp.where(blk == 0, 0, 1)
        elif b == A_STEP_BLOCKS - 1:
            variant = jnp.where(blk == nblocks - 1, 2, 1)
        else:
            variant = 1
        q = q_ref[b * A_BLOCK:(b + 1) * A_BLOCK, :]
        for g in range(A_KV_HEADS):
            heads = range(g * group, (g + 1) * group)
            qs = []
            for h in heads:
                q2 = q[:, (h // 2) * LANES:(h // 2 + 1) * LANES]
                qs.append(jnp.where(low if h % 2 == 0 else jnp.logical_not(low), q2, jnp.zeros_like(q2)))
            q4 = jnp.concatenate(qs, axis=0)
            kg = kw[b * A_BLOCK:(b + 3) * A_BLOCK, g * LANES:(g + 1) * LANES]
            vg = jnp.concatenate([vw[b * A_BLOCK:(b + 3) * A_BLOCK, g * LANES:(g + 1) * LANES], ones],
                                 axis=1)
            s = lax.dot_general(q4, kg, _NT, preferred_element_type=f32)
            s = s + bias_ref[variant, g * group:(g + 1) * group].reshape(group * A_BLOCK, 3 * A_BLOCK)
            snk = jnp.concatenate([jnp.full((A_BLOCK, 1), sink_ref[h], f32) for h in heads], axis=0)
            m = jnp.maximum(jnp.max(s, axis=-1, keepdims=True), snk)
            p = jnp.exp(s - m).astype(bf16)
            res = jnp.dot(p, vg, preferred_element_type=f32)
            on = res[:, 0:LANES] / (res[:, LANES:2 * LANES] + jnp.exp(snk - m))
            for t in range(group // 2):
                even = on[(2 * t) * A_BLOCK:(2 * t + 1) * A_BLOCK]
                odd = on[(2 * t + 1) * A_BLOCK:(2 * t + 2) * A_BLOCK]
                pair = (g * group) // 2 + t
                o_ref[b * A_BLOCK:(b + 1) * A_BLOCK, pair * LANES:(pair + 1) * LANES] = (
                    jnp.where(low, even, odd).astype(o_ref.dtype))


def _diff_attn_kernel(lamv_ref, g_ref, q_ref, k_ref, vt_ref, bias_ref, o_ref,
                      qz_ref, qp_ref, p_ref, r_ref, flag_ref, any_ref, acc_ref, *, nchunks, lambda_init):
    TQ = B_QTILE
    TK = B_TILE
    i = pl.program_id(1)
    f32 = jnp.float32
    bf16 = jnp.bfloat16

    qt = q_ref[...].astype(f32).T
    row = lax.broadcasted_iota(jnp.int32, (LANES, TQ), 0)
    is_q = (row < HEAD_DIM, row >= HEAD_DIM)
    ref_row = (HEAD_DIM, 0)

    ksub = TK // B_BIAS_BLOCK
    qsub = TQ // B_BIAS_BLOCK

    def bias_tile(j, rows=ksub):
        base = j * ksub - i * qsub
        by_dist = {d: bias_ref[0, jnp.clip(base + d, -2, 2) + 2] for d in range(1 - qsub, rows)}
        return jnp.concatenate(
            [jnp.concatenate([by_dist[a - b] for b in range(qsub)], axis=1) for a in range(rows)], axis=0)

    def k_chunk(j, c):
        return k_ref[pl.ds(pl.multiple_of(j * TK, TK), TK), c * LANES:(c + 1) * LANES]

    def set_reference(c, r_row):
        hi = r_row.astype(bf16).astype(f32)
        rem = r_row - hi
        mid = rem.astype(bf16).astype(f32)
        low = (rem - mid).astype(bf16).astype(f32)
        first = ref_row[c]
        ext = jnp.where(row == first, -hi, jnp.where(row == first + 1, -mid,
                                                     jnp.where(row == first + 2, -low, 0.0)))
        qp_ref[c] = jnp.where(is_q[c], qt, ext).astype(bf16)

    def exact_chunk(j, c, flagged):
        s = jnp.dot(k_chunk(j, c), qz_ref[c], preferred_element_type=f32) + bias_tile(j)
        r_old = r_ref[c]
        r_new = jnp.where(flagged, jnp.maximum(r_old, jnp.max(s, axis=0, keepdims=True)), r_old)
        p = jnp.where(flagged, jnp.exp2(s - r_new), 0.0)
        pv = jnp.dot(vt_ref[0, j], p.astype(bf16), preferred_element_type=f32)
        acc_ref[c] = acc_ref[c] * jnp.exp2(r_old - r_new) + pv
        r_ref[c] = r_new

    def stage_a(j, slot):
        bias = bias_tile(j)
        for c in range(2):
            s = jnp.dot(k_chunk(j, c), qp_ref[c], preferred_element_type=f32) + bias
            p_ref[slot, c] = jnp.exp2(s).astype(bf16)

    def stage_c(j, slot):
        vt = vt_ref[0, j]
        for c in range(2):
            pv = jnp.dot(vt, p_ref[slot, c], preferred_element_type=f32)
            ok = pv[B_V_DIM:B_V_DIM + 1] <= ROW_SUM_LIMIT
            acc_ref[c] = acc_ref[c] + jnp.where(ok, pv, 0.0)
            flag = jnp.where(ok, 0.0, 1.0)
            flag_ref[j, c] = flag
            any_ref[c] = jnp.maximum(any_ref[c], flag)

    for c in range(2):
        qz_ref[c] = jnp.where(is_q[c], qt, 0.0).astype(bf16)
    acc_ref[...] = jnp.zeros(acc_ref.shape, f32)
    any_ref[...] = jnp.zeros(any_ref.shape, f32)
    diag = i * (TQ // TK)
    probe_bias = bias_tile(diag, rows=B_PROBE // B_BIAS_BLOCK)
    for c in range(2):
        kp = k_ref[pl.ds(pl.multiple_of(diag * TK, TK), B_PROBE), c * LANES:(c + 1) * LANES]
        s = jnp.dot(kp, qz_ref[c], preferred_element_type=f32) + probe_bias
        r = jnp.max(s, axis=0, keepdims=True)
        r_ref[c] = r
        set_reference(c, r)

    def run(j0, has_next):
        for u in range(B_UNROLL):
            stage_c(j0 + u, u % B_SLOTS)
            if u + B_AHEAD < B_UNROLL or has_next:
                stage_a(j0 + u + B_AHEAD, (u + B_AHEAD) % B_SLOTS)

    nbody = nchunks // B_UNROLL
    for u in range(B_AHEAD):
        stage_a(u, u)

    def body(b, carry):
        run(b * B_UNROLL, True)
        return carry

    lax.fori_loop(0, nbody - 1, body, 0)
    run((nbody - 1) * B_UNROLL, False)

    @pl.when(jnp.max(jnp.maximum(any_ref[0], any_ref[1])) > 0.0)
    def _():
        def redo(j, carry):
            @pl.when(jnp.max(jnp.maximum(flag_ref[j, 0], flag_ref[j, 1])) > 0.0)
            def _():
                for c in range(2):
                    exact_chunk(j, c, flag_ref[j, c] > 0.0)
            return carry

        lax.fori_loop(0, nchunks, redo, 0)

    lv = lamv_ref[...]
    lam = (jnp.exp(jnp.sum(lv[0:1] * lv[1:2], axis=-1, keepdims=True))
           - jnp.exp(jnp.sum(lv[2:3] * lv[3:4], axis=-1, keepdims=True)) + lambda_init)
    a1 = acc_ref[0]
    a2 = acc_ref[1]
    ot = (a1[0:B_V_DIM] / a1[B_V_DIM:B_V_DIM + 1]
          - lam * (a2[0:B_V_DIM] / a2[B_V_DIM:B_V_DIM + 1]))
    o = ot.T
    o_ref[...] = (_rms(o, g_ref[...]) * (1.0 - lambda_init)).astype(o_ref.dtype)


def _out_ffn_kernel(ya_ref, yb_ref, x_ref, woa_ref, wob_ref, gpost_ref, gpre_ref,
                    wg_ref, wu_ref, wd_ref, gfpost_ref, o_ref):
    y = (jnp.dot(ya_ref[...], woa_ref[...], preferred_element_type=jnp.float32)
         + jnp.dot(yb_ref[...], wob_ref[...], preferred_element_type=jnp.float32))
    h1 = x_ref[...] + _rms(y, gpost_ref[...])
    u = _rms(h1, gpre_ref[...]).astype(jnp.bfloat16)
    gate = jnp.dot(u, wg_ref[...], preferred_element_type=jnp.float32)
    up = jnp.dot(u, wu_ref[...], preferred_element_type=jnp.float32)
    act = (gate * jax.nn.sigmoid(gate) * up).astype(jnp.bfloat16)
    f = jnp.dot(act, wd_ref[...], preferred_element_type=jnp.float32)
    o_ref[...] = h1 + _rms(f, gfpost_ref[...])


def _resident(shape):
    zeros = (0,) * len(shape)
    return pl.BlockSpec(shape, lambda *_: zeros, pipeline_mode=pl.Buffered(1))


def _layer(h, l, p, bias_a, bias_b):
    S, D = h.shape
    bf16 = jnp.bfloat16
    lambda_init = 0.8 - 0.6 * math.exp(-0.3 * l)
    scale = HEAD_DIM ** -0.5

    w = p["w_in"]
    a_q, a_kv, b_qk = A_Q_HEADS * HEAD_DIM, A_KV_HEADS * HEAD_DIM, B_HEADS * 2 * HEAD_DIM
    c0 = a_q
    c1 = c0 + a_kv
    c2 = c1 + a_kv
    c3 = c2 + b_qk
    c4 = c3 + b_qk

    def dup(cols):
        parts = []
        for g in range(A_KV_HEADS):
            blk = cols[:, g * HEAD_DIM:(g + 1) * HEAD_DIM]
            parts += [blk, blk]
        return jnp.concatenate(parts, axis=1)

    def widen_b_keys(cols):
        wk = cols.reshape(D, B_HEADS, 2, HEAD_DIM)
        z = jnp.zeros_like(wk[:, :, 0])
        return jnp.stack([jnp.concatenate([wk[:, :, 0], z], axis=-1),
                          jnp.concatenate([z, wk[:, :, 1]], axis=-1)], axis=2).reshape(D, B_HEADS * 2 * LANES)

    kones = np.zeros((B_HEADS, 2, LANES), np.float32)
    kones[:, 0, HEAD_DIM:HEAD_DIM + B_REF_LANES] = 1.0
    kones[:, 1, 0:B_REF_LANES] = 1.0
    kones = jnp.asarray(kones.reshape(1, B_HEADS * 2 * LANES))

    w_cat = jnp.concatenate([w[:, :c0] * scale, dup(w[:, c0:c1]), dup(w[:, c1:c2]),
                             w[:, c2:c3] * (scale * LOG2E), widen_b_keys(w[:, c3:c4]), w[:, c4:]],
                            axis=1).astype(bf16)
    ncols = w_cat.shape[1]
    nrow = S // ROW_TILE
    nchunks = S // B_TILE
    assert ROW_TILE == B_TILE and nchunks % B_UNROLL == 0
    assert B_UNROLL % B_SLOTS == 0 and B_AHEAD < B_SLOTS and B_AHEAD <= B_UNROLL
    kb_cols = B_HEADS * 2 * LANES

    qa, ka, va, qb, kb, vt = pl.pallas_call(
        _in_proj_kernel,
        grid=(nrow,),
        in_specs=[pl.BlockSpec((ROW_TILE, D), lambda i: (i, 0)),
                  _resident((1, D)),
                  _resident((D, ncols)),
                  _resident((1, kb_cols))],
        out_specs=[pl.BlockSpec((ROW_TILE, 512), lambda i: (i, 0)),
                   pl.BlockSpec((ROW_TILE, 256), lambda i: (i, 0)),
                   pl.BlockSpec((ROW_TILE, 256), lambda i: (i, 0)),
                   pl.BlockSpec((ROW_TILE, 512), lambda i: (i, 0)),
                   pl.BlockSpec((ROW_TILE, kb_cols), lambda i: (i, 0)),
                   pl.BlockSpec((B_HEADS, 1, B_ACC_ROWS, ROW_TILE), lambda i: (0, i, 0, 0))],
        out_shape=[jax.ShapeDtypeStruct((S, 512), bf16),
                   jax.ShapeDtypeStruct((S, 256), bf16),
                   jax.ShapeDtypeStruct((S, 256), bf16),
                   jax.ShapeDtypeStruct((S, 512), bf16),
                   jax.ShapeDtypeStruct((S, kb_cols), bf16),
                   jax.ShapeDtypeStruct((B_HEADS, nchunks, B_ACC_ROWS, B_TILE), bf16)],
        compiler_params=pltpu.CompilerParams(dimension_semantics=("arbitrary",),
                                             vmem_limit_bytes=VMEM_LIMIT),
        name="in_proj",
    )(h, p["attn_pre_g"].reshape(1, D), w_cat, kones)

    nblocks = S // A_BLOCK
    a_rows = A_STEP_BLOCKS * A_BLOCK
    assert nblocks % A_STEP_BLOCKS == 0 and A_STEP_BLOCKS >= 2
    ya = pl.pallas_call(
        functools.partial(_win_attn_kernel, nblocks=nblocks),
        grid=(nblocks // A_STEP_BLOCKS,),
        in_specs=[pl.BlockSpec(memory_space=pltpu.SMEM),
                  pl.BlockSpec((a_rows, 512), lambda n: (n, 0)),
                  _resident(ka.shape),
                  _resident(va.shape),
                  _resident(bias_a.shape)],
        out_specs=pl.BlockSpec((a_rows, 512), lambda n: (n, 0)),
        out_shape=jax.ShapeDtypeStruct((S, 512), bf16),
        compiler_params=pltpu.CompilerParams(dimension_semantics=("arbitrary",),
                                             vmem_limit_bytes=VMEM_LIMIT),
        name="win_attn",
    )(p["a_sink"], qa, ka, va, bias_a)

    lamv = jnp.stack([p["lambda_q1"], p["lambda_k1"], p["lambda_q2"], p["lambda_k2"]])
    TQ, TK = B_QTILE, B_TILE
    assert S % TQ == 0 and TQ % TK == 0
    yb = pl.pallas_call(
        functools.partial(_diff_attn_kernel, nchunks=nchunks, lambda_init=lambda_init),
        grid=(B_HEADS, S // TQ),
        in_specs=[pl.BlockSpec((4, HEAD_DIM), lambda hh, i: (0, 0)),
                  pl.BlockSpec((1, B_V_DIM), lambda hh, i: (0, 0)),
                  pl.BlockSpec((TQ, LANES), lambda hh, i: (i, hh)),
                  pl.BlockSpec((S, 2 * LANES), lambda hh, i: (0, hh)),
                  pl.BlockSpec((1, nchunks, B_ACC_ROWS, TK), lambda hh, i: (hh, 0, 0, 0)),
                  pl.BlockSpec((1, 5, B_BIAS_BLOCK, B_BIAS_BLOCK), lambda hh, i: (hh, 0, 0, 0))],
        out_specs=pl.BlockSpec((TQ, LANES), lambda hh, i: (i, hh)),
        out_shape=jax.ShapeDtypeStruct((S, B_HEADS * B_V_DIM), bf16),
        scratch_shapes=[pltpu.VMEM((2, LANES, TQ), bf16),
                        pltpu.VMEM((2, LANES, TQ), bf16),
                        pltpu.VMEM((B_SLOTS, 2, TK, TQ), bf16),
                        pltpu.VMEM((2, 1, TQ), jnp.float32),
                        pltpu.VMEM((nchunks, 2, 1, TQ), jnp.float32),
                        pltpu.VMEM((2, 1, TQ), jnp.float32),
                        pltpu.VMEM((2, B_ACC_ROWS, TQ), jnp.float32)],
        compiler_params=pltpu.CompilerParams(dimension_semantics=("arbitrary", "arbitrary"),
                                             vmem_limit_bytes=VMEM_LIMIT),
        name="diff_attn",
    )(lamv, p["diff_subln_g"].reshape(1, B_V_DIM), qb, kb, vt, bias_b)

    w_out = p["w_out"].astype(bf16)
    a_width = A_Q_HEADS * HEAD_DIM
    d_ff = p["w_gate"].shape[1]
    R = FFN_ROW_TILE
    out = pl.pallas_call(
        _out_ffn_kernel,
        grid=(S // R,),
        in_specs=[pl.BlockSpec((R, 512), lambda i: (i, 0)),
                  pl.BlockSpec((R, 512), lambda i: (i, 0)),
                  pl.BlockSpec((R, D), lambda i: (i, 0)),
                  _resident((a_width, D)),
                  _resident((w_out.shape[0] - a_width, D)),
                  _resident((1, D)),
                  _resident((1, D)),
                  _resident((D, d_ff)),
                  _resident((D, d_ff)),
                  _resident((d_ff, D)),
                  _resident((1, D))],
        out_specs=pl.BlockSpec((R, D), lambda i: (i, 0)),
        out_shape=jax.ShapeDtypeStruct((S, D), jnp.float32),
        compiler_params=pltpu.CompilerParams(dimension_semantics=("arbitrary",),
                                             vmem_limit_bytes=VMEM_LIMIT),
        name="out_ffn",
    )(ya, yb, h, w_out[:a_width], w_out[a_width:], p["attn_post_g"].reshape(1, D),
      p["ffn_pre_g"].reshape(1, D), p["w_gate"].astype(bf16), p["w_up"].astype(bf16),
      p["w_down"].astype(bf16), p["ffn_post_g"].reshape(1, D))
    return out


def kernel(x, attn_pre_g, w_in, a_sink, lambda_q1, lambda_k1, lambda_q2, lambda_k2, diff_subln_g,
           rel_bias, w_out, attn_post_g, ffn_pre_g, w_gate, w_up, w_down, ffn_post_g):
    batch, S, D = x.shape
    depth = w_in.shape[0]
    assert S % B_TILE == 0 and S % A_BLOCK == 0 and S // A_BLOCK >= 2

    tab_a = _toeplitz_bias(rel_bias[:, :A_Q_HEADS], A_BLOCK, 3 * A_BLOCK, lambda t: -t - A_BLOCK)
    qi = np.arange(A_BLOCK)[:, None]
    kj = np.arange(3 * A_BLOCK)[None, :]
    in_window = np.abs(kj - A_BLOCK - qi) <= A_BLOCK
    valid = np.stack([in_window & (kj >= A_BLOCK), in_window, in_window & (kj < 2 * A_BLOCK)])
    bias_a = jnp.where(valid[:, None], tab_a[None], MASK_VALUE)

    nb = B_BIAS_BLOCK
    assert len(set(_t5_bucket_np(np.arange(nb + 1, 4 * nb)))) == 1
    tab_b = rel_bias[:, A_Q_HEADS:]
    near = [_toeplitz_bias(tab_b, nb, nb, lambda t, d=d: d * nb + t) for d in (-1, 0, 1)]
    far = tab_b[_t5_bucket_np(np.array([-2 * nb, 2 * nb]))].astype(jnp.float32)
    const = [jnp.broadcast_to(far[side][:, None, None], (B_HEADS, nb, nb)) for side in (0, 1)]
    bias_b = jnp.stack([const[0]] + near + [const[1]], axis=1) * LOG2E

    outs = []
    for b in range(batch):
        h = x[b]
        for l in range(depth):
            p = dict(attn_pre_g=attn_pre_g[l], w_in=w_in[l], a_sink=a_sink[l], lambda_q1=lambda_q1[l],
                     lambda_k1=lambda_k1[l], lambda_q2=lambda_q2[l], lambda_k2=lambda_k2[l],
                     diff_subln_g=diff_subln_g[l], w_out=w_out[l], attn_post_g=attn_post_g[l],
                     ffn_pre_g=ffn_pre_g[l], w_gate=w_gate[l], w_up=w_up[l], w_down=w_down[l],
                     ffn_post_g=ffn_post_g[l])
            h = _layer(h, l, p, bias_a, bias_b)
        outs.append(h)
    return jnp.stack(outs)
```

```python
import functools
import math

import numpy as np
import jax
import jax.numpy as jnp
from jax import lax
from jax.experimental import pallas as pl
from jax.experimental.pallas import tpu as pltpu

HEAD_DIM = 64
A_Q_HEADS = 8
A_KV_HEADS = 2
A_BLOCK = 128
A_STEP_BLOCKS = 4
B_HEADS = 4
B_V_DIM = 2 * HEAD_DIM
NUM_BUCKETS = 32
MAX_DISTANCE = 128
EPS = 1e-6
MASK_VALUE = -1e30
LOG2E = math.log2(math.e)
ROW_SUM_LIMIT = 2.0 ** 40

LANES = 128
ROW_TILE = 512
FFN_ROW_TILE = 512
B_TILE = 512
B_QTILE = 1024
B_REF_LANES = 3
B_UNROLL = 8
B_AHEAD = 2
B_SLOTS = 4
B_PROBE = 128
B_BIAS_BLOCK = 128
VMEM_LIMIT = 56 * 1024 * 1024

_NT = (((1,), (1,)), ((), ()))


def _t5_bucket_np(rel):
    nb = NUM_BUCKETS // 2
    max_exact = nb // 2
    ret = np.where(rel > 0, nb, 0)
    n = np.abs(rel)
    nf = np.maximum(n, 1).astype(np.float32)
    large = max_exact + (np.log(nf / np.float32(max_exact)) / np.float32(math.log(MAX_DISTANCE / max_exact))
                         * np.float32(nb - max_exact)).astype(np.int32)
    large = np.minimum(large, nb - 1)
    return (ret + np.where(n < max_exact, n, large)).astype(np.int32)


def _toeplitz_bias(table, rows, cols, rel_of):
    length = rows + cols
    u = np.arange(length)
    t = np.where(u < cols, -u, length - u)
    w = table[_t5_bucket_np(rel_of(t))].astype(jnp.float32).T
    x = jnp.tile(w, (1, rows))[:, :rows * (length - 1)].reshape(w.shape[0], rows, length - 1)
    return x[:, :, :cols]


def _rms(xf, g):
    return xf * lax.rsqrt(jnp.mean(xf * xf, axis=-1, keepdims=True) + EPS) * g


def _in_proj_kernel(x_ref, g_ref, w_ref, kones_ref, qa_ref, ka_ref, va_ref, qb_ref, kb_ref, vt_ref):
    u = _rms(x_ref[...], g_ref[...]).astype(jnp.bfloat16)
    proj = jnp.dot(u, w_ref[...], preferred_element_type=jnp.float32)
    qa_ref[...] = proj[:, 0:512].astype(jnp.bfloat16)
    ka_ref[...] = proj[:, 512:768].astype(jnp.bfloat16)
    va_ref[...] = proj[:, 768:1024].astype(jnp.bfloat16)
    qb_ref[...] = proj[:, 1024:1536].astype(jnp.bfloat16)
    kb_ref[...] = (proj[:, 1536:2560] + kones_ref[...]).astype(jnp.bfloat16)
    for h in range(B_HEADS):
        v = proj[:, 2560 + h * B_V_DIM:2560 + (h + 1) * B_V_DIM]
        vt_ref[h, 0] = v.T.astype(jnp.bfloat16)


def _win_attn_kernel(sink_ref, q_ref, k_ref, v_ref, bias_ref, o_ref, *, nblocks):
    f32 = jnp.float32
    bf16 = jnp.bfloat16
    n = pl.program_id(0)
    rows = A_STEP_BLOCKS * A_BLOCK
    seq = k_ref.shape[0]
    group = A_Q_HEADS // A_KV_HEADS
    start = pl.multiple_of(n * rows, rows)
    prev = pl.multiple_of(jnp.maximum(start - A_BLOCK, 0), A_BLOCK)
    nxt = pl.multiple_of(jnp.minimum(start + rows, seq - A_BLOCK), A_BLOCK)
    kw = jnp.concatenate([k_ref[pl.ds(prev, A_BLOCK), :], k_ref[pl.ds(start, rows), :],
                          k_ref[pl.ds(nxt, A_BLOCK), :]], axis=0)
    vw = jnp.concatenate([v_ref[pl.ds(prev, A_BLOCK), :], v_ref[pl.ds(start, rows), :],
                          v_ref[pl.ds(nxt, A_BLOCK), :]], axis=0)
    lane = lax.broadcasted_iota(jnp.int32, (A_BLOCK, LANES), 1)
    low = lane < HEAD_DIM
    ones = jnp.ones((3 * A_BLOCK, LANES), bf16)
    for b in range(A_STEP_BLOCKS):
        blk = n * A_STEP_BLOCKS + b
        if b == 0:
            variant = jnp.where(blk == 0, 0, 1)
        elif b == A_STEP_BLOCKS - 1:
            variant = jnp.where(blk == nblocks - 1, 2, 1)
        else:
            variant = 1
        q = q_ref[b * A_BLOCK:(b + 1) * A_BLOCK, :]
        for g in range(A_KV_HEADS):
            heads = range(g * group, (g + 1) * group)
            qs = []
            for h in heads:
                q2 = q[:, (h // 2) * LANES:(h // 2 + 1) * LANES]
                qs.append(jnp.where(low if h % 2 == 0 else jnp.logical_not(low), q2, jnp.zeros_like(q2)))
            q4 = jnp.concatenate(qs, axis=0)
            kg = kw[b * A_BLOCK:(b + 3) * A_BLOCK, g * LANES:(g + 1) * LANES]
            vg = jnp.concatenate([vw[b * A_BLOCK:(b + 3) * A_BLOCK, g * LANES:(g + 1) * LANES], ones],
                                 axis=1)
            s = lax.dot_general(q4, kg, _NT, preferred_element_type=f32)
            s = s + bias_ref[variant, g * group:(g + 1) * group].reshape(group * A_BLOCK, 3 * A_BLOCK)
            snk = jnp.concatenate([jnp.full((A_BLOCK, 1), sink_ref[h], f32) for h in heads], axis=0)
            m = jnp.maximum(jnp.max(s, axis=-1, keepdims=True), snk)
            p = jnp.exp(s - m).astype(bf16)
            res = jnp.dot(p, vg, preferred_element_type=f32)
            on = res[:, 0:LANES] / (res[:, LANES:2 * LANES] + jnp.exp(snk - m))
            for t in range(group // 2):
                even = on[(2 * t) * A_BLOCK:(2 * t + 1) * A_BLOCK]
                odd = on[(2 * t + 1) * A_BLOCK:(2 * t + 2) * A_BLOCK]
                pair = (g * group) // 2 + t
                o_ref[b * A_BLOCK:(b + 1) * A_BLOCK, pair * LANES:(pair + 1) * LANES] = (
                    jnp.where(low, even, odd).astype(o_ref.dtype))


def _diff_attn_kernel(lamv_ref, g_ref, q_ref, k_ref, vt_ref, bias_ref, o_ref,
                      qz_ref, qp_ref, p_ref, psum_ref, r_ref, flag_ref, any_ref, acc_ref, den_ref,
                      *, nchunks, lambda_init):
    TQ = B_QTILE
    TK = B_TILE
    i = pl.program_id(1)
    f32 = jnp.float32
    bf16 = jnp.bfloat16

    qt = q_ref[...].astype(f32).T
    row = lax.broadcasted_iota(jnp.int32, (LANES, TQ), 0)
    is_q = (row < HEAD_DIM, row >= HEAD_DIM)
    ref_row = (HEAD_DIM, 0)

    ksub = TK // B_BIAS_BLOCK
    qsub = TQ // B_BIAS_BLOCK

    def bias_tile(j, rows=ksub):
        base = j * ksub - i * qsub
        by_dist = {d: bias_ref[0, jnp.clip(base + d, -2, 2) + 2] for d in range(1 - qsub, rows)}
        return jnp.concatenate(
            [jnp.concatenate([by_dist[a - b] for b in range(qsub)], axis=1) for a in range(rows)], axis=0)

    def k_chunk(j, c):
        return k_ref[pl.ds(pl.multiple_of(j * TK, TK), TK), c * LANES:(c + 1) * LANES]

    def set_reference(c, r_row):
        hi = r_row.astype(bf16).astype(f32)
        rem = r_row - hi
        mid = rem.astype(bf16).astype(f32)
        low = (rem - mid).astype(bf16).astype(f32)
        first = ref_row[c]
        ext = jnp.where(row == first, -hi, jnp.where(row == first + 1, -mid,
                                                     jnp.where(row == first + 2, -low, 0.0)))
        qp_ref[c] = jnp.where(is_q[c], qt, ext).astype(bf16)

    def exact_chunk(j, c, flagged):
        s = jnp.dot(k_chunk(j, c), qz_ref[c], preferred_element_type=f32) + bias_tile(j)
        r_old = r_ref[c]
        r_new = jnp.where(flagged, jnp.maximum(r_old, jnp.max(s, axis=0, keepdims=True)), r_old)
        p = jnp.where(flagged, jnp.exp2(s - r_new), 0.0)
        pv = jnp.dot(vt_ref[0, j], p.astype(bf16), preferred_element_type=f32)
        alpha = jnp.exp2(r_old - r_new)
        acc_ref[c] = acc_ref[c] * alpha + pv
        den_ref[c] = den_ref[c] * alpha + jnp.sum(p, axis=0, keepdims=True)
        r_ref[c] = r_new

    def stage_a(j, slot):
        bias = bias_tile(j)
        for c in range(2):
            s = jnp.dot(k_chunk(j, c), qp_ref[c], preferred_element_type=f32) + bias
            p = jnp.exp2(s)
            psum_ref[slot, c] = jnp.sum(p, axis=0, keepdims=True)
            p_ref[slot, c] = p.astype(bf16)

    def stage_c(j, slot):
        vt = vt_ref[0, j]
        for c in range(2):
            pv = jnp.dot(vt, p_ref[slot, c], preferred_element_type=f32)
            psum = psum_ref[slot, c]
            ok = psum <= ROW_SUM_LIMIT
            acc_ref[c] = acc_ref[c] + jnp.where(ok, pv, 0.0)
            den_ref[c] = den_ref[c] + jnp.where(ok, psum, 0.0)
            flag = jnp.where(ok, 0.0, 1.0)
            flag_ref[j, c] = flag
            any_ref[c] = jnp.maximum(any_ref[c], flag)

    for c in range(2):
        qz_ref[c] = jnp.where(is_q[c], qt, 0.0).astype(bf16)
    acc_ref[...] = jnp.zeros(acc_ref.shape, f32)
    den_ref[...] = jnp.zeros(den_ref.shape, f32)
    any_ref[...] = jnp.zeros(any_ref.shape, f32)
    diag = i * (TQ // TK)
    probe_bias = bias_tile(diag, rows=B_PROBE // B_BIAS_BLOCK)
    for c in range(2):
        kp = k_ref[pl.ds(pl.multiple_of(diag * TK, TK), B_PROBE), c * LANES:(c + 1) * LANES]
        s = jnp.dot(kp, qz_ref[c], preferred_element_type=f32) + probe_bias
        r = jnp.max(s, axis=0, keepdims=True)
        r_ref[c] = r
        set_reference(c, r)

    def run(j0, has_next):
        for u in range(B_UNROLL):
            stage_c(j0 + u, u % B_SLOTS)
            if u + B_AHEAD < B_UNROLL or has_next:
                stage_a(j0 + u + B_AHEAD, (u + B_AHEAD) % B_SLOTS)

    nbody = nchunks // B_UNROLL
    for u in range(B_AHEAD):
        stage_a(u, u)

    def body(b, carry):
        run(b * B_UNROLL, True)
        return carry

    lax.fori_loop(0, nbody - 1, body, 0)
    run((nbody - 1) * B_UNROLL, False)

    @pl.when(jnp.max(jnp.maximum(any_ref[0], any_ref[1])) > 0.0)
    def _():
        def redo(j, carry):
            @pl.when(jnp.max(jnp.maximum(flag_ref[j, 0], flag_ref[j, 1])) > 0.0)
            def _():
                for c in range(2):
                    exact_chunk(j, c, flag_ref[j, c] > 0.0)
            return carry

        lax.fori_loop(0, nchunks, redo, 0)

    lv = lamv_ref[...]
    lam = (jnp.exp(jnp.sum(lv[0:1] * lv[1:2], axis=-1, keepdims=True))
           - jnp.exp(jnp.sum(lv[2:3] * lv[3:4], axis=-1, keepdims=True)) + lambda_init)
    ot = acc_ref[0] / den_ref[0] - lam * (acc_ref[1] / den_ref[1])
    o = ot.T
    o_ref[...] = (_rms(o, g_ref[...]) * (1.0 - lambda_init)).astype(o_ref.dtype)


def _out_ffn_kernel(ya_ref, yb_ref, x_ref, woa_ref, wob_ref, gpost_ref, gpre_ref,
                    wg_ref, wu_ref, wd_ref, gfpost_ref, o_ref):
    y = (jnp.dot(ya_ref[...], woa_ref[...], preferred_element_type=jnp.float32)
         + jnp.dot(yb_ref[...], wob_ref[...], preferred_element_type=jnp.float32))
    h1 = x_ref[...] + _rms(y, gpost_ref[...])
    u = _rms(h1, gpre_ref[...]).astype(jnp.bfloat16)
    gate = jnp.dot(u, wg_ref[...], preferred_element_type=jnp.float32)
    up = jnp.dot(u, wu_ref[...], preferred_element_type=jnp.float32)
    act = (gate * jax.nn.sigmoid(gate) * up).astype(jnp.bfloat16)
    f = jnp.dot(act, wd_ref[...], preferred_element_type=jnp.float32)
    o_ref[...] = h1 + _rms(f, gfpost_ref[...])


def _resident(shape):
    zeros = (0,) * len(shape)
    return pl.BlockSpec(shape, lambda *_: zeros, pipeline_mode=pl.Buffered(1))


def _layer(h, l, p, bias_a, bias_b):
    S, D = h.shape
    bf16 = jnp.bfloat16
    lambda_init = 0.8 - 0.6 * math.exp(-0.3 * l)
    scale = HEAD_DIM ** -0.5

    w = p["w_in"]
    a_q, a_kv, b_qk = A_Q_HEADS * HEAD_DIM, A_KV_HEADS * HEAD_DIM, B_HEADS * 2 * HEAD_DIM
    c0 = a_q
    c1 = c0 + a_kv
    c2 = c1 + a_kv
    c3 = c2 + b_qk
    c4 = c3 + b_qk

    def dup(cols):
        parts = []
        for g in range(A_KV_HEADS):
            blk = cols[:, g * HEAD_DIM:(g + 1) * HEAD_DIM]
            parts += [blk, blk]
        return jnp.concatenate(parts, axis=1)

    def widen_b_keys(cols):
        wk = cols.reshape(D, B_HEADS, 2, HEAD_DIM)
        z = jnp.zeros_like(wk[:, :, 0])
        return jnp.stack([jnp.concatenate([wk[:, :, 0], z], axis=-1),
                          jnp.concatenate([z, wk[:, :, 1]], axis=-1)], axis=2).reshape(D, B_HEADS * 2 * LANES)

    kones = np.zeros((B_HEADS, 2, LANES), np.float32)
    kones[:, 0, HEAD_DIM:HEAD_DIM + B_REF_LANES] = 1.0
    kones[:, 1, 0:B_REF_LANES] = 1.0
    kones = jnp.asarray(kones.reshape(1, B_HEADS * 2 * LANES))

    w_cat = jnp.concatenate([w[:, :c0] * scale, dup(w[:, c0:c1]), dup(w[:, c1:c2]),
                             w[:, c2:c3] * (scale * LOG2E), widen_b_keys(w[:, c3:c4]), w[:, c4:]],
                            axis=1).astype(bf16)
    ncols = w_cat.shape[1]
    nrow = S // ROW_TILE
    nchunks = S // B_TILE
    assert ROW_TILE == B_TILE and nchunks % B_UNROLL == 0
    assert B_UNROLL % B_SLOTS == 0 and B_AHEAD < B_SLOTS and B_AHEAD <= B_UNROLL
    kb_cols = B_HEADS * 2 * LANES

    qa, ka, va, qb, kb, vt = pl.pallas_call(
        _in_proj_kernel,
        grid=(nrow,),
        in_specs=[pl.BlockSpec((ROW_TILE, D), lambda i: (i, 0)),
                  _resident((1, D)),
                  _resident((D, ncols)),
                  _resident((1, kb_cols))],
        out_specs=[pl.BlockSpec((ROW_TILE, 512), lambda i: (i, 0)),
                   pl.BlockSpec((ROW_TILE, 256), lambda i: (i, 0)),
                   pl.BlockSpec((ROW_TILE, 256), lambda i: (i, 0)),
                   pl.BlockSpec((ROW_TILE, 512), lambda i: (i, 0)),
                   pl.BlockSpec((ROW_TILE, kb_cols), lambda i: (i, 0)),
                   pl.BlockSpec((B_HEADS, 1, B_V_DIM, ROW_TILE), lambda i: (0, i, 0, 0))],
        out_shape=[jax.ShapeDtypeStruct((S, 512), bf16),
                   jax.ShapeDtypeStruct((S, 256), bf16),
                   jax.ShapeDtypeStruct((S, 256), bf16),
                   jax.ShapeDtypeStruct((S, 512), bf16),
                   jax.ShapeDtypeStruct((S, kb_cols), bf16),
                   jax.ShapeDtypeStruct((B_HEADS, nchunks, B_V_DIM, B_TILE), bf16)],
        compiler_params=pltpu.CompilerParams(dimension_semantics=("arbitrary",),
                                             vmem_limit_bytes=VMEM_LIMIT),
        name="in_proj",
    )(h, p["attn_pre_g"].reshape(1, D), w_cat, kones)

    nblocks = S // A_BLOCK
    a_rows = A_STEP_BLOCKS * A_BLOCK
    assert nblocks % A_STEP_BLOCKS == 0 and A_STEP_BLOCKS >= 2
    ya = pl.pallas_call(
        functools.partial(_win_attn_kernel, nblocks=nblocks),
        grid=(nblocks // A_STEP_BLOCKS,),
        in_specs=[pl.BlockSpec(memory_space=pltpu.SMEM),
                  pl.BlockSpec((a_rows, 512), lambda n: (n, 0)),
                  _resident(ka.shape),
                  _resident(va.shape),
                  _resident(bias_a.shape)],
        out_specs=pl.BlockSpec((a_rows, 512), lambda n: (n, 0)),
        out_shape=jax.ShapeDtypeStruct((S, 512), bf16),
        compiler_params=pltpu.CompilerParams(dimension_semantics=("arbitrary",),
                                             vmem_limit_bytes=VMEM_LIMIT),
        name="win_attn",
    )(p["a_sink"], qa, ka, va, bias_a)

    lamv = jnp.stack([p["lambda_q1"], p["lambda_k1"], p["lambda_q2"], p["lambda_k2"]])
    TQ, TK = B_QTILE, B_TILE
    assert S % TQ == 0 and TQ % TK == 0
    yb = pl.pallas_call(
        functools.partial(_diff_attn_kernel, nchunks=nchunks, lambda_init=lambda_init),
        grid=(B_HEADS, S // TQ),
        in_specs=[pl.BlockSpec((4, HEAD_DIM), lambda hh, i: (0, 0)),
                  pl.BlockSpec((1, B_V_DIM), lambda hh, i: (0, 0)),
                  pl.BlockSpec((TQ, LANES), lambda hh, i: (i, hh)),
                  pl.BlockSpec((S, 2 * LANES), lambda hh, i: (0, hh)),
                  pl.BlockSpec((1, nchunks, B_V_DIM, TK), lambda hh, i: (hh, 0, 0, 0)),
                  pl.BlockSpec((1, 5, B_BIAS_BLOCK, B_BIAS_BLOCK), lambda hh, i: (hh, 0, 0, 0))],
        out_specs=pl.BlockSpec((TQ, LANES), lambda hh, i: (i, hh)),
        out_shape=jax.ShapeDtypeStruct((S, B_HEADS * B_V_DIM), bf16),
        scratch_shapes=[pltpu.VMEM((2, LANES, TQ), bf16),
                        pltpu.VMEM((2, LANES, TQ), bf16),
                        pltpu.VMEM((B_SLOTS, 2, TK, TQ), bf16),
                        pltpu.VMEM((B_SLOTS, 2, 1, TQ), jnp.float32),
                        pltpu.VMEM((2, 1, TQ), jnp.float32),
                        pltpu.VMEM((nchunks, 2, 1, TQ), jnp.float32),
                        pltpu.VMEM((2, 1, TQ), jnp.float32),
                        pltpu.VMEM((2, B_V_DIM, TQ), jnp.float32),
                        pltpu.VMEM((2, 1, TQ), jnp.float32)],
        compiler_params=pltpu.CompilerParams(dimension_semantics=("arbitrary", "arbitrary"),
                                             vmem_limit_bytes=VMEM_LIMIT),
        name="diff_attn",
    )(lamv, p["diff_subln_g"].reshape(1, B_V_DIM), qb, kb, vt, bias_b)

    w_out = p["w_out"].astype(bf16)
    a_width = A_Q_HEADS * HEAD_DIM
    d_ff = p["w_gate"].shape[1]
    R = FFN_ROW_TILE
    out = pl.pallas_call(
        _out_ffn_kernel,
        grid=(S // R,),
        in_specs=[pl.BlockSpec((R, 512), lambda i: (i, 0)),
                  pl.BlockSpec((R, 512), lambda i: (i, 0)),
                  pl.BlockSpec((R, D), lambda i: (i, 0)),
                  _resident((a_width, D)),
                  _resident((w_out.shape[0] - a_width, D)),
                  _resident((1, D)),
                  _resident((1, D)),
                  _resident((D, d_ff)),
                  _resident((D, d_ff)),
                  _resident((d_ff, D)),
                  _resident((1, D))],
        out_specs=pl.BlockSpec((R, D), lambda i: (i, 0)),
        out_shape=jax.ShapeDtypeStruct((S, D), jnp.float32),
        compiler_params=pltpu.CompilerParams(dimension_semantics=("arbitrary",),
                                             vmem_limit_bytes=VMEM_LIMIT),
        name="out_ffn",
    )(ya, yb, h, w_out[:a_width], w_out[a_width:], p["attn_post_g"].reshape(1, D),
      p["ffn_pre_g"].reshape(1, D), p["w_gate"].astype(bf16), p["w_up"].astype(bf16),
      p["w_down"].astype(bf16), p["ffn_post_g"].reshape(1, D))
    return out


def kernel(x, attn_pre_g, w_in, a_sink, lambda_q1, lambda_k1, lambda_q2, lambda_k2, diff_subln_g,
           rel_bias, w_out, attn_post_g, ffn_pre_g, w_gate, w_up, w_down, ffn_post_g):
    batch, S, D = x.shape
    depth = w_in.shape[0]
    assert S % B_TILE == 0 and S % A_BLOCK == 0 and S // A_BLOCK >= 2

    tab_a = _toeplitz_bias(rel_bias[:, :A_Q_HEADS], A_BLOCK, 3 * A_BLOCK, lambda t: -t - A_BLOCK)
    qi = np.arange(A_BLOCK)[:, None]
    kj = np.arange(3 * A_BLOCK)[None, :]
    in_window = np.abs(kj - A_BLOCK - qi) <= A_BLOCK
    valid = np.stack([in_window & (kj >= A_BLOCK), in_window, in_window & (kj < 2 * A_BLOCK)])
    bias_a = jnp.where(valid[:, None], tab_a[None], MASK_VALUE)

    nb = B_BIAS_BLOCK
    assert len(set(_t5_bucket_np(np.arange(nb + 1, 4 * nb)))) == 1
    tab_b = rel_bias[:, A_Q_HEADS:]
    near = [_toeplitz_bias(tab_b, nb, nb, lambda t, d=d: d * nb + t) for d in (-1, 0, 1)]
    far = tab_b[_t5_bucket_np(np.array([-2 * nb, 2 * nb]))].astype(jnp.float32)
    const = [jnp.broadcast_to(far[side][:, None, None], (B_HEADS, nb, nb)) for side in (0, 1)]
    bias_b = jnp.stack([const[0]] + near + [const[1]], axis=1) * LOG2E

    outs = []
    for b in range(batch):
        h = x[b]
        for l in range(depth):
            p = dict(attn_pre_g=attn_pre_g[l], w_in=w_in[l], a_sink=a_sink[l], lambda_q1=lambda_q1[l],
                     lambda_k1=lambda_k1[l], lambda_q2=lambda_q2[l], lambda_k2=lambda_k2[l],
                     diff_subln_g=diff_subln_g[l], w_out=w_out[l], attn_post_g=attn_post_g[l],
                     ffn_pre_g=ffn_pre_g[l], w_gate=w_gate[l], w_up=w_up[l], w_down=w_down[l],
                     ffn_post_g=ffn_post_g[l])
            h = _layer(h, l, p, bias_a, bias_b)
        outs.append(h)
    return jnp.stack(outs)
```

```python
import functools
import math

import numpy as np
import jax
import jax.numpy as jnp
from jax import lax
from jax.experimental import pallas as pl
from jax.experimental.pallas import tpu as pltpu

HEAD_DIM = 64
A_Q_HEADS = 8
A_KV_HEADS = 2
A_BLOCK = 128
A_STEP_BLOCKS = 4
B_HEADS = 4
B_V_DIM = 2 * HEAD_DIM
NUM_BUCKETS = 32
MAX_DISTANCE = 128
EPS = 1e-6
MASK_VALUE = -1e30
LOG2E = math.log2(math.e)
ROW_SUM_LIMIT = 2.0 ** 40

LANES = 128
ROW_TILE = 512
FFN_ROW_TILE = 512
B_TILE = 512
B_QTILE = 1024
B_REF_LANES = 3
B_UNROLL = 8
B_AHEAD = 2
B_SLOTS = 4
B_PROBE = 128
B_BIAS_BLOCK = 128
VMEM_LIMIT = 56 * 1024 * 1024

_NT = (((1,), (1,)), ((), ()))


def _t5_bucket_np(rel):
    nb = NUM_BUCKETS // 2
    max_exact = nb // 2
    ret = np.where(rel > 0, nb, 0)
    n = np.abs(rel)
    nf = np.maximum(n, 1).astype(np.float32)
    large = max_exact + (np.log(nf / np.float32(max_exact)) / np.float32(math.log(MAX_DISTANCE / max_exact))
                         * np.float32(nb - max_exact)).astype(np.int32)
    large = np.minimum(large, nb - 1)
    return (ret + np.where(n < max_exact, n, large)).astype(np.int32)


def _toeplitz_bias(table, rows, cols, rel_of):
    length = rows + cols
    u = np.arange(length)
    t = np.where(u < cols, -u, length - u)
    w = table[_t5_bucket_np(rel_of(t))].astype(jnp.float32).T
    x = jnp.tile(w, (1, rows))[:, :rows * (length - 1)].reshape(w.shape[0], rows, length - 1)
    return x[:, :, :cols]


def _rms(xf, g):
    return xf * lax.rsqrt(jnp.mean(xf * xf, axis=-1, keepdims=True) + EPS) * g


def _in_proj_kernel(x_ref, g_ref, w_ref, kones_ref, qa_ref, ka_ref, va_ref, qb_ref, kb_ref, vt_ref):
    u = _rms(x_ref[...], g_ref[...]).astype(jnp.bfloat16)
    proj = jnp.dot(u, w_ref[...], preferred_element_type=jnp.float32)
    qa_ref[...] = proj[:, 0:512].astype(jnp.bfloat16)
    ka_ref[...] = proj[:, 512:768].astype(jnp.bfloat16)
    va_ref[...] = proj[:, 768:1024].astype(jnp.bfloat16)
    qb_ref[...] = proj[:, 1024:1536].astype(jnp.bfloat16)
    lane = lax.broadcasted_iota(jnp.int32, (proj.shape[0], LANES), 1)
    keep = (lane < HEAD_DIM, lane >= HEAD_DIM)
    for h in range(B_HEADS):
        kpair = proj[:, 1536 + h * LANES:1536 + (h + 1) * LANES]
        for c in range(2):
            col = (2 * h + c) * LANES
            kb_ref[:, col:col + LANES] = jnp.where(keep[c], kpair, kones_ref[:, col:col + LANES]).astype(jnp.bfloat16)
        v = proj[:, 2048 + h * B_V_DIM:2048 + (h + 1) * B_V_DIM]
        vt_ref[h, 0] = v.T.astype(jnp.bfloat16)


def _win_attn_kernel(sink_ref, q_ref, k_ref, v_ref, bias_ref, o_ref, *, nblocks):
    f32 = jnp.float32
    bf16 = jnp.bfloat16
    n = pl.program_id(0)
    rows = A_STEP_BLOCKS * A_BLOCK
    seq = k_ref.shape[0]
    group = A_Q_HEADS // A_KV_HEADS
    start = pl.multiple_of(n * rows, rows)
    prev = pl.multiple_of(jnp.maximum(start - A_BLOCK, 0), A_BLOCK)
    nxt = pl.multiple_of(jnp.minimum(start + rows, seq - A_BLOCK), A_BLOCK)
    kw = jnp.concatenate([k_ref[pl.ds(prev, A_BLOCK), :], k_ref[pl.ds(start, rows), :],
                          k_ref[pl.ds(nxt, A_BLOCK), :]], axis=0)
    vw = jnp.concatenate([v_ref[pl.ds(prev, A_BLOCK), :], v_ref[pl.ds(start, rows), :],
                          v_ref[pl.ds(nxt, A_BLOCK), :]], axis=0)
    lane = lax.broadcasted_iota(jnp.int32, (A_BLOCK, LANES), 1)
    low = lane < HEAD_DIM
    ones = jnp.ones((3 * A_BLOCK, LANES), bf16)
    for b in range(A_STEP_BLOCKS):
        blk = n * A_STEP_BLOCKS + b
        if b == 0:
            variant = jnp.where(blk == 0, 0, 1)
        elif b == A_STEP_BLOCKS - 1:
            variant = jnp.where(blk == nblocks - 1, 2, 1)
        else:
            variant = 1
        q = q_ref[b * A_BLOCK:(b + 1) * A_BLOCK, :]
        for g in range(A_KV_HEADS):
            heads = range(g * group, (g + 1) * group)
            qs = []
            for h in heads:
                q2 = q[:, (h // 2) * LANES:(h // 2 + 1) * LANES]
                qs.append(jnp.where(low if h % 2 == 0 else jnp.logical_not(low), q2, jnp.zeros_like(q2)))
            q4 = jnp.concatenate(qs, axis=0)
            kg = kw[b * A_BLOCK:(b + 3) * A_BLOCK, g * LANES:(g + 1) * LANES]
            vg = jnp.concatenate([vw[b * A_BLOCK:(b + 3) * A_BLOCK, g * LANES:(g + 1) * LANES], ones],
                                 axis=1)
            s = lax.dot_general(q4, kg, _NT, preferred_element_type=f32)
            s = s + bias_ref[variant, g * group:(g + 1) * group].reshape(group * A_BLOCK, 3 * A_BLOCK)
            snk = jnp.concatenate([jnp.full((A_BLOCK, 1), sink_ref[h] * LOG2E, f32) for h in heads], axis=0)
            m = jnp.maximum(jnp.max(s, axis=-1, keepdims=True), snk)
            p = jnp.exp2(s - m).astype(bf16)
            res = jnp.dot(p, vg, preferred_element_type=f32)
            on = res[:, 0:LANES] / (res[:, LANES:2 * LANES] + jnp.exp2(snk - m))
            for t in range(group // 2):
                even = on[(2 * t) * A_BLOCK:(2 * t + 1) * A_BLOCK]
                odd = on[(2 * t + 1) * A_BLOCK:(2 * t + 2) * A_BLOCK]
                pair = (g * group) // 2 + t
                o_ref[b * A_BLOCK:(b + 1) * A_BLOCK, pair * LANES:(pair + 1) * LANES] = (
                    jnp.where(low, even, odd).astype(o_ref.dtype))


def _diff_attn_kernel(lamv_ref, g_ref, q_ref, k_ref, vt_ref, bias_ref, o_ref,
                      qz_ref, qp_ref, p_ref, psum_ref, r_ref, flag_ref, any_ref, acc_ref, den_ref,
                      *, nchunks, lambda_init):
    TQ = B_QTILE
    TK = B_TILE
    i = pl.program_id(1)
    f32 = jnp.float32
    bf16 = jnp.bfloat16

    qt = q_ref[...].astype(f32).T
    row = lax.broadcasted_iota(jnp.int32, (LANES, TQ), 0)
    is_q = (row < HEAD_DIM, row >= HEAD_DIM)
    ref_row = (HEAD_DIM, 0)

    ksub = TK // B_BIAS_BLOCK
    qsub = TQ // B_BIAS_BLOCK

    def bias_tile(j, rows=ksub):
        base = j * ksub - i * qsub
        by_dist = {d: bias_ref[0, jnp.clip(base + d, -2, 2) + 2] for d in range(1 - qsub, rows)}
        return jnp.concatenate(
            [jnp.concatenate([by_dist[a - b] for b in range(qsub)], axis=1) for a in range(rows)], axis=0)

    def k_chunk(j, c):
        return k_ref[pl.ds(pl.multiple_of(j * TK, TK), TK), c * LANES:(c + 1) * LANES]

    def set_reference(c, r_row):
        hi = r_row.astype(bf16).astype(f32)
        rem = r_row - hi
        mid = rem.astype(bf16).astype(f32)
        low = (rem - mid).astype(bf16).astype(f32)
        first = ref_row[c]
        ext = jnp.where(row == first, -hi, jnp.where(row == first + 1, -mid,
                                                     jnp.where(row == first + 2, -low, 0.0)))
        qp_ref[c] = jnp.where(is_q[c], qt, ext).astype(bf16)

    def exact_chunk(j, c, flagged):
        s = jnp.dot(k_chunk(j, c), qz_ref[c], preferred_element_type=f32) + bias_tile(j)
        r_old = r_ref[c]
        r_new = jnp.where(flagged, jnp.maximum(r_old, jnp.max(s, axis=0, keepdims=True)), r_old)
        p = jnp.where(flagged, jnp.exp2(s - r_new), 0.0)
        pv = jnp.dot(vt_ref[0, j], p.astype(bf16), preferred_element_type=f32)
        alpha = jnp.exp2(r_old - r_new)
        acc_ref[c] = acc_ref[c] * alpha + pv
        den_ref[c] = den_ref[c] * alpha + jnp.sum(p, axis=0, keepdims=True)
        r_ref[c] = r_new

    def stage_a(j, slot):
        bias = bias_tile(j)
        for c in range(2):
            s = jnp.dot(k_chunk(j, c), qp_ref[c], preferred_element_type=f32) + bias
            p = jnp.exp2(s)
            psum_ref[slot, c] = jnp.sum(p, axis=0, keepdims=True)
            p_ref[slot, c] = p.astype(bf16)

    def stage_c(j, slot):
        vt = vt_ref[0, j]
        for c in range(2):
            pv = jnp.dot(vt, p_ref[slot, c], preferred_element_type=f32)
            psum = psum_ref[slot, c]
            ok = psum <= ROW_SUM_LIMIT
            acc_ref[c] = acc_ref[c] + jnp.where(ok, pv, 0.0)
            den_ref[c] = den_ref[c] + jnp.where(ok, psum, 0.0)
            flag = jnp.where(ok, 0.0, 1.0)
            flag_ref[j, c] = flag
            any_ref[c] = jnp.maximum(any_ref[c], flag)

    for c in range(2):
        qz_ref[c] = jnp.where(is_q[c], qt, 0.0).astype(bf16)
    acc_ref[...] = jnp.zeros(acc_ref.shape, f32)
    den_ref[...] = jnp.zeros(den_ref.shape, f32)
    any_ref[...] = jnp.zeros(any_ref.shape, f32)
    diag = i * (TQ // TK)
    probe_bias = bias_tile(diag, rows=B_PROBE // B_BIAS_BLOCK)
    for c in range(2):
        kp = k_ref[pl.ds(pl.multiple_of(diag * TK, TK), B_PROBE), c * LANES:(c + 1) * LANES]
        s = jnp.dot(kp, qz_ref[c], preferred_element_type=f32) + probe_bias
        r = jnp.max(s, axis=0, keepdims=True)
        r_ref[c] = r
        set_reference(c, r)

    def run(j0, has_next):
        for u in range(B_UNROLL):
            stage_c(j0 + u, u % B_SLOTS)
            if u + B_AHEAD < B_UNROLL or has_next:
                stage_a(j0 + u + B_AHEAD, (u + B_AHEAD) % B_SLOTS)

    nbody = nchunks // B_UNROLL
    for u in range(B_AHEAD):
        stage_a(u, u)

    def body(b, carry):
        run(b * B_UNROLL, True)
        return carry

    lax.fori_loop(0, nbody - 1, body, 0)
    run((nbody - 1) * B_UNROLL, False)

    @pl.when(jnp.max(jnp.maximum(any_ref[0], any_ref[1])) > 0.0)
    def _():
        def redo(j, carry):
            @pl.when(jnp.max(jnp.maximum(flag_ref[j, 0], flag_ref[j, 1])) > 0.0)
            def _():
                for c in range(2):
                    exact_chunk(j, c, flag_ref[j, c] > 0.0)
            return carry

        lax.fori_loop(0, nchunks, redo, 0)

    lv = lamv_ref[...]
    lam = (jnp.exp(jnp.sum(lv[0:1] * lv[1:2], axis=-1, keepdims=True))
           - jnp.exp(jnp.sum(lv[2:3] * lv[3:4], axis=-1, keepdims=True)) + lambda_init)
    ot = acc_ref[0] / den_ref[0] - lam * (acc_ref[1] / den_ref[1])
    o = ot.T
    o_ref[...] = (_rms(o, g_ref[...]) * (1.0 - lambda_init)).astype(o_ref.dtype)


def _out_ffn_kernel(ya_ref, yb_ref, x_ref, woa_ref, wob_ref, gpost_ref, gpre_ref,
                    wg_ref, wu_ref, wd_ref, gfpost_ref, o_ref):
    y = (jnp.dot(ya_ref[...], woa_ref[...], preferred_element_type=jnp.float32)
         + jnp.dot(yb_ref[...], wob_ref[...], preferred_element_type=jnp.float32))
    h1 = x_ref[...] + _rms(y, gpost_ref[...])
    u = _rms(h1, gpre_ref[...]).astype(jnp.bfloat16)
    gate = jnp.dot(u, wg_ref[...], preferred_element_type=jnp.float32)
    up = jnp.dot(u, wu_ref[...], preferred_element_type=jnp.float32)
    act = (gate * jax.nn.sigmoid(gate) * up).astype(jnp.bfloat16)
    f = jnp.dot(act, wd_ref[...], preferred_element_type=jnp.float32)
    o_ref[...] = h1 + _rms(f, gfpost_ref[...])


def _resident(shape):
    zeros = (0,) * len(shape)
    return pl.BlockSpec(shape, lambda *_: zeros, pipeline_mode=pl.Buffered(1))


def _layer(h, l, p, bias_a, bias_b):
    S, D = h.shape
    bf16 = jnp.bfloat16
    lambda_init = 0.8 - 0.6 * math.exp(-0.3 * l)
    scale = HEAD_DIM ** -0.5

    w = p["w_in"]
    a_q, a_kv, b_qk = A_Q_HEADS * HEAD_DIM, A_KV_HEADS * HEAD_DIM, B_HEADS * 2 * HEAD_DIM
    c0 = a_q
    c1 = c0 + a_kv
    c2 = c1 + a_kv
    c3 = c2 + b_qk

    def dup(cols):
        parts = []
        for g in range(A_KV_HEADS):
            blk = cols[:, g * HEAD_DIM:(g + 1) * HEAD_DIM]
            parts += [blk, blk]
        return jnp.concatenate(parts, axis=1)

    kones = np.zeros((B_HEADS, 2, LANES), np.float32)
    kones[:, 0, HEAD_DIM:HEAD_DIM + B_REF_LANES] = 1.0
    kones[:, 1, 0:B_REF_LANES] = 1.0
    kones = jnp.asarray(kones.reshape(1, B_HEADS * 2 * LANES))

    w_cat = jnp.concatenate([w[:, :c0] * (scale * LOG2E), dup(w[:, c0:c1]), dup(w[:, c1:c2]),
                             w[:, c2:c3] * (scale * LOG2E), w[:, c3:]],
                            axis=1).astype(bf16)
    ncols = w_cat.shape[1]
    nrow = S // ROW_TILE
    nchunks = S // B_TILE
    assert ROW_TILE == B_TILE and nchunks % B_UNROLL == 0
    assert B_UNROLL % B_SLOTS == 0 and B_AHEAD < B_SLOTS and B_AHEAD <= B_UNROLL
    kb_cols = B_HEADS * 2 * LANES

    qa, ka, va, qb, kb, vt = pl.pallas_call(
        _in_proj_kernel,
        grid=(nrow,),
        in_specs=[pl.BlockSpec((ROW_TILE, D), lambda i: (i, 0)),
                  _resident((1, D)),
                  _resident((D, ncols)),
                  _resident((1, kb_cols))],
        out_specs=[pl.BlockSpec((ROW_TILE, 512), lambda i: (i, 0)),
                   pl.BlockSpec((ROW_TILE, 256), lambda i: (i, 0)),
                   pl.BlockSpec((ROW_TILE, 256), lambda i: (i, 0)),
                   pl.BlockSpec((ROW_TILE, 512), lambda i: (i, 0)),
                   pl.BlockSpec((ROW_TILE, kb_cols), lambda i: (i, 0)),
                   pl.BlockSpec((B_HEADS, 1, B_V_DIM, ROW_TILE), lambda i: (0, i, 0, 0))],
        out_shape=[jax.ShapeDtypeStruct((S, 512), bf16),
                   jax.ShapeDtypeStruct((S, 256), bf16),
                   jax.ShapeDtypeStruct((S, 256), bf16),
                   jax.ShapeDtypeStruct((S, 512), bf16),
                   jax.ShapeDtypeStruct((S, kb_cols), bf16),
                   jax.ShapeDtypeStruct((B_HEADS, nchunks, B_V_DIM, B_TILE), bf16)],
        compiler_params=pltpu.CompilerParams(dimension_semantics=("arbitrary",),
                                             vmem_limit_bytes=VMEM_LIMIT),
        name="in_proj",
    )(h, p["attn_pre_g"].reshape(1, D), w_cat, kones)

    nblocks = S // A_BLOCK
    a_rows = A_STEP_BLOCKS * A_BLOCK
    assert nblocks % A_STEP_BLOCKS == 0 and A_STEP_BLOCKS >= 2
    ya = pl.pallas_call(
        functools.partial(_win_attn_kernel, nblocks=nblocks),
        grid=(nblocks // A_STEP_BLOCKS,),
        in_specs=[pl.BlockSpec(memory_space=pltpu.SMEM),
                  pl.BlockSpec((a_rows, 512), lambda n: (n, 0)),
                  _resident(ka.shape),
                  _resident(va.shape),
                  _resident(bias_a.shape)],
        out_specs=pl.BlockSpec((a_rows, 512), lambda n: (n, 0)),
        out_shape=jax.ShapeDtypeStruct((S, 512), bf16),
        compiler_params=pltpu.CompilerParams(dimension_semantics=("arbitrary",),
                                             vmem_limit_bytes=VMEM_LIMIT),
        name="win_attn",
    )(p["a_sink"], qa, ka, va, bias_a)

    lamv = jnp.stack([p["lambda_q1"], p["lambda_k1"], p["lambda_q2"], p["lambda_k2"]])
    TQ, TK = B_QTILE, B_TILE
    assert S % TQ == 0 and TQ % TK == 0
    yb = pl.pallas_call(
        functools.partial(_diff_attn_kernel, nchunks=nchunks, lambda_init=lambda_init),
        grid=(B_HEADS, S // TQ),
        in_specs=[pl.BlockSpec((4, HEAD_DIM), lambda hh, i: (0, 0)),
                  pl.BlockSpec((1, B_V_DIM), lambda hh, i: (0, 0)),
                  pl.BlockSpec((TQ, LANES), lambda hh, i: (i, hh)),
                  pl.BlockSpec((S, 2 * LANES), lambda hh, i: (0, hh)),
                  pl.BlockSpec((1, nchunks, B_V_DIM, TK), lambda hh, i: (hh, 0, 0, 0)),
                  pl.BlockSpec((1, 5, B_BIAS_BLOCK, B_BIAS_BLOCK), lambda hh, i: (hh, 0, 0, 0))],
        out_specs=pl.BlockSpec((TQ, LANES), lambda hh, i: (i, hh)),
        out_shape=jax.ShapeDtypeStruct((S, B_HEADS * B_V_DIM), bf16),
        scratch_shapes=[pltpu.VMEM((2, LANES, TQ), bf16),
                        pltpu.VMEM((2, LANES, TQ), bf16),
                        pltpu.VMEM((B_SLOTS, 2, TK, TQ), bf16),
                        pltpu.VMEM((B_SLOTS, 2, 1, TQ), jnp.float32),
                        pltpu.VMEM((2, 1, TQ), jnp.float32),
                        pltpu.VMEM((nchunks, 2, 1, TQ), jnp.float32),
                        pltpu.VMEM((2, 1, TQ), jnp.float32),
                        pltpu.VMEM((2, B_V_DIM, TQ), jnp.float32),
                        pltpu.VMEM((2, 1, TQ), jnp.float32)],
        compiler_params=pltpu.CompilerParams(dimension_semantics=("arbitrary", "arbitrary"),
                                             vmem_limit_bytes=VMEM_LIMIT),
        name="diff_attn",
    )(lamv, p["diff_subln_g"].reshape(1, B_V_DIM), qb, kb, vt, bias_b)

    w_out = p["w_out"].astype(bf16)
    a_width = A_Q_HEADS * HEAD_DIM
    d_ff = p["w_gate"].shape[1]
    R = FFN_ROW_TILE
    out = pl.pallas_call(
        _out_ffn_kernel,
        grid=(S // R,),
        in_specs=[pl.BlockSpec((R, 512), lambda i: (i, 0)),
                  pl.BlockSpec((R, 512), lambda i: (i, 0)),
                  pl.BlockSpec((R, D), lambda i: (i, 0)),
                  _resident((a_width, D)),
                  _resident((w_out.shape[0] - a_width, D)),
                  _resident((1, D)),
                  _resident((1, D)),
                  _resident((D, d_ff)),
                  _resident((D, d_ff)),
                  _resident((d_ff, D)),
                  _resident((1, D))],
        out_specs=pl.BlockSpec((R, D), lambda i: (i, 0)),
        out_shape=jax.ShapeDtypeStruct((S, D), jnp.float32),
        compiler_params=pltpu.CompilerParams(dimension_semantics=("arbitrary",),
                                             vmem_limit_bytes=VMEM_LIMIT),
        name="out_ffn",
    )(ya, yb, h, w_out[:a_width], w_out[a_width:], p["attn_post_g"].reshape(1, D),
      p["ffn_pre_g"].reshape(1, D), p["w_gate"].astype(bf16), p["w_up"].astype(bf16),
      p["w_down"].astype(bf16), p["ffn_post_g"].reshape(1, D))
    return out


def kernel(x, attn_pre_g, w_in, a_sink, lambda_q1, lambda_k1, lambda_q2, lambda_k2, diff_subln_g,
           rel_bias, w_out, attn_post_g, ffn_pre_g, w_gate, w_up, w_down, ffn_post_g):
    batch, S, D = x.shape
    depth = w_in.shape[0]
    assert S % B_TILE == 0 and S % A_BLOCK == 0 and S // A_BLOCK >= 2

    tab_a = _toeplitz_bias(rel_bias[:, :A_Q_HEADS], A_BLOCK, 3 * A_BLOCK, lambda t: -t - A_BLOCK)
    qi = np.arange(A_BLOCK)[:, None]
    kj = np.arange(3 * A_BLOCK)[None, :]
    in_window = np.abs(kj - A_BLOCK - qi) <= A_BLOCK
    valid = np.stack([in_window & (kj >= A_BLOCK), in_window, in_window & (kj < 2 * A_BLOCK)])
    bias_a = jnp.where(valid[:, None], tab_a[None] * LOG2E, MASK_VALUE)

    nb = B_BIAS_BLOCK
    assert len(set(_t5_bucket_np(np.arange(nb + 1, 4 * nb)))) == 1
    tab_b = rel_bias[:, A_Q_HEADS:]
    near = [_toeplitz_bias(tab_b, nb, nb, lambda t, d=d: d * nb + t) for d in (-1, 0, 1)]
    far = tab_b[_t5_bucket_np(np.array([-2 * nb, 2 * nb]))].astype(jnp.float32)
    const = [jnp.broadcast_to(far[side][:, None, None], (B_HEADS, nb, nb)) for side in (0, 1)]
    bias_b = jnp.stack([const[0]] + near + [const[1]], axis=1) * LOG2E

    outs = []
    for b in range(batch):
        h = x[b]
        for l in range(depth):
            p = dict(attn_pre_g=attn_pre_g[l], w_in=w_in[l], a_sink=a_sink[l], lambda_q1=lambda_q1[l],
                     lambda_k1=lambda_k1[l], lambda_q2=lambda_q2[l], lambda_k2=lambda_k2[l],
                     diff_subln_g=diff_subln_g[l], w_out=w_out[l], attn_post_g=attn_post_g[l],
                     ffn_pre_g=ffn_pre_g[l], w_gate=w_gate[l], w_up=w_up[l], w_down=w_down[l],
                     ffn_post_g=ffn_post_g[l])
            h = _layer(h, l, p, bias_a, bias_b)
        outs.append(h)
    return jnp.stack(outs)
```

```python
import functools
import math

import numpy as np
import jax
import jax.numpy as jnp
from jax import lax
from jax.experimental import pallas as pl
from jax.experimental.pallas import tpu as pltpu

HEAD_DIM = 64
A_Q_HEADS = 8
A_KV_HEADS = 2
A_BLOCK = 128
A_STEP_BLOCKS = 4
B_HEADS = 4
B_V_DIM = 2 * HEAD_DIM
NUM_BUCKETS = 32
MAX_DISTANCE = 128
EPS = 1e-6
MASK_VALUE = -1e30
LOG2E = math.log2(math.e)
ROW_SUM_LIMIT = 2.0 ** 40

LANES = 128
ROW_TILE = 512
FFN_ROW_TILE = 512
B_TILE = 512
B_QTILE = 1024
B_REF_LANES = 3
B_UNROLL = 8
B_AHEAD = 2
B_SLOTS = 4
B_PROBE = 128
B_BIAS_BLOCK = 128
VMEM_LIMIT = 56 * 1024 * 1024

_NT = (((1,), (1,)), ((), ()))


def _t5_bucket_np(rel):
    nb = NUM_BUCKETS // 2
    max_exact = nb // 2
    ret = np.where(rel > 0, nb, 0)
    n = np.abs(rel)
    nf = np.maximum(n, 1).astype(np.float32)
    large = max_exact + (np.log(nf / np.float32(max_exact)) / np.float32(math.log(MAX_DISTANCE / max_exact))
                         * np.float32(nb - max_exact)).astype(np.int32)
    large = np.minimum(large, nb - 1)
    return (ret + np.where(n < max_exact, n, large)).astype(np.int32)


def _toeplitz_bias(table, rows, cols, rel_of):
    length = rows + cols
    u = np.arange(length)
    t = np.where(u < cols, -u, length - u)
    w = table[_t5_bucket_np(rel_of(t))].astype(jnp.float32).T
    x = jnp.tile(w, (1, rows))[:, :rows * (length - 1)].reshape(w.shape[0], rows, length - 1)
    return x[:, :, :cols]


def _rms(xf, g):
    return xf * lax.rsqrt(jnp.mean(xf * xf, axis=-1, keepdims=True) + EPS) * g


def _in_proj_kernel(x_ref, g_ref, w_ref, kones_ref, qa_ref, ka_ref, va_ref, qb_ref, kb_ref, vt_ref):
    u = _rms(x_ref[...], g_ref[...]).astype(jnp.bfloat16)
    proj = jnp.dot(u, w_ref[...], preferred_element_type=jnp.float32)
    qa_ref[...] = proj[:, 0:512].astype(jnp.bfloat16)
    ka_ref[...] = proj[:, 512:768].astype(jnp.bfloat16)
    va_ref[...] = proj[:, 768:1024].astype(jnp.bfloat16)
    qb_ref[...] = proj[:, 1024:1536].astype(jnp.bfloat16)
    lane = lax.broadcasted_iota(jnp.int32, (proj.shape[0], LANES), 1)
    keep = (lane < HEAD_DIM, lane >= HEAD_DIM)
    for h in range(B_HEADS):
        kpair = proj[:, 1536 + h * LANES:1536 + (h + 1) * LANES]
        for c in range(2):
            col = (2 * h + c) * LANES
            kb_ref[:, col:col + LANES] = jnp.where(keep[c], kpair, kones_ref[:, col:col + LANES]).astype(jnp.bfloat16)
        v = proj[:, 2048 + h * B_V_DIM:2048 + (h + 1) * B_V_DIM]
        vt_ref[h, 0] = v.T.astype(jnp.bfloat16)


def _win_attn_kernel(sink_ref, q_ref, k_ref, v_ref, bias_ref, o_ref, *, nblocks):
    f32 = jnp.float32
    bf16 = jnp.bfloat16
    n = pl.program_id(0)
    rows = A_STEP_BLOCKS * A_BLOCK
    seq = k_ref.shape[0]
    group = A_Q_HEADS // A_KV_HEADS
    start = pl.multiple_of(n * rows, rows)
    prev = pl.multiple_of(jnp.maximum(start - A_BLOCK, 0), A_BLOCK)
    nxt = pl.multiple_of(jnp.minimum(start + rows, seq - A_BLOCK), A_BLOCK)
    kw = jnp.concatenate([k_ref[pl.ds(prev, A_BLOCK), :], k_ref[pl.ds(start, rows), :],
                          k_ref[pl.ds(nxt, A_BLOCK), :]], axis=0)
    vw = jnp.concatenate([v_ref[pl.ds(prev, A_BLOCK), :], v_ref[pl.ds(start, rows), :],
                          v_ref[pl.ds(nxt, A_BLOCK), :]], axis=0)
    lane = lax.broadcasted_iota(jnp.int32, (A_BLOCK, LANES), 1)
    low = lane < HEAD_DIM
    ones = jnp.ones((3 * A_BLOCK, LANES), bf16)
    for b in range(A_STEP_BLOCKS):
        blk = n * A_STEP_BLOCKS + b
        if b == 0:
            variant = jnp.where(blk == 0, 0, 1)
        elif b == A_STEP_BLOCKS - 1:
            variant = jnp.where(blk == nblocks - 1, 2, 1)
        else:
            variant = 1
        q = q_ref[b * A_BLOCK:(b + 1) * A_BLOCK, :]
        for g in range(A_KV_HEADS):
            heads = range(g * group, (g + 1) * group)
            qs = []
            for h in heads:
                q2 = q[:, (h // 2) * LANES:(h // 2 + 1) * LANES]
                qs.append(jnp.where(low if h % 2 == 0 else jnp.logical_not(low), q2, jnp.zeros_like(q2)))
            q4 = jnp.concatenate(qs, axis=0)
            kg = kw[b * A_BLOCK:(b + 3) * A_BLOCK, g * LANES:(g + 1) * LANES]
            vg = jnp.concatenate([vw[b * A_BLOCK:(b + 3) * A_BLOCK, g * LANES:(g + 1) * LANES], ones],
                                 axis=1)
            s = lax.dot_general(q4, kg, _NT, preferred_element_type=f32)
            ps, sink_terms = [], []
            for t, h in enumerate(heads):
                sh = s[t * A_BLOCK:(t + 1) * A_BLOCK] + bias_ref[variant, h]
                snk = sink_ref[h] * LOG2E
                m = jnp.maximum(jnp.max(sh, axis=-1, keepdims=True), snk)
                ps.append(jnp.exp2(sh - m).astype(bf16))
                sink_terms.append(jnp.exp2(snk - m))
            p = jnp.concatenate(ps, axis=0)
            res = jnp.dot(p, vg, preferred_element_type=f32)
            on = res[:, 0:LANES] / (res[:, LANES:2 * LANES] + jnp.concatenate(sink_terms, axis=0))
            for t in range(group // 2):
                even = on[(2 * t) * A_BLOCK:(2 * t + 1) * A_BLOCK]
                odd = on[(2 * t + 1) * A_BLOCK:(2 * t + 2) * A_BLOCK]
                pair = (g * group) // 2 + t
                o_ref[b * A_BLOCK:(b + 1) * A_BLOCK, pair * LANES:(pair + 1) * LANES] = (
                    jnp.where(low, even, odd).astype(o_ref.dtype))


def _diff_attn_kernel(lamv_ref, g_ref, q_ref, k_ref, vt_ref, bias_ref, o_ref,
                      qz_ref, qp_ref, p_ref, psum_ref, r_ref, flag_ref, any_ref, acc_ref, den_ref,
                      *, nchunks, lambda_init):
    TQ = B_QTILE
    TK = B_TILE
    i = pl.program_id(1)
    f32 = jnp.float32
    bf16 = jnp.bfloat16

    qt = q_ref[...].astype(f32).T
    row = lax.broadcasted_iota(jnp.int32, (LANES, TQ), 0)
    is_q = (row < HEAD_DIM, row >= HEAD_DIM)
    ref_row = (HEAD_DIM, 0)

    ksub = TK // B_BIAS_BLOCK
    qsub = TQ // B_BIAS_BLOCK

    def bias_tile(j, rows=ksub):
        base = j * ksub - i * qsub
        by_dist = {d: bias_ref[0, jnp.clip(base + d, -2, 2) + 2] for d in range(1 - qsub, rows)}
        return jnp.concatenate(
            [jnp.concatenate([by_dist[a - b] for b in range(qsub)], axis=1) for a in range(rows)], axis=0)

    def k_chunk(j, c):
        return k_ref[pl.ds(pl.multiple_of(j * TK, TK), TK), c * LANES:(c + 1) * LANES]

    def set_reference(c, r_row):
        hi = r_row.astype(bf16).astype(f32)
        rem = r_row - hi
        mid = rem.astype(bf16).astype(f32)
        low = (rem - mid).astype(bf16).astype(f32)
        first = ref_row[c]
        ext = jnp.where(row == first, -hi, jnp.where(row == first + 1, -mid,
                                                     jnp.where(row == first + 2, -low, 0.0)))
        qp_ref[c] = jnp.where(is_q[c], qt, ext).astype(bf16)

    def exact_chunk(j, c, flagged):
        s = jnp.dot(k_chunk(j, c), qz_ref[c], preferred_element_type=f32) + bias_tile(j)
        r_old = r_ref[c]
        r_new = jnp.where(flagged, jnp.maximum(r_old, jnp.max(s, axis=0, keepdims=True)), r_old)
        p = jnp.where(flagged, jnp.exp2(s - r_new), 0.0)
        pv = jnp.dot(vt_ref[0, j], p.astype(bf16), preferred_element_type=f32)
        alpha = jnp.exp2(r_old - r_new)
        acc_ref[c] = acc_ref[c] * alpha + pv
        den_ref[c] = den_ref[c] * alpha + jnp.sum(p, axis=0, keepdims=True)
        r_ref[c] = r_new

    def stage_a(j, slot):
        bias = bias_tile(j)
        for c in range(2):
            s = jnp.dot(k_chunk(j, c), qp_ref[c], preferred_element_type=f32) + bias
            p = jnp.exp2(s)
            psum_ref[slot, c] = jnp.sum(p, axis=0, keepdims=True)
            p_ref[slot, c] = p.astype(bf16)

    def stage_c(j, slot):
        vt = vt_ref[0, j]
        for c in range(2):
            pv = jnp.dot(vt, p_ref[slot, c], preferred_element_type=f32)
            psum = psum_ref[slot, c]
            ok = psum <= ROW_SUM_LIMIT
            acc_ref[c] = acc_ref[c] + jnp.where(ok, pv, 0.0)
            den_ref[c] = den_ref[c] + jnp.where(ok, psum, 0.0)
            flag = jnp.where(ok, 0.0, 1.0)
            flag_ref[j, c] = flag
            any_ref[c] = jnp.maximum(any_ref[c], flag)

    for c in range(2):
        qz_ref[c] = jnp.where(is_q[c], qt, 0.0).astype(bf16)
    acc_ref[...] = jnp.zeros(acc_ref.shape, f32)
    den_ref[...] = jnp.zeros(den_ref.shape, f32)
    any_ref[...] = jnp.zeros(any_ref.shape, f32)
    diag = i * (TQ // TK)
    probe_bias = bias_tile(diag, rows=B_PROBE // B_BIAS_BLOCK)
    for c in range(2):
        kp = k_ref[pl.ds(pl.multiple_of(diag * TK, TK), B_PROBE), c * LANES:(c + 1) * LANES]
        s = jnp.dot(kp, qz_ref[c], preferred_element_type=f32) + probe_bias
        r = jnp.max(s, axis=0, keepdims=True)
        r_ref[c] = r
        set_reference(c, r)

    def run(j0, has_next):
        for u in range(B_UNROLL):
            stage_c(j0 + u, u % B_SLOTS)
            if u + B_AHEAD < B_UNROLL or has_next:
                stage_a(j0 + u + B_AHEAD, (u + B_AHEAD) % B_SLOTS)

    nbody = nchunks // B_UNROLL
    for u in range(B_AHEAD):
        stage_a(u, u)

    def body(b, carry):
        run(b * B_UNROLL, True)
        return carry

    lax.fori_loop(0, nbody - 1, body, 0)
    run((nbody - 1) * B_UNROLL, False)

    @pl.when(jnp.max(jnp.maximum(any_ref[0], any_ref[1])) > 0.0)
    def _():
        def redo(j, carry):
            @pl.when(jnp.max(jnp.maximum(flag_ref[j, 0], flag_ref[j, 1])) > 0.0)
            def _():
                for c in range(2):
                    exact_chunk(j, c, flag_ref[j, c] > 0.0)
            return carry

        lax.fori_loop(0, nchunks, redo, 0)

    lv = lamv_ref[...]
    lam = (jnp.exp(jnp.sum(lv[0:1] * lv[1:2], axis=-1, keepdims=True))
           - jnp.exp(jnp.sum(lv[2:3] * lv[3:4], axis=-1, keepdims=True)) + lambda_init)
    ot = acc_ref[0] / den_ref[0] - lam * (acc_ref[1] / den_ref[1])
    o = ot.T
    o_ref[...] = (_rms(o, g_ref[...]) * (1.0 - lambda_init)).astype(o_ref.dtype)


def _out_ffn_kernel(ya_ref, yb_ref, x_ref, woa_ref, wob_ref, gpost_ref, gpre_ref,
                    wg_ref, wu_ref, wd_ref, gfpost_ref, o_ref):
    y = (jnp.dot(ya_ref[...], woa_ref[...], preferred_element_type=jnp.float32)
         + jnp.dot(yb_ref[...], wob_ref[...], preferred_element_type=jnp.float32))
    h1 = x_ref[...] + _rms(y, gpost_ref[...])
    u = _rms(h1, gpre_ref[...]).astype(jnp.bfloat16)
    gate = jnp.dot(u, wg_ref[...], preferred_element_type=jnp.float32)
    up = jnp.dot(u, wu_ref[...], preferred_element_type=jnp.float32)
    act = (gate * jax.nn.sigmoid(gate) * up).astype(jnp.bfloat16)
    f = jnp.dot(act, wd_ref[...], preferred_element_type=jnp.float32)
    o_ref[...] = h1 + _rms(f, gfpost_ref[...])


def _resident(shape):
    zeros = (0,) * len(shape)
    return pl.BlockSpec(shape, lambda *_: zeros, pipeline_mode=pl.Buffered(1))


def _layer(h, l, p, bias_a, bias_b):
    S, D = h.shape
    bf16 = jnp.bfloat16
    lambda_init = 0.8 - 0.6 * math.exp(-0.3 * l)
    scale = HEAD_DIM ** -0.5

    w = p["w_in"]
    a_q, a_kv, b_qk = A_Q_HEADS * HEAD_DIM, A_KV_HEADS * HEAD_DIM, B_HEADS * 2 * HEAD_DIM
    c0 = a_q
    c1 = c0 + a_kv
    c2 = c1 + a_kv
    c3 = c2 + b_qk

    def dup(cols):
        parts = []
        for g in range(A_KV_HEADS):
            blk = cols[:, g * HEAD_DIM:(g + 1) * HEAD_DIM]
            parts += [blk, blk]
        return jnp.concatenate(parts, axis=1)

    kones = np.zeros((B_HEADS, 2, LANES), np.float32)
    kones[:, 0, HEAD_DIM:HEAD_DIM + B_REF_LANES] = 1.0
    kones[:, 1, 0:B_REF_LANES] = 1.0
    kones = jnp.asarray(kones.reshape(1, B_HEADS * 2 * LANES))

    w_cat = jnp.concatenate([w[:, :c0] * (scale * LOG2E), dup(w[:, c0:c1]), dup(w[:, c1:c2]),
                             w[:, c2:c3] * (scale * LOG2E), w[:, c3:]],
                            axis=1).astype(bf16)
    ncols = w_cat.shape[1]
    nrow = S // ROW_TILE
    nchunks = S // B_TILE
    assert ROW_TILE == B_TILE and nchunks % B_UNROLL == 0
    assert B_UNROLL % B_SLOTS == 0 and B_AHEAD < B_SLOTS and B_AHEAD <= B_UNROLL
    kb_cols = B_HEADS * 2 * LANES

    qa, ka, va, qb, kb, vt = pl.pallas_call(
        _in_proj_kernel,
        grid=(nrow,),
        in_specs=[pl.BlockSpec((ROW_TILE, D), lambda i: (i, 0)),
                  _resident((1, D)),
                  _resident((D, ncols)),
                  _resident((1, kb_cols))],
        out_specs=[pl.BlockSpec((ROW_TILE, 512), lambda i: (i, 0)),
                   pl.BlockSpec((ROW_TILE, 256), lambda i: (i, 0)),
                   pl.BlockSpec((ROW_TILE, 256), lambda i: (i, 0)),
                   pl.BlockSpec((ROW_TILE, 512), lambda i: (i, 0)),
                   pl.BlockSpec((ROW_TILE, kb_cols), lambda i: (i, 0)),
                   pl.BlockSpec((B_HEADS, 1, B_V_DIM, ROW_TILE), lambda i: (0, i, 0, 0))],
        out_shape=[jax.ShapeDtypeStruct((S, 512), bf16),
                   jax.ShapeDtypeStruct((S, 256), bf16),
                   jax.ShapeDtypeStruct((S, 256), bf16),
                   jax.ShapeDtypeStruct((S, 512), bf16),
                   jax.ShapeDtypeStruct((S, kb_cols), bf16),
                   jax.ShapeDtypeStruct((B_HEADS, nchunks, B_V_DIM, B_TILE), bf16)],
        compiler_params=pltpu.CompilerParams(dimension_semantics=("arbitrary",),
                                             vmem_limit_bytes=VMEM_LIMIT),
        name="in_proj",
    )(h, p["attn_pre_g"].reshape(1, D), w_cat, kones)

    nblocks = S // A_BLOCK
    a_rows = A_STEP_BLOCKS * A_BLOCK
    assert nblocks % A_STEP_BLOCKS == 0 and A_STEP_BLOCKS >= 2
    ya = pl.pallas_call(
        functools.partial(_win_attn_kernel, nblocks=nblocks),
        grid=(nblocks // A_STEP_BLOCKS,),
        in_specs=[pl.BlockSpec(memory_space=pltpu.SMEM),
                  pl.BlockSpec((a_rows, 512), lambda n: (n, 0)),
                  _resident(ka.shape),
                  _resident(va.shape),
                  _resident(bias_a.shape)],
        out_specs=pl.BlockSpec((a_rows, 512), lambda n: (n, 0)),
        out_shape=jax.ShapeDtypeStruct((S, 512), bf16),
        compiler_params=pltpu.CompilerParams(dimension_semantics=("arbitrary",),
                                             vmem_limit_bytes=VMEM_LIMIT),
        name="win_attn",
    )(p["a_sink"], qa, ka, va, bias_a)

    lamv = jnp.stack([p["lambda_q1"], p["lambda_k1"], p["lambda_q2"], p["lambda_k2"]])
    TQ, TK = B_QTILE, B_TILE
    assert S % TQ == 0 and TQ % TK == 0
    yb = pl.pallas_call(
        functools.partial(_diff_attn_kernel, nchunks=nchunks, lambda_init=lambda_init),
        grid=(B_HEADS, S // TQ),
        in_specs=[pl.BlockSpec((4, HEAD_DIM), lambda hh, i: (0, 0)),
                  pl.BlockSpec((1, B_V_DIM), lambda hh, i: (0, 0)),
                  pl.BlockSpec((TQ, LANES), lambda hh, i: (i, hh)),
                  pl.BlockSpec((S, 2 * LANES), lambda hh, i: (0, hh)),
                  pl.BlockSpec((1, nchunks, B_V_DIM, TK), lambda hh, i: (hh, 0, 0, 0)),
                  pl.BlockSpec((1, 5, B_BIAS_BLOCK, B_BIAS_BLOCK), lambda hh, i: (hh, 0, 0, 0))],
        out_specs=pl.BlockSpec((TQ, LANES), lambda hh, i: (i, hh)),
        out_shape=jax.ShapeDtypeStruct((S, B_HEADS * B_V_DIM), bf16),
        scratch_shapes=[pltpu.VMEM((2, LANES, TQ), bf16),
                        pltpu.VMEM((2, LANES, TQ), bf16),
                        pltpu.VMEM((B_SLOTS, 2, TK, TQ), bf16),
                        pltpu.VMEM((B_SLOTS, 2, 1, TQ), jnp.float32),
                        pltpu.VMEM((2, 1, TQ), jnp.float32),
                        pltpu.VMEM((nchunks, 2, 1, TQ), jnp.float32),
                        pltpu.VMEM((2, 1, TQ), jnp.float32),
                        pltpu.VMEM((2, B_V_DIM, TQ), jnp.float32),
                        pltpu.VMEM((2, 1, TQ), jnp.float32)],
        compiler_params=pltpu.CompilerParams(dimension_semantics=("arbitrary", "arbitrary"),
                                             vmem_limit_bytes=VMEM_LIMIT),
        name="diff_attn",
    )(lamv, p["diff_subln_g"].reshape(1, B_V_DIM), qb, kb, vt, bias_b)

    w_out = p["w_out"].astype(bf16)
    a_width = A_Q_HEADS * HEAD_DIM
    d_ff = p["w_gate"].shape[1]
    R = FFN_ROW_TILE
    out = pl.pallas_call(
        _out_ffn_kernel,
        grid=(S // R,),
        in_specs=[pl.BlockSpec((R, 512), lambda i: (i, 0)),
                  pl.BlockSpec((R, 512), lambda i: (i, 0)),
                  pl.BlockSpec((R, D), lambda i: (i, 0)),
                  _resident((a_width, D)),
                  _resident((w_out.shape[0] - a_width, D)),
                  _resident((1, D)),
                  _resident((1, D)),
                  _resident((D, d_ff)),
                  _resident((D, d_ff)),
                  _resident((d_ff, D)),
                  _resident((1, D))],
        out_specs=pl.BlockSpec((R, D), lambda i: (i, 0)),
        out_shape=jax.ShapeDtypeStruct((S, D), jnp.float32),
        compiler_params=pltpu.CompilerParams(dimension_semantics=("arbitrary",),
                                             vmem_limit_bytes=VMEM_LIMIT),
        name="out_ffn",
    )(ya, yb, h, w_out[:a_width], w_out[a_width:], p["attn_post_g"].reshape(1, D),
      p["ffn_pre_g"].reshape(1, D), p["w_gate"].astype(bf16), p["w_up"].astype(bf16),
      p["w_down"].astype(bf16), p["ffn_post_g"].reshape(1, D))
    return out


def kernel(x, attn_pre_g, w_in, a_sink, lambda_q1, lambda_k1, lambda_q2, lambda_k2, diff_subln_g,
           rel_bias, w_out, attn_post_g, ffn_pre_g, w_gate, w_up, w_down, ffn_post_g):
    batch, S, D = x.shape
    depth = w_in.shape[0]
    assert S % B_TILE == 0 and S % A_BLOCK == 0 and S // A_BLOCK >= 2

    tab_a = _toeplitz_bias(rel_bias[:, :A_Q_HEADS], A_BLOCK, 3 * A_BLOCK, lambda t: -t - A_BLOCK)
    qi = np.arange(A_BLOCK)[:, None]
    kj = np.arange(3 * A_BLOCK)[None, :]
    in_window = np.abs(kj - A_BLOCK - qi) <= A_BLOCK
    valid = np.stack([in_window & (kj >= A_BLOCK), in_window, in_window & (kj < 2 * A_BLOCK)])
    bias_a = jnp.where(valid[:, None], tab_a[None] * LOG2E, MASK_VALUE)

    nb = B_BIAS_BLOCK
    assert len(set(_t5_bucket_np(np.arange(nb + 1, 4 * nb)))) == 1
    tab_b = rel_bias[:, A_Q_HEADS:]
    near = [_toeplitz_bias(tab_b, nb, nb, lambda t, d=d: d * nb + t) for d in (-1, 0, 1)]
    far = tab_b[_t5_bucket_np(np.array([-2 * nb, 2 * nb]))].astype(jnp.float32)
    const = [jnp.broadcast_to(far[side][:, None, None], (B_HEADS, nb, nb)) for side in (0, 1)]
    bias_b = jnp.stack([const[0]] + near + [const[1]], axis=1) * LOG2E

    outs = []
    for b in range(batch):
        h = x[b]
        for l in range(depth):
            p = dict(attn_pre_g=attn_pre_g[l], w_in=w_in[l], a_sink=a_sink[l], lambda_q1=lambda_q1[l],
                     lambda_k1=lambda_k1[l], lambda_q2=lambda_q2[l], lambda_k2=lambda_k2[l],
                     diff_subln_g=diff_subln_g[l], w_out=w_out[l], attn_post_g=attn_post_g[l],
                     ffn_pre_g=ffn_pre_g[l], w_gate=w_gate[l], w_up=w_up[l], w_down=w_down[l],
                     ffn_post_g=ffn_post_g[l])
            h = _layer(h, l, p, bias_a, bias_b)
        outs.append(h)
    return jnp.stack(outs)
```

```python
import functools
import math

import numpy as np
import jax
import jax.numpy as jnp
from jax import lax
from jax.experimental import pallas as pl
from jax.experimental.pallas import tpu as pltpu

HEAD_DIM = 64
A_Q_HEADS = 8
A_KV_HEADS = 2
A_BLOCK = 128
A_STEP_BLOCKS = 4
B_HEADS = 4
B_V_DIM = 2 * HEAD_DIM
NUM_BUCKETS = 32
MAX_DISTANCE = 128
EPS = 1e-6
MASK_VALUE = -1e30
LOG2E = math.log2(math.e)
ROW_SUM_LIMIT = 2.0 ** 40

LANES = 128
ROW_TILE = 512
FFN_ROW_TILE = 512
B_TILE = 512
B_QTILE = 2048
B_REF_LANES = 3
B_UNROLL = 4
B_AHEAD = 2
B_SLOTS = 4
B_PROBE = 128
B_BIAS_BLOCK = 128
VMEM_LIMIT = 56 * 1024 * 1024

_NT = (((1,), (1,)), ((), ()))


def _t5_bucket_np(rel):
    nb = NUM_BUCKETS // 2
    max_exact = nb // 2
    ret = np.where(rel > 0, nb, 0)
    n = np.abs(rel)
    nf = np.maximum(n, 1).astype(np.float32)
    large = max_exact + (np.log(nf / np.float32(max_exact)) / np.float32(math.log(MAX_DISTANCE / max_exact))
                         * np.float32(nb - max_exact)).astype(np.int32)
    large = np.minimum(large, nb - 1)
    return (ret + np.where(n < max_exact, n, large)).astype(np.int32)


def _toeplitz_bias(table, rows, cols, rel_of):
    length = rows + cols
    u = np.arange(length)
    t = np.where(u < cols, -u, length - u)
    w = table[_t5_bucket_np(rel_of(t))].astype(jnp.float32).T
    x = jnp.tile(w, (1, rows))[:, :rows * (length - 1)].reshape(w.shape[0], rows, length - 1)
    return x[:, :, :cols]


def _rms(xf, g):
    return xf * lax.rsqrt(jnp.mean(xf * xf, axis=-1, keepdims=True) + EPS) * g


def _in_proj_kernel(x_ref, g_ref, w_ref, kones_ref, qa_ref, ka_ref, va_ref, qb_ref, kb_ref, vt_ref):
    u = _rms(x_ref[...], g_ref[...]).astype(jnp.bfloat16)
    proj = jnp.dot(u, w_ref[...], preferred_element_type=jnp.float32)
    qa_ref[...] = proj[:, 0:512].astype(jnp.bfloat16)
    ka_ref[...] = proj[:, 512:768].astype(jnp.bfloat16)
    va_ref[...] = proj[:, 768:1024].astype(jnp.bfloat16)
    qb_ref[...] = proj[:, 1024:1536].astype(jnp.bfloat16)
    lane = lax.broadcasted_iota(jnp.int32, (proj.shape[0], LANES), 1)
    keep = (lane < HEAD_DIM, lane >= HEAD_DIM)
    for h in range(B_HEADS):
        kpair = proj[:, 1536 + h * LANES:1536 + (h + 1) * LANES]
        for c in range(2):
            col = (2 * h + c) * LANES
            kb_ref[:, col:col + LANES] = jnp.where(keep[c], kpair, kones_ref[:, col:col + LANES]).astype(jnp.bfloat16)
        v = proj[:, 2048 + h * B_V_DIM:2048 + (h + 1) * B_V_DIM]
        vt_ref[h, 0] = v.T.astype(jnp.bfloat16)


def _win_attn_kernel(sink_ref, q_ref, k_ref, v_ref, bias_ref, o_ref, *, nblocks):
    f32 = jnp.float32
    bf16 = jnp.bfloat16
    n = pl.program_id(0)
    rows = A_STEP_BLOCKS * A_BLOCK
    seq = k_ref.shape[0]
    group = A_Q_HEADS // A_KV_HEADS
    start = pl.multiple_of(n * rows, rows)
    prev = pl.multiple_of(jnp.maximum(start - A_BLOCK, 0), A_BLOCK)
    nxt = pl.multiple_of(jnp.minimum(start + rows, seq - A_BLOCK), A_BLOCK)
    kw = jnp.concatenate([k_ref[pl.ds(prev, A_BLOCK), :], k_ref[pl.ds(start, rows), :],
                          k_ref[pl.ds(nxt, A_BLOCK), :]], axis=0)
    vw = jnp.concatenate([v_ref[pl.ds(prev, A_BLOCK), :], v_ref[pl.ds(start, rows), :],
                          v_ref[pl.ds(nxt, A_BLOCK), :]], axis=0)
    lane = lax.broadcasted_iota(jnp.int32, (A_BLOCK, LANES), 1)
    low = lane < HEAD_DIM
    ones = jnp.ones((3 * A_BLOCK, LANES), bf16)
    for b in range(A_STEP_BLOCKS):
        blk = n * A_STEP_BLOCKS + b
        if b == 0:
            variant = jnp.where(blk == 0, 0, 1)
        elif b == A_STEP_BLOCKS - 1:
            variant = jnp.where(blk == nblocks - 1, 2, 1)
        else:
            variant = 1
        q = q_ref[b * A_BLOCK:(b + 1) * A_BLOCK, :]
        for g in range(A_KV_HEADS):
            heads = range(g * group, (g + 1) * group)
            qs = []
            for h in heads:
                q2 = q[:, (h // 2) * LANES:(h // 2 + 1) * LANES]
                qs.append(jnp.where(low if h % 2 == 0 else jnp.logical_not(low), q2, jnp.zeros_like(q2)))
            q4 = jnp.concatenate(qs, axis=0)
            kg = kw[b * A_BLOCK:(b + 3) * A_BLOCK, g * LANES:(g + 1) * LANES]
            vg = jnp.concatenate([vw[b * A_BLOCK:(b + 3) * A_BLOCK, g * LANES:(g + 1) * LANES], ones],
                                 axis=1)
            s = lax.dot_general(q4, kg, _NT, preferred_element_type=f32)
            ps, sink_terms = [], []
            for t, h in enumerate(heads):
                sh = s[t * A_BLOCK:(t + 1) * A_BLOCK] + bias_ref[variant, h]
                snk = sink_ref[h] * LOG2E
                m = jnp.maximum(jnp.max(sh, axis=-1, keepdims=True), snk)
                ps.append(jnp.exp2(sh - m).astype(bf16))
                sink_terms.append(jnp.exp2(snk - m))
            p = jnp.concatenate(ps, axis=0)
            res = jnp.dot(p, vg, preferred_element_type=f32)
            on = res[:, 0:LANES] / (res[:, LANES:2 * LANES] + jnp.concatenate(sink_terms, axis=0))
            for t in range(group // 2):
                even = on[(2 * t) * A_BLOCK:(2 * t + 1) * A_BLOCK]
                odd = on[(2 * t + 1) * A_BLOCK:(2 * t + 2) * A_BLOCK]
                pair = (g * group) // 2 + t
                o_ref[b * A_BLOCK:(b + 1) * A_BLOCK, pair * LANES:(pair + 1) * LANES] = (
                    jnp.where(low, even, odd).astype(o_ref.dtype))


def _diff_attn_kernel(lamv_ref, g_ref, q_ref, k_ref, vt_ref, bias_ref, o_ref,
                      qz_ref, qp_ref, p_ref, psum_ref, r_ref, flag_ref, any_ref, acc_ref, den_ref,
                      *, nchunks, lambda_init):
    TQ = B_QTILE
    TK = B_TILE
    i = pl.program_id(1)
    f32 = jnp.float32
    bf16 = jnp.bfloat16

    qt = q_ref[...].astype(f32).T
    row = lax.broadcasted_iota(jnp.int32, (LANES, TQ), 0)
    is_q = (row < HEAD_DIM, row >= HEAD_DIM)
    ref_row = (HEAD_DIM, 0)

    ksub = TK // B_BIAS_BLOCK
    qsub = TQ // B_BIAS_BLOCK

    def bias_tile(j, rows=ksub):
        base = j * ksub - i * qsub
        by_dist = {d: bias_ref[0, jnp.clip(base + d, -2, 2) + 2] for d in range(1 - qsub, rows)}
        return jnp.concatenate(
            [jnp.concatenate([by_dist[a - b] for b in range(qsub)], axis=1) for a in range(rows)], axis=0)

    def k_chunk(j, c):
        return k_ref[pl.ds(pl.multiple_of(j * TK, TK), TK), c * LANES:(c + 1) * LANES]

    def set_reference(c, r_row):
        hi = r_row.astype(bf16).astype(f32)
        rem = r_row - hi
        mid = rem.astype(bf16).astype(f32)
        low = (rem - mid).astype(bf16).astype(f32)
        first = ref_row[c]
        ext = jnp.where(row == first, -hi, jnp.where(row == first + 1, -mid,
                                                     jnp.where(row == first + 2, -low, 0.0)))
        qp_ref[c] = jnp.where(is_q[c], qt, ext).astype(bf16)

    def exact_chunk(j, c, flagged):
        s = jnp.dot(k_chunk(j, c), qz_ref[c], preferred_element_type=f32) + bias_tile(j)
        r_old = r_ref[c]
        r_new = jnp.where(flagged, jnp.maximum(r_old, jnp.max(s, axis=0, keepdims=True)), r_old)
        p = jnp.where(flagged, jnp.exp2(s - r_new), 0.0)
        pv = jnp.dot(vt_ref[0, j], p.astype(bf16), preferred_element_type=f32)
        alpha = jnp.exp2(r_old - r_new)
        acc_ref[c] = acc_ref[c] * alpha + pv
        den_ref[c] = den_ref[c] * alpha + jnp.sum(p, axis=0, keepdims=True)
        r_ref[c] = r_new

    def stage_a(j, slot):
        bias = bias_tile(j)
        for c in range(2):
            s = jnp.dot(k_chunk(j, c), qp_ref[c], preferred_element_type=f32) + bias
            p = jnp.exp2(s)
            psum_ref[slot, c] = jnp.sum(p, axis=0, keepdims=True)
            p_ref[slot, c] = p.astype(bf16)

    def stage_c(j, slot):
        vt = vt_ref[0, j]
        for c in range(2):
            pv = jnp.dot(vt, p_ref[slot, c], preferred_element_type=f32)
            psum = psum_ref[slot, c]
            ok = psum <= ROW_SUM_LIMIT
            acc_ref[c] = acc_ref[c] + jnp.where(ok, pv, 0.0)
            den_ref[c] = den_ref[c] + jnp.where(ok, psum, 0.0)
            flag = jnp.where(ok, 0.0, 1.0)
            flag_ref[j, c] = flag
            any_ref[c] = jnp.maximum(any_ref[c], flag)

    for c in range(2):
        qz_ref[c] = jnp.where(is_q[c], qt, 0.0).astype(bf16)
    acc_ref[...] = jnp.zeros(acc_ref.shape, f32)
    den_ref[...] = jnp.zeros(den_ref.shape, f32)
    any_ref[...] = jnp.zeros(any_ref.shape, f32)
    diag = i * (TQ // TK)
    probe_bias = bias_tile(diag, rows=B_PROBE // B_BIAS_BLOCK)
    for c in range(2):
        kp = k_ref[pl.ds(pl.multiple_of(diag * TK, TK), B_PROBE), c * LANES:(c + 1) * LANES]
        s = jnp.dot(kp, qz_ref[c], preferred_element_type=f32) + probe_bias
        r = jnp.max(s, axis=0, keepdims=True)
        r_ref[c] = r
        set_reference(c, r)

    def run(j0, has_next):
        for u in range(B_UNROLL):
            stage_c(j0 + u, u % B_SLOTS)
            if u + B_AHEAD < B_UNROLL or has_next:
                stage_a(j0 + u + B_AHEAD, (u + B_AHEAD) % B_SLOTS)

    nbody = nchunks // B_UNROLL
    for u in range(B_AHEAD):
        stage_a(u, u)

    def body(b, carry):
        run(b * B_UNROLL, True)
        return carry

    lax.fori_loop(0, nbody - 1, body, 0)
    run((nbody - 1) * B_UNROLL, False)

    @pl.when(jnp.max(jnp.maximum(any_ref[0], any_ref[1])) > 0.0)
    def _():
        def redo(j, carry):
            @pl.when(jnp.max(jnp.maximum(flag_ref[j, 0], flag_ref[j, 1])) > 0.0)
            def _():
                for c in range(2):
                    exact_chunk(j, c, flag_ref[j, c] > 0.0)
            return carry

        lax.fori_loop(0, nchunks, redo, 0)

    lv = lamv_ref[...]
    lam = (jnp.exp(jnp.sum(lv[0:1] * lv[1:2], axis=-1, keepdims=True))
           - jnp.exp(jnp.sum(lv[2:3] * lv[3:4], axis=-1, keepdims=True)) + lambda_init)
    ot = acc_ref[0] / den_ref[0] - lam * (acc_ref[1] / den_ref[1])
    o = ot.T
    o_ref[...] = (_rms(o, g_ref[...]) * (1.0 - lambda_init)).astype(o_ref.dtype)


def _out_ffn_kernel(ya_ref, yb_ref, x_ref, woa_ref, wob_ref, gpost_ref, gpre_ref,
                    wg_ref, wu_ref, wd_ref, gfpost_ref, o_ref):
    y = (jnp.dot(ya_ref[...], woa_ref[...], preferred_element_type=jnp.float32)
         + jnp.dot(yb_ref[...], wob_ref[...], preferred_element_type=jnp.float32))
    h1 = x_ref[...] + _rms(y, gpost_ref[...])
    u = _rms(h1, gpre_ref[...]).astype(jnp.bfloat16)
    gate = jnp.dot(u, wg_ref[...], preferred_element_type=jnp.float32)
    up = jnp.dot(u, wu_ref[...], preferred_element_type=jnp.float32)
    act = (gate * jax.nn.sigmoid(gate) * up).astype(jnp.bfloat16)
    f = jnp.dot(act, wd_ref[...], preferred_element_type=jnp.float32)
    o_ref[...] = h1 + _rms(f, gfpost_ref[...])


def _resident(shape):
    zeros = (0,) * len(shape)
    return pl.BlockSpec(shape, lambda *_: zeros, pipeline_mode=pl.Buffered(1))


def _layer(h, l, p, bias_a, bias_b):
    S, D = h.shape
    bf16 = jnp.bfloat16
    lambda_init = 0.8 - 0.6 * math.exp(-0.3 * l)
    scale = HEAD_DIM ** -0.5

    w = p["w_in"]
    a_q, a_kv, b_qk = A_Q_HEADS * HEAD_DIM, A_KV_HEADS * HEAD_DIM, B_HEADS * 2 * HEAD_DIM
    c0 = a_q
    c1 = c0 + a_kv
    c2 = c1 + a_kv
    c3 = c2 + b_qk

    def dup(cols):
        parts = []
        for g in range(A_KV_HEADS):
            blk = cols[:, g * HEAD_DIM:(g + 1) * HEAD_DIM]
            parts += [blk, blk]
        return jnp.concatenate(parts, axis=1)

    kones = np.zeros((B_HEADS, 2, LANES), np.float32)
    kones[:, 0, HEAD_DIM:HEAD_DIM + B_REF_LANES] = 1.0
    kones[:, 1, 0:B_REF_LANES] = 1.0
    kones = jnp.asarray(kones.reshape(1, B_HEADS * 2 * LANES))

    w_cat = jnp.concatenate([w[:, :c0] * (scale * LOG2E), dup(w[:, c0:c1]), dup(w[:, c1:c2]),
                             w[:, c2:c3] * (scale * LOG2E), w[:, c3:]],
                            axis=1).astype(bf16)
    ncols = w_cat.shape[1]
    nrow = S // ROW_TILE
    nchunks = S // B_TILE
    assert ROW_TILE == B_TILE and nchunks % B_UNROLL == 0
    assert B_UNROLL % B_SLOTS == 0 and B_AHEAD < B_SLOTS and B_AHEAD <= B_UNROLL
    kb_cols = B_HEADS * 2 * LANES

    qa, ka, va, qb, kb, vt = pl.pallas_call(
        _in_proj_kernel,
        grid=(nrow,),
        in_specs=[pl.BlockSpec((ROW_TILE, D), lambda i: (i, 0)),
                  _resident((1, D)),
                  _resident((D, ncols)),
                  _resident((1, kb_cols))],
        out_specs=[pl.BlockSpec((ROW_TILE, 512), lambda i: (i, 0)),
                   pl.BlockSpec((ROW_TILE, 256), lambda i: (i, 0)),
                   pl.BlockSpec((ROW_TILE, 256), lambda i: (i, 0)),
                   pl.BlockSpec((ROW_TILE, 512), lambda i: (i, 0)),
                   pl.BlockSpec((ROW_TILE, kb_cols), lambda i: (i, 0)),
                   pl.BlockSpec((B_HEADS, 1, B_V_DIM, ROW_TILE), lambda i: (0, i, 0, 0))],
        out_shape=[jax.ShapeDtypeStruct((S, 512), bf16),
                   jax.ShapeDtypeStruct((S, 256), bf16),
                   jax.ShapeDtypeStruct((S, 256), bf16),
                   jax.ShapeDtypeStruct((S, 512), bf16),
                   jax.ShapeDtypeStruct((S, kb_cols), bf16),
                   jax.ShapeDtypeStruct((B_HEADS, nchunks, B_V_DIM, B_TILE), bf16)],
        compiler_params=pltpu.CompilerParams(dimension_semantics=("arbitrary",),
                                             vmem_limit_bytes=VMEM_LIMIT),
        name="in_proj",
    )(h, p["attn_pre_g"].reshape(1, D), w_cat, kones)

    nblocks = S // A_BLOCK
    a_rows = A_STEP_BLOCKS * A_BLOCK
    assert nblocks % A_STEP_BLOCKS == 0 and A_STEP_BLOCKS >= 2
    ya = pl.pallas_call(
        functools.partial(_win_attn_kernel, nblocks=nblocks),
        grid=(nblocks // A_STEP_BLOCKS,),
        in_specs=[pl.BlockSpec(memory_space=pltpu.SMEM),
                  pl.BlockSpec((a_rows, 512), lambda n: (n, 0)),
                  _resident(ka.shape),
                  _resident(va.shape),
                  _resident(bias_a.shape)],
        out_specs=pl.BlockSpec((a_rows, 512), lambda n: (n, 0)),
        out_shape=jax.ShapeDtypeStruct((S, 512), bf16),
        compiler_params=pltpu.CompilerParams(dimension_semantics=("arbitrary",),
                                             vmem_limit_bytes=VMEM_LIMIT),
        name="win_attn",
    )(p["a_sink"], qa, ka, va, bias_a)

    lamv = jnp.stack([p["lambda_q1"], p["lambda_k1"], p["lambda_q2"], p["lambda_k2"]])
    TQ, TK = B_QTILE, B_TILE
    assert S % TQ == 0 and TQ % TK == 0
    yb = pl.pallas_call(
        functools.partial(_diff_attn_kernel, nchunks=nchunks, lambda_init=lambda_init),
        grid=(B_HEADS, S // TQ),
        in_specs=[pl.BlockSpec((4, HEAD_DIM), lambda hh, i: (0, 0)),
                  pl.BlockSpec((1, B_V_DIM), lambda hh, i: (0, 0)),
                  pl.BlockSpec((TQ, LANES), lambda hh, i: (i, hh)),
                  pl.BlockSpec((S, 2 * LANES), lambda hh, i: (0, hh), pipeline_mode=pl.Buffered(1)),
                  pl.BlockSpec((1, nchunks, B_V_DIM, TK), lambda hh, i: (hh, 0, 0, 0),
                               pipeline_mode=pl.Buffered(1)),
                  pl.BlockSpec((1, 5, B_BIAS_BLOCK, B_BIAS_BLOCK), lambda hh, i: (hh, 0, 0, 0))],
        out_specs=pl.BlockSpec((TQ, LANES), lambda hh, i: (i, hh)),
        out_shape=jax.ShapeDtypeStruct((S, B_HEADS * B_V_DIM), bf16),
        scratch_shapes=[pltpu.VMEM((2, LANES, TQ), bf16),
                        pltpu.VMEM((2, LANES, TQ), bf16),
                        pltpu.VMEM((B_SLOTS, 2, TK, TQ), bf16),
                        pltpu.VMEM((B_SLOTS, 2, 1, TQ), jnp.float32),
                        pltpu.VMEM((2, 1, TQ), jnp.float32),
                        pltpu.VMEM((nchunks, 2, 1, TQ), jnp.float32),
                        pltpu.VMEM((2, 1, TQ), jnp.float32),
                        pltpu.VMEM((2, B_V_DIM, TQ), jnp.float32),
                        pltpu.VMEM((2, 1, TQ), jnp.float32)],
        compiler_params=pltpu.CompilerParams(dimension_semantics=("arbitrary", "arbitrary"),
                                             vmem_limit_bytes=VMEM_LIMIT),
        name="diff_attn",
    )(lamv, p["diff_subln_g"].reshape(1, B_V_DIM), qb, kb, vt, bias_b)

    w_out = p["w_out"].astype(bf16)
    a_width = A_Q_HEADS * HEAD_DIM
    d_ff = p["w_gate"].shape[1]
    R = FFN_ROW_TILE
    out = pl.pallas_call(
        _out_ffn_kernel,
        grid=(S // R,),
        in_specs=[pl.BlockSpec((R, 512), lambda i: (i, 0)),
                  pl.BlockSpec((R, 512), lambda i: (i, 0)),
                  pl.BlockSpec((R, D), lambda i: (i, 0)),
                  _resident((a_width, D)),
                  _resident((w_out.shape[0] - a_width, D)),
                  _resident((1, D)),
                  _resident((1, D)),
                  _resident((D, d_ff)),
                  _resident((D, d_ff)),
                  _resident((d_ff, D)),
                  _resident((1, D))],
        out_specs=pl.BlockSpec((R, D), lambda i: (i, 0)),
        out_shape=jax.ShapeDtypeStruct((S, D), jnp.float32),
        compiler_params=pltpu.CompilerParams(dimension_semantics=("arbitrary",),
                                             vmem_limit_bytes=VMEM_LIMIT),
        name="out_ffn",
    )(ya, yb, h, w_out[:a_width], w_out[a_width:], p["attn_post_g"].reshape(1, D),
      p["ffn_pre_g"].reshape(1, D), p["w_gate"].astype(bf16), p["w_up"].astype(bf16),
      p["w_down"].astype(bf16), p["ffn_post_g"].reshape(1, D))
    return out


def kernel(x, attn_pre_g, w_in, a_sink, lambda_q1, lambda_k1, lambda_q2, lambda_k2, diff_subln_g,
           rel_bias, w_out, attn_post_g, ffn_pre_g, w_gate, w_up, w_down, ffn_post_g):
    batch, S, D = x.shape
    depth = w_in.shape[0]
    assert S % B_TILE == 0 and S % A_BLOCK == 0 and S // A_BLOCK >= 2

    tab_a = _toeplitz_bias(rel_bias[:, :A_Q_HEADS], A_BLOCK, 3 * A_BLOCK, lambda t: -t - A_BLOCK)
    qi = np.arange(A_BLOCK)[:, None]
    kj = np.arange(3 * A_BLOCK)[None, :]
    in_window = np.abs(kj - A_BLOCK - qi) <= A_BLOCK
    valid = np.stack([in_window & (kj >= A_BLOCK), in_window, in_window & (kj < 2 * A_BLOCK)])
    bias_a = jnp.where(valid[:, None], tab_a[None] * LOG2E, MASK_VALUE)

    nb = B_BIAS_BLOCK
    assert len(set(_t5_bucket_np(np.arange(nb + 1, 4 * nb)))) == 1
    tab_b = rel_bias[:, A_Q_HEADS:]
    near = [_toeplitz_bias(tab_b, nb, nb, lambda t, d=d: d * nb + t) for d in (-1, 0, 1)]
    far = tab_b[_t5_bucket_np(np.array([-2 * nb, 2 * nb]))].astype(jnp.float32)
    const = [jnp.broadcast_to(far[side][:, None, None], (B_HEADS, nb, nb)) for side in (0, 1)]
    bias_b = jnp.stack([const[0]] + near + [const[1]], axis=1) * LOG2E

    outs = []
    for b in range(batch):
        h = x[b]
        for l in range(depth):
            p = dict(attn_pre_g=attn_pre_g[l], w_in=w_in[l], a_sink=a_sink[l], lambda_q1=lambda_q1[l],
                     lambda_k1=lambda_k1[l], lambda_q2=lambda_q2[l], lambda_k2=lambda_k2[l],
                     diff_subln_g=diff_subln_g[l], w_out=w_out[l], attn_post_g=attn_post_g[l],
                     ffn_pre_g=ffn_pre_g[l], w_gate=w_gate[l], w_up=w_up[l], w_down=w_down[l],
                     ffn_post_g=ffn_post_g[l])
            h = _layer(h, l, p, bias_a, bias_b)
        outs.append(h)
    return jnp.stack(outs)
```

```python
import functools
import math

import numpy as np
import jax
import jax.numpy as jnp
from jax import lax
from jax.experimental import pallas as pl
from jax.experimental.pallas import tpu as pltpu

HEAD_DIM = 64
A_Q_HEADS = 8
A_KV_HEADS = 2
A_BLOCK = 128
A_STEP_BLOCKS = 4
B_HEADS = 4
B_V_DIM = 2 * HEAD_DIM
NUM_BUCKETS = 32
MAX_DISTANCE = 128
EPS = 1e-6
MASK_VALUE = -1e30
LOG2E = math.log2(math.e)
ROW_SUM_LIMIT = 2.0 ** 40

LANES = 128
ROW_TILE = 512
FFN_ROW_TILE = 512
B_TILE = 512
B_QTILE = 4096
B_REF_LANES = 3
B_UNROLL = 2
B_AHEAD = 1
B_SLOTS = 2
B_PROBE = 128
B_BIAS_BLOCK = 128
VMEM_LIMIT = 56 * 1024 * 1024

_NT = (((1,), (1,)), ((), ()))


def _t5_bucket_np(rel):
    nb = NUM_BUCKETS // 2
    max_exact = nb // 2
    ret = np.where(rel > 0, nb, 0)
    n = np.abs(rel)
    nf = np.maximum(n, 1).astype(np.float32)
    large = max_exact + (np.log(nf / np.float32(max_exact)) / np.float32(math.log(MAX_DISTANCE / max_exact))
                         * np.float32(nb - max_exact)).astype(np.int32)
    large = np.minimum(large, nb - 1)
    return (ret + np.where(n < max_exact, n, large)).astype(np.int32)


def _toeplitz_bias(table, rows, cols, rel_of):
    length = rows + cols
    u = np.arange(length)
    t = np.where(u < cols, -u, length - u)
    w = table[_t5_bucket_np(rel_of(t))].astype(jnp.float32).T
    x = jnp.tile(w, (1, rows))[:, :rows * (length - 1)].reshape(w.shape[0], rows, length - 1)
    return x[:, :, :cols]


def _rms(xf, g):
    return xf * lax.rsqrt(jnp.mean(xf * xf, axis=-1, keepdims=True) + EPS) * g


def _in_proj_kernel(x_ref, g_ref, w_ref, kones_ref, qa_ref, ka_ref, va_ref, qb_ref, kb_ref, vt_ref):
    u = _rms(x_ref[...], g_ref[...]).astype(jnp.bfloat16)
    proj = jnp.dot(u, w_ref[...], preferred_element_type=jnp.float32)
    qa_ref[...] = proj[:, 0:512].astype(jnp.bfloat16)
    ka_ref[...] = proj[:, 512:768].astype(jnp.bfloat16)
    va_ref[...] = proj[:, 768:1024].astype(jnp.bfloat16)
    qb_ref[...] = proj[:, 1024:1536].astype(jnp.bfloat16)
    lane = lax.broadcasted_iota(jnp.int32, (proj.shape[0], LANES), 1)
    keep = (lane < HEAD_DIM, lane >= HEAD_DIM)
    for h in range(B_HEADS):
        kpair = proj[:, 1536 + h * LANES:1536 + (h + 1) * LANES]
        for c in range(2):
            col = (2 * h + c) * LANES
            kb_ref[:, col:col + LANES] = jnp.where(keep[c], kpair, kones_ref[:, col:col + LANES]).astype(jnp.bfloat16)
        v = proj[:, 2048 + h * B_V_DIM:2048 + (h + 1) * B_V_DIM]
        vt_ref[h, 0] = v.T.astype(jnp.bfloat16)


def _win_attn_kernel(sink_ref, q_ref, k_ref, v_ref, bias_ref, o_ref, *, nblocks):
    f32 = jnp.float32
    bf16 = jnp.bfloat16
    n = pl.program_id(0)
    rows = A_STEP_BLOCKS * A_BLOCK
    seq = k_ref.shape[0]
    group = A_Q_HEADS // A_KV_HEADS
    start = pl.multiple_of(n * rows, rows)
    prev = pl.multiple_of(jnp.maximum(start - A_BLOCK, 0), A_BLOCK)
    nxt = pl.multiple_of(jnp.minimum(start + rows, seq - A_BLOCK), A_BLOCK)
    kw = jnp.concatenate([k_ref[pl.ds(prev, A_BLOCK), :], k_ref[pl.ds(start, rows), :],
                          k_ref[pl.ds(nxt, A_BLOCK), :]], axis=0)
    vw = jnp.concatenate([v_ref[pl.ds(prev, A_BLOCK), :], v_ref[pl.ds(start, rows), :],
                          v_ref[pl.ds(nxt, A_BLOCK), :]], axis=0)
    lane = lax.broadcasted_iota(jnp.int32, (A_BLOCK, LANES), 1)
    low = lane < HEAD_DIM
    ones = jnp.ones((3 * A_BLOCK, LANES), bf16)
    for b in range(A_STEP_BLOCKS):
        blk = n * A_STEP_BLOCKS + b
        if b == 0:
            variant = jnp.where(blk == 0, 0, 1)
        elif b == A_STEP_BLOCKS - 1:
            variant = jnp.where(blk == nblocks - 1, 2, 1)
        else:
            variant = 1
        q = q_ref[b * A_BLOCK:(b + 1) * A_BLOCK, :]
        for g in range(A_KV_HEADS):
            heads = range(g * group, (g + 1) * group)
            qs = []
            for h in heads:
                q2 = q[:, (h // 2) * LANES:(h // 2 + 1) * LANES]
                qs.append(jnp.where(low if h % 2 == 0 else jnp.logical_not(low), q2, jnp.zeros_like(q2)))
            q4 = jnp.concatenate(qs, axis=0)
            kg = kw[b * A_BLOCK:(b + 3) * A_BLOCK, g * LANES:(g + 1) * LANES]
            vg = jnp.concatenate([vw[b * A_BLOCK:(b + 3) * A_BLOCK, g * LANES:(g + 1) * LANES], ones],
                                 axis=1)
            s = lax.dot_general(q4, kg, _NT, preferred_element_type=f32)
            ps, sink_terms = [], []
            for t, h in enumerate(heads):
                sh = s[t * A_BLOCK:(t + 1) * A_BLOCK] + bias_ref[variant, h]
                snk = sink_ref[h] * LOG2E
                m = jnp.maximum(jnp.max(sh, axis=-1, keepdims=True), snk)
                ps.append(jnp.exp2(sh - m).astype(bf16))
                sink_terms.append(jnp.exp2(snk - m))
            p = jnp.concatenate(ps, axis=0)
            res = jnp.dot(p, vg, preferred_element_type=f32)
            on = res[:, 0:LANES] / (res[:, LANES:2 * LANES] + jnp.concatenate(sink_terms, axis=0))
            for t in range(group // 2):
                even = on[(2 * t) * A_BLOCK:(2 * t + 1) * A_BLOCK]
                odd = on[(2 * t + 1) * A_BLOCK:(2 * t + 2) * A_BLOCK]
                pair = (g * group) // 2 + t
                o_ref[b * A_BLOCK:(b + 1) * A_BLOCK, pair * LANES:(pair + 1) * LANES] = (
                    jnp.where(low, even, odd).astype(o_ref.dtype))


def _diff_attn_kernel(lamv_ref, g_ref, q_ref, k_ref, vt_ref, bias_ref, o_ref,
                      qz_ref, qp_ref, p_ref, psum_ref, r_ref, flag_ref, any_ref, acc_ref, den_ref,
                      *, nchunks, lambda_init):
    TQ = B_QTILE
    TK = B_TILE
    i = pl.program_id(1)
    f32 = jnp.float32
    bf16 = jnp.bfloat16

    qt = q_ref[...].astype(f32).T
    row = lax.broadcasted_iota(jnp.int32, (LANES, TQ), 0)
    is_q = (row < HEAD_DIM, row >= HEAD_DIM)
    ref_row = (HEAD_DIM, 0)

    ksub = TK // B_BIAS_BLOCK
    qsub = TQ // B_BIAS_BLOCK

    def bias_tile(j, rows=ksub):
        base = j * ksub - i * qsub
        by_dist = {d: bias_ref[0, jnp.clip(base + d, -2, 2) + 2] for d in range(1 - qsub, rows)}
        return jnp.concatenate(
            [jnp.concatenate([by_dist[a - b] for b in range(qsub)], axis=1) for a in range(rows)], axis=0)

    def k_chunk(j, c):
        return k_ref[pl.ds(pl.multiple_of(j * TK, TK), TK), c * LANES:(c + 1) * LANES]

    def set_reference(c, r_row):
        hi = r_row.astype(bf16).astype(f32)
        rem = r_row - hi
        mid = rem.astype(bf16).astype(f32)
        low = (rem - mid).astype(bf16).astype(f32)
        first = ref_row[c]
        ext = jnp.where(row == first, -hi, jnp.where(row == first + 1, -mid,
                                                     jnp.where(row == first + 2, -low, 0.0)))
        qp_ref[c] = jnp.where(is_q[c], qt, ext).astype(bf16)

    def exact_chunk(j, c, flagged):
        s = jnp.dot(k_chunk(j, c), qz_ref[c], preferred_element_type=f32) + bias_tile(j)
        r_old = r_ref[c]
        r_new = jnp.where(flagged, jnp.maximum(r_old, jnp.max(s, axis=0, keepdims=True)), r_old)
        p = jnp.where(flagged, jnp.exp2(s - r_new), 0.0)
        pv = jnp.dot(vt_ref[0, j], p.astype(bf16), preferred_element_type=f32)
        alpha = jnp.exp2(r_old - r_new)
        acc_ref[c] = acc_ref[c] * alpha + pv
        den_ref[c] = den_ref[c] * alpha + jnp.sum(p, axis=0, keepdims=True)
        r_ref[c] = r_new

    def stage_a(j, slot):
        bias = bias_tile(j)
        for c in range(2):
            s = jnp.dot(k_chunk(j, c), qp_ref[c], preferred_element_type=f32) + bias
            p = jnp.exp2(s)
            psum_ref[slot, c] = jnp.sum(p, axis=0, keepdims=True)
            p_ref[slot, c] = p.astype(bf16)

    def stage_c(j, slot):
        vt = vt_ref[0, j]
        for c in range(2):
            pv = jnp.dot(vt, p_ref[slot, c], preferred_element_type=f32)
            psum = psum_ref[slot, c]
            ok = psum <= ROW_SUM_LIMIT
            acc_ref[c] = acc_ref[c] + jnp.where(ok, pv, 0.0)
            den_ref[c] = den_ref[c] + jnp.where(ok, psum, 0.0)
            flag = jnp.where(ok, 0.0, 1.0)
            flag_ref[c, pl.ds(j, 1), :] = flag
            any_ref[c] = jnp.maximum(any_ref[c], flag)

    for c in range(2):
        qz_ref[c] = jnp.where(is_q[c], qt, 0.0).astype(bf16)
    acc_ref[...] = jnp.zeros(acc_ref.shape, f32)
    den_ref[...] = jnp.zeros(den_ref.shape, f32)
    any_ref[...] = jnp.zeros(any_ref.shape, f32)
    diag = i * (TQ // TK)
    probe_bias = bias_tile(diag, rows=B_PROBE // B_BIAS_BLOCK)
    for c in range(2):
        kp = k_ref[pl.ds(pl.multiple_of(diag * TK, TK), B_PROBE), c * LANES:(c + 1) * LANES]
        s = jnp.dot(kp, qz_ref[c], preferred_element_type=f32) + probe_bias
        r = jnp.max(s, axis=0, keepdims=True)
        r_ref[c] = r
        set_reference(c, r)

    def run(j0, has_next):
        for u in range(B_UNROLL):
            stage_c(j0 + u, u % B_SLOTS)
            if u + B_AHEAD < B_UNROLL or has_next:
                stage_a(j0 + u + B_AHEAD, (u + B_AHEAD) % B_SLOTS)

    nbody = nchunks // B_UNROLL
    for u in range(B_AHEAD):
        stage_a(u, u)

    def body(b, carry):
        run(b * B_UNROLL, True)
        return carry

    lax.fori_loop(0, nbody - 1, body, 0)
    run((nbody - 1) * B_UNROLL, False)

    @pl.when(jnp.max(jnp.maximum(any_ref[0], any_ref[1])) > 0.0)
    def _():
        def redo(j, carry):
            @pl.when(jnp.max(jnp.maximum(flag_ref[0, pl.ds(j, 1), :], flag_ref[1, pl.ds(j, 1), :])) > 0.0)
            def _():
                for c in range(2):
                    exact_chunk(j, c, flag_ref[c, pl.ds(j, 1), :] > 0.0)
            return carry

        lax.fori_loop(0, nchunks, redo, 0)

    lv = lamv_ref[...]
    lam = (jnp.exp(jnp.sum(lv[0:1] * lv[1:2], axis=-1, keepdims=True))
           - jnp.exp(jnp.sum(lv[2:3] * lv[3:4], axis=-1, keepdims=True)) + lambda_init)
    ot = acc_ref[0] / den_ref[0] - lam * (acc_ref[1] / den_ref[1])
    o = ot.T
    o_ref[...] = (_rms(o, g_ref[...]) * (1.0 - lambda_init)).astype(o_ref.dtype)


def _out_ffn_kernel(ya_ref, yb_ref, x_ref, woa_ref, wob_ref, gpost_ref, gpre_ref,
                    wg_ref, wu_ref, wd_ref, gfpost_ref, o_ref):
    y = (jnp.dot(ya_ref[...], woa_ref[...], preferred_element_type=jnp.float32)
         + jnp.dot(yb_ref[...], wob_ref[...], preferred_element_type=jnp.float32))
    h1 = x_ref[...] + _rms(y, gpost_ref[...])
    u = _rms(h1, gpre_ref[...]).astype(jnp.bfloat16)
    gate = jnp.dot(u, wg_ref[...], preferred_element_type=jnp.float32)
    up = jnp.dot(u, wu_ref[...], preferred_element_type=jnp.float32)
    act = (gate * jax.nn.sigmoid(gate) * up).astype(jnp.bfloat16)
    f = jnp.dot(act, wd_ref[...], preferred_element_type=jnp.float32)
    o_ref[...] = h1 + _rms(f, gfpost_ref[...])


def _resident(shape):
    zeros = (0,) * len(shape)
    return pl.BlockSpec(shape, lambda *_: zeros, pipeline_mode=pl.Buffered(1))


def _layer(h, l, p, bias_a, bias_b):
    S, D = h.shape
    bf16 = jnp.bfloat16
    lambda_init = 0.8 - 0.6 * math.exp(-0.3 * l)
    scale = HEAD_DIM ** -0.5

    w = p["w_in"]
    a_q, a_kv, b_qk = A_Q_HEADS * HEAD_DIM, A_KV_HEADS * HEAD_DIM, B_HEADS * 2 * HEAD_DIM
    c0 = a_q
    c1 = c0 + a_kv
    c2 = c1 + a_kv
    c3 = c2 + b_qk

    def dup(cols):
        parts = []
        for g in range(A_KV_HEADS):
            blk = cols[:, g * HEAD_DIM:(g + 1) * HEAD_DIM]
            parts += [blk, blk]
        return jnp.concatenate(parts, axis=1)

    kones = np.zeros((B_HEADS, 2, LANES), np.float32)
    kones[:, 0, HEAD_DIM:HEAD_DIM + B_REF_LANES] = 1.0
    kones[:, 1, 0:B_REF_LANES] = 1.0
    kones = jnp.asarray(kones.reshape(1, B_HEADS * 2 * LANES))

    w_cat = jnp.concatenate([w[:, :c0] * (scale * LOG2E), dup(w[:, c0:c1]), dup(w[:, c1:c2]),
                             w[:, c2:c3] * (scale * LOG2E), w[:, c3:]],
                            axis=1).astype(bf16)
    ncols = w_cat.shape[1]
    nrow = S // ROW_TILE
    nchunks = S // B_TILE
    assert ROW_TILE == B_TILE and nchunks % B_UNROLL == 0
    assert B_UNROLL % B_SLOTS == 0 and B_AHEAD < B_SLOTS and B_AHEAD <= B_UNROLL
    kb_cols = B_HEADS * 2 * LANES

    qa, ka, va, qb, kb, vt = pl.pallas_call(
        _in_proj_kernel,
        grid=(nrow,),
        in_specs=[pl.BlockSpec((ROW_TILE, D), lambda i: (i, 0)),
                  _resident((1, D)),
                  _resident((D, ncols)),
                  _resident((1, kb_cols))],
        out_specs=[pl.BlockSpec((ROW_TILE, 512), lambda i: (i, 0)),
                   pl.BlockSpec((ROW_TILE, 256), lambda i: (i, 0)),
                   pl.BlockSpec((ROW_TILE, 256), lambda i: (i, 0)),
                   pl.BlockSpec((ROW_TILE, 512), lambda i: (i, 0)),
                   pl.BlockSpec((ROW_TILE, kb_cols), lambda i: (i, 0)),
                   pl.BlockSpec((B_HEADS, 1, B_V_DIM, ROW_TILE), lambda i: (0, i, 0, 0))],
        out_shape=[jax.ShapeDtypeStruct((S, 512), bf16),
                   jax.ShapeDtypeStruct((S, 256), bf16),
                   jax.ShapeDtypeStruct((S, 256), bf16),
                   jax.ShapeDtypeStruct((S, 512), bf16),
                   jax.ShapeDtypeStruct((S, kb_cols), bf16),
                   jax.ShapeDtypeStruct((B_HEADS, nchunks, B_V_DIM, B_TILE), bf16)],
        compiler_params=pltpu.CompilerParams(dimension_semantics=("arbitrary",),
                                             vmem_limit_bytes=VMEM_LIMIT),
        name="in_proj",
    )(h, p["attn_pre_g"].reshape(1, D), w_cat, kones)

    nblocks = S // A_BLOCK
    a_rows = A_STEP_BLOCKS * A_BLOCK
    assert nblocks % A_STEP_BLOCKS == 0 and A_STEP_BLOCKS >= 2
    ya = pl.pallas_call(
        functools.partial(_win_attn_kernel, nblocks=nblocks),
        grid=(nblocks // A_STEP_BLOCKS,),
        in_specs=[pl.BlockSpec(memory_space=pltpu.SMEM),
                  pl.BlockSpec((a_rows, 512), lambda n: (n, 0)),
                  _resident(ka.shape),
                  _resident(va.shape),
                  _resident(bias_a.shape)],
        out_specs=pl.BlockSpec((a_rows, 512), lambda n: (n, 0)),
        out_shape=jax.ShapeDtypeStruct((S, 512), bf16),
        compiler_params=pltpu.CompilerParams(dimension_semantics=("arbitrary",),
                                             vmem_limit_bytes=VMEM_LIMIT),
        name="win_attn",
    )(p["a_sink"], qa, ka, va, bias_a)

    lamv = jnp.stack([p["lambda_q1"], p["lambda_k1"], p["lambda_q2"], p["lambda_k2"]])
    TQ, TK = B_QTILE, B_TILE
    assert S % TQ == 0 and TQ % TK == 0
    yb = pl.pallas_call(
        functools.partial(_diff_attn_kernel, nchunks=nchunks, lambda_init=lambda_init),
        grid=(B_HEADS, S // TQ),
        in_specs=[pl.BlockSpec((4, HEAD_DIM), lambda hh, i: (0, 0)),
                  pl.BlockSpec((1, B_V_DIM), lambda hh, i: (0, 0)),
                  pl.BlockSpec((TQ, LANES), lambda hh, i: (i, hh)),
                  pl.BlockSpec((S, 2 * LANES), lambda hh, i: (0, hh), pipeline_mode=pl.Buffered(1)),
                  pl.BlockSpec((1, nchunks, B_V_DIM, TK), lambda hh, i: (hh, 0, 0, 0),
                               pipeline_mode=pl.Buffered(1)),
                  pl.BlockSpec((1, 5, B_BIAS_BLOCK, B_BIAS_BLOCK), lambda hh, i: (hh, 0, 0, 0))],
        out_specs=pl.BlockSpec((TQ, LANES), lambda hh, i: (i, hh)),
        out_shape=jax.ShapeDtypeStruct((S, B_HEADS * B_V_DIM), bf16),
        scratch_shapes=[pltpu.VMEM((2, LANES, TQ), bf16),
                        pltpu.VMEM((2, LANES, TQ), bf16),
                        pltpu.VMEM((B_SLOTS, 2, TK, TQ), bf16),
                        pltpu.VMEM((B_SLOTS, 2, 1, TQ), jnp.float32),
                        pltpu.VMEM((2, 1, TQ), jnp.float32),
                        pltpu.VMEM((2, nchunks, TQ), jnp.float32),
                        pltpu.VMEM((2, 1, TQ), jnp.float32),
                        pltpu.VMEM((2, B_V_DIM, TQ), jnp.float32),
                        pltpu.VMEM((2, 1, TQ), jnp.float32)],
        compiler_params=pltpu.CompilerParams(dimension_semantics=("arbitrary", "arbitrary"),
                                             vmem_limit_bytes=VMEM_LIMIT),
        name="diff_attn",
    )(lamv, p["diff_subln_g"].reshape(1, B_V_DIM), qb, kb, vt, bias_b)

    w_out = p["w_out"].astype(bf16)
    a_width = A_Q_HEADS * HEAD_DIM
    d_ff = p["w_gate"].shape[1]
    R = FFN_ROW_TILE
    out = pl.pallas_call(
        _out_ffn_kernel,
        grid=(S // R,),
        in_specs=[pl.BlockSpec((R, 512), lambda i: (i, 0)),
                  pl.BlockSpec((R, 512), lambda i: (i, 0)),
                  pl.BlockSpec((R, D), lambda i: (i, 0)),
                  _resident((a_width, D)),
                  _resident((w_out.shape[0] - a_width, D)),
                  _resident((1, D)),
                  _resident((1, D)),
                  _resident((D, d_ff)),
                  _resident((D, d_ff)),
                  _resident((d_ff, D)),
                  _resident((1, D))],
        out_specs=pl.BlockSpec((R, D), lambda i: (i, 0)),
        out_shape=jax.ShapeDtypeStruct((S, D), jnp.float32),
        compiler_params=pltpu.CompilerParams(dimension_semantics=("arbitrary",),
                                             vmem_limit_bytes=VMEM_LIMIT),
        name="out_ffn",
    )(ya, yb, h, w_out[:a_width], w_out[a_width:], p["attn_post_g"].reshape(1, D),
      p["ffn_pre_g"].reshape(1, D), p["w_gate"].astype(bf16), p["w_up"].astype(bf16),
      p["w_down"].astype(bf16), p["ffn_post_g"].reshape(1, D))
    return out


def kernel(x, attn_pre_g, w_in, a_sink, lambda_q1, lambda_k1, lambda_q2, lambda_k2, diff_subln_g,
           rel_bias, w_out, attn_post_g, ffn_pre_g, w_gate, w_up, w_down, ffn_post_g):
    batch, S, D = x.shape
    depth = w_in.shape[0]
    assert S % B_TILE == 0 and S % A_BLOCK == 0 and S // A_BLOCK >= 2

    tab_a = _toeplitz_bias(rel_bias[:, :A_Q_HEADS], A_BLOCK, 3 * A_BLOCK, lambda t: -t - A_BLOCK)
    qi = np.arange(A_BLOCK)[:, None]
    kj = np.arange(3 * A_BLOCK)[None, :]
    in_window = np.abs(kj - A_BLOCK - qi) <= A_BLOCK
    valid = np.stack([in_window & (kj >= A_BLOCK), in_window, in_window & (kj < 2 * A_BLOCK)])
    bias_a = jnp.where(valid[:, None], tab_a[None] * LOG2E, MASK_VALUE)

    nb = B_BIAS_BLOCK
    assert len(set(_t5_bucket_np(np.arange(nb + 1, 4 * nb)))) == 1
    tab_b = rel_bias[:, A_Q_HEADS:]
    near = [_toeplitz_bias(tab_b, nb, nb, lambda t, d=d: d * nb + t) for d in (-1, 0, 1)]
    far = tab_b[_t5_bucket_np(np.array([-2 * nb, 2 * nb]))].astype(jnp.float32)
    const = [jnp.broadcast_to(far[side][:, None, None], (B_HEADS, nb, nb)) for side in (0, 1)]
    bias_b = jnp.stack([const[0]] + near + [const[1]], axis=1) * LOG2E

    outs = []
    for b in range(batch):
        h = x[b]
        for l in range(depth):
            p = dict(attn_pre_g=attn_pre_g[l], w_in=w_in[l], a_sink=a_sink[l], lambda_q1=lambda_q1[l],
                     lambda_k1=lambda_k1[l], lambda_q2=lambda_q2[l], lambda_k2=lambda_k2[l],
                     diff_subln_g=diff_subln_g[l], w_out=w_out[l], attn_post_g=attn_post_g[l],
                     ffn_pre_g=ffn_pre_g[l], w_gate=w_gate[l], w_up=w_up[l], w_down=w_down[l],
                     ffn_post_g=ffn_post_g[l])
            h = _layer(h, l, p, bias_a, bias_b)
        outs.append(h)
    return jnp.stack(outs)
```

```python
import functools
import math

import numpy as np
import jax
import jax.numpy as jnp
from jax import lax
from jax.experimental import pallas as pl
from jax.experimental.pallas import tpu as pltpu

HEAD_DIM = 64
A_Q_HEADS = 8
A_KV_HEADS = 2
A_BLOCK = 128
A_STEP_BLOCKS = 8
B_HEADS = 4
B_V_DIM = 2 * HEAD_DIM
NUM_BUCKETS = 32
MAX_DISTANCE = 128
EPS = 1e-6
MASK_VALUE = -1e30
LOG2E = math.log2(math.e)
ROW_SUM_LIMIT = 2.0 ** 40

LANES = 128
ROW_TILE = 1024
FFN_ROW_TILE = 512
B_TILE = 512
B_QTILE = 4096
B_REF_LANES = 3
B_UNROLL = 2
B_AHEAD = 1
B_SLOTS = 2
B_PROBE = 128
B_BIAS_BLOCK = 128
VMEM_LIMIT = 56 * 1024 * 1024

_NT = (((1,), (1,)), ((), ()))


def _t5_bucket_np(rel):
    nb = NUM_BUCKETS // 2
    max_exact = nb // 2
    ret = np.where(rel > 0, nb, 0)
    n = np.abs(rel)
    nf = np.maximum(n, 1).astype(np.float32)
    large = max_exact + (np.log(nf / np.float32(max_exact)) / np.float32(math.log(MAX_DISTANCE / max_exact))
                         * np.float32(nb - max_exact)).astype(np.int32)
    large = np.minimum(large, nb - 1)
    return (ret + np.where(n < max_exact, n, large)).astype(np.int32)


def _toeplitz_bias(table, rows, cols, rel_of):
    length = rows + cols
    u = np.arange(length)
    t = np.where(u < cols, -u, length - u)
    w = table[_t5_bucket_np(rel_of(t))].astype(jnp.float32).T
    x = jnp.tile(w, (1, rows))[:, :rows * (length - 1)].reshape(w.shape[0], rows, length - 1)
    return x[:, :, :cols]


def _rms(xf, g):
    return xf * lax.rsqrt(jnp.mean(xf * xf, axis=-1, keepdims=True) + EPS) * g


def _in_proj_kernel(x_ref, g_ref, w_ref, kones_ref, qa_ref, ka_ref, va_ref, qb_ref, kb_ref, vt_ref):
    u = _rms(x_ref[...], g_ref[...]).astype(jnp.bfloat16)
    proj = jnp.dot(u, w_ref[...], preferred_element_type=jnp.float32)
    qa_ref[...] = proj[:, 0:512].astype(jnp.bfloat16)
    ka_ref[...] = proj[:, 512:768].astype(jnp.bfloat16)
    va_ref[...] = proj[:, 768:1024].astype(jnp.bfloat16)
    qb_ref[...] = proj[:, 1024:1536].astype(jnp.bfloat16)
    lane = lax.broadcasted_iota(jnp.int32, (proj.shape[0], LANES), 1)
    keep = (lane < HEAD_DIM, lane >= HEAD_DIM)
    for h in range(B_HEADS):
        kpair = proj[:, 1536 + h * LANES:1536 + (h + 1) * LANES]
        for c in range(2):
            col = (2 * h + c) * LANES
            kb_ref[:, col:col + LANES] = jnp.where(keep[c], kpair, kones_ref[:, col:col + LANES]).astype(jnp.bfloat16)
        vt = proj[:, 2048 + h * B_V_DIM:2048 + (h + 1) * B_V_DIM].T.astype(jnp.bfloat16)
        for t in range(vt_ref.shape[1]):
            vt_ref[h, t] = vt[:, t * B_TILE:(t + 1) * B_TILE]


def _win_attn_kernel(sink_ref, q_ref, k_ref, v_ref, bias_ref, o_ref, *, nblocks):
    f32 = jnp.float32
    bf16 = jnp.bfloat16
    n = pl.program_id(0)
    rows = A_STEP_BLOCKS * A_BLOCK
    seq = k_ref.shape[0]
    group = A_Q_HEADS // A_KV_HEADS
    start = pl.multiple_of(n * rows, rows)
    prev = pl.multiple_of(jnp.maximum(start - A_BLOCK, 0), A_BLOCK)
    nxt = pl.multiple_of(jnp.minimum(start + rows, seq - A_BLOCK), A_BLOCK)
    kw = jnp.concatenate([k_ref[pl.ds(prev, A_BLOCK), :], k_ref[pl.ds(start, rows), :],
                          k_ref[pl.ds(nxt, A_BLOCK), :]], axis=0)
    vw = jnp.concatenate([v_ref[pl.ds(prev, A_BLOCK), :], v_ref[pl.ds(start, rows), :],
                          v_ref[pl.ds(nxt, A_BLOCK), :]], axis=0)
    lane = lax.broadcasted_iota(jnp.int32, (A_BLOCK, LANES), 1)
    low = lane < HEAD_DIM
    ones = jnp.ones((3 * A_BLOCK, LANES), bf16)
    for b in range(A_STEP_BLOCKS):
        blk = n * A_STEP_BLOCKS + b
        if b == 0:
            variant = jnp.where(blk == 0, 0, 1)
        elif b == A_STEP_BLOCKS - 1:
            variant = jnp.where(blk == nblocks - 1, 2, 1)
        else:
            variant = 1
        q = q_ref[b * A_BLOCK:(b + 1) * A_BLOCK, :]
        for g in range(A_KV_HEADS):
            heads = range(g * group, (g + 1) * group)
            qs = []
            for h in heads:
                q2 = q[:, (h // 2) * LANES:(h // 2 + 1) * LANES]
                qs.append(jnp.where(low if h % 2 == 0 else jnp.logical_not(low), q2, jnp.zeros_like(q2)))
            q4 = jnp.concatenate(qs, axis=0)
            kg = kw[b * A_BLOCK:(b + 3) * A_BLOCK, g * LANES:(g + 1) * LANES]
            vg = jnp.concatenate([vw[b * A_BLOCK:(b + 3) * A_BLOCK, g * LANES:(g + 1) * LANES], ones],
                                 axis=1)
            s = lax.dot_general(q4, kg, _NT, preferred_element_type=f32)
            ps, sink_terms = [], []
            for t, h in enumerate(heads):
                sh = s[t * A_BLOCK:(t + 1) * A_BLOCK] + bias_ref[variant, h]
                snk = sink_ref[h] * LOG2E
                m = jnp.maximum(jnp.max(sh, axis=-1, keepdims=True), snk)
                ps.append(jnp.exp2(sh - m).astype(bf16))
                sink_terms.append(jnp.exp2(snk - m))
            p = jnp.concatenate(ps, axis=0)
            res = jnp.dot(p, vg, preferred_element_type=f32)
            on = res[:, 0:LANES] / (res[:, LANES:2 * LANES] + jnp.concatenate(sink_terms, axis=0))
            for t in range(group // 2):
                even = on[(2 * t) * A_BLOCK:(2 * t + 1) * A_BLOCK]
                odd = on[(2 * t + 1) * A_BLOCK:(2 * t + 2) * A_BLOCK]
                pair = (g * group) // 2 + t
                o_ref[b * A_BLOCK:(b + 1) * A_BLOCK, pair * LANES:(pair + 1) * LANES] = (
                    jnp.where(low, even, odd).astype(o_ref.dtype))


def _diff_attn_kernel(lamv_ref, g_ref, q_ref, k_ref, vt_ref, bias_ref, o_ref,
                      qz_ref, qp_ref, p_ref, psum_ref, r_ref, flag_ref, any_ref, acc_ref, den_ref,
                      *, nchunks, lambda_init):
    TQ = B_QTILE
    TK = B_TILE
    i = pl.program_id(1)
    f32 = jnp.float32
    bf16 = jnp.bfloat16

    qt = q_ref[...].astype(f32).T
    row = lax.broadcasted_iota(jnp.int32, (LANES, TQ), 0)
    is_q = (row < HEAD_DIM, row >= HEAD_DIM)
    ref_row = (HEAD_DIM, 0)

    ksub = TK // B_BIAS_BLOCK
    qsub = TQ // B_BIAS_BLOCK

    def bias_tile(j, rows=ksub):
        base = j * ksub - i * qsub
        by_dist = {d: bias_ref[0, jnp.clip(base + d, -2, 2) + 2] for d in range(1 - qsub, rows)}
        return jnp.concatenate(
            [jnp.concatenate([by_dist[a - b] for b in range(qsub)], axis=1) for a in range(rows)], axis=0)

    def k_chunk(j, c):
        return k_ref[pl.ds(pl.multiple_of(j * TK, TK), TK), c * LANES:(c + 1) * LANES]

    def set_reference(c, r_row):
        hi = r_row.astype(bf16).astype(f32)
        rem = r_row - hi
        mid = rem.astype(bf16).astype(f32)
        low = (rem - mid).astype(bf16).astype(f32)
        first = ref_row[c]
        ext = jnp.where(row == first, -hi, jnp.where(row == first + 1, -mid,
                                                     jnp.where(row == first + 2, -low, 0.0)))
        qp_ref[c] = jnp.where(is_q[c], qt, ext).astype(bf16)

    def exact_chunk(j, c, flagged):
        s = jnp.dot(k_chunk(j, c), qz_ref[c], preferred_element_type=f32) + bias_tile(j)
        r_old = r_ref[c]
        r_new = jnp.where(flagged, jnp.maximum(r_old, jnp.max(s, axis=0, keepdims=True)), r_old)
        p = jnp.where(flagged, jnp.exp2(s - r_new), 0.0)
        pv = jnp.dot(vt_ref[0, j], p.astype(bf16), preferred_element_type=f32)
        alpha = jnp.exp2(r_old - r_new)
        acc_ref[c] = acc_ref[c] * alpha + pv
        den_ref[c] = den_ref[c] * alpha + jnp.sum(p, axis=0, keepdims=True)
        r_ref[c] = r_new

    def stage_a(j, slot):
        bias = bias_tile(j)
        for c in range(2):
            s = jnp.dot(k_chunk(j, c), qp_ref[c], preferred_element_type=f32) + bias
            p = jnp.exp2(s)
            psum_ref[slot, c] = jnp.sum(p, axis=0, keepdims=True)
            p_ref[slot, c] = p.astype(bf16)

    def stage_c(j, slot):
        vt = vt_ref[0, j]
        for c in range(2):
            pv = jnp.dot(vt, p_ref[slot, c], preferred_element_type=f32)
            psum = psum_ref[slot, c]
            ok = psum <= ROW_SUM_LIMIT
            acc_ref[c] = acc_ref[c] + jnp.where(ok, pv, 0.0)
            den_ref[c] = den_ref[c] + jnp.where(ok, psum, 0.0)
            flag = jnp.where(ok, 0.0, 1.0)
            flag_ref[c, pl.ds(j, 1), :] = flag
            any_ref[c] = jnp.maximum(any_ref[c], flag)

    for c in range(2):
        qz_ref[c] = jnp.where(is_q[c], qt, 0.0).astype(bf16)
    acc_ref[...] = jnp.zeros(acc_ref.shape, f32)
    den_ref[...] = jnp.zeros(den_ref.shape, f32)
    any_ref[...] = jnp.zeros(any_ref.shape, f32)
    diag = i * (TQ // TK)
    probe_bias = bias_tile(diag, rows=B_PROBE // B_BIAS_BLOCK)
    for c in range(2):
        kp = k_ref[pl.ds(pl.multiple_of(diag * TK, TK), B_PROBE), c * LANES:(c + 1) * LANES]
        s = jnp.dot(kp, qz_ref[c], preferred_element_type=f32) + probe_bias
        r = jnp.max(s, axis=0, keepdims=True)
        r_ref[c] = r
        set_reference(c, r)

    def run(j0, has_next):
        for u in range(B_UNROLL):
            stage_c(j0 + u, u % B_SLOTS)
            if u + B_AHEAD < B_UNROLL or has_next:
                stage_a(j0 + u + B_AHEAD, (u + B_AHEAD) % B_SLOTS)

    nbody = nchunks // B_UNROLL
    for u in range(B_AHEAD):
        stage_a(u, u)

    def body(b, carry):
        run(b * B_UNROLL, True)
        return carry

    lax.fori_loop(0, nbody - 1, body, 0)
    run((nbody - 1) * B_UNROLL, False)

    @pl.when(jnp.max(jnp.maximum(any_ref[0], any_ref[1])) > 0.0)
    def _():
        def redo(j, carry):
            @pl.when(jnp.max(jnp.maximum(flag_ref[0, pl.ds(j, 1), :], flag_ref[1, pl.ds(j, 1), :])) > 0.0)
            def _():
                for c in range(2):
                    exact_chunk(j, c, flag_ref[c, pl.ds(j, 1), :] > 0.0)
            return carry

        lax.fori_loop(0, nchunks, redo, 0)

    lv = lamv_ref[...]
    lam = (jnp.exp(jnp.sum(lv[0:1] * lv[1:2], axis=-1, keepdims=True))
           - jnp.exp(jnp.sum(lv[2:3] * lv[3:4], axis=-1, keepdims=True)) + lambda_init)
    ot = acc_ref[0] / den_ref[0] - lam * (acc_ref[1] / den_ref[1])
    o = ot.T
    o_ref[...] = (_rms(o, g_ref[...]) * (1.0 - lambda_init)).astype(o_ref.dtype)


def _out_ffn_kernel(ya_ref, yb_ref, x_ref, woa_ref, wob_ref, gpost_ref, gpre_ref,
                    wg_ref, wu_ref, wd_ref, gfpost_ref, o_ref):
    y = (jnp.dot(ya_ref[...], woa_ref[...], preferred_element_type=jnp.float32)
         + jnp.dot(yb_ref[...], wob_ref[...], preferred_element_type=jnp.float32))
    h1 = x_ref[...] + _rms(y, gpost_ref[...])
    u = _rms(h1, gpre_ref[...]).astype(jnp.bfloat16)
    gate = jnp.dot(u, wg_ref[...], preferred_element_type=jnp.float32)
    up = jnp.dot(u, wu_ref[...], preferred_element_type=jnp.float32)
    act = (gate * jax.nn.sigmoid(gate) * up).astype(jnp.bfloat16)
    f = jnp.dot(act, wd_ref[...], preferred_element_type=jnp.float32)
    o_ref[...] = h1 + _rms(f, gfpost_ref[...])


def _resident(shape):
    zeros = (0,) * len(shape)
    return pl.BlockSpec(shape, lambda *_: zeros, pipeline_mode=pl.Buffered(1))


def _layer(h, l, p, bias_a, bias_b):
    S, D = h.shape
    bf16 = jnp.bfloat16
    lambda_init = 0.8 - 0.6 * math.exp(-0.3 * l)
    scale = HEAD_DIM ** -0.5

    w = p["w_in"]
    a_q, a_kv, b_qk = A_Q_HEADS * HEAD_DIM, A_KV_HEADS * HEAD_DIM, B_HEADS * 2 * HEAD_DIM
    c0 = a_q
    c1 = c0 + a_kv
    c2 = c1 + a_kv
    c3 = c2 + b_qk

    def dup(cols):
        parts = []
        for g in range(A_KV_HEADS):
            blk = cols[:, g * HEAD_DIM:(g + 1) * HEAD_DIM]
            parts += [blk, blk]
        return jnp.concatenate(parts, axis=1)

    kones = np.zeros((B_HEADS, 2, LANES), np.float32)
    kones[:, 0, HEAD_DIM:HEAD_DIM + B_REF_LANES] = 1.0
    kones[:, 1, 0:B_REF_LANES] = 1.0
    kones = jnp.asarray(kones.reshape(1, B_HEADS * 2 * LANES))

    w_cat = jnp.concatenate([w[:, :c0] * (scale * LOG2E), dup(w[:, c0:c1]), dup(w[:, c1:c2]),
                             w[:, c2:c3] * (scale * LOG2E), w[:, c3:]],
                            axis=1).astype(bf16)
    ncols = w_cat.shape[1]
    nrow = S // ROW_TILE
    nchunks = S // B_TILE
    assert ROW_TILE % B_TILE == 0 and S % ROW_TILE == 0 and nchunks % B_UNROLL == 0
    assert B_UNROLL % B_SLOTS == 0 and B_AHEAD < B_SLOTS and B_AHEAD <= B_UNROLL
    kb_cols = B_HEADS * 2 * LANES

    qa, ka, va, qb, kb, vt = pl.pallas_call(
        _in_proj_kernel,
        grid=(nrow,),
        in_specs=[pl.BlockSpec((ROW_TILE, D), lambda i: (i, 0)),
                  _resident((1, D)),
                  _resident((D, ncols)),
                  _resident((1, kb_cols))],
        out_specs=[pl.BlockSpec((ROW_TILE, 512), lambda i: (i, 0)),
                   pl.BlockSpec((ROW_TILE, 256), lambda i: (i, 0)),
                   pl.BlockSpec((ROW_TILE, 256), lambda i: (i, 0)),
                   pl.BlockSpec((ROW_TILE, 512), lambda i: (i, 0)),
                   pl.BlockSpec((ROW_TILE, kb_cols), lambda i: (i, 0)),
                   pl.BlockSpec((B_HEADS, ROW_TILE // B_TILE, B_V_DIM, B_TILE), lambda i: (0, i, 0, 0))],
        out_shape=[jax.ShapeDtypeStruct((S, 512), bf16),
                   jax.ShapeDtypeStruct((S, 256), bf16),
                   jax.ShapeDtypeStruct((S, 256), bf16),
                   jax.ShapeDtypeStruct((S, 512), bf16),
                   jax.ShapeDtypeStruct((S, kb_cols), bf16),
                   jax.ShapeDtypeStruct((B_HEADS, nchunks, B_V_DIM, B_TILE), bf16)],
        compiler_params=pltpu.CompilerParams(dimension_semantics=("arbitrary",),
                                             vmem_limit_bytes=VMEM_LIMIT),
        name="in_proj",
    )(h, p["attn_pre_g"].reshape(1, D), w_cat, kones)

    nblocks = S // A_BLOCK
    a_rows = A_STEP_BLOCKS * A_BLOCK
    assert nblocks % A_STEP_BLOCKS == 0 and A_STEP_BLOCKS >= 2
    ya = pl.pallas_call(
        functools.partial(_win_attn_kernel, nblocks=nblocks),
        grid=(nblocks // A_STEP_BLOCKS,),
        in_specs=[pl.BlockSpec(memory_space=pltpu.SMEM),
                  pl.BlockSpec((a_rows, 512), lambda n: (n, 0)),
                  _resident(ka.shape),
                  _resident(va.shape),
                  _resident(bias_a.shape)],
        out_specs=pl.BlockSpec((a_rows, 512), lambda n: (n, 0)),
        out_shape=jax.ShapeDtypeStruct((S, 512), bf16),
        compiler_params=pltpu.CompilerParams(dimension_semantics=("arbitrary",),
                                             vmem_limit_bytes=VMEM_LIMIT),
        name="win_attn",
    )(p["a_sink"], qa, ka, va, bias_a)

    lamv = jnp.stack([p["lambda_q1"], p["lambda_k1"], p["lambda_q2"], p["lambda_k2"]])
    TQ, TK = B_QTILE, B_TILE
    assert S % TQ == 0 and TQ % TK == 0
    yb = pl.pallas_call(
        functools.partial(_diff_attn_kernel, nchunks=nchunks, lambda_init=lambda_init),
        grid=(B_HEADS, S // TQ),
        in_specs=[pl.BlockSpec((4, HEAD_DIM), lambda hh, i: (0, 0)),
                  pl.BlockSpec((1, B_V_DIM), lambda hh, i: (0, 0)),
                  pl.BlockSpec((TQ, LANES), lambda hh, i: (i, hh)),
                  pl.BlockSpec((S, 2 * LANES), lambda hh, i: (0, hh), pipeline_mode=pl.Buffered(1)),
                  pl.BlockSpec((1, nchunks, B_V_DIM, TK), lambda hh, i: (hh, 0, 0, 0),
                               pipeline_mode=pl.Buffered(1)),
                  pl.BlockSpec((1, 5, B_BIAS_BLOCK, B_BIAS_BLOCK), lambda hh, i: (hh, 0, 0, 0))],
        out_specs=pl.BlockSpec((TQ, LANES), lambda hh, i: (i, hh)),
        out_shape=jax.ShapeDtypeStruct((S, B_HEADS * B_V_DIM), bf16),
        scratch_shapes=[pltpu.VMEM((2, LANES, TQ), bf16),
                        pltpu.VMEM((2, LANES, TQ), bf16),
                        pltpu.VMEM((B_SLOTS, 2, TK, TQ), bf16),
                        pltpu.VMEM((B_SLOTS, 2, 1, TQ), jnp.float32),
                        pltpu.VMEM((2, 1, TQ), jnp.float32),
                        pltpu.VMEM((2, nchunks, TQ), jnp.float32),
                        pltpu.VMEM((2, 1, TQ), jnp.float32),
                        pltpu.VMEM((2, B_V_DIM, TQ), jnp.float32),
                        pltpu.VMEM((2, 1, TQ), jnp.float32)],
        compiler_params=pltpu.CompilerParams(dimension_semantics=("arbitrary", "arbitrary"),
                                             vmem_limit_bytes=VMEM_LIMIT),
        name="diff_attn",
    )(lamv, p["diff_subln_g"].reshape(1, B_V_DIM), qb, kb, vt, bias_b)

    w_out = p["w_out"].astype(bf16)
    a_width = A_Q_HEADS * HEAD_DIM
    d_ff = p["w_gate"].shape[1]
    R = FFN_ROW_TILE
    out = pl.pallas_call(
        _out_ffn_kernel,
        grid=(S // R,),
        in_specs=[pl.BlockSpec((R, 512), lambda i: (i, 0)),
                  pl.BlockSpec((R, 512), lambda i: (i, 0)),
                  pl.BlockSpec((R, D), lambda i: (i, 0)),
                  _resident((a_width, D)),
                  _resident((w_out.shape[0] - a_width, D)),
                  _resident((1, D)),
                  _resident((1, D)),
                  _resident((D, d_ff)),
                  _resident((D, d_ff)),
                  _resident((d_ff, D)),
                  _resident((1, D))],
        out_specs=pl.BlockSpec((R, D), lambda i: (i, 0)),
        out_shape=jax.ShapeDtypeStruct((S, D), jnp.float32),
        compiler_params=pltpu.CompilerParams(dimension_semantics=("arbitrary",),
                                             vmem_limit_bytes=VMEM_LIMIT),
        name="out_ffn",
    )(ya, yb, h, w_out[:a_width], w_out[a_width:], p["attn_post_g"].reshape(1, D),
      p["ffn_pre_g"].reshape(1, D), p["w_gate"].astype(bf16), p["w_up"].astype(bf16),
      p["w_down"].astype(bf16), p["ffn_post_g"].reshape(1, D))
    return out


def kernel(x, attn_pre_g, w_in, a_sink, lambda_q1, lambda_k1, lambda_q2, lambda_k2, diff_subln_g,
           rel_bias, w_out, attn_post_g, ffn_pre_g, w_gate, w_up, w_down, ffn_post_g):
    batch, S, D = x.shape
    depth = w_in.shape[0]
    assert S % B_TILE == 0 and S % A_BLOCK == 0 and S // A_BLOCK >= 2

    tab_a = _toeplitz_bias(rel_bias[:, :A_Q_HEADS], A_BLOCK, 3 * A_BLOCK, lambda t: -t - A_BLOCK)
    qi = np.arange(A_BLOCK)[:, None]
    kj = np.arange(3 * A_BLOCK)[None, :]
    in_window = np.abs(kj - A_BLOCK - qi) <= A_BLOCK
    valid = np.stack([in_window & (kj >= A_BLOCK), in_window, in_window & (kj < 2 * A_BLOCK)])
    bias_a = jnp.where(valid[:, None], tab_a[None] * LOG2E, MASK_VALUE)

    nb = B_BIAS_BLOCK
    assert len(set(_t5_bucket_np(np.arange(nb + 1, 4 * nb)))) == 1
    tab_b = rel_bias[:, A_Q_HEADS:]
    near = [_toeplitz_bias(tab_b, nb, nb, lambda t, d=d: d * nb + t) for d in (-1, 0, 1)]
    far = tab_b[_t5_bucket_np(np.array([-2 * nb, 2 * nb]))].astype(jnp.float32)
    const = [jnp.broadcast_to(far[side][:, None, None], (B_HEADS, nb, nb)) for side in (0, 1)]
    bias_b = jnp.stack([const[0]] + near + [const[1]], axis=1) * LOG2E

    outs = []
    for b in range(batch):
        h = x[b]
        for l in range(depth):
            p = dict(attn_pre_g=attn_pre_g[l], w_in=w_in[l], a_sink=a_sink[l], lambda_q1=lambda_q1[l],
                     lambda_k1=lambda_k1[l], lambda_q2=lambda_q2[l], lambda_k2=lambda_k2[l],
                     diff_subln_g=diff_subln_g[l], w_out=w_out[l], attn_post_g=attn_post_g[l],
                     ffn_pre_g=ffn_pre_g[l], w_gate=w_gate[l], w_up=w_up[l], w_down=w_down[l],
                     ffn_post_g=ffn_post_g[l])
            h = _layer(h, l, p, bias_a, bias_b)
        outs.append(h)
    return jnp.stack(outs)
```

```python
import functools
import math

import numpy as np
import jax
import jax.numpy as jnp
from jax import lax
from jax.experimental import pallas as pl
from jax.experimental.pallas import tpu as pltpu

HEAD_DIM = 64
A_Q_HEADS = 8
A_KV_HEADS = 2
A_BLOCK = 128
A_STEP_BLOCKS = 16
B_HEADS = 4
B_V_DIM = 2 * HEAD_DIM
NUM_BUCKETS = 32
MAX_DISTANCE = 128
EPS = 1e-6
MASK_VALUE = -1e30
LOG2E = math.log2(math.e)
ROW_SUM_LIMIT = 2.0 ** 40

A_WIDTH = A_Q_HEADS * HEAD_DIM
A_KV_WIDTH = 2 * A_KV_HEADS * HEAD_DIM
B_WIDTH = B_HEADS * B_V_DIM
COL_KA = A_WIDTH
COL_VA = COL_KA + A_KV_WIDTH
COL_QB = COL_VA + A_KV_WIDTH
COL_KB = COL_QB + B_WIDTH
COL_VB = COL_KB + B_WIDTH
PROJ_COLS = COL_VB + B_WIDTH

LANES = 128
ROW_TILE = 1024
FFN_ROW_TILE = 512
B_TILE = 512
B_QTILE = 4096
B_REF_LANES = 3
B_UNROLL = 2
B_AHEAD = 1
B_SLOTS = 2
B_PROBE = 128
B_BIAS_BLOCK = 128
VMEM_LIMIT = 56 * 1024 * 1024

_NT = (((1,), (1,)), ((), ()))


def _t5_bucket_np(rel):
    nb = NUM_BUCKETS // 2
    max_exact = nb // 2
    ret = np.where(rel > 0, nb, 0)
    n = np.abs(rel)
    nf = np.maximum(n, 1).astype(np.float32)
    large = max_exact + (np.log(nf / np.float32(max_exact)) / np.float32(math.log(MAX_DISTANCE / max_exact))
                         * np.float32(nb - max_exact)).astype(np.int32)
    large = np.minimum(large, nb - 1)
    return (ret + np.where(n < max_exact, n, large)).astype(np.int32)


def _toeplitz_bias(table, rows, cols, rel_of):
    length = rows + cols
    u = np.arange(length)
    t = np.where(u < cols, -u, length - u)
    w = table[_t5_bucket_np(rel_of(t))].astype(jnp.float32).T
    x = jnp.tile(w, (1, rows))[:, :rows * (length - 1)].reshape(w.shape[0], rows, length - 1)
    return x[:, :, :cols]


def _rms(xf, g):
    return xf * lax.rsqrt(jnp.mean(xf * xf, axis=-1, keepdims=True) + EPS) * g


def _in_proj_kernel(x_ref, g_ref, w_ref, kones_ref, qa_ref, ka_ref, va_ref, qb_ref, kb_ref, vt_ref):
    u = _rms(x_ref[...], g_ref[...]).astype(jnp.bfloat16)
    proj = jnp.dot(u, w_ref[...], preferred_element_type=jnp.float32)
    qa_ref[...] = proj[:, 0:COL_KA].astype(jnp.bfloat16)
    ka_ref[...] = proj[:, COL_KA:COL_VA].astype(jnp.bfloat16)
    va_ref[...] = proj[:, COL_VA:COL_QB].astype(jnp.bfloat16)
    qb_ref[...] = proj[:, COL_QB:COL_KB].astype(jnp.bfloat16)
    lane = lax.broadcasted_iota(jnp.int32, (proj.shape[0], LANES), 1)
    keep = (lane < HEAD_DIM, lane >= HEAD_DIM)
    for h in range(B_HEADS):
        kpair = proj[:, COL_KB + h * LANES:COL_KB + (h + 1) * LANES]
        for c in range(2):
            col = (2 * h + c) * LANES
            kb_ref[:, col:col + LANES] = jnp.where(keep[c], kpair, kones_ref[:, col:col + LANES]).astype(jnp.bfloat16)
        vt = proj[:, COL_VB + h * B_V_DIM:COL_VB + (h + 1) * B_V_DIM].T.astype(jnp.bfloat16)
        for t in range(vt_ref.shape[1]):
            vt_ref[h, t] = vt[:, t * B_TILE:(t + 1) * B_TILE]


def _win_attn_kernel(sink_ref, q_ref, k_ref, v_ref, bias_ref, o_ref, *, nblocks):
    f32 = jnp.float32
    bf16 = jnp.bfloat16
    n = pl.program_id(0)
    rows = A_STEP_BLOCKS * A_BLOCK
    seq = k_ref.shape[0]
    group = A_Q_HEADS // A_KV_HEADS
    start = pl.multiple_of(n * rows, rows)
    prev = pl.multiple_of(jnp.maximum(start - A_BLOCK, 0), A_BLOCK)
    nxt = pl.multiple_of(jnp.minimum(start + rows, seq - A_BLOCK), A_BLOCK)
    kw = jnp.concatenate([k_ref[pl.ds(prev, A_BLOCK), :], k_ref[pl.ds(start, rows), :],
                          k_ref[pl.ds(nxt, A_BLOCK), :]], axis=0)
    vw = jnp.concatenate([v_ref[pl.ds(prev, A_BLOCK), :], v_ref[pl.ds(start, rows), :],
                          v_ref[pl.ds(nxt, A_BLOCK), :]], axis=0)
    lane = lax.broadcasted_iota(jnp.int32, (A_BLOCK, LANES), 1)
    low = lane < HEAD_DIM
    ones = jnp.ones((3 * A_BLOCK, LANES), bf16)
    for b in range(A_STEP_BLOCKS):
        blk = n * A_STEP_BLOCKS + b
        if b == 0:
            variant = jnp.where(blk == 0, 0, 1)
        elif b == A_STEP_BLOCKS - 1:
            variant = jnp.where(blk == nblocks - 1, 2, 1)
        else:
            variant = 1
        q = q_ref[b * A_BLOCK:(b + 1) * A_BLOCK, :]
        for g in range(A_KV_HEADS):
            heads = range(g * group, (g + 1) * group)
            qs = []
            for h in heads:
                q2 = q[:, (h // 2) * LANES:(h // 2 + 1) * LANES]
                qs.append(jnp.where(low if h % 2 == 0 else jnp.logical_not(low), q2, jnp.zeros_like(q2)))
            q4 = jnp.concatenate(qs, axis=0)
            kg = kw[b * A_BLOCK:(b + 3) * A_BLOCK, g * LANES:(g + 1) * LANES]
            vg = jnp.concatenate([vw[b * A_BLOCK:(b + 3) * A_BLOCK, g * LANES:(g + 1) * LANES], ones],
                                 axis=1)
            s = lax.dot_general(q4, kg, _NT, preferred_element_type=f32)
            ps, sink_terms = [], []
            for t, h in enumerate(heads):
                sh = s[t * A_BLOCK:(t + 1) * A_BLOCK] + bias_ref[variant, h]
                snk = sink_ref[h] * LOG2E
                m = jnp.maximum(jnp.max(sh, axis=-1, keepdims=True), snk)
                ps.append(jnp.exp2(sh - m).astype(bf16))
                sink_terms.append(jnp.exp2(snk - m))
            p = jnp.concatenate(ps, axis=0)
            res = jnp.dot(p, vg, preferred_element_type=f32)
            on = res[:, 0:LANES] / (res[:, LANES:2 * LANES] + jnp.concatenate(sink_terms, axis=0))
            for t in range(group // 2):
                even = on[(2 * t) * A_BLOCK:(2 * t + 1) * A_BLOCK]
                odd = on[(2 * t + 1) * A_BLOCK:(2 * t + 2) * A_BLOCK]
                pair = (g * group) // 2 + t
                o_ref[b * A_BLOCK:(b + 1) * A_BLOCK, pair * LANES:(pair + 1) * LANES] = (
                    jnp.where(low, even, odd).astype(o_ref.dtype))


def _diff_attn_kernel(lamv_ref, g_ref, q_ref, k_ref, vt_ref, bias_ref, o_ref,
                      qz_ref, qp_ref, p_ref, psum_ref, r_ref, flag_ref, any_ref, acc_ref, den_ref,
                      *, nchunks, lambda_init):
    TQ = B_QTILE
    TK = B_TILE
    i = pl.program_id(1)
    f32 = jnp.float32
    bf16 = jnp.bfloat16

    qt = q_ref[...].astype(f32).T
    row = lax.broadcasted_iota(jnp.int32, (LANES, TQ), 0)
    is_q = (row < HEAD_DIM, row >= HEAD_DIM)
    ref_row = (HEAD_DIM, 0)

    ksub = TK // B_BIAS_BLOCK
    qsub = TQ // B_BIAS_BLOCK

    def bias_tile(j, rows=ksub):
        base = j * ksub - i * qsub
        by_dist = {d: bias_ref[0, jnp.clip(base + d, -2, 2) + 2] for d in range(1 - qsub, rows)}
        return jnp.concatenate(
            [jnp.concatenate([by_dist[a - b] for b in range(qsub)], axis=1) for a in range(rows)], axis=0)

    def k_chunk(j, c):
        return k_ref[pl.ds(pl.multiple_of(j * TK, TK), TK), c * LANES:(c + 1) * LANES]

    def set_reference(c, r_row):
        hi = r_row.astype(bf16).astype(f32)
        rem = r_row - hi
        mid = rem.astype(bf16).astype(f32)
        low = (rem - mid).astype(bf16).astype(f32)
        first = ref_row[c]
        ext = jnp.where(row == first, -hi, jnp.where(row == first + 1, -mid,
                                                     jnp.where(row == first + 2, -low, 0.0)))
        qp_ref[c] = jnp.where(is_q[c], qt, ext).astype(bf16)

    def exact_chunk(j, c, flagged):
        s = jnp.dot(k_chunk(j, c), qz_ref[c], preferred_element_type=f32) + bias_tile(j)
        r_old = r_ref[c]
        r_new = jnp.where(flagged, jnp.maximum(r_old, jnp.max(s, axis=0, keepdims=True)), r_old)
        p = jnp.where(flagged, jnp.exp2(s - r_new), 0.0)
        pv = jnp.dot(vt_ref[0, j], p.astype(bf16), preferred_element_type=f32)
        alpha = jnp.exp2(r_old - r_new)
        acc_ref[c] = acc_ref[c] * alpha + pv
        den_ref[c] = den_ref[c] * alpha + jnp.sum(p, axis=0, keepdims=True)
        r_ref[c] = r_new

    def stage_a(j, slot):
        bias = bias_tile(j)
        for c in range(2):
            s = jnp.dot(k_chunk(j, c), qp_ref[c], preferred_element_type=f32) + bias
            p = jnp.exp2(s)
            psum_ref[slot, c] = jnp.sum(p, axis=0, keepdims=True)
            p_ref[slot, c] = p.astype(bf16)

    def stage_c(j, slot):
        vt = vt_ref[0, j]
        for c in range(2):
            pv = jnp.dot(vt, p_ref[slot, c], preferred_element_type=f32)
            psum = psum_ref[slot, c]
            ok = psum <= ROW_SUM_LIMIT
            acc_ref[c] = acc_ref[c] + jnp.where(ok, pv, 0.0)
            den_ref[c] = den_ref[c] + jnp.where(ok, psum, 0.0)
            flag = jnp.where(ok, 0.0, 1.0)
            flag_ref[c, pl.ds(j, 1), :] = flag
            any_ref[c] = jnp.maximum(any_ref[c], flag)

    for c in range(2):
        qz_ref[c] = jnp.where(is_q[c], qt, 0.0).astype(bf16)
    acc_ref[...] = jnp.zeros(acc_ref.shape, f32)
    den_ref[...] = jnp.zeros(den_ref.shape, f32)
    any_ref[...] = jnp.zeros(any_ref.shape, f32)
    diag = i * (TQ // TK)
    probe_bias = bias_tile(diag, rows=B_PROBE // B_BIAS_BLOCK)
    for c in range(2):
        kp = k_ref[pl.ds(pl.multiple_of(diag * TK, TK), B_PROBE), c * LANES:(c + 1) * LANES]
        s = jnp.dot(kp, qz_ref[c], preferred_element_type=f32) + probe_bias
        r = jnp.max(s, axis=0, keepdims=True)
        r_ref[c] = r
        set_reference(c, r)

    def run(j0, has_next):
        for u in range(B_UNROLL):
            stage_c(j0 + u, u % B_SLOTS)
            if u + B_AHEAD < B_UNROLL or has_next:
                stage_a(j0 + u + B_AHEAD, (u + B_AHEAD) % B_SLOTS)

    nbody = nchunks // B_UNROLL
    for u in range(B_AHEAD):
        stage_a(u, u)

    def body(b, carry):
        run(b * B_UNROLL, True)
        return carry

    lax.fori_loop(0, nbody - 1, body, 0)
    run((nbody - 1) * B_UNROLL, False)

    @pl.when(jnp.max(jnp.maximum(any_ref[0], any_ref[1])) > 0.0)
    def _():
        def redo(j, carry):
            @pl.when(jnp.max(jnp.maximum(flag_ref[0, pl.ds(j, 1), :], flag_ref[1, pl.ds(j, 1), :])) > 0.0)
            def _():
                for c in range(2):
                    exact_chunk(j, c, flag_ref[c, pl.ds(j, 1), :] > 0.0)
            return carry

        lax.fori_loop(0, nchunks, redo, 0)

    lv = lamv_ref[...]
    lam = (jnp.exp(jnp.sum(lv[0:1] * lv[1:2], axis=-1, keepdims=True))
           - jnp.exp(jnp.sum(lv[2:3] * lv[3:4], axis=-1, keepdims=True)) + lambda_init)
    ot = acc_ref[0] / den_ref[0] - lam * (acc_ref[1] / den_ref[1])
    o = ot.T
    o_ref[...] = (_rms(o, g_ref[...]) * (1.0 - lambda_init)).astype(o_ref.dtype)


def _out_ffn_kernel(ya_ref, yb_ref, x_ref, woa_ref, wob_ref, gpost_ref, gpre_ref,
                    wg_ref, wu_ref, wd_ref, gfpost_ref, o_ref):
    y = (jnp.dot(ya_ref[...], woa_ref[...], preferred_element_type=jnp.float32)
         + jnp.dot(yb_ref[...], wob_ref[...], preferred_element_type=jnp.float32))
    h1 = x_ref[...] + _rms(y, gpost_ref[...])
    u = _rms(h1, gpre_ref[...]).astype(jnp.bfloat16)
    gate = jnp.dot(u, wg_ref[...], preferred_element_type=jnp.float32)
    up = jnp.dot(u, wu_ref[...], preferred_element_type=jnp.float32)
    act = (gate * jax.nn.sigmoid(gate) * up).astype(jnp.bfloat16)
    f = jnp.dot(act, wd_ref[...], preferred_element_type=jnp.float32)
    o_ref[...] = h1 + _rms(f, gfpost_ref[...])


def _resident(shape):
    zeros = (0,) * len(shape)
    return pl.BlockSpec(shape, lambda *_: zeros, pipeline_mode=pl.Buffered(1))


def _layer(h, l, p, bias_a, bias_b):
    S, D = h.shape
    bf16 = jnp.bfloat16
    lambda_init = 0.8 - 0.6 * math.exp(-0.3 * l)
    scale = HEAD_DIM ** -0.5

    w = p["w_in"]
    a_q, a_kv, b_qk = A_Q_HEADS * HEAD_DIM, A_KV_HEADS * HEAD_DIM, B_HEADS * 2 * HEAD_DIM
    c0 = a_q
    c1 = c0 + a_kv
    c2 = c1 + a_kv
    c3 = c2 + b_qk

    def dup(cols):
        parts = []
        for g in range(A_KV_HEADS):
            blk = cols[:, g * HEAD_DIM:(g + 1) * HEAD_DIM]
            parts += [blk, blk]
        return jnp.concatenate(parts, axis=1)

    kones = np.zeros((B_HEADS, 2, LANES), np.float32)
    kones[:, 0, HEAD_DIM:HEAD_DIM + B_REF_LANES] = 1.0
    kones[:, 1, 0:B_REF_LANES] = 1.0
    kones = jnp.asarray(kones.reshape(1, B_HEADS * 2 * LANES))

    w_cat = jnp.concatenate([w[:, :c0] * (scale * LOG2E), dup(w[:, c0:c1]), dup(w[:, c1:c2]),
                             w[:, c2:c3] * (scale * LOG2E), w[:, c3:]],
                            axis=1).astype(bf16)
    ncols = w_cat.shape[1]
    assert ncols == PROJ_COLS and w.shape[1] == c3 + 2 * B_WIDTH
    nrow = S // ROW_TILE
    nchunks = S // B_TILE
    assert ROW_TILE % B_TILE == 0 and S % ROW_TILE == 0 and nchunks % B_UNROLL == 0
    assert B_UNROLL % B_SLOTS == 0 and B_AHEAD < B_SLOTS and B_AHEAD <= B_UNROLL
    kb_cols = B_HEADS * 2 * LANES

    qa, ka, va, qb, kb, vt = pl.pallas_call(
        _in_proj_kernel,
        grid=(nrow,),
        in_specs=[pl.BlockSpec((ROW_TILE, D), lambda i: (i, 0)),
                  _resident((1, D)),
                  _resident((D, ncols)),
                  _resident((1, kb_cols))],
        out_specs=[pl.BlockSpec((ROW_TILE, A_WIDTH), lambda i: (i, 0)),
                   pl.BlockSpec((ROW_TILE, A_KV_WIDTH), lambda i: (i, 0)),
                   pl.BlockSpec((ROW_TILE, A_KV_WIDTH), lambda i: (i, 0)),
                   pl.BlockSpec((ROW_TILE, B_WIDTH), lambda i: (i, 0)),
                   pl.BlockSpec((ROW_TILE, kb_cols), lambda i: (i, 0)),
                   pl.BlockSpec((B_HEADS, ROW_TILE // B_TILE, B_V_DIM, B_TILE), lambda i: (0, i, 0, 0))],
        out_shape=[jax.ShapeDtypeStruct((S, A_WIDTH), bf16),
                   jax.ShapeDtypeStruct((S, A_KV_WIDTH), bf16),
                   jax.ShapeDtypeStruct((S, A_KV_WIDTH), bf16),
                   jax.ShapeDtypeStruct((S, B_WIDTH), bf16),
                   jax.ShapeDtypeStruct((S, kb_cols), bf16),
                   jax.ShapeDtypeStruct((B_HEADS, nchunks, B_V_DIM, B_TILE), bf16)],
        compiler_params=pltpu.CompilerParams(dimension_semantics=("arbitrary",),
                                             vmem_limit_bytes=VMEM_LIMIT),
        name="in_proj",
    )(h, p["attn_pre_g"].reshape(1, D), w_cat, kones)

    nblocks = S // A_BLOCK
    a_rows = A_STEP_BLOCKS * A_BLOCK
    assert nblocks % A_STEP_BLOCKS == 0 and A_STEP_BLOCKS >= 2
    ya = pl.pallas_call(
        functools.partial(_win_attn_kernel, nblocks=nblocks),
        grid=(nblocks // A_STEP_BLOCKS,),
        in_specs=[pl.BlockSpec(memory_space=pltpu.SMEM),
                  pl.BlockSpec((a_rows, A_WIDTH), lambda n: (n, 0)),
                  _resident(ka.shape),
                  _resident(va.shape),
                  _resident(bias_a.shape)],
        out_specs=pl.BlockSpec((a_rows, A_WIDTH), lambda n: (n, 0)),
        out_shape=jax.ShapeDtypeStruct((S, A_WIDTH), bf16),
        compiler_params=pltpu.CompilerParams(dimension_semantics=("arbitrary",),
                                             vmem_limit_bytes=VMEM_LIMIT),
        name="win_attn",
    )(p["a_sink"], qa, ka, va, bias_a)

    lamv = jnp.stack([p["lambda_q1"], p["lambda_k1"], p["lambda_q2"], p["lambda_k2"]])
    TQ, TK = B_QTILE, B_TILE
    assert S % TQ == 0 and TQ % TK == 0
    yb = pl.pallas_call(
        functools.partial(_diff_attn_kernel, nchunks=nchunks, lambda_init=lambda_init),
        grid=(B_HEADS, S // TQ),
        in_specs=[pl.BlockSpec((4, HEAD_DIM), lambda hh, i: (0, 0)),
                  pl.BlockSpec((1, B_V_DIM), lambda hh, i: (0, 0)),
                  pl.BlockSpec((TQ, LANES), lambda hh, i: (i, hh)),
                  pl.BlockSpec((S, 2 * LANES), lambda hh, i: (0, hh), pipeline_mode=pl.Buffered(1)),
                  pl.BlockSpec((1, nchunks, B_V_DIM, TK), lambda hh, i: (hh, 0, 0, 0),
                               pipeline_mode=pl.Buffered(1)),
                  pl.BlockSpec((1, 5, B_BIAS_BLOCK, B_BIAS_BLOCK), lambda hh, i: (hh, 0, 0, 0))],
        out_specs=pl.BlockSpec((TQ, LANES), lambda hh, i: (i, hh)),
        out_shape=jax.ShapeDtypeStruct((S, B_HEADS * B_V_DIM), bf16),
        scratch_shapes=[pltpu.VMEM((2, LANES, TQ), bf16),
                        pltpu.VMEM((2, LANES, TQ), bf16),
                        pltpu.VMEM((B_SLOTS, 2, TK, TQ), bf16),
                        pltpu.VMEM((B_SLOTS, 2, 1, TQ), jnp.float32),
                        pltpu.VMEM((2, 1, TQ), jnp.float32),
                        pltpu.VMEM((2, nchunks, TQ), jnp.float32),
                        pltpu.VMEM((2, 1, TQ), jnp.float32),
                        pltpu.VMEM((2, B_V_DIM, TQ), jnp.float32),
                        pltpu.VMEM((2, 1, TQ), jnp.float32)],
        compiler_params=pltpu.CompilerParams(dimension_semantics=("arbitrary", "arbitrary"),
                                             vmem_limit_bytes=VMEM_LIMIT),
        name="diff_attn",
    )(lamv, p["diff_subln_g"].reshape(1, B_V_DIM), qb, kb, vt, bias_b)

    w_out = p["w_out"].astype(bf16)
    a_width = A_WIDTH
    assert w_out.shape[0] == A_WIDTH + B_WIDTH
    d_ff = p["w_gate"].shape[1]
    R = FFN_ROW_TILE
    out = pl.pallas_call(
        _out_ffn_kernel,
        grid=(S // R,),
        in_specs=[pl.BlockSpec((R, A_WIDTH), lambda i: (i, 0)),
                  pl.BlockSpec((R, B_WIDTH), lambda i: (i, 0)),
                  pl.BlockSpec((R, D), lambda i: (i, 0)),
                  _resident((a_width, D)),
                  _resident((w_out.shape[0] - a_width, D)),
                  _resident((1, D)),
                  _resident((1, D)),
                  _resident((D, d_ff)),
                  _resident((D, d_ff)),
                  _resident((d_ff, D)),
                  _resident((1, D))],
        out_specs=pl.BlockSpec((R, D), lambda i: (i, 0)),
        out_shape=jax.ShapeDtypeStruct((S, D), jnp.float32),
        compiler_params=pltpu.CompilerParams(dimension_semantics=("arbitrary",),
                                             vmem_limit_bytes=VMEM_LIMIT),
        name="out_ffn",
    )(ya, yb, h, w_out[:a_width], w_out[a_width:], p["attn_post_g"].reshape(1, D),
      p["ffn_pre_g"].reshape(1, D), p["w_gate"].astype(bf16), p["w_up"].astype(bf16),
      p["w_down"].astype(bf16), p["ffn_post_g"].reshape(1, D))
    return out


def kernel(x, attn_pre_g, w_in, a_sink, lambda_q1, lambda_k1, lambda_q2, lambda_k2, diff_subln_g,
           rel_bias, w_out, attn_post_g, ffn_pre_g, w_gate, w_up, w_down, ffn_post_g):
    batch, S, D = x.shape
    depth = w_in.shape[0]
    assert S % B_TILE == 0 and S % A_BLOCK == 0 and S // A_BLOCK >= 2

    tab_a = _toeplitz_bias(rel_bias[:, :A_Q_HEADS], A_BLOCK, 3 * A_BLOCK, lambda t: -t - A_BLOCK)
    qi = np.arange(A_BLOCK)[:, None]
    kj = np.arange(3 * A_BLOCK)[None, :]
    in_window = np.abs(kj - A_BLOCK - qi) <= A_BLOCK
    valid = np.stack([in_window & (kj >= A_BLOCK), in_window, in_window & (kj < 2 * A_BLOCK)])
    bias_a = jnp.where(valid[:, None], tab_a[None] * LOG2E, MASK_VALUE)

    nb = B_BIAS_BLOCK
    assert len(set(_t5_bucket_np(np.arange(nb + 1, 4 * nb)))) == 1
    tab_b = rel_bias[:, A_Q_HEADS:]
    near = [_toeplitz_bias(tab_b, nb, nb, lambda t, d=d: d * nb + t) for d in (-1, 0, 1)]
    far = tab_b[_t5_bucket_np(np.array([-2 * nb, 2 * nb]))].astype(jnp.float32)
    const = [jnp.broadcast_to(far[side][:, None, None], (B_HEADS, nb, nb)) for side in (0, 1)]
    bias_b = jnp.stack([const[0]] + near + [const[1]], axis=1) * LOG2E

    outs = []
    for b in range(batch):
        h = x[b]
        for l in range(depth):
            p = dict(attn_pre_g=attn_pre_g[l], w_in=w_in[l], a_sink=a_sink[l], lambda_q1=lambda_q1[l],
                     lambda_k1=lambda_k1[l], lambda_q2=lambda_q2[l], lambda_k2=lambda_k2[l],
                     diff_subln_g=diff_subln_g[l], w_out=w_out[l], attn_post_g=attn_post_g[l],
                     ffn_pre_g=ffn_pre_g[l], w_gate=w_gate[l], w_up=w_up[l], w_down=w_down[l],
                     ffn_post_g=ffn_post_g[l])
            h = _layer(h, l, p, bias_a, bias_b)
        outs.append(h)
    return jnp.stack(outs)
```

```python
import functools
import math

import numpy as np
import jax
import jax.numpy as jnp
from jax import lax
from jax.experimental import pallas as pl
from jax.experimental.pallas import tpu as pltpu

HEAD_DIM = 64
A_Q_HEADS = 8
A_KV_HEADS = 2
A_BLOCK = 128
A_STEP_BLOCKS = 8
B_HEADS = 4
B_V_DIM = 2 * HEAD_DIM
NUM_BUCKETS = 32
MAX_DISTANCE = 128
EPS = 1e-6
MASK_VALUE = -1e30
LOG2E = math.log2(math.e)
ROW_SUM_LIMIT = 2.0 ** 40

A_WIDTH = A_Q_HEADS * HEAD_DIM
A_KV_WIDTH = 2 * A_KV_HEADS * HEAD_DIM
B_WIDTH = B_HEADS * B_V_DIM
COL_KA = A_WIDTH
COL_VA = COL_KA + A_KV_WIDTH
COL_QB = COL_VA + A_KV_WIDTH
COL_KB = COL_QB + B_WIDTH
COL_VB = COL_KB + B_WIDTH
PROJ_COLS = COL_VB + B_WIDTH

LANES = 128
ROW_TILE = 1024
FFN_ROW_TILE = 512
B_TILE = 512
B_QTILE = 4096
B_REF_LANES = 3
B_UNROLL = 2
B_AHEAD = 1
B_SLOTS = 2
B_PROBE = 128
B_BIAS_BLOCK = 128
VMEM_LIMIT = 56 * 1024 * 1024

_NT = (((1,), (1,)), ((), ()))


def _t5_bucket_np(rel):
    nb = NUM_BUCKETS // 2
    max_exact = nb // 2
    ret = np.where(rel > 0, nb, 0)
    n = np.abs(rel)
    nf = np.maximum(n, 1).astype(np.float32)
    large = max_exact + (np.log(nf / np.float32(max_exact)) / np.float32(math.log(MAX_DISTANCE / max_exact))
                         * np.float32(nb - max_exact)).astype(np.int32)
    large = np.minimum(large, nb - 1)
    return (ret + np.where(n < max_exact, n, large)).astype(np.int32)


def _toeplitz_bias(table, rows, cols, rel_of):
    length = rows + cols
    u = np.arange(length)
    t = np.where(u < cols, -u, length - u)
    w = table[_t5_bucket_np(rel_of(t))].astype(jnp.float32).T
    x = jnp.tile(w, (1, rows))[:, :rows * (length - 1)].reshape(w.shape[0], rows, length - 1)
    return x[:, :, :cols]


def _rms(xf, g):
    return xf * lax.rsqrt(jnp.mean(xf * xf, axis=-1, keepdims=True) + EPS) * g


def _in_proj_kernel(x_ref, g_ref, w_ref, kones_ref, qa_ref, ka_ref, va_ref, qb_ref, kb_ref, vt_ref):
    u = _rms(x_ref[...], g_ref[...]).astype(jnp.bfloat16)
    proj = jnp.dot(u, w_ref[...], preferred_element_type=jnp.float32)
    qa_ref[...] = proj[:, 0:COL_KA].astype(jnp.bfloat16)
    ka_ref[...] = proj[:, COL_KA:COL_VA].astype(jnp.bfloat16)
    va_ref[...] = proj[:, COL_VA:COL_QB].astype(jnp.bfloat16)
    qb_ref[...] = proj[:, COL_QB:COL_KB].astype(jnp.bfloat16)
    lane = lax.broadcasted_iota(jnp.int32, (proj.shape[0], LANES), 1)
    keep = (lane < HEAD_DIM, lane >= HEAD_DIM)
    for h in range(B_HEADS):
        kpair = proj[:, COL_KB + h * LANES:COL_KB + (h + 1) * LANES]
        for c in range(2):
            col = (2 * h + c) * LANES
            kb_ref[:, col:col + LANES] = jnp.where(keep[c], kpair, kones_ref[:, col:col + LANES]).astype(jnp.bfloat16)
        vt = proj[:, COL_VB + h * B_V_DIM:COL_VB + (h + 1) * B_V_DIM].T.astype(jnp.bfloat16)
        for t in range(vt_ref.shape[1]):
            vt_ref[h, t] = vt[:, t * B_TILE:(t + 1) * B_TILE]


def _win_attn_kernel(sink_ref, q_ref, k_ref, v_ref, bias_ref, o_ref, *, nblocks):
    f32 = jnp.float32
    bf16 = jnp.bfloat16
    n = pl.program_id(0)
    rows = A_STEP_BLOCKS * A_BLOCK
    seq = k_ref.shape[0]
    group = A_Q_HEADS // A_KV_HEADS
    start = pl.multiple_of(n * rows, rows)
    prev = pl.multiple_of(jnp.maximum(start - A_BLOCK, 0), A_BLOCK)
    nxt = pl.multiple_of(jnp.minimum(start + rows, seq - A_BLOCK), A_BLOCK)
    kw = jnp.concatenate([k_ref[pl.ds(prev, A_BLOCK), :], k_ref[pl.ds(start, rows), :],
                          k_ref[pl.ds(nxt, A_BLOCK), :]], axis=0)
    vw = jnp.concatenate([v_ref[pl.ds(prev, A_BLOCK), :], v_ref[pl.ds(start, rows), :],
                          v_ref[pl.ds(nxt, A_BLOCK), :]], axis=0)
    lane = lax.broadcasted_iota(jnp.int32, (A_BLOCK, LANES), 1)
    low = lane < HEAD_DIM
    ones = jnp.ones((3 * A_BLOCK, LANES), bf16)
    for b in range(A_STEP_BLOCKS):
        blk = n * A_STEP_BLOCKS + b
        if b == 0:
            variant = jnp.where(blk == 0, 0, 1)
        elif b == A_STEP_BLOCKS - 1:
            variant = jnp.where(blk == nblocks - 1, 2, 1)
        else:
            variant = 1
        q = q_ref[b * A_BLOCK:(b + 1) * A_BLOCK, :]
        for g in range(A_KV_HEADS):
            heads = range(g * group, (g + 1) * group)
            qs = []
            for h in heads:
                q2 = q[:, (h // 2) * LANES:(h // 2 + 1) * LANES]
                qs.append(jnp.where(low if h % 2 == 0 else jnp.logical_not(low), q2, jnp.zeros_like(q2)))
            q4 = jnp.concatenate(qs, axis=0)
            kg = kw[b * A_BLOCK:(b + 3) * A_BLOCK, g * LANES:(g + 1) * LANES]
            vg = jnp.concatenate([vw[b * A_BLOCK:(b + 3) * A_BLOCK, g * LANES:(g + 1) * LANES], ones],
                                 axis=1)
            s = lax.dot_general(q4, kg, _NT, preferred_element_type=f32)
            ps, sink_terms = [], []
            for t, h in enumerate(heads):
                sh = s[t * A_BLOCK:(t + 1) * A_BLOCK] + bias_ref[variant, h]
                snk = sink_ref[h] * LOG2E
                m = jnp.maximum(jnp.max(sh, axis=-1, keepdims=True), snk)
                ps.append(jnp.exp2(sh - m).astype(bf16))
                sink_terms.append(jnp.exp2(snk - m))
            p = jnp.concatenate(ps, axis=0)
            res = jnp.dot(p, vg, preferred_element_type=f32)
            on = res[:, 0:LANES] / (res[:, LANES:2 * LANES] + jnp.concatenate(sink_terms, axis=0))
            for t in range(group // 2):
                even = on[(2 * t) * A_BLOCK:(2 * t + 1) * A_BLOCK]
                odd = on[(2 * t + 1) * A_BLOCK:(2 * t + 2) * A_BLOCK]
                pair = (g * group) // 2 + t
                o_ref[b * A_BLOCK:(b + 1) * A_BLOCK, pair * LANES:(pair + 1) * LANES] = (
                    jnp.where(low, even, odd).astype(o_ref.dtype))


def _diff_attn_kernel(lamv_ref, g_ref, q_ref, k_ref, vt_ref, bias_ref, o_ref,
                      qz_ref, qp_ref, p_ref, psum_ref, r_ref, flag_ref, any_ref, acc_ref, den_ref,
                      *, nchunks, lambda_init):
    TQ = B_QTILE
    TK = B_TILE
    i = pl.program_id(1)
    f32 = jnp.float32
    bf16 = jnp.bfloat16

    qt = q_ref[...].astype(f32).T
    row = lax.broadcasted_iota(jnp.int32, (LANES, TQ), 0)
    is_q = (row < HEAD_DIM, row >= HEAD_DIM)
    ref_row = (HEAD_DIM, 0)

    ksub = TK // B_BIAS_BLOCK
    qsub = TQ // B_BIAS_BLOCK

    def bias_tile(j, rows=ksub):
        base = j * ksub - i * qsub
        by_dist = {d: bias_ref[0, jnp.clip(base + d, -2, 2) + 2] for d in range(1 - qsub, rows)}
        return jnp.concatenate(
            [jnp.concatenate([by_dist[a - b] for b in range(qsub)], axis=1) for a in range(rows)], axis=0)

    def k_chunk(j, c):
        return k_ref[pl.ds(pl.multiple_of(j * TK, TK), TK), c * LANES:(c + 1) * LANES]

    def set_reference(c, r_row):
        hi = r_row.astype(bf16).astype(f32)
        rem = r_row - hi
        mid = rem.astype(bf16).astype(f32)
        low = (rem - mid).astype(bf16).astype(f32)
        first = ref_row[c]
        ext = jnp.where(row == first, -hi, jnp.where(row == first + 1, -mid,
                                                     jnp.where(row == first + 2, -low, 0.0)))
        qp_ref[c] = jnp.where(is_q[c], qt, ext).astype(bf16)

    def exact_chunk(j, c, flagged):
        s = jnp.dot(k_chunk(j, c), qz_ref[c], preferred_element_type=f32) + bias_tile(j)
        r_old = r_ref[c]
        r_new = jnp.where(flagged, jnp.maximum(r_old, jnp.max(s, axis=0, keepdims=True)), r_old)
        p = jnp.where(flagged, jnp.exp2(s - r_new), 0.0)
        pv = jnp.dot(vt_ref[0, j], p.astype(bf16), preferred_element_type=f32)
        alpha = jnp.exp2(r_old - r_new)
        acc_ref[c] = acc_ref[c] * alpha + pv
        den_ref[c] = den_ref[c] * alpha + jnp.sum(p, axis=0, keepdims=True)
        r_ref[c] = r_new

    def stage_a(j, slot):
        bias = bias_tile(j)
        for c in range(2):
            s = jnp.dot(k_chunk(j, c), qp_ref[c], preferred_element_type=f32) + bias
            p = jnp.exp2(s)
            psum_ref[slot, c] = jnp.sum(p, axis=0, keepdims=True)
            p_ref[slot, c] = p.astype(bf16)

    def stage_c(j, slot):
        vt = vt_ref[0, j]
        for c in range(2):
            pv = jnp.dot(vt, p_ref[slot, c], preferred_element_type=f32)
            psum = psum_ref[slot, c]
            ok = psum <= ROW_SUM_LIMIT
            acc_ref[c] = acc_ref[c] + jnp.where(ok, pv, 0.0)
            den_ref[c] = den_ref[c] + jnp.where(ok, psum, 0.0)
            flag = jnp.where(ok, 0.0, 1.0)
            flag_ref[c, pl.ds(j, 1), :] = flag
            any_ref[c] = jnp.maximum(any_ref[c], flag)

    for c in range(2):
        qz_ref[c] = jnp.where(is_q[c], qt, 0.0).astype(bf16)
    acc_ref[...] = jnp.zeros(acc_ref.shape, f32)
    den_ref[...] = jnp.zeros(den_ref.shape, f32)
    any_ref[...] = jnp.zeros(any_ref.shape, f32)
    diag = i * (TQ // TK)
    probe_bias = bias_tile(diag, rows=B_PROBE // B_BIAS_BLOCK)
    for c in range(2):
        kp = k_ref[pl.ds(pl.multiple_of(diag * TK, TK), B_PROBE), c * LANES:(c + 1) * LANES]
        s = jnp.dot(kp, qz_ref[c], preferred_element_type=f32) + probe_bias
        r = jnp.max(s, axis=0, keepdims=True)
        r_ref[c] = r
        set_reference(c, r)

    def run(j0, has_next):
        for u in range(B_UNROLL):
            stage_c(j0 + u, u % B_SLOTS)
            if u + B_AHEAD < B_UNROLL or has_next:
                stage_a(j0 + u + B_AHEAD, (u + B_AHEAD) % B_SLOTS)

    nbody = nchunks // B_UNROLL
    for u in range(B_AHEAD):
        stage_a(u, u)

    def body(b, carry):
        run(b * B_UNROLL, True)
        return carry

    lax.fori_loop(0, nbody - 1, body, 0)
    run((nbody - 1) * B_UNROLL, False)

    @pl.when(jnp.max(jnp.maximum(any_ref[0], any_ref[1])) > 0.0)
    def _():
        def redo(j, carry):
            @pl.when(jnp.max(jnp.maximum(flag_ref[0, pl.ds(j, 1), :], flag_ref[1, pl.ds(j, 1), :])) > 0.0)
            def _():
                for c in range(2):
                    exact_chunk(j, c, flag_ref[c, pl.ds(j, 1), :] > 0.0)
            return carry

        lax.fori_loop(0, nchunks, redo, 0)

    lv = lamv_ref[...]
    lam = (jnp.exp(jnp.sum(lv[0:1] * lv[1:2], axis=-1, keepdims=True))
           - jnp.exp(jnp.sum(lv[2:3] * lv[3:4], axis=-1, keepdims=True)) + lambda_init)
    ot = acc_ref[0] / den_ref[0] - lam * (acc_ref[1] / den_ref[1])
    o = ot.T
    o_ref[...] = (_rms(o, g_ref[...]) * (1.0 - lambda_init)).astype(o_ref.dtype)


def _out_ffn_kernel(ya_ref, yb_ref, x_ref, woa_ref, wob_ref, gpost_ref, gpre_ref,
                    wg_ref, wu_ref, wd_ref, gfpost_ref, o_ref):
    y = (jnp.dot(ya_ref[...], woa_ref[...], preferred_element_type=jnp.float32)
         + jnp.dot(yb_ref[...], wob_ref[...], preferred_element_type=jnp.float32))
    h1 = x_ref[...] + _rms(y, gpost_ref[...])
    u = _rms(h1, gpre_ref[...]).astype(jnp.bfloat16)
    gate = jnp.dot(u, wg_ref[...], preferred_element_type=jnp.float32)
    up = jnp.dot(u, wu_ref[...], preferred_element_type=jnp.float32)
    act = (gate * jax.nn.sigmoid(gate) * up).astype(jnp.bfloat16)
    f = jnp.dot(act, wd_ref[...], preferred_element_type=jnp.float32)
    o_ref[...] = h1 + _rms(f, gfpost_ref[...])


def _resident(shape):
    zeros = (0,) * len(shape)
    return pl.BlockSpec(shape, lambda *_: zeros, pipeline_mode=pl.Buffered(1))


def _layer(h, l, p, bias_a, bias_b):
    S, D = h.shape
    bf16 = jnp.bfloat16
    lambda_init = 0.8 - 0.6 * math.exp(-0.3 * l)
    scale = HEAD_DIM ** -0.5

    w = p["w_in"]
    a_q, a_kv, b_qk = A_Q_HEADS * HEAD_DIM, A_KV_HEADS * HEAD_DIM, B_HEADS * 2 * HEAD_DIM
    c0 = a_q
    c1 = c0 + a_kv
    c2 = c1 + a_kv
    c3 = c2 + b_qk

    def dup(cols):
        parts = []
        for g in range(A_KV_HEADS):
            blk = cols[:, g * HEAD_DIM:(g + 1) * HEAD_DIM]
            parts += [blk, blk]
        return jnp.concatenate(parts, axis=1)

    kones = np.zeros((B_HEADS, 2, LANES), np.float32)
    kones[:, 0, HEAD_DIM:HEAD_DIM + B_REF_LANES] = 1.0
    kones[:, 1, 0:B_REF_LANES] = 1.0
    kones = jnp.asarray(kones.reshape(1, B_HEADS * 2 * LANES))

    w_cat = jnp.concatenate([w[:, :c0] * (scale * LOG2E), dup(w[:, c0:c1]), dup(w[:, c1:c2]),
                             w[:, c2:c3] * (scale * LOG2E), w[:, c3:]],
                            axis=1).astype(bf16)
    ncols = w_cat.shape[1]
    assert ncols == PROJ_COLS and w.shape[1] == c3 + 2 * B_WIDTH
    nrow = S // ROW_TILE
    nchunks = S // B_TILE
    assert ROW_TILE % B_TILE == 0 and S % ROW_TILE == 0 and nchunks % B_UNROLL == 0
    assert B_UNROLL % B_SLOTS == 0 and B_AHEAD < B_SLOTS and B_AHEAD <= B_UNROLL
    kb_cols = B_HEADS * 2 * LANES

    qa, ka, va, qb, kb, vt = pl.pallas_call(
        _in_proj_kernel,
        grid=(nrow,),
        in_specs=[pl.BlockSpec((ROW_TILE, D), lambda i: (i, 0)),
                  _resident((1, D)),
                  _resident((D, ncols)),
                  _resident((1, kb_cols))],
        out_specs=[pl.BlockSpec((ROW_TILE, A_WIDTH), lambda i: (i, 0)),
                   pl.BlockSpec((ROW_TILE, A_KV_WIDTH), lambda i: (i, 0)),
                   pl.BlockSpec((ROW_TILE, A_KV_WIDTH), lambda i: (i, 0)),
                   pl.BlockSpec((ROW_TILE, B_WIDTH), lambda i: (i, 0)),
                   pl.BlockSpec((ROW_TILE, kb_cols), lambda i: (i, 0)),
                   pl.BlockSpec((B_HEADS, ROW_TILE // B_TILE, B_V_DIM, B_TILE), lambda i: (0, i, 0, 0))],
        out_shape=[jax.ShapeDtypeStruct((S, A_WIDTH), bf16),
                   jax.ShapeDtypeStruct((S, A_KV_WIDTH), bf16),
                   jax.ShapeDtypeStruct((S, A_KV_WIDTH), bf16),
                   jax.ShapeDtypeStruct((S, B_WIDTH), bf16),
                   jax.ShapeDtypeStruct((S, kb_cols), bf16),
                   jax.ShapeDtypeStruct((B_HEADS, nchunks, B_V_DIM, B_TILE), bf16)],
        compiler_params=pltpu.CompilerParams(dimension_semantics=("arbitrary",),
                                             vmem_limit_bytes=VMEM_LIMIT),
        name="in_proj",
    )(h, p["attn_pre_g"].reshape(1, D), w_cat, kones)

    nblocks = S // A_BLOCK
    a_rows = A_STEP_BLOCKS * A_BLOCK
    assert nblocks % A_STEP_BLOCKS == 0 and A_STEP_BLOCKS >= 2
    ya = pl.pallas_call(
        functools.partial(_win_attn_kernel, nblocks=nblocks),
        grid=(nblocks // A_STEP_BLOCKS,),
        in_specs=[pl.BlockSpec(memory_space=pltpu.SMEM),
                  pl.BlockSpec((a_rows, A_WIDTH), lambda n: (n, 0)),
                  _resident(ka.shape),
                  _resident(va.shape),
                  _resident(bias_a.shape)],
        out_specs=pl.BlockSpec((a_rows, A_WIDTH), lambda n: (n, 0)),
        out_shape=jax.ShapeDtypeStruct((S, A_WIDTH), bf16),
        compiler_params=pltpu.CompilerParams(dimension_semantics=("arbitrary",),
                                             vmem_limit_bytes=VMEM_LIMIT),
        name="win_attn",
    )(p["a_sink"], qa, ka, va, bias_a)

    lamv = jnp.stack([p["lambda_q1"], p["lambda_k1"], p["lambda_q2"], p["lambda_k2"]])
    TQ, TK = B_QTILE, B_TILE
    assert S % TQ == 0 and TQ % TK == 0
    yb = pl.pallas_call(
        functools.partial(_diff_attn_kernel, nchunks=nchunks, lambda_init=lambda_init),
        grid=(B_HEADS, S // TQ),
        in_specs=[pl.BlockSpec((4, HEAD_DIM), lambda hh, i: (0, 0)),
                  pl.BlockSpec((1, B_V_DIM), lambda hh, i: (0, 0)),
                  pl.BlockSpec((TQ, LANES), lambda hh, i: (i, hh)),
                  pl.BlockSpec((S, 2 * LANES), lambda hh, i: (0, hh), pipeline_mode=pl.Buffered(1)),
                  pl.BlockSpec((1, nchunks, B_V_DIM, TK), lambda hh, i: (hh, 0, 0, 0),
                               pipeline_mode=pl.Buffered(1)),
                  pl.BlockSpec((1, 5, B_BIAS_BLOCK, B_BIAS_BLOCK), lambda hh, i: (hh, 0, 0, 0))],
        out_specs=pl.BlockSpec((TQ, LANES), lambda hh, i: (i, hh)),
        out_shape=jax.ShapeDtypeStruct((S, B_HEADS * B_V_DIM), bf16),
        scratch_shapes=[pltpu.VMEM((2, LANES, TQ), bf16),
                        pltpu.VMEM((2, LANES, TQ), bf16),
                        pltpu.VMEM((B_SLOTS, 2, TK, TQ), bf16),
                        pltpu.VMEM((B_SLOTS, 2, 1, TQ), jnp.float32),
                        pltpu.VMEM((2, 1, TQ), jnp.float32),
                        pltpu.VMEM((2, nchunks, TQ), jnp.float32),
                        pltpu.VMEM((2, 1, TQ), jnp.float32),
                        pltpu.VMEM((2, B_V_DIM, TQ), jnp.float32),
                        pltpu.VMEM((2, 1, TQ), jnp.float32)],
        compiler_params=pltpu.CompilerParams(dimension_semantics=("arbitrary", "arbitrary"),
                                             vmem_limit_bytes=VMEM_LIMIT),
        name="diff_attn",
    )(lamv, p["diff_subln_g"].reshape(1, B_V_DIM), qb, kb, vt, bias_b)

    w_out = p["w_out"].astype(bf16)
    a_width = A_WIDTH
    assert w_out.shape[0] == A_WIDTH + B_WIDTH
    d_ff = p["w_gate"].shape[1]
    R = FFN_ROW_TILE
    out = pl.pallas_call(
        _out_ffn_kernel,
        grid=(S // R,),
        in_specs=[pl.BlockSpec((R, A_WIDTH), lambda i: (i, 0)),
                  pl.BlockSpec((R, B_WIDTH), lambda i: (i, 0)),
                  pl.BlockSpec((R, D), lambda i: (i, 0)),
                  _resident((a_width, D)),
                  _resident((w_out.shape[0] - a_width, D)),
                  _resident((1, D)),
                  _resident((1, D)),
                  _resident((D, d_ff)),
                  _resident((D, d_ff)),
                  _resident((d_ff, D)),
                  _resident((1, D))],
        out_specs=pl.BlockSpec((R, D), lambda i: (i, 0)),
        out_shape=jax.ShapeDtypeStruct((S, D), jnp.float32),
        compiler_params=pltpu.CompilerParams(dimension_semantics=("arbitrary",),
                                             vmem_limit_bytes=VMEM_LIMIT),
        name="out_ffn",
    )(ya, yb, h, w_out[:a_width], w_out[a_width:], p["attn_post_g"].reshape(1, D),
      p["ffn_pre_g"].reshape(1, D), p["w_gate"].astype(bf16), p["w_up"].astype(bf16),
      p["w_down"].astype(bf16), p["ffn_post_g"].reshape(1, D))
    return out


def kernel(x, attn_pre_g, w_in, a_sink, lambda_q1, lambda_k1, lambda_q2, lambda_k2, diff_subln_g,
           rel_bias, w_out, attn_post_g, ffn_pre_g, w_gate, w_up, w_down, ffn_post_g):
    batch, S, D = x.shape
    depth = w_in.shape[0]
    assert S % B_TILE == 0 and S % A_BLOCK == 0 and S // A_BLOCK >= 2

    tab_a = _toeplitz_bias(rel_bias[:, :A_Q_HEADS], A_BLOCK, 3 * A_BLOCK, lambda t: -t - A_BLOCK)
    qi = np.arange(A_BLOCK)[:, None]
    kj = np.arange(3 * A_BLOCK)[None, :]
    in_window = np.abs(kj - A_BLOCK - qi) <= A_BLOCK
    valid = np.stack([in_window & (kj >= A_BLOCK), in_window, in_window & (kj < 2 * A_BLOCK)])
    bias_a = jnp.where(valid[:, None], tab_a[None] * LOG2E, MASK_VALUE)

    nb = B_BIAS_BLOCK
    assert len(set(_t5_bucket_np(np.arange(nb + 1, 4 * nb)))) == 1
    tab_b = rel_bias[:, A_Q_HEADS:]
    near = [_toeplitz_bias(tab_b, nb, nb, lambda t, d=d: d * nb + t) for d in (-1, 0, 1)]
    far = tab_b[_t5_bucket_np(np.array([-2 * nb, 2 * nb]))].astype(jnp.float32)
    const = [jnp.broadcast_to(far[side][:, None, None], (B_HEADS, nb, nb)) for side in (0, 1)]
    bias_b = jnp.stack([const[0]] + near + [const[1]], axis=1) * LOG2E

    outs = []
    for b in range(batch):
        h = x[b]
        for l in range(depth):
            p = dict(attn_pre_g=attn_pre_g[l], w_in=w_in[l], a_sink=a_sink[l], lambda_q1=lambda_q1[l],
                     lambda_k1=lambda_k1[l], lambda_q2=lambda_q2[l], lambda_k2=lambda_k2[l],
                     diff_subln_g=diff_subln_g[l], w_out=w_out[l], attn_post_g=attn_post_g[l],
                     ffn_pre_g=ffn_pre_g[l], w_gate=w_gate[l], w_up=w_up[l], w_down=w_down[l],
                     ffn_post_g=ffn_post_g[l])
            h = _layer(h, l, p, bias_a, bias_b)
        outs.append(h)
    return jnp.stack(outs)
```

```python
import functools
import math

import numpy as np
import jax
import jax.numpy as jnp
from jax import lax
from jax.experimental import pallas as pl
from jax.experimental.pallas import tpu as pltpu

HEAD_DIM = 64
A_Q_HEADS = 8
A_KV_HEADS = 2
A_BLOCK = 128
A_STEP_BLOCKS = 8
B_HEADS = 4
B_V_DIM = 2 * HEAD_DIM
NUM_BUCKETS = 32
MAX_DISTANCE = 128
EPS = 1e-6
MASK_VALUE = -1e30
LOG2E = math.log2(math.e)
ROW_SUM_LIMIT = 2.0 ** 40

A_WIDTH = A_Q_HEADS * HEAD_DIM
A_KV_WIDTH = 2 * A_KV_HEADS * HEAD_DIM
B_WIDTH = B_HEADS * B_V_DIM
COL_KA = A_WIDTH
COL_VA = COL_KA + A_KV_WIDTH
COL_QB = COL_VA + A_KV_WIDTH
COL_KB = COL_QB + B_WIDTH
COL_VB = COL_KB + B_WIDTH
PROJ_COLS = COL_VB + B_WIDTH

LANES = 128
ROW_TILE = 1024
FFN_ROW_TILE = 512
B_TILE = 512
B_QTILE = 4096
B_REF_LANES = 3
B_UNROLL = 2
B_AHEAD = 1
B_SLOTS = 2
B_PROBE = 128
B_BIAS_BLOCK = 128
VMEM_LIMIT = 56 * 1024 * 1024

_NT = (((1,), (1,)), ((), ()))


def _t5_bucket_np(rel):
    nb = NUM_BUCKETS // 2
    max_exact = nb // 2
    ret = np.where(rel > 0, nb, 0)
    n = np.abs(rel)
    nf = np.maximum(n, 1).astype(np.float32)
    large = max_exact + (np.log(nf / np.float32(max_exact)) / np.float32(math.log(MAX_DISTANCE / max_exact))
                         * np.float32(nb - max_exact)).astype(np.int32)
    large = np.minimum(large, nb - 1)
    return (ret + np.where(n < max_exact, n, large)).astype(np.int32)


def _toeplitz_bias(table, rows, cols, rel_of):
    length = rows + cols
    u = np.arange(length)
    t = np.where(u < cols, -u, length - u)
    w = table[_t5_bucket_np(rel_of(t))].astype(jnp.float32).T
    x = jnp.tile(w, (1, rows))[:, :rows * (length - 1)].reshape(w.shape[0], rows, length - 1)
    return x[:, :, :cols]


def _rms(xf, g):
    return xf * lax.rsqrt(jnp.mean(xf * xf, axis=-1, keepdims=True) + EPS) * g


def _in_proj_kernel(x_ref, g_ref, w_ref, kones_ref, qa_ref, ka_ref, va_ref, qb_ref, kb_ref, vt_ref):
    u = _rms(x_ref[...], g_ref[...]).astype(jnp.bfloat16)
    proj = jnp.dot(u, w_ref[...], preferred_element_type=jnp.float32)
    qa_ref[...] = proj[:, 0:COL_KA].astype(jnp.bfloat16)
    ka_ref[...] = proj[:, COL_KA:COL_VA].astype(jnp.bfloat16)
    va_ref[...] = proj[:, COL_VA:COL_QB].astype(jnp.bfloat16)
    qb_ref[...] = proj[:, COL_QB:COL_KB].astype(jnp.bfloat16)
    lane = lax.broadcasted_iota(jnp.int32, (proj.shape[0], LANES), 1)
    keep = (lane < HEAD_DIM, lane >= HEAD_DIM)
    for h in range(B_HEADS):
        kpair = proj[:, COL_KB + h * LANES:COL_KB + (h + 1) * LANES]
        for c in range(2):
            col = (2 * h + c) * LANES
            kb_ref[:, col:col + LANES] = jnp.where(keep[c], kpair, kones_ref[:, col:col + LANES]).astype(jnp.bfloat16)
        vt = proj[:, COL_VB + h * B_V_DIM:COL_VB + (h + 1) * B_V_DIM].T.astype(jnp.bfloat16)
        for t in range(vt_ref.shape[1]):
            vt_ref[h, t] = vt[:, t * B_TILE:(t + 1) * B_TILE]


def _win_attn_kernel(sink_ref, q_ref, k_ref, v_ref, bias_ref, o_ref, *, nblocks):
    f32 = jnp.float32
    bf16 = jnp.bfloat16
    n = pl.program_id(0)
    rows = A_STEP_BLOCKS * A_BLOCK
    seq = k_ref.shape[0]
    group = A_Q_HEADS // A_KV_HEADS
    start = pl.multiple_of(n * rows, rows)
    prev = pl.multiple_of(jnp.maximum(start - A_BLOCK, 0), A_BLOCK)
    nxt = pl.multiple_of(jnp.minimum(start + rows, seq - A_BLOCK), A_BLOCK)
    kw = jnp.concatenate([k_ref[pl.ds(prev, A_BLOCK), :], k_ref[pl.ds(start, rows), :],
                          k_ref[pl.ds(nxt, A_BLOCK), :]], axis=0)
    vw = jnp.concatenate([v_ref[pl.ds(prev, A_BLOCK), :], v_ref[pl.ds(start, rows), :],
                          v_ref[pl.ds(nxt, A_BLOCK), :]], axis=0)
    lane = lax.broadcasted_iota(jnp.int32, (A_BLOCK, LANES), 1)
    low = lane < HEAD_DIM
    ones = jnp.ones((3 * A_BLOCK, LANES), bf16)
    for b in range(A_STEP_BLOCKS):
        blk = n * A_STEP_BLOCKS + b
        if b == 0:
            variant = jnp.where(blk == 0, 0, 1)
        elif b == A_STEP_BLOCKS - 1:
            variant = jnp.where(blk == nblocks - 1, 2, 1)
        else:
            variant = 1
        q = q_ref[b * A_BLOCK:(b + 1) * A_BLOCK, :]
        for g in range(A_KV_HEADS):
            heads = range(g * group, (g + 1) * group)
            qs = []
            for h in heads:
                q2 = q[:, (h // 2) * LANES:(h // 2 + 1) * LANES]
                qs.append(jnp.where(low if h % 2 == 0 else jnp.logical_not(low), q2, jnp.zeros_like(q2)))
            q4 = jnp.concatenate(qs, axis=0)
            kg = kw[b * A_BLOCK:(b + 3) * A_BLOCK, g * LANES:(g + 1) * LANES]
            vg = jnp.concatenate([vw[b * A_BLOCK:(b + 3) * A_BLOCK, g * LANES:(g + 1) * LANES], ones],
                                 axis=1)
            s = lax.dot_general(q4, kg, _NT, preferred_element_type=f32)
            ps, sink_terms = [], []
            for t, h in enumerate(heads):
                sh = s[t * A_BLOCK:(t + 1) * A_BLOCK] + bias_ref[variant, h]
                snk = sink_ref[h] * LOG2E
                m = jnp.maximum(jnp.max(sh, axis=-1, keepdims=True), snk)
                ps.append(jnp.exp2(sh - m).astype(bf16))
                sink_terms.append(jnp.exp2(snk - m))
            p = jnp.concatenate(ps, axis=0)
            res = jnp.dot(p, vg, preferred_element_type=f32)
            on = res[:, 0:LANES] / (res[:, LANES:2 * LANES] + jnp.concatenate(sink_terms, axis=0))
            for t in range(group // 2):
                even = on[(2 * t) * A_BLOCK:(2 * t + 1) * A_BLOCK]
                odd = on[(2 * t + 1) * A_BLOCK:(2 * t + 2) * A_BLOCK]
                pair = (g * group) // 2 + t
                o_ref[b * A_BLOCK:(b + 1) * A_BLOCK, pair * LANES:(pair + 1) * LANES] = (
                    jnp.where(low, even, odd).astype(o_ref.dtype))


def _diff_attn_kernel(lamv_ref, g_ref, q_ref, k_ref, vt_ref, bias_ref, o_ref,
                      qz_ref, qp_ref, p_ref, psum_ref, r_ref, flag_ref, any_ref, acc_ref, den_ref,
                      *, nchunks, lambda_init):
    TQ = B_QTILE
    TK = B_TILE
    i = pl.program_id(1)
    f32 = jnp.float32
    bf16 = jnp.bfloat16

    qt = q_ref[...].astype(f32).T
    row = lax.broadcasted_iota(jnp.int32, (LANES, TQ), 0)
    is_q = (row < HEAD_DIM, row >= HEAD_DIM)
    ref_row = (HEAD_DIM, 0)

    ksub = TK // B_BIAS_BLOCK
    qsub = TQ // B_BIAS_BLOCK

    def bias_tile(j, rows=ksub):
        base = j * ksub - i * qsub
        by_dist = {d: bias_ref[0, jnp.clip(base + d, -2, 2) + 2] for d in range(1 - qsub, rows)}
        return jnp.concatenate(
            [jnp.concatenate([by_dist[a - b] for b in range(qsub)], axis=1) for a in range(rows)], axis=0)

    def k_chunk(j, c):
        return k_ref[pl.ds(pl.multiple_of(j * TK, TK), TK), c * LANES:(c + 1) * LANES]

    def set_reference(c, r_row):
        hi = r_row.astype(bf16).astype(f32)
        rem = r_row - hi
        mid = rem.astype(bf16).astype(f32)
        low = (rem - mid).astype(bf16).astype(f32)
        first = ref_row[c]
        ext = jnp.where(row == first, -hi, jnp.where(row == first + 1, -mid,
                                                     jnp.where(row == first + 2, -low, 0.0)))
        qp_ref[c] = jnp.where(is_q[c], qt, ext).T.astype(bf16)

    def exact_chunk(j, c, flagged):
        s = jnp.dot(k_chunk(j, c), qz_ref[c], preferred_element_type=f32) + bias_tile(j)
        r_old = r_ref[c]
        r_new = jnp.where(flagged, jnp.maximum(r_old, jnp.max(s, axis=0, keepdims=True)), r_old)
        p = jnp.where(flagged, jnp.exp2(s - r_new), 0.0)
        pv = jnp.dot(vt_ref[0, j], p.astype(bf16), preferred_element_type=f32)
        alpha = jnp.exp2(r_old - r_new)
        acc_ref[c] = acc_ref[c] * alpha + pv
        den_ref[c] = den_ref[c] * alpha + jnp.sum(p, axis=0, keepdims=True)
        r_ref[c] = r_new

    def stage_a(j, slot):
        bias = bias_tile(j)
        for c in range(2):
            s = lax.dot_general(k_chunk(j, c), qp_ref[c], _NT, preferred_element_type=f32) + bias
            p = jnp.exp2(s)
            psum_ref[slot, c] = jnp.sum(p, axis=0, keepdims=True)
            p_ref[slot, c] = p.astype(bf16)

    def stage_c(j, slot):
        vt = vt_ref[0, j]
        for c in range(2):
            pv = jnp.dot(vt, p_ref[slot, c], preferred_element_type=f32)
            psum = psum_ref[slot, c]
            ok = psum <= ROW_SUM_LIMIT
            acc_ref[c] = acc_ref[c] + jnp.where(ok, pv, 0.0)
            den_ref[c] = den_ref[c] + jnp.where(ok, psum, 0.0)
            flag = jnp.where(ok, 0.0, 1.0)
            flag_ref[c, pl.ds(j, 1), :] = flag
            any_ref[c] = jnp.maximum(any_ref[c], flag)

    for c in range(2):
        qz_ref[c] = jnp.where(is_q[c], qt, 0.0).astype(bf16)
    acc_ref[...] = jnp.zeros(acc_ref.shape, f32)
    den_ref[...] = jnp.zeros(den_ref.shape, f32)
    any_ref[...] = jnp.zeros(any_ref.shape, f32)
    diag = i * (TQ // TK)
    probe_bias = bias_tile(diag, rows=B_PROBE // B_BIAS_BLOCK)
    for c in range(2):
        kp = k_ref[pl.ds(pl.multiple_of(diag * TK, TK), B_PROBE), c * LANES:(c + 1) * LANES]
        s = jnp.dot(kp, qz_ref[c], preferred_element_type=f32) + probe_bias
        r = jnp.max(s, axis=0, keepdims=True)
        r_ref[c] = r
        set_reference(c, r)

    def run(j0, has_next):
        for u in range(B_UNROLL):
            stage_c(j0 + u, u % B_SLOTS)
            if u + B_AHEAD < B_UNROLL or has_next:
                stage_a(j0 + u + B_AHEAD, (u + B_AHEAD) % B_SLOTS)

    nbody = nchunks // B_UNROLL
    for u in range(B_AHEAD):
        stage_a(u, u)

    def body(b, carry):
        run(b * B_UNROLL, True)
        return carry

    lax.fori_loop(0, nbody - 1, body, 0)
    run((nbody - 1) * B_UNROLL, False)

    @pl.when(jnp.max(jnp.maximum(any_ref[0], any_ref[1])) > 0.0)
    def _():
        def redo(j, carry):
            @pl.when(jnp.max(jnp.maximum(flag_ref[0, pl.ds(j, 1), :], flag_ref[1, pl.ds(j, 1), :])) > 0.0)
            def _():
                for c in range(2):
                    exact_chunk(j, c, flag_ref[c, pl.ds(j, 1), :] > 0.0)
            return carry

        lax.fori_loop(0, nchunks, redo, 0)

    lv = lamv_ref[...]
    lam = (jnp.exp(jnp.sum(lv[0:1] * lv[1:2], axis=-1, keepdims=True))
           - jnp.exp(jnp.sum(lv[2:3] * lv[3:4], axis=-1, keepdims=True)) + lambda_init)
    ot = acc_ref[0] / den_ref[0] - lam * (acc_ref[1] / den_ref[1])
    o = ot.T
    o_ref[...] = (_rms(o, g_ref[...]) * (1.0 - lambda_init)).astype(o_ref.dtype)


def _out_ffn_kernel(ya_ref, yb_ref, x_ref, woa_ref, wob_ref, gpost_ref, gpre_ref,
                    wg_ref, wu_ref, wd_ref, gfpost_ref, o_ref):
    y = (jnp.dot(ya_ref[...], woa_ref[...], preferred_element_type=jnp.float32)
         + jnp.dot(yb_ref[...], wob_ref[...], preferred_element_type=jnp.float32))
    h1 = x_ref[...] + _rms(y, gpost_ref[...])
    u = _rms(h1, gpre_ref[...]).astype(jnp.bfloat16)
    gate = jnp.dot(u, wg_ref[...], preferred_element_type=jnp.float32)
    up = jnp.dot(u, wu_ref[...], preferred_element_type=jnp.float32)
    act = (gate * jax.nn.sigmoid(gate) * up).astype(jnp.bfloat16)
    f = jnp.dot(act, wd_ref[...], preferred_element_type=jnp.float32)
    o_ref[...] = h1 + _rms(f, gfpost_ref[...])


def _resident(shape):
    zeros = (0,) * len(shape)
    return pl.BlockSpec(shape, lambda *_: zeros, pipeline_mode=pl.Buffered(1))


def _layer(h, l, p, bias_a, bias_b):
    S, D = h.shape
    bf16 = jnp.bfloat16
    lambda_init = 0.8 - 0.6 * math.exp(-0.3 * l)
    scale = HEAD_DIM ** -0.5

    w = p["w_in"]
    a_q, a_kv, b_qk = A_Q_HEADS * HEAD_DIM, A_KV_HEADS * HEAD_DIM, B_HEADS * 2 * HEAD_DIM
    c0 = a_q
    c1 = c0 + a_kv
    c2 = c1 + a_kv
    c3 = c2 + b_qk

    def dup(cols):
        parts = []
        for g in range(A_KV_HEADS):
            blk = cols[:, g * HEAD_DIM:(g + 1) * HEAD_DIM]
            parts += [blk, blk]
        return jnp.concatenate(parts, axis=1)

    kones = np.zeros((B_HEADS, 2, LANES), np.float32)
    kones[:, 0, HEAD_DIM:HEAD_DIM + B_REF_LANES] = 1.0
    kones[:, 1, 0:B_REF_LANES] = 1.0
    kones = jnp.asarray(kones.reshape(1, B_HEADS * 2 * LANES))

    w_cat = jnp.concatenate([w[:, :c0] * (scale * LOG2E), dup(w[:, c0:c1]), dup(w[:, c1:c2]),
                             w[:, c2:c3] * (scale * LOG2E), w[:, c3:]],
                            axis=1).astype(bf16)
    ncols = w_cat.shape[1]
    assert ncols == PROJ_COLS and w.shape[1] == c3 + 2 * B_WIDTH
    nrow = S // ROW_TILE
    nchunks = S // B_TILE
    assert ROW_TILE % B_TILE == 0 and S % ROW_TILE == 0 and nchunks % B_UNROLL == 0
    assert B_UNROLL % B_SLOTS == 0 and B_AHEAD < B_SLOTS and B_AHEAD <= B_UNROLL
    kb_cols = B_HEADS * 2 * LANES

    qa, ka, va, qb, kb, vt = pl.pallas_call(
        _in_proj_kernel,
        grid=(nrow,),
        in_specs=[pl.BlockSpec((ROW_TILE, D), lambda i: (i, 0)),
                  _resident((1, D)),
                  _resident((D, ncols)),
                  _resident((1, kb_cols))],
        out_specs=[pl.BlockSpec((ROW_TILE, A_WIDTH), lambda i: (i, 0)),
                   pl.BlockSpec((ROW_TILE, A_KV_WIDTH), lambda i: (i, 0)),
                   pl.BlockSpec((ROW_TILE, A_KV_WIDTH), lambda i: (i, 0)),
                   pl.BlockSpec((ROW_TILE, B_WIDTH), lambda i: (i, 0)),
                   pl.BlockSpec((ROW_TILE, kb_cols), lambda i: (i, 0)),
                   pl.BlockSpec((B_HEADS, ROW_TILE // B_TILE, B_V_DIM, B_TILE), lambda i: (0, i, 0, 0))],
        out_shape=[jax.ShapeDtypeStruct((S, A_WIDTH), bf16),
                   jax.ShapeDtypeStruct((S, A_KV_WIDTH), bf16),
                   jax.ShapeDtypeStruct((S, A_KV_WIDTH), bf16),
                   jax.ShapeDtypeStruct((S, B_WIDTH), bf16),
                   jax.ShapeDtypeStruct((S, kb_cols), bf16),
                   jax.ShapeDtypeStruct((B_HEADS, nchunks, B_V_DIM, B_TILE), bf16)],
        compiler_params=pltpu.CompilerParams(dimension_semantics=("arbitrary",),
                                             vmem_limit_bytes=VMEM_LIMIT),
        name="in_proj",
    )(h, p["attn_pre_g"].reshape(1, D), w_cat, kones)

    nblocks = S // A_BLOCK
    a_rows = A_STEP_BLOCKS * A_BLOCK
    assert nblocks % A_STEP_BLOCKS == 0 and A_STEP_BLOCKS >= 2
    ya = pl.pallas_call(
        functools.partial(_win_attn_kernel, nblocks=nblocks),
        grid=(nblocks // A_STEP_BLOCKS,),
        in_specs=[pl.BlockSpec(memory_space=pltpu.SMEM),
                  pl.BlockSpec((a_rows, A_WIDTH), lambda n: (n, 0)),
                  _resident(ka.shape),
                  _resident(va.shape),
                  _resident(bias_a.shape)],
        out_specs=pl.BlockSpec((a_rows, A_WIDTH), lambda n: (n, 0)),
        out_shape=jax.ShapeDtypeStruct((S, A_WIDTH), bf16),
        compiler_params=pltpu.CompilerParams(dimension_semantics=("arbitrary",),
                                             vmem_limit_bytes=VMEM_LIMIT),
        name="win_attn",
    )(p["a_sink"], qa, ka, va, bias_a)

    lamv = jnp.stack([p["lambda_q1"], p["lambda_k1"], p["lambda_q2"], p["lambda_k2"]])
    TQ, TK = B_QTILE, B_TILE
    assert S % TQ == 0 and TQ % TK == 0
    yb = pl.pallas_call(
        functools.partial(_diff_attn_kernel, nchunks=nchunks, lambda_init=lambda_init),
        grid=(B_HEADS, S // TQ),
        in_specs=[pl.BlockSpec((4, HEAD_DIM), lambda hh, i: (0, 0)),
                  pl.BlockSpec((1, B_V_DIM), lambda hh, i: (0, 0)),
                  pl.BlockSpec((TQ, LANES), lambda hh, i: (i, hh)),
                  pl.BlockSpec((S, 2 * LANES), lambda hh, i: (0, hh), pipeline_mode=pl.Buffered(1)),
                  pl.BlockSpec((1, nchunks, B_V_DIM, TK), lambda hh, i: (hh, 0, 0, 0),
                               pipeline_mode=pl.Buffered(1)),
                  pl.BlockSpec((1, 5, B_BIAS_BLOCK, B_BIAS_BLOCK), lambda hh, i: (hh, 0, 0, 0))],
        out_specs=pl.BlockSpec((TQ, LANES), lambda hh, i: (i, hh)),
        out_shape=jax.ShapeDtypeStruct((S, B_HEADS * B_V_DIM), bf16),
        scratch_shapes=[pltpu.VMEM((2, LANES, TQ), bf16),
                        pltpu.VMEM((2, TQ, LANES), bf16),
                        pltpu.VMEM((B_SLOTS, 2, TK, TQ), bf16),
                        pltpu.VMEM((B_SLOTS, 2, 1, TQ), jnp.float32),
                        pltpu.VMEM((2, 1, TQ), jnp.float32),
                        pltpu.VMEM((2, nchunks, TQ), jnp.float32),
                        pltpu.VMEM((2, 1, TQ), jnp.float32),
                        pltpu.VMEM((2, B_V_DIM, TQ), jnp.float32),
                        pltpu.VMEM((2, 1, TQ), jnp.float32)],
        compiler_params=pltpu.CompilerParams(dimension_semantics=("arbitrary", "arbitrary"),
                                             vmem_limit_bytes=VMEM_LIMIT),
        name="diff_attn",
    )(lamv, p["diff_subln_g"].reshape(1, B_V_DIM), qb, kb, vt, bias_b)

    w_out = p["w_out"].astype(bf16)
    a_width = A_WIDTH
    assert w_out.shape[0] == A_WIDTH + B_WIDTH
    d_ff = p["w_gate"].shape[1]
    R = FFN_ROW_TILE
    out = pl.pallas_call(
        _out_ffn_kernel,
        grid=(S // R,),
        in_specs=[pl.BlockSpec((R, A_WIDTH), lambda i: (i, 0)),
                  pl.BlockSpec((R, B_WIDTH), lambda i: (i, 0)),
                  pl.BlockSpec((R, D), lambda i: (i, 0)),
                  _resident((a_width, D)),
                  _resident((w_out.shape[0] - a_width, D)),
                  _resident((1, D)),
                  _resident((1, D)),
                  _resident((D, d_ff)),
                  _resident((D, d_ff)),
                  _resident((d_ff, D)),
                  _resident((1, D))],
        out_specs=pl.BlockSpec((R, D), lambda i: (i, 0)),
        out_shape=jax.ShapeDtypeStruct((S, D), jnp.float32),
        compiler_params=pltpu.CompilerParams(dimension_semantics=("arbitrary",),
                                             vmem_limit_bytes=VMEM_LIMIT),
        name="out_ffn",
    )(ya, yb, h, w_out[:a_width], w_out[a_width:], p["attn_post_g"].reshape(1, D),
      p["ffn_pre_g"].reshape(1, D), p["w_gate"].astype(bf16), p["w_up"].astype(bf16),
      p["w_down"].astype(bf16), p["ffn_post_g"].reshape(1, D))
    return out


def kernel(x, attn_pre_g, w_in, a_sink, lambda_q1, lambda_k1, lambda_q2, lambda_k2, diff_subln_g,
           rel_bias, w_out, attn_post_g, ffn_pre_g, w_gate, w_up, w_down, ffn_post_g):
    batch, S, D = x.shape
    depth = w_in.shape[0]
    assert S % B_TILE == 0 and S % A_BLOCK == 0 and S // A_BLOCK >= 2

    tab_a = _toeplitz_bias(rel_bias[:, :A_Q_HEADS], A_BLOCK, 3 * A_BLOCK, lambda t: -t - A_BLOCK)
    qi = np.arange(A_BLOCK)[:, None]
    kj = np.arange(3 * A_BLOCK)[None, :]
    in_window = np.abs(kj - A_BLOCK - qi) <= A_BLOCK
    valid = np.stack([in_window & (kj >= A_BLOCK), in_window, in_window & (kj < 2 * A_BLOCK)])
    bias_a = jnp.where(valid[:, None], tab_a[None] * LOG2E, MASK_VALUE)

    nb = B_BIAS_BLOCK
    assert len(set(_t5_bucket_np(np.arange(nb + 1, 4 * nb)))) == 1
    tab_b = rel_bias[:, A_Q_HEADS:]
    near = [_toeplitz_bias(tab_b, nb, nb, lambda t, d=d: d * nb + t) for d in (-1, 0, 1)]
    far = tab_b[_t5_bucket_np(np.array([-2 * nb, 2 * nb]))].astype(jnp.float32)
    const = [jnp.broadcast_to(far[side][:, None, None], (B_HEADS, nb, nb)) for side in (0, 1)]
    bias_b = jnp.stack([const[0]] + near + [const[1]], axis=1) * LOG2E

    outs = []
    for b in range(batch):
        h = x[b]
        for l in range(depth):
            p = dict(attn_pre_g=attn_pre_g[l], w_in=w_in[l], a_sink=a_sink[l], lambda_q1=lambda_q1[l],
                     lambda_k1=lambda_k1[l], lambda_q2=lambda_q2[l], lambda_k2=lambda_k2[l],
                     diff_subln_g=diff_subln_g[l], w_out=w_out[l], attn_post_g=attn_post_g[l],
                     ffn_pre_g=ffn_pre_g[l], w_gate=w_gate[l], w_up=w_up[l], w_down=w_down[l],
                     ffn_post_g=ffn_post_g[l])
            h = _layer(h, l, p, bias_a, bias_b)
        outs.append(h)
    return jnp.stack(outs)
```

```python
import functools
import math

import numpy as np
import jax
import jax.numpy as jnp
from jax import lax
from jax.experimental import pallas as pl
from jax.experimental.pallas import tpu as pltpu

HEAD_DIM = 64
A_Q_HEADS = 8
A_KV_HEADS = 2
A_BLOCK = 128
A_STEP_BLOCKS = 8
B_HEADS = 4
B_V_DIM = 2 * HEAD_DIM
NUM_BUCKETS = 32
MAX_DISTANCE = 128
EPS = 1e-6
MASK_VALUE = -1e30
LOG2E = math.log2(math.e)
ROW_SUM_LIMIT = 2.0 ** 40

A_WIDTH = A_Q_HEADS * HEAD_DIM
A_KV_WIDTH = 2 * A_KV_HEADS * HEAD_DIM
B_WIDTH = B_HEADS * B_V_DIM
COL_KA = A_WIDTH
COL_VA = COL_KA + A_KV_WIDTH
COL_QB = COL_VA + A_KV_WIDTH
COL_KB = COL_QB + B_WIDTH
COL_VB = COL_KB + B_WIDTH
PROJ_COLS = COL_VB + B_WIDTH

LANES = 128
ROW_TILE = 1024
FFN_ROW_TILE = 512
B_TILE = 512
B_QTILE = 4096
B_REF_LANES = 3
B_UNROLL = 2
B_AHEAD = 1
B_SLOTS = 2
B_PROBE = 128
B_BIAS_BLOCK = 128
VMEM_LIMIT = 56 * 1024 * 1024

_NT = (((1,), (1,)), ((), ()))


def _t5_bucket_np(rel):
    nb = NUM_BUCKETS // 2
    max_exact = nb // 2
    ret = np.where(rel > 0, nb, 0)
    n = np.abs(rel)
    nf = np.maximum(n, 1).astype(np.float32)
    large = max_exact + (np.log(nf / np.float32(max_exact)) / np.float32(math.log(MAX_DISTANCE / max_exact))
                         * np.float32(nb - max_exact)).astype(np.int32)
    large = np.minimum(large, nb - 1)
    return (ret + np.where(n < max_exact, n, large)).astype(np.int32)


def _toeplitz_bias(table, rows, cols, rel_of):
    length = rows + cols
    u = np.arange(length)
    t = np.where(u < cols, -u, length - u)
    w = table[_t5_bucket_np(rel_of(t))].astype(jnp.float32).T
    x = jnp.tile(w, (1, rows))[:, :rows * (length - 1)].reshape(w.shape[0], rows, length - 1)
    return x[:, :, :cols]


def _rms(xf, g):
    return xf * lax.rsqrt(jnp.mean(xf * xf, axis=-1, keepdims=True) + EPS) * g


def _in_proj_kernel(x_ref, g_ref, w_ref, kones_ref, qa_ref, ka_ref, va_ref, qb_ref, kb_ref, vt_ref):
    u = _rms(x_ref[...], g_ref[...]).astype(jnp.bfloat16)
    proj = jnp.dot(u, w_ref[...], preferred_element_type=jnp.float32)
    qa_ref[...] = proj[:, 0:COL_KA].astype(jnp.bfloat16)
    ka_ref[...] = proj[:, COL_KA:COL_VA].astype(jnp.bfloat16)
    va_ref[...] = proj[:, COL_VA:COL_QB].astype(jnp.bfloat16)
    qb_ref[...] = proj[:, COL_QB:COL_KB].astype(jnp.bfloat16)
    lane = lax.broadcasted_iota(jnp.int32, (proj.shape[0], LANES), 1)
    keep = (lane < HEAD_DIM, lane >= HEAD_DIM)
    for h in range(B_HEADS):
        kpair = proj[:, COL_KB + h * LANES:COL_KB + (h + 1) * LANES]
        for c in range(2):
            col = (2 * h + c) * LANES
            kb_ref[:, col:col + LANES] = jnp.where(keep[c], kpair, kones_ref[:, col:col + LANES]).astype(jnp.bfloat16)
        vt = proj[:, COL_VB + h * B_V_DIM:COL_VB + (h + 1) * B_V_DIM].T.astype(jnp.bfloat16)
        for t in range(vt_ref.shape[1]):
            vt_ref[h, t] = vt[:, t * B_TILE:(t + 1) * B_TILE]


def _win_attn_kernel(q_ref, k_ref, v_ref, bias_ref, o_ref, *, nblocks):
    f32 = jnp.float32
    bf16 = jnp.bfloat16
    n = pl.program_id(0)
    rows = A_STEP_BLOCKS * A_BLOCK
    seq = k_ref.shape[0]
    group = A_Q_HEADS // A_KV_HEADS
    start = pl.multiple_of(n * rows, rows)
    prev = pl.multiple_of(jnp.maximum(start - A_BLOCK, 0), A_BLOCK)
    nxt = pl.multiple_of(jnp.minimum(start + rows, seq - A_BLOCK), A_BLOCK)
    kw = jnp.concatenate([k_ref[pl.ds(prev, A_BLOCK), :], k_ref[pl.ds(start, rows), :],
                          k_ref[pl.ds(nxt, A_BLOCK), :]], axis=0)
    vw = jnp.concatenate([v_ref[pl.ds(prev, A_BLOCK), :], v_ref[pl.ds(start, rows), :],
                          v_ref[pl.ds(nxt, A_BLOCK), :]], axis=0)
    lane = lax.broadcasted_iota(jnp.int32, (A_BLOCK, LANES), 1)
    low = lane < HEAD_DIM
    ones = jnp.ones((3 * A_BLOCK, LANES), bf16)

    def attend(use_row_max):
        over = jnp.zeros((group * A_BLOCK, LANES), f32)
        for b in range(A_STEP_BLOCKS):
            blk = n * A_STEP_BLOCKS + b
            if b == 0:
                variant = jnp.where(blk == 0, 0, 1)
            elif b == A_STEP_BLOCKS - 1:
                variant = jnp.where(blk == nblocks - 1, 2, 1)
            else:
                variant = 1
            q = q_ref[b * A_BLOCK:(b + 1) * A_BLOCK, :]
            for g in range(A_KV_HEADS):
                heads = range(g * group, (g + 1) * group)
                qs = []
                for h in heads:
                    q2 = q[:, (h // 2) * LANES:(h // 2 + 1) * LANES]
                    qs.append(jnp.where(low if h % 2 == 0 else jnp.logical_not(low), q2, jnp.zeros_like(q2)))
                q4 = jnp.concatenate(qs, axis=0)
                kg = kw[b * A_BLOCK:(b + 3) * A_BLOCK, g * LANES:(g + 1) * LANES]
                vg = jnp.concatenate([vw[b * A_BLOCK:(b + 3) * A_BLOCK, g * LANES:(g + 1) * LANES], ones],
                                     axis=1)
                s = lax.dot_general(q4, kg, _NT, preferred_element_type=f32)
                ps, sink_terms = [], []
                for t, h in enumerate(heads):
                    sh = s[t * A_BLOCK:(t + 1) * A_BLOCK] + bias_ref[variant, h]
                    if use_row_max:
                        m = jnp.maximum(jnp.max(sh, axis=-1, keepdims=True), 0.0)
                        ps.append(jnp.exp2(sh - m).astype(bf16))
                        sink_terms.append(jnp.exp2(-m))
                    else:
                        ps.append(jnp.exp2(sh).astype(bf16))
                p = jnp.concatenate(ps, axis=0)
                res = jnp.dot(p, vg, preferred_element_type=f32)
                row_sum = res[:, LANES:2 * LANES]
                sink_term = jnp.concatenate(sink_terms, axis=0) if use_row_max else 1.0
                on = res[:, 0:LANES] / (row_sum + sink_term)
                over = jnp.maximum(over, jnp.where(row_sum <= ROW_SUM_LIMIT, 0.0, 1.0))
                for t in range(group // 2):
                    even = on[(2 * t) * A_BLOCK:(2 * t + 1) * A_BLOCK]
                    odd = on[(2 * t + 1) * A_BLOCK:(2 * t + 2) * A_BLOCK]
                    pair = (g * group) // 2 + t
                    o_ref[b * A_BLOCK:(b + 1) * A_BLOCK, pair * LANES:(pair + 1) * LANES] = (
                        jnp.where(low, even, odd).astype(o_ref.dtype))
        return over

    over = attend(use_row_max=False)

    @pl.when(jnp.max(over) > 0.0)
    def _():
        attend(use_row_max=True)


def _diff_attn_kernel(lamv_ref, g_ref, q_ref, k_ref, vt_ref, bias_ref, o_ref,
                      qz_ref, qp_ref, p_ref, psum_ref, r_ref, flag_ref, any_ref, acc_ref, den_ref,
                      *, nchunks, lambda_init):
    TQ = B_QTILE
    TK = B_TILE
    i = pl.program_id(1)
    f32 = jnp.float32
    bf16 = jnp.bfloat16

    qt = q_ref[...].astype(f32).T
    row = lax.broadcasted_iota(jnp.int32, (LANES, TQ), 0)
    is_q = (row < HEAD_DIM, row >= HEAD_DIM)
    ref_row = (HEAD_DIM, 0)

    ksub = TK // B_BIAS_BLOCK
    qsub = TQ // B_BIAS_BLOCK

    def bias_tile(j, rows=ksub):
        base = j * ksub - i * qsub
        by_dist = {d: bias_ref[0, jnp.clip(base + d, -2, 2) + 2] for d in range(1 - qsub, rows)}
        return jnp.concatenate(
            [jnp.concatenate([by_dist[a - b] for b in range(qsub)], axis=1) for a in range(rows)], axis=0)

    def k_chunk(j, c):
        return k_ref[pl.ds(pl.multiple_of(j * TK, TK), TK), c * LANES:(c + 1) * LANES]

    def set_reference(c, r_row):
        hi = r_row.astype(bf16).astype(f32)
        rem = r_row - hi
        mid = rem.astype(bf16).astype(f32)
        low = (rem - mid).astype(bf16).astype(f32)
        first = ref_row[c]
        ext = jnp.where(row == first, -hi, jnp.where(row == first + 1, -mid,
                                                     jnp.where(row == first + 2, -low, 0.0)))
        qp_ref[c] = jnp.where(is_q[c], qt, ext).astype(bf16)

    def exact_chunk(j, c, flagged):
        s = jnp.dot(k_chunk(j, c), qz_ref[c], preferred_element_type=f32) + bias_tile(j)
        r_old = r_ref[c]
        r_new = jnp.where(flagged, jnp.maximum(r_old, jnp.max(s, axis=0, keepdims=True)), r_old)
        p = jnp.where(flagged, jnp.exp2(s - r_new), 0.0)
        pv = jnp.dot(vt_ref[0, j], p.astype(bf16), preferred_element_type=f32)
        alpha = jnp.exp2(r_old - r_new)
        acc_ref[c] = acc_ref[c] * alpha + pv
        den_ref[c] = den_ref[c] * alpha + jnp.sum(p, axis=0, keepdims=True)
        r_ref[c] = r_new

    def stage_a(j, slot):
        bias = bias_tile(j)
        for c in range(2):
            s = jnp.dot(k_chunk(j, c), qp_ref[c], preferred_element_type=f32) + bias
            p = jnp.exp2(s)
            psum_ref[slot, c] = jnp.sum(p, axis=0, keepdims=True)
            p_ref[slot, c] = p.astype(bf16)

    def stage_c(j, slot):
        vt = vt_ref[0, j]
        for c in range(2):
            pv = jnp.dot(vt, p_ref[slot, c], preferred_element_type=f32)
            psum = psum_ref[slot, c]
            ok = psum <= ROW_SUM_LIMIT
            acc_ref[c] = acc_ref[c] + jnp.where(ok, pv, 0.0)
            den_ref[c] = den_ref[c] + jnp.where(ok, psum, 0.0)
            flag = jnp.where(ok, 0.0, 1.0)
            flag_ref[c, pl.ds(j, 1), :] = flag
            any_ref[c] = jnp.maximum(any_ref[c], flag)

    for c in range(2):
        qz_ref[c] = jnp.where(is_q[c], qt, 0.0).astype(bf16)
    acc_ref[...] = jnp.zeros(acc_ref.shape, f32)
    den_ref[...] = jnp.zeros(den_ref.shape, f32)
    any_ref[...] = jnp.zeros(any_ref.shape, f32)
    diag = i * (TQ // TK)
    probe_bias = bias_tile(diag, rows=B_PROBE // B_BIAS_BLOCK)
    for c in range(2):
        kp = k_ref[pl.ds(pl.multiple_of(diag * TK, TK), B_PROBE), c * LANES:(c + 1) * LANES]
        s = jnp.dot(kp, qz_ref[c], preferred_element_type=f32) + probe_bias
        r = jnp.max(s, axis=0, keepdims=True)
        r_ref[c] = r
        set_reference(c, r)

    def run(j0, has_next):
        for u in range(B_UNROLL):
            stage_c(j0 + u, u % B_SLOTS)
            if u + B_AHEAD < B_UNROLL or has_next:
                stage_a(j0 + u + B_AHEAD, (u + B_AHEAD) % B_SLOTS)

    nbody = nchunks // B_UNROLL
    for u in range(B_AHEAD):
        stage_a(u, u)

    def body(b, carry):
        run(b * B_UNROLL, True)
        return carry

    lax.fori_loop(0, nbody - 1, body, 0)
    run((nbody - 1) * B_UNROLL, False)

    @pl.when(jnp.max(jnp.maximum(any_ref[0], any_ref[1])) > 0.0)
    def _():
        def redo(j, carry):
            @pl.when(jnp.max(jnp.maximum(flag_ref[0, pl.ds(j, 1), :], flag_ref[1, pl.ds(j, 1), :])) > 0.0)
            def _():
                for c in range(2):
                    exact_chunk(j, c, flag_ref[c, pl.ds(j, 1), :] > 0.0)
            return carry

        lax.fori_loop(0, nchunks, redo, 0)

    lv = lamv_ref[...]
    lam = (jnp.exp(jnp.sum(lv[0:1] * lv[1:2], axis=-1, keepdims=True))
           - jnp.exp(jnp.sum(lv[2:3] * lv[3:4], axis=-1, keepdims=True)) + lambda_init)
    ot = acc_ref[0] / den_ref[0] - lam * (acc_ref[1] / den_ref[1])
    o = ot.T
    o_ref[...] = (_rms(o, g_ref[...]) * (1.0 - lambda_init)).astype(o_ref.dtype)


def _out_ffn_kernel(ya_ref, yb_ref, x_ref, woa_ref, wob_ref, gpost_ref, gpre_ref,
                    wg_ref, wu_ref, wd_ref, gfpost_ref, o_ref):
    y = (jnp.dot(ya_ref[...], woa_ref[...], preferred_element_type=jnp.float32)
         + jnp.dot(yb_ref[...], wob_ref[...], preferred_element_type=jnp.float32))
    h1 = x_ref[...] + _rms(y, gpost_ref[...])
    u = _rms(h1, gpre_ref[...]).astype(jnp.bfloat16)
    gate = jnp.dot(u, wg_ref[...], preferred_element_type=jnp.float32)
    up = jnp.dot(u, wu_ref[...], preferred_element_type=jnp.float32)
    act = (gate * jax.nn.sigmoid(gate) * up).astype(jnp.bfloat16)
    f = jnp.dot(act, wd_ref[...], preferred_element_type=jnp.float32)
    o_ref[...] = h1 + _rms(f, gfpost_ref[...])


def _resident(shape):
    zeros = (0,) * len(shape)
    return pl.BlockSpec(shape, lambda *_: zeros, pipeline_mode=pl.Buffered(1))


def _layer(h, l, p, bias_a, bias_b):
    S, D = h.shape
    bf16 = jnp.bfloat16
    lambda_init = 0.8 - 0.6 * math.exp(-0.3 * l)
    scale = HEAD_DIM ** -0.5

    w = p["w_in"]
    a_q, a_kv, b_qk = A_Q_HEADS * HEAD_DIM, A_KV_HEADS * HEAD_DIM, B_HEADS * 2 * HEAD_DIM
    c0 = a_q
    c1 = c0 + a_kv
    c2 = c1 + a_kv
    c3 = c2 + b_qk

    def dup(cols):
        parts = []
        for g in range(A_KV_HEADS):
            blk = cols[:, g * HEAD_DIM:(g + 1) * HEAD_DIM]
            parts += [blk, blk]
        return jnp.concatenate(parts, axis=1)

    kones = np.zeros((B_HEADS, 2, LANES), np.float32)
    kones[:, 0, HEAD_DIM:HEAD_DIM + B_REF_LANES] = 1.0
    kones[:, 1, 0:B_REF_LANES] = 1.0
    kones = jnp.asarray(kones.reshape(1, B_HEADS * 2 * LANES))

    w_cat = jnp.concatenate([w[:, :c0] * (scale * LOG2E), dup(w[:, c0:c1]), dup(w[:, c1:c2]),
                             w[:, c2:c3] * (scale * LOG2E), w[:, c3:]],
                            axis=1).astype(bf16)
    ncols = w_cat.shape[1]
    assert ncols == PROJ_COLS and w.shape[1] == c3 + 2 * B_WIDTH
    nrow = S // ROW_TILE
    nchunks = S // B_TILE
    assert ROW_TILE % B_TILE == 0 and S % ROW_TILE == 0 and nchunks % B_UNROLL == 0
    assert B_UNROLL % B_SLOTS == 0 and B_AHEAD < B_SLOTS and B_AHEAD <= B_UNROLL
    kb_cols = B_HEADS * 2 * LANES

    qa, ka, va, qb, kb, vt = pl.pallas_call(
        _in_proj_kernel,
        grid=(nrow,),
        in_specs=[pl.BlockSpec((ROW_TILE, D), lambda i: (i, 0)),
                  _resident((1, D)),
                  _resident((D, ncols)),
                  _resident((1, kb_cols))],
        out_specs=[pl.BlockSpec((ROW_TILE, A_WIDTH), lambda i: (i, 0)),
                   pl.BlockSpec((ROW_TILE, A_KV_WIDTH), lambda i: (i, 0)),
                   pl.BlockSpec((ROW_TILE, A_KV_WIDTH), lambda i: (i, 0)),
                   pl.BlockSpec((ROW_TILE, B_WIDTH), lambda i: (i, 0)),
                   pl.BlockSpec((ROW_TILE, kb_cols), lambda i: (i, 0)),
                   pl.BlockSpec((B_HEADS, ROW_TILE // B_TILE, B_V_DIM, B_TILE), lambda i: (0, i, 0, 0))],
        out_shape=[jax.ShapeDtypeStruct((S, A_WIDTH), bf16),
                   jax.ShapeDtypeStruct((S, A_KV_WIDTH), bf16),
                   jax.ShapeDtypeStruct((S, A_KV_WIDTH), bf16),
                   jax.ShapeDtypeStruct((S, B_WIDTH), bf16),
                   jax.ShapeDtypeStruct((S, kb_cols), bf16),
                   jax.ShapeDtypeStruct((B_HEADS, nchunks, B_V_DIM, B_TILE), bf16)],
        compiler_params=pltpu.CompilerParams(dimension_semantics=("arbitrary",),
                                             vmem_limit_bytes=VMEM_LIMIT),
        name="in_proj",
    )(h, p["attn_pre_g"].reshape(1, D), w_cat, kones)

    nblocks = S // A_BLOCK
    a_rows = A_STEP_BLOCKS * A_BLOCK
    assert nblocks % A_STEP_BLOCKS == 0 and A_STEP_BLOCKS >= 2
    bias_a = bias_a - (p["a_sink"].astype(jnp.float32) * LOG2E)[None, :, None, None]
    ya = pl.pallas_call(
        functools.partial(_win_attn_kernel, nblocks=nblocks),
        grid=(nblocks // A_STEP_BLOCKS,),
        in_specs=[pl.BlockSpec((a_rows, A_WIDTH), lambda n: (n, 0)),
                  _resident(ka.shape),
                  _resident(va.shape),
                  _resident(bias_a.shape)],
        out_specs=pl.BlockSpec((a_rows, A_WIDTH), lambda n: (n, 0)),
        out_shape=jax.ShapeDtypeStruct((S, A_WIDTH), bf16),
        compiler_params=pltpu.CompilerParams(dimension_semantics=("arbitrary",),
                                             vmem_limit_bytes=VMEM_LIMIT),
        name="win_attn",
    )(qa, ka, va, bias_a)

    lamv = jnp.stack([p["lambda_q1"], p["lambda_k1"], p["lambda_q2"], p["lambda_k2"]])
    TQ, TK = B_QTILE, B_TILE
    assert S % TQ == 0 and TQ % TK == 0
    yb = pl.pallas_call(
        functools.partial(_diff_attn_kernel, nchunks=nchunks, lambda_init=lambda_init),
        grid=(B_HEADS, S // TQ),
        in_specs=[pl.BlockSpec((4, HEAD_DIM), lambda hh, i: (0, 0)),
                  pl.BlockSpec((1, B_V_DIM), lambda hh, i: (0, 0)),
                  pl.BlockSpec((TQ, LANES), lambda hh, i: (i, hh)),
                  pl.BlockSpec((S, 2 * LANES), lambda hh, i: (0, hh), pipeline_mode=pl.Buffered(1)),
                  pl.BlockSpec((1, nchunks, B_V_DIM, TK), lambda hh, i: (hh, 0, 0, 0),
                               pipeline_mode=pl.Buffered(1)),
                  pl.BlockSpec((1, 5, B_BIAS_BLOCK, B_BIAS_BLOCK), lambda hh, i: (hh, 0, 0, 0))],
        out_specs=pl.BlockSpec((TQ, LANES), lambda hh, i: (i, hh)),
        out_shape=jax.ShapeDtypeStruct((S, B_HEADS * B_V_DIM), bf16),
        scratch_shapes=[pltpu.VMEM((2, LANES, TQ), bf16),
                        pltpu.VMEM((2, LANES, TQ), bf16),
                        pltpu.VMEM((B_SLOTS, 2, TK, TQ), bf16),
                        pltpu.VMEM((B_SLOTS, 2, 1, TQ), jnp.float32),
                        pltpu.VMEM((2, 1, TQ), jnp.float32),
                        pltpu.VMEM((2, nchunks, TQ), jnp.float32),
                        pltpu.VMEM((2, 1, TQ), jnp.float32),
                        pltpu.VMEM((2, B_V_DIM, TQ), jnp.float32),
                        pltpu.VMEM((2, 1, TQ), jnp.float32)],
        compiler_params=pltpu.CompilerParams(dimension_semantics=("arbitrary", "arbitrary"),
                                             vmem_limit_bytes=VMEM_LIMIT),
        name="diff_attn",
    )(lamv, p["diff_subln_g"].reshape(1, B_V_DIM), qb, kb, vt, bias_b)

    w_out = p["w_out"].astype(bf16)
    a_width = A_WIDTH
    assert w_out.shape[0] == A_WIDTH + B_WIDTH
    d_ff = p["w_gate"].shape[1]
    R = FFN_ROW_TILE
    out = pl.pallas_call(
        _out_ffn_kernel,
        grid=(S // R,),
        in_specs=[pl.BlockSpec((R, A_WIDTH), lambda i: (i, 0)),
                  pl.BlockSpec((R, B_WIDTH), lambda i: (i, 0)),
                  pl.BlockSpec((R, D), lambda i: (i, 0)),
                  _resident((a_width, D)),
                  _resident((w_out.shape[0] - a_width, D)),
                  _resident((1, D)),
                  _resident((1, D)),
                  _resident((D, d_ff)),
                  _resident((D, d_ff)),
                  _resident((d_ff, D)),
                  _resident((1, D))],
        out_specs=pl.BlockSpec((R, D), lambda i: (i, 0)),
        out_shape=jax.ShapeDtypeStruct((S, D), jnp.float32),
        compiler_params=pltpu.CompilerParams(dimension_semantics=("arbitrary",),
                                             vmem_limit_bytes=VMEM_LIMIT),
        name="out_ffn",
    )(ya, yb, h, w_out[:a_width], w_out[a_width:], p["attn_post_g"].reshape(1, D),
      p["ffn_pre_g"].reshape(1, D), p["w_gate"].astype(bf16), p["w_up"].astype(bf16),
      p["w_down"].astype(bf16), p["ffn_post_g"].reshape(1, D))
    return out


def kernel(x, attn_pre_g, w_in, a_sink, lambda_q1, lambda_k1, lambda_q2, lambda_k2, diff_subln_g,
           rel_bias, w_out, attn_post_g, ffn_pre_g, w_gate, w_up, w_down, ffn_post_g):
    batch, S, D = x.shape
    depth = w_in.shape[0]
    assert S % B_TILE == 0 and S % A_BLOCK == 0 and S // A_BLOCK >= 2

    tab_a = _toeplitz_bias(rel_bias[:, :A_Q_HEADS], A_BLOCK, 3 * A_BLOCK, lambda t: -t - A_BLOCK)
    qi = np.arange(A_BLOCK)[:, None]
    kj = np.arange(3 * A_BLOCK)[None, :]
    in_window = np.abs(kj - A_BLOCK - qi) <= A_BLOCK
    valid = np.stack([in_window & (kj >= A_BLOCK), in_window, in_window & (kj < 2 * A_BLOCK)])
    bias_a = jnp.where(valid[:, None], tab_a[None] * LOG2E, MASK_VALUE)

    nb = B_BIAS_BLOCK
    assert len(set(_t5_bucket_np(np.arange(nb + 1, 4 * nb)))) == 1
    tab_b = rel_bias[:, A_Q_HEADS:]
    near = [_toeplitz_bias(tab_b, nb, nb, lambda t, d=d: d * nb + t) for d in (-1, 0, 1)]
    far = tab_b[_t5_bucket_np(np.array([-2 * nb, 2 * nb]))].astype(jnp.float32)
    const = [jnp.broadcast_to(far[side][:, None, None], (B_HEADS, nb, nb)) for side in (0, 1)]
    bias_b = jnp.stack([const[0]] + near + [const[1]], axis=1) * LOG2E

    outs = []
    for b in range(batch):
        h = x[b]
        for l in range(depth):
            p = dict(attn_pre_g=attn_pre_g[l], w_in=w_in[l], a_sink=a_sink[l], lambda_q1=lambda_q1[l],
                     lambda_k1=lambda_k1[l], lambda_q2=lambda_q2[l], lambda_k2=lambda_k2[l],
                     diff_subln_g=diff_subln_g[l], w_out=w_out[l], attn_post_g=attn_post_g[l],
                     ffn_pre_g=ffn_pre_g[l], w_gate=w_gate[l], w_up=w_up[l], w_down=w_down[l],
                     ffn_post_g=ffn_post_g[l])
            h = _layer(h, l, p, bias_a, bias_b)
        outs.append(h)
    return jnp.stack(outs)
```

```python
import functools
import math

import numpy as np
import jax
import jax.numpy as jnp
from jax import lax
from jax.experimental import pallas as pl
from jax.experimental.pallas import tpu as pltpu

HEAD_DIM = 64
A_Q_HEADS = 8
A_KV_HEADS = 2
A_BLOCK = 128
A_STEP_BLOCKS = 8
B_HEADS = 4
B_V_DIM = 2 * HEAD_DIM
NUM_BUCKETS = 32
MAX_DISTANCE = 128
EPS = 1e-6
MASK_VALUE = -1e30
LOG2E = math.log2(math.e)
ROW_SUM_LIMIT = 2.0 ** 40

A_WIDTH = A_Q_HEADS * HEAD_DIM
A_KV_WIDTH = 2 * A_KV_HEADS * HEAD_DIM
B_WIDTH = B_HEADS * B_V_DIM
COL_KA = A_WIDTH
COL_VA = COL_KA + A_KV_WIDTH
COL_QB = COL_VA + A_KV_WIDTH
COL_KB = COL_QB + B_WIDTH
COL_VB = COL_KB + B_WIDTH
PROJ_COLS = COL_VB + B_WIDTH

LANES = 128
ROW_TILE = 1024
FFN_ROW_TILE = 512
B_TILE = 512
B_QTILE = 4096
B_REF_LANES = 3
B_UNROLL = 4
B_AHEAD = 1
B_SLOTS = 2
B_PROBE = 128
B_BIAS_BLOCK = 128
VMEM_LIMIT = 56 * 1024 * 1024

_NT = (((1,), (1,)), ((), ()))


def _t5_bucket_np(rel):
    nb = NUM_BUCKETS // 2
    max_exact = nb // 2
    ret = np.where(rel > 0, nb, 0)
    n = np.abs(rel)
    nf = np.maximum(n, 1).astype(np.float32)
    large = max_exact + (np.log(nf / np.float32(max_exact)) / np.float32(math.log(MAX_DISTANCE / max_exact))
                         * np.float32(nb - max_exact)).astype(np.int32)
    large = np.minimum(large, nb - 1)
    return (ret + np.where(n < max_exact, n, large)).astype(np.int32)


def _toeplitz_bias(table, rows, cols, rel_of):
    length = rows + cols
    u = np.arange(length)
    t = np.where(u < cols, -u, length - u)
    w = table[_t5_bucket_np(rel_of(t))].astype(jnp.float32).T
    x = jnp.tile(w, (1, rows))[:, :rows * (length - 1)].reshape(w.shape[0], rows, length - 1)
    return x[:, :, :cols]


def _rms(xf, g):
    return xf * lax.rsqrt(jnp.mean(xf * xf, axis=-1, keepdims=True) + EPS) * g


def _in_proj_kernel(x_ref, g_ref, w_ref, kones_ref, qa_ref, ka_ref, va_ref, qb_ref, kb_ref, vt_ref):
    u = _rms(x_ref[...], g_ref[...]).astype(jnp.bfloat16)
    proj = jnp.dot(u, w_ref[...], preferred_element_type=jnp.float32)
    qa_ref[...] = proj[:, 0:COL_KA].astype(jnp.bfloat16)
    ka_ref[...] = proj[:, COL_KA:COL_VA].astype(jnp.bfloat16)
    va_ref[...] = proj[:, COL_VA:COL_QB].astype(jnp.bfloat16)
    qb_ref[...] = proj[:, COL_QB:COL_KB].astype(jnp.bfloat16)
    lane = lax.broadcasted_iota(jnp.int32, (proj.shape[0], LANES), 1)
    keep = (lane < HEAD_DIM, lane >= HEAD_DIM)
    for h in range(B_HEADS):
        kpair = proj[:, COL_KB + h * LANES:COL_KB + (h + 1) * LANES]
        for c in range(2):
            col = (2 * h + c) * LANES
            kb_ref[:, col:col + LANES] = jnp.where(keep[c], kpair, kones_ref[:, col:col + LANES]).astype(jnp.bfloat16)
        vt = proj[:, COL_VB + h * B_V_DIM:COL_VB + (h + 1) * B_V_DIM].T.astype(jnp.bfloat16)
        for t in range(vt_ref.shape[1]):
            vt_ref[h, t] = vt[:, t * B_TILE:(t + 1) * B_TILE]


def _win_attn_kernel(q_ref, k_ref, v_ref, bias_ref, o_ref, *, nblocks):
    f32 = jnp.float32
    bf16 = jnp.bfloat16
    n = pl.program_id(0)
    rows = A_STEP_BLOCKS * A_BLOCK
    seq = k_ref.shape[0]
    group = A_Q_HEADS // A_KV_HEADS
    start = pl.multiple_of(n * rows, rows)
    prev = pl.multiple_of(jnp.maximum(start - A_BLOCK, 0), A_BLOCK)
    nxt = pl.multiple_of(jnp.minimum(start + rows, seq - A_BLOCK), A_BLOCK)
    kw = jnp.concatenate([k_ref[pl.ds(prev, A_BLOCK), :], k_ref[pl.ds(start, rows), :],
                          k_ref[pl.ds(nxt, A_BLOCK), :]], axis=0)
    vw = jnp.concatenate([v_ref[pl.ds(prev, A_BLOCK), :], v_ref[pl.ds(start, rows), :],
                          v_ref[pl.ds(nxt, A_BLOCK), :]], axis=0)
    lane = lax.broadcasted_iota(jnp.int32, (A_BLOCK, LANES), 1)
    low = lane < HEAD_DIM
    ones = jnp.ones((3 * A_BLOCK, LANES), bf16)

    def attend(use_row_max):
        over = jnp.zeros((group * A_BLOCK, LANES), f32)
        for b in range(A_STEP_BLOCKS):
            blk = n * A_STEP_BLOCKS + b
            if b == 0:
                variant = jnp.where(blk == 0, 0, 1)
            elif b == A_STEP_BLOCKS - 1:
                variant = jnp.where(blk == nblocks - 1, 2, 1)
            else:
                variant = 1
            q = q_ref[b * A_BLOCK:(b + 1) * A_BLOCK, :]
            for g in range(A_KV_HEADS):
                heads = range(g * group, (g + 1) * group)
                qs = []
                for h in heads:
                    q2 = q[:, (h // 2) * LANES:(h // 2 + 1) * LANES]
                    qs.append(jnp.where(low if h % 2 == 0 else jnp.logical_not(low), q2, jnp.zeros_like(q2)))
                q4 = jnp.concatenate(qs, axis=0)
                kg = kw[b * A_BLOCK:(b + 3) * A_BLOCK, g * LANES:(g + 1) * LANES]
                vg = jnp.concatenate([vw[b * A_BLOCK:(b + 3) * A_BLOCK, g * LANES:(g + 1) * LANES], ones],
                                     axis=1)
                s = lax.dot_general(q4, kg, _NT, preferred_element_type=f32)
                ps, sink_terms = [], []
                for t, h in enumerate(heads):
                    sh = s[t * A_BLOCK:(t + 1) * A_BLOCK] + bias_ref[variant, h]
                    if use_row_max:
                        m = jnp.maximum(jnp.max(sh, axis=-1, keepdims=True), 0.0)
                        ps.append(jnp.exp2(sh - m).astype(bf16))
                        sink_terms.append(jnp.exp2(-m))
                    else:
                        ps.append(jnp.exp2(sh).astype(bf16))
                p = jnp.concatenate(ps, axis=0)
                res = jnp.dot(p, vg, preferred_element_type=f32)
                row_sum = res[:, LANES:2 * LANES]
                sink_term = jnp.concatenate(sink_terms, axis=0) if use_row_max else 1.0
                on = res[:, 0:LANES] / (row_sum + sink_term)
                over = jnp.maximum(over, jnp.where(row_sum <= ROW_SUM_LIMIT, 0.0, 1.0))
                for t in range(group // 2):
                    even = on[(2 * t) * A_BLOCK:(2 * t + 1) * A_BLOCK]
                    odd = on[(2 * t + 1) * A_BLOCK:(2 * t + 2) * A_BLOCK]
                    pair = (g * group) // 2 + t
                    o_ref[b * A_BLOCK:(b + 1) * A_BLOCK, pair * LANES:(pair + 1) * LANES] = (
                        jnp.where(low, even, odd).astype(o_ref.dtype))
        return over

    over = attend(use_row_max=False)

    @pl.when(jnp.max(over) > 0.0)
    def _():
        attend(use_row_max=True)


def _diff_attn_kernel(lamv_ref, g_ref, q_ref, k_ref, vt_ref, bias_ref, o_ref,
                      qz_ref, qp_ref, p_ref, psum_ref, r_ref, flag_ref, any_ref, acc_ref, den_ref,
                      *, nchunks, lambda_init):
    TQ = B_QTILE
    TK = B_TILE
    i = pl.program_id(1)
    f32 = jnp.float32
    bf16 = jnp.bfloat16

    qt = q_ref[...].astype(f32).T
    row = lax.broadcasted_iota(jnp.int32, (LANES, TQ), 0)
    is_q = (row < HEAD_DIM, row >= HEAD_DIM)
    ref_row = (HEAD_DIM, 0)

    ksub = TK // B_BIAS_BLOCK
    qsub = TQ // B_BIAS_BLOCK

    def bias_tile(j, rows=ksub):
        base = j * ksub - i * qsub
        by_dist = {d: bias_ref[0, jnp.clip(base + d, -2, 2) + 2] for d in range(1 - qsub, rows)}
        return jnp.concatenate(
            [jnp.concatenate([by_dist[a - b] for b in range(qsub)], axis=1) for a in range(rows)], axis=0)

    def k_chunk(j, c):
        return k_ref[pl.ds(pl.multiple_of(j * TK, TK), TK), c * LANES:(c + 1) * LANES]

    def set_reference(c, r_row):
        hi = r_row.astype(bf16).astype(f32)
        rem = r_row - hi
        mid = rem.astype(bf16).astype(f32)
        low = (rem - mid).astype(bf16).astype(f32)
        first = ref_row[c]
        ext = jnp.where(row == first, -hi, jnp.where(row == first + 1, -mid,
                                                     jnp.where(row == first + 2, -low, 0.0)))
        qp_ref[c] = jnp.where(is_q[c], qt, ext).astype(bf16)

    def exact_chunk(j, c, flagged):
        s = jnp.dot(k_chunk(j, c), qz_ref[c], preferred_element_type=f32) + bias_tile(j)
        r_old = r_ref[c]
        r_new = jnp.where(flagged, jnp.maximum(r_old, jnp.max(s, axis=0, keepdims=True)), r_old)
        p = jnp.where(flagged, jnp.exp2(s - r_new), 0.0)
        pv = jnp.dot(vt_ref[0, j], p.astype(bf16), preferred_element_type=f32)
        alpha = jnp.exp2(r_old - r_new)
        acc_ref[c] = acc_ref[c] * alpha + pv
        den_ref[c] = den_ref[c] * alpha + jnp.sum(p, axis=0, keepdims=True)
        r_ref[c] = r_new

    def stage_a(j, slot):
        bias = bias_tile(j)
        for c in range(2):
            s = jnp.dot(k_chunk(j, c), qp_ref[c], preferred_element_type=f32) + bias
            p = jnp.exp2(s)
            psum_ref[slot, c] = jnp.sum(p, axis=0, keepdims=True)
            p_ref[slot, c] = p.astype(bf16)

    def stage_c(j, slot):
        vt = vt_ref[0, j]
        for c in range(2):
            pv = jnp.dot(vt, p_ref[slot, c], preferred_element_type=f32)
            psum = psum_ref[slot, c]
            ok = psum <= ROW_SUM_LIMIT
            acc_ref[c] = acc_ref[c] + jnp.where(ok, pv, 0.0)
            den_ref[c] = den_ref[c] + jnp.where(ok, psum, 0.0)
            flag = jnp.where(ok, 0.0, 1.0)
            flag_ref[c, pl.ds(j, 1), :] = flag
            any_ref[c] = jnp.maximum(any_ref[c], flag)

    for c in range(2):
        qz_ref[c] = jnp.where(is_q[c], qt, 0.0).astype(bf16)
    acc_ref[...] = jnp.zeros(acc_ref.shape, f32)
    den_ref[...] = jnp.zeros(den_ref.shape, f32)
    any_ref[...] = jnp.zeros(any_ref.shape, f32)
    diag = i * (TQ // TK)
    probe_bias = bias_tile(diag, rows=B_PROBE // B_BIAS_BLOCK)
    for c in range(2):
        kp = k_ref[pl.ds(pl.multiple_of(diag * TK, TK), B_PROBE), c * LANES:(c + 1) * LANES]
        s = jnp.dot(kp, qz_ref[c], preferred_element_type=f32) + probe_bias
        r = jnp.max(s, axis=0, keepdims=True)
        r_ref[c] = r
        set_reference(c, r)

    def run(j0, has_next):
        for u in range(B_UNROLL):
            stage_c(j0 + u, u % B_SLOTS)
            if u + B_AHEAD < B_UNROLL or has_next:
                stage_a(j0 + u + B_AHEAD, (u + B_AHEAD) % B_SLOTS)

    nbody = nchunks // B_UNROLL
    for u in range(B_AHEAD):
        stage_a(u, u)

    def body(b, carry):
        run(b * B_UNROLL, True)
        return carry

    lax.fori_loop(0, nbody - 1, body, 0)
    run((nbody - 1) * B_UNROLL, False)

    @pl.when(jnp.max(jnp.maximum(any_ref[0], any_ref[1])) > 0.0)
    def _():
        def redo(j, carry):
            @pl.when(jnp.max(jnp.maximum(flag_ref[0, pl.ds(j, 1), :], flag_ref[1, pl.ds(j, 1), :])) > 0.0)
            def _():
                for c in range(2):
                    exact_chunk(j, c, flag_ref[c, pl.ds(j, 1), :] > 0.0)
            return carry

        lax.fori_loop(0, nchunks, redo, 0)

    lv = lamv_ref[...]
    lam = (jnp.exp(jnp.sum(lv[0:1] * lv[1:2], axis=-1, keepdims=True))
           - jnp.exp(jnp.sum(lv[2:3] * lv[3:4], axis=-1, keepdims=True)) + lambda_init)
    ot = acc_ref[0] / den_ref[0] - lam * (acc_ref[1] / den_ref[1])
    o = ot.T
    o_ref[...] = (_rms(o, g_ref[...]) * (1.0 - lambda_init)).astype(o_ref.dtype)


def _out_ffn_kernel(ya_ref, yb_ref, x_ref, woa_ref, wob_ref, gpost_ref, gpre_ref,
                    wg_ref, wu_ref, wd_ref, gfpost_ref, o_ref):
    y = (jnp.dot(ya_ref[...], woa_ref[...], preferred_element_type=jnp.float32)
         + jnp.dot(yb_ref[...], wob_ref[...], preferred_element_type=jnp.float32))
    h1 = x_ref[...] + _rms(y, gpost_ref[...])
    u = _rms(h1, gpre_ref[...]).astype(jnp.bfloat16)
    gate = jnp.dot(u, wg_ref[...], preferred_element_type=jnp.float32)
    up = jnp.dot(u, wu_ref[...], preferred_element_type=jnp.float32)
    act = (gate * jax.nn.sigmoid(gate) * up).astype(jnp.bfloat16)
    f = jnp.dot(act, wd_ref[...], preferred_element_type=jnp.float32)
    o_ref[...] = h1 + _rms(f, gfpost_ref[...])


def _resident(shape):
    zeros = (0,) * len(shape)
    return pl.BlockSpec(shape, lambda *_: zeros, pipeline_mode=pl.Buffered(1))


def _layer(h, l, p, bias_a, bias_b):
    S, D = h.shape
    bf16 = jnp.bfloat16
    lambda_init = 0.8 - 0.6 * math.exp(-0.3 * l)
    scale = HEAD_DIM ** -0.5

    w = p["w_in"]
    a_q, a_kv, b_qk = A_Q_HEADS * HEAD_DIM, A_KV_HEADS * HEAD_DIM, B_HEADS * 2 * HEAD_DIM
    c0 = a_q
    c1 = c0 + a_kv
    c2 = c1 + a_kv
    c3 = c2 + b_qk

    def dup(cols):
        parts = []
        for g in range(A_KV_HEADS):
            blk = cols[:, g * HEAD_DIM:(g + 1) * HEAD_DIM]
            parts += [blk, blk]
        return jnp.concatenate(parts, axis=1)

    kones = np.zeros((B_HEADS, 2, LANES), np.float32)
    kones[:, 0, HEAD_DIM:HEAD_DIM + B_REF_LANES] = 1.0
    kones[:, 1, 0:B_REF_LANES] = 1.0
    kones = jnp.asarray(kones.reshape(1, B_HEADS * 2 * LANES))

    w_cat = jnp.concatenate([w[:, :c0] * (scale * LOG2E), dup(w[:, c0:c1]), dup(w[:, c1:c2]),
                             w[:, c2:c3] * (scale * LOG2E), w[:, c3:]],
                            axis=1).astype(bf16)
    ncols = w_cat.shape[1]
    assert ncols == PROJ_COLS and w.shape[1] == c3 + 2 * B_WIDTH
    nrow = S // ROW_TILE
    nchunks = S // B_TILE
    assert ROW_TILE % B_TILE == 0 and S % ROW_TILE == 0 and nchunks % B_UNROLL == 0
    assert B_UNROLL % B_SLOTS == 0 and B_AHEAD < B_SLOTS and B_AHEAD <= B_UNROLL
    kb_cols = B_HEADS * 2 * LANES

    qa, ka, va, qb, kb, vt = pl.pallas_call(
        _in_proj_kernel,
        grid=(nrow,),
        in_specs=[pl.BlockSpec((ROW_TILE, D), lambda i: (i, 0)),
                  _resident((1, D)),
                  _resident((D, ncols)),
                  _resident((1, kb_cols))],
        out_specs=[pl.BlockSpec((ROW_TILE, A_WIDTH), lambda i: (i, 0)),
                   pl.BlockSpec((ROW_TILE, A_KV_WIDTH), lambda i: (i, 0)),
                   pl.BlockSpec((ROW_TILE, A_KV_WIDTH), lambda i: (i, 0)),
                   pl.BlockSpec((ROW_TILE, B_WIDTH), lambda i: (i, 0)),
                   pl.BlockSpec((ROW_TILE, kb_cols), lambda i: (i, 0)),
                   pl.BlockSpec((B_HEADS, ROW_TILE // B_TILE, B_V_DIM, B_TILE), lambda i: (0, i, 0, 0))],
        out_shape=[jax.ShapeDtypeStruct((S, A_WIDTH), bf16),
                   jax.ShapeDtypeStruct((S, A_KV_WIDTH), bf16),
                   jax.ShapeDtypeStruct((S, A_KV_WIDTH), bf16),
                   jax.ShapeDtypeStruct((S, B_WIDTH), bf16),
                   jax.ShapeDtypeStruct((S, kb_cols), bf16),
                   jax.ShapeDtypeStruct((B_HEADS, nchunks, B_V_DIM, B_TILE), bf16)],
        compiler_params=pltpu.CompilerParams(dimension_semantics=("arbitrary",),
                                             vmem_limit_bytes=VMEM_LIMIT),
        name="in_proj",
    )(h, p["attn_pre_g"].reshape(1, D), w_cat, kones)

    nblocks = S // A_BLOCK
    a_rows = A_STEP_BLOCKS * A_BLOCK
    assert nblocks % A_STEP_BLOCKS == 0 and A_STEP_BLOCKS >= 2
    bias_a = bias_a - (p["a_sink"].astype(jnp.float32) * LOG2E)[None, :, None, None]
    ya = pl.pallas_call(
        functools.partial(_win_attn_kernel, nblocks=nblocks),
        grid=(nblocks // A_STEP_BLOCKS,),
        in_specs=[pl.BlockSpec((a_rows, A_WIDTH), lambda n: (n, 0)),
                  _resident(ka.shape),
                  _resident(va.shape),
                  _resident(bias_a.shape)],
        out_specs=pl.BlockSpec((a_rows, A_WIDTH), lambda n: (n, 0)),
        out_shape=jax.ShapeDtypeStruct((S, A_WIDTH), bf16),
        compiler_params=pltpu.CompilerParams(dimension_semantics=("arbitrary",),
                                             vmem_limit_bytes=VMEM_LIMIT),
        name="win_attn",
    )(qa, ka, va, bias_a)

    lamv = jnp.stack([p["lambda_q1"], p["lambda_k1"], p["lambda_q2"], p["lambda_k2"]])
    TQ, TK = B_QTILE, B_TILE
    assert S % TQ == 0 and TQ % TK == 0
    yb = pl.pallas_call(
        functools.partial(_diff_attn_kernel, nchunks=nchunks, lambda_init=lambda_init),
        grid=(B_HEADS, S // TQ),
        in_specs=[pl.BlockSpec((4, HEAD_DIM), lambda hh, i: (0, 0)),
                  pl.BlockSpec((1, B_V_DIM), lambda hh, i: (0, 0)),
                  pl.BlockSpec((TQ, LANES), lambda hh, i: (i, hh)),
                  pl.BlockSpec((S, 2 * LANES), lambda hh, i: (0, hh), pipeline_mode=pl.Buffered(1)),
                  pl.BlockSpec((1, nchunks, B_V_DIM, TK), lambda hh, i: (hh, 0, 0, 0),
                               pipeline_mode=pl.Buffered(1)),
                  pl.BlockSpec((1, 5, B_BIAS_BLOCK, B_BIAS_BLOCK), lambda hh, i: (hh, 0, 0, 0))],
        out_specs=pl.BlockSpec((TQ, LANES), lambda hh, i: (i, hh)),
        out_shape=jax.ShapeDtypeStruct((S, B_HEADS * B_V_DIM), bf16),
        scratch_shapes=[pltpu.VMEM((2, LANES, TQ), bf16),
                        pltpu.VMEM((2, LANES, TQ), bf16),
                        pltpu.VMEM((B_SLOTS, 2, TK, TQ), bf16),
                        pltpu.VMEM((B_SLOTS, 2, 1, TQ), jnp.float32),
                        pltpu.VMEM((2, 1, TQ), jnp.float32),
                        pltpu.VMEM((2, nchunks, TQ), jnp.float32),
                        pltpu.VMEM((2, 1, TQ), jnp.float32),
                        pltpu.VMEM((2, B_V_DIM, TQ), jnp.float32),
                        pltpu.VMEM((2, 1, TQ), jnp.float32)],
        compiler_params=pltpu.CompilerParams(dimension_semantics=("arbitrary", "arbitrary"),
                                             vmem_limit_bytes=VMEM_LIMIT),
        name="diff_attn",
    )(lamv, p["diff_subln_g"].reshape(1, B_V_DIM), qb, kb, vt, bias_b)

    w_out = p["w_out"].astype(bf16)
    a_width = A_WIDTH
    assert w_out.shape[0] == A_WIDTH + B_WIDTH
    d_ff = p["w_gate"].shape[1]
    R = FFN_ROW_TILE
    out = pl.pallas_call(
        _out_ffn_kernel,
        grid=(S // R,),
        in_specs=[pl.BlockSpec((R, A_WIDTH), lambda i: (i, 0)),
                  pl.BlockSpec((R, B_WIDTH), lambda i: (i, 0)),
                  pl.BlockSpec((R, D), lambda i: (i, 0)),
                  _resident((a_width, D)),
                  _resident((w_out.shape[0] - a_width, D)),
                  _resident((1, D)),
                  _resident((1, D)),
                  _resident((D, d_ff)),
                  _resident((D, d_ff)),
                  _resident((d_ff, D)),
                  _resident((1, D))],
        out_specs=pl.BlockSpec((R, D), lambda i: (i, 0)),
        out_shape=jax.ShapeDtypeStruct((S, D), jnp.float32),
        compiler_params=pltpu.CompilerParams(dimension_semantics=("arbitrary",),
                                             vmem_limit_bytes=VMEM_LIMIT),
        name="out_ffn",
    )(ya, yb, h, w_out[:a_width], w_out[a_width:], p["attn_post_g"].reshape(1, D),
      p["ffn_pre_g"].reshape(1, D), p["w_gate"].astype(bf16), p["w_up"].astype(bf16),
      p["w_down"].astype(bf16), p["ffn_post_g"].reshape(1, D))
    return out


def kernel(x, attn_pre_g, w_in, a_sink, lambda_q1, lambda_k1, lambda_q2, lambda_k2, diff_subln_g,
           rel_bias, w_out, attn_post_g, ffn_pre_g, w_gate, w_up, w_down, ffn_post_g):
    batch, S, D = x.shape
    depth = w_in.shape[0]
    assert S % B_TILE == 0 and S % A_BLOCK == 0 and S // A_BLOCK >= 2

    tab_a = _toeplitz_bias(rel_bias[:, :A_Q_HEADS], A_BLOCK, 3 * A_BLOCK, lambda t: -t - A_BLOCK)
    qi = np.arange(A_BLOCK)[:, None]
    kj = np.arange(3 * A_BLOCK)[None, :]
    in_window = np.abs(kj - A_BLOCK - qi) <= A_BLOCK
    valid = np.stack([in_window & (kj >= A_BLOCK), in_window, in_window & (kj < 2 * A_BLOCK)])
    bias_a = jnp.where(valid[:, None], tab_a[None] * LOG2E, MASK_VALUE)

    nb = B_BIAS_BLOCK
    assert len(set(_t5_bucket_np(np.arange(nb + 1, 4 * nb)))) == 1
    tab_b = rel_bias[:, A_Q_HEADS:]
    near = [_toeplitz_bias(tab_b, nb, nb, lambda t, d=d: d * nb + t) for d in (-1, 0, 1)]
    far = tab_b[_t5_bucket_np(np.array([-2 * nb, 2 * nb]))].astype(jnp.float32)
    const = [jnp.broadcast_to(far[side][:, None, None], (B_HEADS, nb, nb)) for side in (0, 1)]
    bias_b = jnp.stack([const[0]] + near + [const[1]], axis=1) * LOG2E

    outs = []
    for b in range(batch):
        h = x[b]
        for l in range(depth):
            p = dict(attn_pre_g=attn_pre_g[l], w_in=w_in[l], a_sink=a_sink[l], lambda_q1=lambda_q1[l],
                     lambda_k1=lambda_k1[l], lambda_q2=lambda_q2[l], lambda_k2=lambda_k2[l],
                     diff_subln_g=diff_subln_g[l], w_out=w_out[l], attn_post_g=attn_post_g[l],
                     ffn_pre_g=ffn_pre_g[l], w_gate=w_gate[l], w_up=w_up[l], w_down=w_down[l],
                     ffn_post_g=ffn_post_g[l])
            h = _layer(h, l, p, bias_a, bias_b)
        outs.append(h)
    return jnp.stack(outs)
```

```python
import functools
import math

import numpy as np
import jax
import jax.numpy as jnp
from jax import lax
from jax.experimental import pallas as pl
from jax.experimental.pallas import tpu as pltpu

HEAD_DIM = 64
A_Q_HEADS = 8
A_KV_HEADS = 2
A_BLOCK = 128
A_STEP_BLOCKS = 8
B_HEADS = 4
B_V_DIM = 2 * HEAD_DIM
NUM_BUCKETS = 32
MAX_DISTANCE = 128
EPS = 1e-6
MASK_VALUE = -1e30
LOG2E = math.log2(math.e)
ROW_SUM_LIMIT = 2.0 ** 40

A_WIDTH = A_Q_HEADS * HEAD_DIM
A_KV_WIDTH = 2 * A_KV_HEADS * HEAD_DIM
B_WIDTH = B_HEADS * B_V_DIM
COL_KA = A_WIDTH
COL_VA = COL_KA + A_KV_WIDTH
COL_QB = COL_VA + A_KV_WIDTH
COL_KB = COL_QB + B_WIDTH
COL_VB = COL_KB + B_WIDTH
PROJ_COLS = COL_VB + B_WIDTH

LANES = 128
ROW_TILE = 1024
FFN_ROW_TILE = 512
B_TILE = 512
B_QTILE = 4096
B_REF_LANES = 3
B_UNROLL = 2
B_AHEAD = 1
B_SLOTS = 2
B_PROBE = 128
B_BIAS_BLOCK = 128
VMEM_LIMIT = 56 * 1024 * 1024

_NT = (((1,), (1,)), ((), ()))


def _t5_bucket_np(rel):
    nb = NUM_BUCKETS // 2
    max_exact = nb // 2
    ret = np.where(rel > 0, nb, 0)
    n = np.abs(rel)
    nf = np.maximum(n, 1).astype(np.float32)
    large = max_exact + (np.log(nf / np.float32(max_exact)) / np.float32(math.log(MAX_DISTANCE / max_exact))
                         * np.float32(nb - max_exact)).astype(np.int32)
    large = np.minimum(large, nb - 1)
    return (ret + np.where(n < max_exact, n, large)).astype(np.int32)


def _toeplitz_bias(table, rows, cols, rel_of):
    length = rows + cols
    u = np.arange(length)
    t = np.where(u < cols, -u, length - u)
    w = table[_t5_bucket_np(rel_of(t))].astype(jnp.float32).T
    x = jnp.tile(w, (1, rows))[:, :rows * (length - 1)].reshape(w.shape[0], rows, length - 1)
    return x[:, :, :cols]


def _rms(xf, g):
    return xf * lax.rsqrt(jnp.mean(xf * xf, axis=-1, keepdims=True) + EPS) * g


def _in_proj_kernel(x_ref, g_ref, w_ref, kones_ref, qa_ref, ka_ref, va_ref, qb_ref, kb_ref, vt_ref):
    u = _rms(x_ref[...], g_ref[...]).astype(jnp.bfloat16)
    proj = jnp.dot(u, w_ref[...], preferred_element_type=jnp.float32)
    qa_ref[...] = proj[:, 0:COL_KA].astype(jnp.bfloat16)
    ka_ref[...] = proj[:, COL_KA:COL_VA].astype(jnp.bfloat16)
    va_ref[...] = proj[:, COL_VA:COL_QB].astype(jnp.bfloat16)
    qb_ref[...] = proj[:, COL_QB:COL_KB].astype(jnp.bfloat16)
    lane = lax.broadcasted_iota(jnp.int32, (proj.shape[0], LANES), 1)
    keep = (lane < HEAD_DIM, lane >= HEAD_DIM)
    for h in range(B_HEADS):
        kpair = proj[:, COL_KB + h * LANES:COL_KB + (h + 1) * LANES]
        for c in range(2):
            col = (2 * h + c) * LANES
            kb_ref[:, col:col + LANES] = jnp.where(keep[c], kpair, kones_ref[:, col:col + LANES]).astype(jnp.bfloat16)
        vt = proj[:, COL_VB + h * B_V_DIM:COL_VB + (h + 1) * B_V_DIM].T.astype(jnp.bfloat16)
        for t in range(vt_ref.shape[1]):
            vt_ref[h, t] = vt[:, t * B_TILE:(t + 1) * B_TILE]


def _win_attn_kernel(sink_ref, q_ref, k_ref, v_ref, bias_ref, rel_bias_ref, o_ref, *, nblocks):
    f32 = jnp.float32
    bf16 = jnp.bfloat16
    n = pl.program_id(0)
    rows = A_STEP_BLOCKS * A_BLOCK
    seq = k_ref.shape[0]
    group = A_Q_HEADS // A_KV_HEADS
    start = pl.multiple_of(n * rows, rows)
    prev = pl.multiple_of(jnp.maximum(start - A_BLOCK, 0), A_BLOCK)
    nxt = pl.multiple_of(jnp.minimum(start + rows, seq - A_BLOCK), A_BLOCK)
    kw = jnp.concatenate([k_ref[pl.ds(prev, A_BLOCK), :], k_ref[pl.ds(start, rows), :],
                          k_ref[pl.ds(nxt, A_BLOCK), :]], axis=0)
    vw = jnp.concatenate([v_ref[pl.ds(prev, A_BLOCK), :], v_ref[pl.ds(start, rows), :],
                          v_ref[pl.ds(nxt, A_BLOCK), :]], axis=0)
    lane = lax.broadcasted_iota(jnp.int32, (A_BLOCK, LANES), 1)
    low = lane < HEAD_DIM
    ones = jnp.ones((3 * A_BLOCK, LANES), bf16)

    def attend(use_row_max):
        over = jnp.zeros((group * A_BLOCK, LANES), f32)
        for b in range(A_STEP_BLOCKS):
            blk = n * A_STEP_BLOCKS + b
            if b == 0:
                variant = jnp.where(blk == 0, 0, 1)
            elif b == A_STEP_BLOCKS - 1:
                variant = jnp.where(blk == nblocks - 1, 2, 1)
            else:
                variant = 1
            q = q_ref[b * A_BLOCK:(b + 1) * A_BLOCK, :]
            for g in range(A_KV_HEADS):
                heads = range(g * group, (g + 1) * group)
                qs = []
                for h in heads:
                    q2 = q[:, (h // 2) * LANES:(h // 2 + 1) * LANES]
                    qs.append(jnp.where(low if h % 2 == 0 else jnp.logical_not(low), q2, jnp.zeros_like(q2)))
                q4 = jnp.concatenate(qs, axis=0)
                kg = kw[b * A_BLOCK:(b + 3) * A_BLOCK, g * LANES:(g + 1) * LANES]
                vg = jnp.concatenate([vw[b * A_BLOCK:(b + 3) * A_BLOCK, g * LANES:(g + 1) * LANES], ones],
                                     axis=1)
                s = lax.dot_general(q4, kg, _NT, preferred_element_type=f32)
                ps, sink_terms = [], []
                for t, h in enumerate(heads):
                    sh = s[t * A_BLOCK:(t + 1) * A_BLOCK]
                    if use_row_max:
                        sh = sh + bias_ref[variant, h]
                        snk = sink_ref[h] * LOG2E
                        m = jnp.maximum(jnp.max(sh, axis=-1, keepdims=True), snk)
                        ps.append(jnp.exp2(sh - m).astype(bf16))
                        sink_terms.append(jnp.exp2(snk - m))
                    else:
                        ps.append(jnp.exp2(sh + rel_bias_ref[variant, h]).astype(bf16))
                p = jnp.concatenate(ps, axis=0)
                res = jnp.dot(p, vg, preferred_element_type=f32)
                row_sum = res[:, LANES:2 * LANES]
                sink_term = jnp.concatenate(sink_terms, axis=0) if use_row_max else 1.0
                on = res[:, 0:LANES] / (row_sum + sink_term)
                over = jnp.maximum(over, jnp.where(row_sum <= ROW_SUM_LIMIT, 0.0, 1.0))
                for t in range(group // 2):
                    even = on[(2 * t) * A_BLOCK:(2 * t + 1) * A_BLOCK]
                    odd = on[(2 * t + 1) * A_BLOCK:(2 * t + 2) * A_BLOCK]
                    pair = (g * group) // 2 + t
                    o_ref[b * A_BLOCK:(b + 1) * A_BLOCK, pair * LANES:(pair + 1) * LANES] = (
                        jnp.where(low, even, odd).astype(o_ref.dtype))
        return over

    over = attend(use_row_max=False)

    @pl.when(jnp.max(over) > 0.0)
    def _():
        attend(use_row_max=True)


def _diff_attn_kernel(lamv_ref, g_ref, q_ref, k_ref, vt_ref, bias_ref, o_ref,
                      qz_ref, qp_ref, p_ref, psum_ref, r_ref, flag_ref, any_ref, acc_ref, den_ref,
                      *, nchunks, lambda_init):
    TQ = B_QTILE
    TK = B_TILE
    i = pl.program_id(1)
    f32 = jnp.float32
    bf16 = jnp.bfloat16

    qt = q_ref[...].astype(f32).T
    row = lax.broadcasted_iota(jnp.int32, (LANES, TQ), 0)
    is_q = (row < HEAD_DIM, row >= HEAD_DIM)
    ref_row = (HEAD_DIM, 0)

    ksub = TK // B_BIAS_BLOCK
    qsub = TQ // B_BIAS_BLOCK

    def bias_tile(j, rows=ksub):
        base = j * ksub - i * qsub
        by_dist = {d: bias_ref[0, jnp.clip(base + d, -2, 2) + 2] for d in range(1 - qsub, rows)}
        return jnp.concatenate(
            [jnp.concatenate([by_dist[a - b] for b in range(qsub)], axis=1) for a in range(rows)], axis=0)

    def k_chunk(j, c):
        return k_ref[pl.ds(pl.multiple_of(j * TK, TK), TK), c * LANES:(c + 1) * LANES]

    def set_reference(c, r_row):
        hi = r_row.astype(bf16).astype(f32)
        rem = r_row - hi
        mid = rem.astype(bf16).astype(f32)
        low = (rem - mid).astype(bf16).astype(f32)
        first = ref_row[c]
        ext = jnp.where(row == first, -hi, jnp.where(row == first + 1, -mid,
                                                     jnp.where(row == first + 2, -low, 0.0)))
        qp_ref[c] = jnp.where(is_q[c], qt, ext).astype(bf16)

    def exact_chunk(j, c, flagged):
        s = jnp.dot(k_chunk(j, c), qz_ref[c], preferred_element_type=f32) + bias_tile(j)
        r_old = r_ref[c]
        r_new = jnp.where(flagged, jnp.maximum(r_old, jnp.max(s, axis=0, keepdims=True)), r_old)
        p = jnp.where(flagged, jnp.exp2(s - r_new), 0.0)
        pv = jnp.dot(vt_ref[0, j], p.astype(bf16), preferred_element_type=f32)
        alpha = jnp.exp2(r_old - r_new)
        acc_ref[c] = acc_ref[c] * alpha + pv
        den_ref[c] = den_ref[c] * alpha + jnp.sum(p, axis=0, keepdims=True)
        r_ref[c] = r_new

    def stage_a(j, slot):
        bias = bias_tile(j)
        for c in range(2):
            s = jnp.dot(k_chunk(j, c), qp_ref[c], preferred_element_type=f32) + bias
            p = jnp.exp2(s)
            psum_ref[slot, c] = jnp.sum(p, axis=0, keepdims=True)
            p_ref[slot, c] = p.astype(bf16)

    def stage_c(j, slot):
        vt = vt_ref[0, j]
        for c in range(2):
            pv = jnp.dot(vt, p_ref[slot, c], preferred_element_type=f32)
            psum = psum_ref[slot, c]
            ok = psum <= ROW_SUM_LIMIT
            acc_ref[c] = acc_ref[c] + jnp.where(ok, pv, 0.0)
            den_ref[c] = den_ref[c] + jnp.where(ok, psum, 0.0)
            flag = jnp.where(ok, 0.0, 1.0)
            flag_ref[c, pl.ds(j, 1), :] = flag
            any_ref[c] = jnp.maximum(any_ref[c], flag)

    for c in range(2):
        qz_ref[c] = jnp.where(is_q[c], qt, 0.0).astype(bf16)
    acc_ref[...] = jnp.zeros(acc_ref.shape, f32)
    den_ref[...] = jnp.zeros(den_ref.shape, f32)
    any_ref[...] = jnp.zeros(any_ref.shape, f32)
    diag = i * (TQ // TK)
    probe_bias = bias_tile(diag, rows=B_PROBE // B_BIAS_BLOCK)
    for c in range(2):
        kp = k_ref[pl.ds(pl.multiple_of(diag * TK, TK), B_PROBE), c * LANES:(c + 1) * LANES]
        s = jnp.dot(kp, qz_ref[c], preferred_element_type=f32) + probe_bias
        r = jnp.max(s, axis=0, keepdims=True)
        r_ref[c] = r
        set_reference(c, r)

    def run(j0, has_next):
        for u in range(B_UNROLL):
            stage_c(j0 + u, u % B_SLOTS)
            if u + B_AHEAD < B_UNROLL or has_next:
                stage_a(j0 + u + B_AHEAD, (u + B_AHEAD) % B_SLOTS)

    nbody = nchunks // B_UNROLL
    for u in range(B_AHEAD):
        stage_a(u, u)

    def body(b, carry):
        run(b * B_UNROLL, True)
        return carry

    lax.fori_loop(0, nbody - 1, body, 0)
    run((nbody - 1) * B_UNROLL, False)

    @pl.when(jnp.max(jnp.maximum(any_ref[0], any_ref[1])) > 0.0)
    def _():
        def redo(j, carry):
            @pl.when(jnp.max(jnp.maximum(flag_ref[0, pl.ds(j, 1), :], flag_ref[1, pl.ds(j, 1), :])) > 0.0)
            def _():
                for c in range(2):
                    exact_chunk(j, c, flag_ref[c, pl.ds(j, 1), :] > 0.0)
            return carry

        lax.fori_loop(0, nchunks, redo, 0)

    lv = lamv_ref[...]
    lam = (jnp.exp(jnp.sum(lv[0:1] * lv[1:2], axis=-1, keepdims=True))
           - jnp.exp(jnp.sum(lv[2:3] * lv[3:4], axis=-1, keepdims=True)) + lambda_init)
    ot = acc_ref[0] / den_ref[0] - lam * (acc_ref[1] / den_ref[1])
    o = ot.T
    o_ref[...] = (_rms(o, g_ref[...]) * (1.0 - lambda_init)).astype(o_ref.dtype)


def _out_ffn_kernel(ya_ref, yb_ref, x_ref, woa_ref, wob_ref, gpost_ref, gpre_ref,
                    wg_ref, wu_ref, wd_ref, gfpost_ref, o_ref):
    y = (jnp.dot(ya_ref[...], woa_ref[...], preferred_element_type=jnp.float32)
         + jnp.dot(yb_ref[...], wob_ref[...], preferred_element_type=jnp.float32))
    h1 = x_ref[...] + _rms(y, gpost_ref[...])
    u = _rms(h1, gpre_ref[...]).astype(jnp.bfloat16)
    gate = jnp.dot(u, wg_ref[...], preferred_element_type=jnp.float32)
    up = jnp.dot(u, wu_ref[...], preferred_element_type=jnp.float32)
    act = (gate * jax.nn.sigmoid(gate) * up).astype(jnp.bfloat16)
    f = jnp.dot(act, wd_ref[...], preferred_element_type=jnp.float32)
    o_ref[...] = h1 + _rms(f, gfpost_ref[...])


def _resident(shape):
    zeros = (0,) * len(shape)
    return pl.BlockSpec(shape, lambda *_: zeros, pipeline_mode=pl.Buffered(1))


def _layer(h, l, p, bias_a, bias_b):
    S, D = h.shape
    bf16 = jnp.bfloat16
    lambda_init = 0.8 - 0.6 * math.exp(-0.3 * l)
    scale = HEAD_DIM ** -0.5

    w = p["w_in"]
    a_q, a_kv, b_qk = A_Q_HEADS * HEAD_DIM, A_KV_HEADS * HEAD_DIM, B_HEADS * 2 * HEAD_DIM
    c0 = a_q
    c1 = c0 + a_kv
    c2 = c1 + a_kv
    c3 = c2 + b_qk

    def dup(cols):
        parts = []
        for g in range(A_KV_HEADS):
            blk = cols[:, g * HEAD_DIM:(g + 1) * HEAD_DIM]
            parts += [blk, blk]
        return jnp.concatenate(parts, axis=1)

    kones = np.zeros((B_HEADS, 2, LANES), np.float32)
    kones[:, 0, HEAD_DIM:HEAD_DIM + B_REF_LANES] = 1.0
    kones[:, 1, 0:B_REF_LANES] = 1.0
    kones = jnp.asarray(kones.reshape(1, B_HEADS * 2 * LANES))

    w_cat = jnp.concatenate([w[:, :c0] * (scale * LOG2E), dup(w[:, c0:c1]), dup(w[:, c1:c2]),
                             w[:, c2:c3] * (scale * LOG2E), w[:, c3:]],
                            axis=1).astype(bf16)
    ncols = w_cat.shape[1]
    assert ncols == PROJ_COLS and w.shape[1] == c3 + 2 * B_WIDTH
    nrow = S // ROW_TILE
    nchunks = S // B_TILE
    assert ROW_TILE % B_TILE == 0 and S % ROW_TILE == 0 and nchunks % B_UNROLL == 0
    assert B_UNROLL % B_SLOTS == 0 and B_AHEAD < B_SLOTS and B_AHEAD <= B_UNROLL
    kb_cols = B_HEADS * 2 * LANES

    qa, ka, va, qb, kb, vt = pl.pallas_call(
        _in_proj_kernel,
        grid=(nrow,),
        in_specs=[pl.BlockSpec((ROW_TILE, D), lambda i: (i, 0)),
                  _resident((1, D)),
                  _resident((D, ncols)),
                  _resident((1, kb_cols))],
        out_specs=[pl.BlockSpec((ROW_TILE, A_WIDTH), lambda i: (i, 0)),
                   pl.BlockSpec((ROW_TILE, A_KV_WIDTH), lambda i: (i, 0)),
                   pl.BlockSpec((ROW_TILE, A_KV_WIDTH), lambda i: (i, 0)),
                   pl.BlockSpec((ROW_TILE, B_WIDTH), lambda i: (i, 0)),
                   pl.BlockSpec((ROW_TILE, kb_cols), lambda i: (i, 0)),
                   pl.BlockSpec((B_HEADS, ROW_TILE // B_TILE, B_V_DIM, B_TILE), lambda i: (0, i, 0, 0))],
        out_shape=[jax.ShapeDtypeStruct((S, A_WIDTH), bf16),
                   jax.ShapeDtypeStruct((S, A_KV_WIDTH), bf16),
                   jax.ShapeDtypeStruct((S, A_KV_WIDTH), bf16),
                   jax.ShapeDtypeStruct((S, B_WIDTH), bf16),
                   jax.ShapeDtypeStruct((S, kb_cols), bf16),
                   jax.ShapeDtypeStruct((B_HEADS, nchunks, B_V_DIM, B_TILE), bf16)],
        compiler_params=pltpu.CompilerParams(dimension_semantics=("arbitrary",),
                                             vmem_limit_bytes=VMEM_LIMIT),
        name="in_proj",
    )(h, p["attn_pre_g"].reshape(1, D), w_cat, kones)

    nblocks = S // A_BLOCK
    a_rows = A_STEP_BLOCKS * A_BLOCK
    assert nblocks % A_STEP_BLOCKS == 0 and A_STEP_BLOCKS >= 2
    rel_bias_a = bias_a - (p["a_sink"].astype(jnp.float32) * LOG2E)[None, :, None, None]
    ya = pl.pallas_call(
        functools.partial(_win_attn_kernel, nblocks=nblocks),
        grid=(nblocks // A_STEP_BLOCKS,),
        in_specs=[pl.BlockSpec(memory_space=pltpu.SMEM),
                  pl.BlockSpec((a_rows, A_WIDTH), lambda n: (n, 0)),
                  _resident(ka.shape),
                  _resident(va.shape),
                  _resident(bias_a.shape),
                  _resident(bias_a.shape)],
        out_specs=pl.BlockSpec((a_rows, A_WIDTH), lambda n: (n, 0)),
        out_shape=jax.ShapeDtypeStruct((S, A_WIDTH), bf16),
        compiler_params=pltpu.CompilerParams(dimension_semantics=("arbitrary",),
                                             vmem_limit_bytes=VMEM_LIMIT),
        name="win_attn",
    )(p["a_sink"], qa, ka, va, bias_a, rel_bias_a)

    lamv = jnp.stack([p["lambda_q1"], p["lambda_k1"], p["lambda_q2"], p["lambda_k2"]])
    TQ, TK = B_QTILE, B_TILE
    assert S % TQ == 0 and TQ % TK == 0
    yb = pl.pallas_call(
        functools.partial(_diff_attn_kernel, nchunks=nchunks, lambda_init=lambda_init),
        grid=(B_HEADS, S // TQ),
        in_specs=[pl.BlockSpec((4, HEAD_DIM), lambda hh, i: (0, 0)),
                  pl.BlockSpec((1, B_V_DIM), lambda hh, i: (0, 0)),
                  pl.BlockSpec((TQ, LANES), lambda hh, i: (i, hh)),
                  pl.BlockSpec((S, 2 * LANES), lambda hh, i: (0, hh), pipeline_mode=pl.Buffered(1)),
                  pl.BlockSpec((1, nchunks, B_V_DIM, TK), lambda hh, i: (hh, 0, 0, 0),
                               pipeline_mode=pl.Buffered(1)),
                  pl.BlockSpec((1, 5, B_BIAS_BLOCK, B_BIAS_BLOCK), lambda hh, i: (hh, 0, 0, 0))],
        out_specs=pl.BlockSpec((TQ, LANES), lambda hh, i: (i, hh)),
        out_shape=jax.ShapeDtypeStruct((S, B_HEADS * B_V_DIM), bf16),
        scratch_shapes=[pltpu.VMEM((2, LANES, TQ), bf16),
                        pltpu.VMEM((2, LANES, TQ), bf16),
                        pltpu.VMEM((B_SLOTS, 2, TK, TQ), bf16),
                        pltpu.VMEM((B_SLOTS, 2, 1, TQ), jnp.float32),
                        pltpu.VMEM((2, 1, TQ), jnp.float32),
                        pltpu.VMEM((2, nchunks, TQ), jnp.float32),
                        pltpu.VMEM((2, 1, TQ), jnp.float32),
                        pltpu.VMEM((2, B_V_DIM, TQ), jnp.float32),
                        pltpu.VMEM((2, 1, TQ), jnp.float32)],
        compiler_params=pltpu.CompilerParams(dimension_semantics=("arbitrary", "arbitrary"),
                                             vmem_limit_bytes=VMEM_LIMIT),
        name="diff_attn",
    )(lamv, p["diff_subln_g"].reshape(1, B_V_DIM), qb, kb, vt, bias_b)

    w_out = p["w_out"].astype(bf16)
    a_width = A_WIDTH
    assert w_out.shape[0] == A_WIDTH + B_WIDTH
    d_ff = p["w_gate"].shape[1]
    R = FFN_ROW_TILE
    out = pl.pallas_call(
        _out_ffn_kernel,
        grid=(S // R,),
        in_specs=[pl.BlockSpec((R, A_WIDTH), lambda i: (i, 0)),
                  pl.BlockSpec((R, B_WIDTH), lambda i: (i, 0)),
                  pl.BlockSpec((R, D), lambda i: (i, 0)),
                  _resident((a_width, D)),
                  _resident((w_out.shape[0] - a_width, D)),
                  _resident((1, D)),
                  _resident((1, D)),
                  _resident((D, d_ff)),
                  _resident((D, d_ff)),
                  _resident((d_ff, D)),
                  _resident((1, D))],
        out_specs=pl.BlockSpec((R, D), lambda i: (i, 0)),
        out_shape=jax.ShapeDtypeStruct((S, D), jnp.float32),
        compiler_params=pltpu.CompilerParams(dimension_semantics=("arbitrary",),
                                             vmem_limit_bytes=VMEM_LIMIT),
        name="out_ffn",
    )(ya, yb, h, w_out[:a_width], w_out[a_width:], p["attn_post_g"].reshape(1, D),
      p["ffn_pre_g"].reshape(1, D), p["w_gate"].astype(bf16), p["w_up"].astype(bf16),
      p["w_down"].astype(bf16), p["ffn_post_g"].reshape(1, D))
    return out


def kernel(x, attn_pre_g, w_in, a_sink, lambda_q1, lambda_k1, lambda_q2, lambda_k2, diff_subln_g,
           rel_bias, w_out, attn_post_g, ffn_pre_g, w_gate, w_up, w_down, ffn_post_g):
    batch, S, D = x.shape
    depth = w_in.shape[0]
    assert S % B_TILE == 0 and S % A_BLOCK == 0 and S // A_BLOCK >= 2

    tab_a = _toeplitz_bias(rel_bias[:, :A_Q_HEADS], A_BLOCK, 3 * A_BLOCK, lambda t: -t - A_BLOCK)
    qi = np.arange(A_BLOCK)[:, None]
    kj = np.arange(3 * A_BLOCK)[None, :]
    in_window = np.abs(kj - A_BLOCK - qi) <= A_BLOCK
    valid = np.stack([in_window & (kj >= A_BLOCK), in_window, in_window & (kj < 2 * A_BLOCK)])
    bias_a = jnp.where(valid[:, None], tab_a[None] * LOG2E, MASK_VALUE)

    nb = B_BIAS_BLOCK
    assert len(set(_t5_bucket_np(np.arange(nb + 1, 4 * nb)))) == 1
    tab_b = rel_bias[:, A_Q_HEADS:]
    near = [_toeplitz_bias(tab_b, nb, nb, lambda t, d=d: d * nb + t) for d in (-1, 0, 1)]
    far = tab_b[_t5_bucket_np(np.array([-2 * nb, 2 * nb]))].astype(jnp.float32)
    const = [jnp.broadcast_to(far[side][:, None, None], (B_HEADS, nb, nb)) for side in (0, 1)]
    bias_b = jnp.stack([const[0]] + near + [const[1]], axis=1) * LOG2E

    outs = []
    for b in range(batch):
        h = x[b]
        for l in range(depth):
            p = dict(attn_pre_g=attn_pre_g[l], w_in=w_in[l], a_sink=a_sink[l], lambda_q1=lambda_q1[l],
                     lambda_k1=lambda_k1[l], lambda_q2=lambda_q2[l], lambda_k2=lambda_k2[l],
                     diff_subln_g=diff_subln_g[l], w_out=w_out[l], attn_post_g=attn_post_g[l],
                     ffn_pre_g=ffn_pre_g[l], w_gate=w_gate[l], w_up=w_up[l], w_down=w_down[l],
                     ffn_post_g=ffn_post_g[l])
            h = _layer(h, l, p, bias_a, bias_b)
        outs.append(h)
    return jnp.stack(outs)
```

```python
import functools
import math

import numpy as np
import jax
import jax.numpy as jnp
from jax import lax
from jax.experimental import pallas as pl
from jax.experimental.pallas import tpu as pltpu

HEAD_DIM = 64
A_Q_HEADS = 8
A_KV_HEADS = 2
A_BLOCK = 128
A_STEP_BLOCKS = 8
B_HEADS = 4
B_V_DIM = 2 * HEAD_DIM
NUM_BUCKETS = 32
MAX_DISTANCE = 128
EPS = 1e-6
MASK_VALUE = -1e30
LOG2E = math.log2(math.e)
ROW_SUM_LIMIT = 2.0 ** 40

A_WIDTH = A_Q_HEADS * HEAD_DIM
A_KV_WIDTH = 2 * A_KV_HEADS * HEAD_DIM
B_WIDTH = B_HEADS * B_V_DIM
COL_KA = A_WIDTH
COL_VA = COL_KA + A_KV_WIDTH
COL_QB = COL_VA + A_KV_WIDTH
COL_KB = COL_QB + B_WIDTH
COL_VB = COL_KB + B_WIDTH
PROJ_COLS = COL_VB + B_WIDTH

LANES = 128
ROW_TILE = 1024
FFN_ROW_TILE = 1024
FFN_CHUNK = 256
B_TILE = 512
B_QTILE = 4096
B_REF_LANES = 3
B_UNROLL = 2
B_AHEAD = 1
B_SLOTS = 2
B_PROBE = 128
B_BIAS_BLOCK = 128
VMEM_LIMIT = 56 * 1024 * 1024

_NT = (((1,), (1,)), ((), ()))


def _t5_bucket_np(rel):
    nb = NUM_BUCKETS // 2
    max_exact = nb // 2
    ret = np.where(rel > 0, nb, 0)
    n = np.abs(rel)
    nf = np.maximum(n, 1).astype(np.float32)
    large = max_exact + (np.log(nf / np.float32(max_exact)) / np.float32(math.log(MAX_DISTANCE / max_exact))
                         * np.float32(nb - max_exact)).astype(np.int32)
    large = np.minimum(large, nb - 1)
    return (ret + np.where(n < max_exact, n, large)).astype(np.int32)


def _toeplitz_bias(table, rows, cols, rel_of):
    length = rows + cols
    u = np.arange(length)
    t = np.where(u < cols, -u, length - u)
    w = table[_t5_bucket_np(rel_of(t))].astype(jnp.float32).T
    x = jnp.tile(w, (1, rows))[:, :rows * (length - 1)].reshape(w.shape[0], rows, length - 1)
    return x[:, :, :cols]


def _rms(xf, g):
    return xf * lax.rsqrt(jnp.mean(xf * xf, axis=-1, keepdims=True) + EPS) * g


def _in_proj_kernel(x_ref, g_ref, w_ref, kones_ref, qa_ref, ka_ref, va_ref, qb_ref, kb_ref, vt_ref):
    u = _rms(x_ref[...], g_ref[...]).astype(jnp.bfloat16)
    proj = jnp.dot(u, w_ref[...], preferred_element_type=jnp.float32)
    qa_ref[...] = proj[:, 0:COL_KA].astype(jnp.bfloat16)
    ka_ref[...] = proj[:, COL_KA:COL_VA].astype(jnp.bfloat16)
    va_ref[...] = proj[:, COL_VA:COL_QB].astype(jnp.bfloat16)
    qb_ref[...] = proj[:, COL_QB:COL_KB].astype(jnp.bfloat16)
    lane = lax.broadcasted_iota(jnp.int32, (proj.shape[0], LANES), 1)
    keep = (lane < HEAD_DIM, lane >= HEAD_DIM)
    for h in range(B_HEADS):
        kpair = proj[:, COL_KB + h * LANES:COL_KB + (h + 1) * LANES]
        for c in range(2):
            col = (2 * h + c) * LANES
            kb_ref[:, col:col + LANES] = jnp.where(keep[c], kpair, kones_ref[:, col:col + LANES]).astype(jnp.bfloat16)
        vt = proj[:, COL_VB + h * B_V_DIM:COL_VB + (h + 1) * B_V_DIM].T.astype(jnp.bfloat16)
        for t in range(vt_ref.shape[1]):
            vt_ref[h, t] = vt[:, t * B_TILE:(t + 1) * B_TILE]


def _win_attn_kernel(sink_ref, q_ref, k_ref, v_ref, bias_ref, rel_bias_ref, o_ref, *, nblocks):
    f32 = jnp.float32
    bf16 = jnp.bfloat16
    n = pl.program_id(0)
    rows = A_STEP_BLOCKS * A_BLOCK
    seq = k_ref.shape[0]
    group = A_Q_HEADS // A_KV_HEADS
    start = pl.multiple_of(n * rows, rows)
    prev = pl.multiple_of(jnp.maximum(start - A_BLOCK, 0), A_BLOCK)
    nxt = pl.multiple_of(jnp.minimum(start + rows, seq - A_BLOCK), A_BLOCK)
    kw = jnp.concatenate([k_ref[pl.ds(prev, A_BLOCK), :], k_ref[pl.ds(start, rows), :],
                          k_ref[pl.ds(nxt, A_BLOCK), :]], axis=0)
    vw = jnp.concatenate([v_ref[pl.ds(prev, A_BLOCK), :], v_ref[pl.ds(start, rows), :],
                          v_ref[pl.ds(nxt, A_BLOCK), :]], axis=0)
    lane = lax.broadcasted_iota(jnp.int32, (A_BLOCK, LANES), 1)
    low = lane < HEAD_DIM
    ones = jnp.ones((3 * A_BLOCK, LANES), bf16)

    def attend(use_row_max):
        over = jnp.zeros((group * A_BLOCK, LANES), f32)
        for b in range(A_STEP_BLOCKS):
            blk = n * A_STEP_BLOCKS + b
            if b == 0:
                variant = jnp.where(blk == 0, 0, 1)
            elif b == A_STEP_BLOCKS - 1:
                variant = jnp.where(blk == nblocks - 1, 2, 1)
            else:
                variant = 1
            q = q_ref[b * A_BLOCK:(b + 1) * A_BLOCK, :]
            for g in range(A_KV_HEADS):
                heads = range(g * group, (g + 1) * group)
                qs = []
                for h in heads:
                    q2 = q[:, (h // 2) * LANES:(h // 2 + 1) * LANES]
                    qs.append(jnp.where(low if h % 2 == 0 else jnp.logical_not(low), q2, jnp.zeros_like(q2)))
                q4 = jnp.concatenate(qs, axis=0)
                kg = kw[b * A_BLOCK:(b + 3) * A_BLOCK, g * LANES:(g + 1) * LANES]
                vg = jnp.concatenate([vw[b * A_BLOCK:(b + 3) * A_BLOCK, g * LANES:(g + 1) * LANES], ones],
                                     axis=1)
                s = lax.dot_general(q4, kg, _NT, preferred_element_type=f32)
                ps, sink_terms = [], []
                for t, h in enumerate(heads):
                    sh = s[t * A_BLOCK:(t + 1) * A_BLOCK]
                    if use_row_max:
                        sh = sh + bias_ref[variant, h]
                        snk = sink_ref[h] * LOG2E
                        m = jnp.maximum(jnp.max(sh, axis=-1, keepdims=True), snk)
                        ps.append(jnp.exp2(sh - m).astype(bf16))
                        sink_terms.append(jnp.exp2(snk - m))
                    else:
                        ps.append(jnp.exp2(sh + rel_bias_ref[variant, h]).astype(bf16))
                p = jnp.concatenate(ps, axis=0)
                res = jnp.dot(p, vg, preferred_element_type=f32)
                row_sum = res[:, LANES:2 * LANES]
                sink_term = jnp.concatenate(sink_terms, axis=0) if use_row_max else 1.0
                on = res[:, 0:LANES] / (row_sum + sink_term)
                over = jnp.maximum(over, jnp.where(row_sum <= ROW_SUM_LIMIT, 0.0, 1.0))
                for t in range(group // 2):
                    even = on[(2 * t) * A_BLOCK:(2 * t + 1) * A_BLOCK]
                    odd = on[(2 * t + 1) * A_BLOCK:(2 * t + 2) * A_BLOCK]
                    pair = (g * group) // 2 + t
                    o_ref[b * A_BLOCK:(b + 1) * A_BLOCK, pair * LANES:(pair + 1) * LANES] = (
                        jnp.where(low, even, odd).astype(o_ref.dtype))
        return over

    over = attend(use_row_max=False)

    @pl.when(jnp.max(over) > 0.0)
    def _():
        attend(use_row_max=True)


def _diff_attn_kernel(lamv_ref, g_ref, q_ref, k_ref, vt_ref, bias_ref, o_ref,
                      qz_ref, qp_ref, p_ref, psum_ref, r_ref, flag_ref, any_ref, acc_ref, den_ref,
                      *, nchunks, lambda_init):
    TQ = B_QTILE
    TK = B_TILE
    i = pl.program_id(1)
    f32 = jnp.float32
    bf16 = jnp.bfloat16

    qt = q_ref[...].astype(f32).T
    row = lax.broadcasted_iota(jnp.int32, (LANES, TQ), 0)
    is_q = (row < HEAD_DIM, row >= HEAD_DIM)
    ref_row = (HEAD_DIM, 0)

    ksub = TK // B_BIAS_BLOCK
    qsub = TQ // B_BIAS_BLOCK

    def bias_tile(j, rows=ksub):
        base = j * ksub - i * qsub
        by_dist = {d: bias_ref[0, jnp.clip(base + d, -2, 2) + 2] for d in range(1 - qsub, rows)}
        return jnp.concatenate(
            [jnp.concatenate([by_dist[a - b] for b in range(qsub)], axis=1) for a in range(rows)], axis=0)

    def k_chunk(j, c):
        return k_ref[pl.ds(pl.multiple_of(j * TK, TK), TK), c * LANES:(c + 1) * LANES]

    def set_reference(c, r_row):
        hi = r_row.astype(bf16).astype(f32)
        rem = r_row - hi
        mid = rem.astype(bf16).astype(f32)
        low = (rem - mid).astype(bf16).astype(f32)
        first = ref_row[c]
        ext = jnp.where(row == first, -hi, jnp.where(row == first + 1, -mid,
                                                     jnp.where(row == first + 2, -low, 0.0)))
        qp_ref[c] = jnp.where(is_q[c], qt, ext).astype(bf16)

    def exact_chunk(j, c, flagged):
        s = jnp.dot(k_chunk(j, c), qz_ref[c], preferred_element_type=f32) + bias_tile(j)
        r_old = r_ref[c]
        r_new = jnp.where(flagged, jnp.maximum(r_old, jnp.max(s, axis=0, keepdims=True)), r_old)
        p = jnp.where(flagged, jnp.exp2(s - r_new), 0.0)
        pv = jnp.dot(vt_ref[0, j], p.astype(bf16), preferred_element_type=f32)
        alpha = jnp.exp2(r_old - r_new)
        acc_ref[c] = acc_ref[c] * alpha + pv
        den_ref[c] = den_ref[c] * alpha + jnp.sum(p, axis=0, keepdims=True)
        r_ref[c] = r_new

    def stage_a(j, slot):
        bias = bias_tile(j)
        for c in range(2):
            s = jnp.dot(k_chunk(j, c), qp_ref[c], preferred_element_type=f32) + bias
            p = jnp.exp2(s)
            psum_ref[slot, c] = jnp.sum(p, axis=0, keepdims=True)
            p_ref[slot, c] = p.astype(bf16)

    def stage_c(j, slot):
        vt = vt_ref[0, j]
        for c in range(2):
            pv = jnp.dot(vt, p_ref[slot, c], preferred_element_type=f32)
            psum = psum_ref[slot, c]
            ok = psum <= ROW_SUM_LIMIT
            acc_ref[c] = acc_ref[c] + jnp.where(ok, pv, 0.0)
            den_ref[c] = den_ref[c] + jnp.where(ok, psum, 0.0)
            flag = jnp.where(ok, 0.0, 1.0)
            flag_ref[c, pl.ds(j, 1), :] = flag
            any_ref[c] = jnp.maximum(any_ref[c], flag)

    for c in range(2):
        qz_ref[c] = jnp.where(is_q[c], qt, 0.0).astype(bf16)
    acc_ref[...] = jnp.zeros(acc_ref.shape, f32)
    den_ref[...] = jnp.zeros(den_ref.shape, f32)
    any_ref[...] = jnp.zeros(any_ref.shape, f32)
    diag = i * (TQ // TK)
    probe_bias = bias_tile(diag, rows=B_PROBE // B_BIAS_BLOCK)
    for c in range(2):
        kp = k_ref[pl.ds(pl.multiple_of(diag * TK, TK), B_PROBE), c * LANES:(c + 1) * LANES]
        s = jnp.dot(kp, qz_ref[c], preferred_element_type=f32) + probe_bias
        r = jnp.max(s, axis=0, keepdims=True)
        r_ref[c] = r
        set_reference(c, r)

    def run(j0, has_next):
        for u in range(B_UNROLL):
            stage_c(j0 + u, u % B_SLOTS)
            if u + B_AHEAD < B_UNROLL or has_next:
                stage_a(j0 + u + B_AHEAD, (u + B_AHEAD) % B_SLOTS)

    nbody = nchunks // B_UNROLL
    for u in range(B_AHEAD):
        stage_a(u, u)

    def body(b, carry):
        run(b * B_UNROLL, True)
        return carry

    lax.fori_loop(0, nbody - 1, body, 0)
    run((nbody - 1) * B_UNROLL, False)

    @pl.when(jnp.max(jnp.maximum(any_ref[0], any_ref[1])) > 0.0)
    def _():
        def redo(j, carry):
            @pl.when(jnp.max(jnp.maximum(flag_ref[0, pl.ds(j, 1), :], flag_ref[1, pl.ds(j, 1), :])) > 0.0)
            def _():
                for c in range(2):
                    exact_chunk(j, c, flag_ref[c, pl.ds(j, 1), :] > 0.0)
            return carry

        lax.fori_loop(0, nchunks, redo, 0)

    lv = lamv_ref[...]
    lam = (jnp.exp(jnp.sum(lv[0:1] * lv[1:2], axis=-1, keepdims=True))
           - jnp.exp(jnp.sum(lv[2:3] * lv[3:4], axis=-1, keepdims=True)) + lambda_init)
    ot = acc_ref[0] / den_ref[0] - lam * (acc_ref[1] / den_ref[1])
    o = ot.T
    o_ref[...] = (_rms(o, g_ref[...]) * (1.0 - lambda_init)).astype(o_ref.dtype)


def _out_ffn_kernel(ya_ref, yb_ref, x_ref, woa_ref, wob_ref, gpost_ref, gpre_ref,
                    wg_ref, wu_ref, wd_ref, gfpost_ref, o_ref, f_ref):
    y = (jnp.dot(ya_ref[...], woa_ref[...], preferred_element_type=jnp.float32)
         + jnp.dot(yb_ref[...], wob_ref[...], preferred_element_type=jnp.float32))
    h1 = x_ref[...] + _rms(y, gpost_ref[...])
    u = _rms(h1, gpre_ref[...]).astype(jnp.bfloat16)
    d_ff = wg_ref.shape[1]
    for c in range(d_ff // FFN_CHUNK):
        sl = slice(c * FFN_CHUNK, (c + 1) * FFN_CHUNK)
        gate = jnp.dot(u, wg_ref[:, sl], preferred_element_type=jnp.float32)
        up = jnp.dot(u, wu_ref[:, sl], preferred_element_type=jnp.float32)
        act = (gate * jax.nn.sigmoid(gate) * up).astype(jnp.bfloat16)
        part = jnp.dot(act, wd_ref[sl, :], preferred_element_type=jnp.float32)
        if c == 0:
            f_ref[...] = part
        else:
            f_ref[...] += part
    o_ref[...] = h1 + _rms(f_ref[...], gfpost_ref[...])


def _resident(shape):
    zeros = (0,) * len(shape)
    return pl.BlockSpec(shape, lambda *_: zeros, pipeline_mode=pl.Buffered(1))


def _layer(h, l, p, bias_a, bias_b):
    S, D = h.shape
    bf16 = jnp.bfloat16
    lambda_init = 0.8 - 0.6 * math.exp(-0.3 * l)
    scale = HEAD_DIM ** -0.5

    w = p["w_in"]
    a_q, a_kv, b_qk = A_Q_HEADS * HEAD_DIM, A_KV_HEADS * HEAD_DIM, B_HEADS * 2 * HEAD_DIM
    c0 = a_q
    c1 = c0 + a_kv
    c2 = c1 + a_kv
    c3 = c2 + b_qk

    def dup(cols):
        parts = []
        for g in range(A_KV_HEADS):
            blk = cols[:, g * HEAD_DIM:(g + 1) * HEAD_DIM]
            parts += [blk, blk]
        return jnp.concatenate(parts, axis=1)

    kones = np.zeros((B_HEADS, 2, LANES), np.float32)
    kones[:, 0, HEAD_DIM:HEAD_DIM + B_REF_LANES] = 1.0
    kones[:, 1, 0:B_REF_LANES] = 1.0
    kones = jnp.asarray(kones.reshape(1, B_HEADS * 2 * LANES))

    w_cat = jnp.concatenate([w[:, :c0] * (scale * LOG2E), dup(w[:, c0:c1]), dup(w[:, c1:c2]),
                             w[:, c2:c3] * (scale * LOG2E), w[:, c3:]],
                            axis=1).astype(bf16)
    ncols = w_cat.shape[1]
    assert ncols == PROJ_COLS and w.shape[1] == c3 + 2 * B_WIDTH
    nrow = S // ROW_TILE
    nchunks = S // B_TILE
    assert ROW_TILE % B_TILE == 0 and S % ROW_TILE == 0 and nchunks % B_UNROLL == 0
    assert B_UNROLL % B_SLOTS == 0 and B_AHEAD < B_SLOTS and B_AHEAD <= B_UNROLL
    kb_cols = B_HEADS * 2 * LANES

    qa, ka, va, qb, kb, vt = pl.pallas_call(
        _in_proj_kernel,
        grid=(nrow,),
        in_specs=[pl.BlockSpec((ROW_TILE, D), lambda i: (i, 0)),
                  _resident((1, D)),
                  _resident((D, ncols)),
                  _resident((1, kb_cols))],
        out_specs=[pl.BlockSpec((ROW_TILE, A_WIDTH), lambda i: (i, 0)),
                   pl.BlockSpec((ROW_TILE, A_KV_WIDTH), lambda i: (i, 0)),
                   pl.BlockSpec((ROW_TILE, A_KV_WIDTH), lambda i: (i, 0)),
                   pl.BlockSpec((ROW_TILE, B_WIDTH), lambda i: (i, 0)),
                   pl.BlockSpec((ROW_TILE, kb_cols), lambda i: (i, 0)),
                   pl.BlockSpec((B_HEADS, ROW_TILE // B_TILE, B_V_DIM, B_TILE), lambda i: (0, i, 0, 0))],
        out_shape=[jax.ShapeDtypeStruct((S, A_WIDTH), bf16),
                   jax.ShapeDtypeStruct((S, A_KV_WIDTH), bf16),
                   jax.ShapeDtypeStruct((S, A_KV_WIDTH), bf16),
                   jax.ShapeDtypeStruct((S, B_WIDTH), bf16),
                   jax.ShapeDtypeStruct((S, kb_cols), bf16),
                   jax.ShapeDtypeStruct((B_HEADS, nchunks, B_V_DIM, B_TILE), bf16)],
        compiler_params=pltpu.CompilerParams(dimension_semantics=("arbitrary",),
                                             vmem_limit_bytes=VMEM_LIMIT),
        name="in_proj",
    )(h, p["attn_pre_g"].reshape(1, D), w_cat, kones)

    nblocks = S // A_BLOCK
    a_rows = A_STEP_BLOCKS * A_BLOCK
    assert nblocks % A_STEP_BLOCKS == 0 and A_STEP_BLOCKS >= 2
    rel_bias_a = bias_a - (p["a_sink"].astype(jnp.float32) * LOG2E)[None, :, None, None]
    ya = pl.pallas_call(
        functools.partial(_win_attn_kernel, nblocks=nblocks),
        grid=(nblocks // A_STEP_BLOCKS,),
        in_specs=[pl.BlockSpec(memory_space=pltpu.SMEM),
                  pl.BlockSpec((a_rows, A_WIDTH), lambda n: (n, 0)),
                  _resident(ka.shape),
                  _resident(va.shape),
                  _resident(bias_a.shape),
                  _resident(bias_a.shape)],
        out_specs=pl.BlockSpec((a_rows, A_WIDTH), lambda n: (n, 0)),
        out_shape=jax.ShapeDtypeStruct((S, A_WIDTH), bf16),
        compiler_params=pltpu.CompilerParams(dimension_semantics=("arbitrary",),
                                             vmem_limit_bytes=VMEM_LIMIT),
        name="win_attn",
    )(p["a_sink"], qa, ka, va, bias_a, rel_bias_a)

    lamv = jnp.stack([p["lambda_q1"], p["lambda_k1"], p["lambda_q2"], p["lambda_k2"]])
    TQ, TK = B_QTILE, B_TILE
    assert S % TQ == 0 and TQ % TK == 0
    yb = pl.pallas_call(
        functools.partial(_diff_attn_kernel, nchunks=nchunks, lambda_init=lambda_init),
        grid=(B_HEADS, S // TQ),
        in_specs=[pl.BlockSpec((4, HEAD_DIM), lambda hh, i: (0, 0)),
                  pl.BlockSpec((1, B_V_DIM), lambda hh, i: (0, 0)),
                  pl.BlockSpec((TQ, LANES), lambda hh, i: (i, hh)),
                  pl.BlockSpec((S, 2 * LANES), lambda hh, i: (0, hh), pipeline_mode=pl.Buffered(1)),
                  pl.BlockSpec((1, nchunks, B_V_DIM, TK), lambda hh, i: (hh, 0, 0, 0),
                               pipeline_mode=pl.Buffered(1)),
                  pl.BlockSpec((1, 5, B_BIAS_BLOCK, B_BIAS_BLOCK), lambda hh, i: (hh, 0, 0, 0))],
        out_specs=pl.BlockSpec((TQ, LANES), lambda hh, i: (i, hh)),
        out_shape=jax.ShapeDtypeStruct((S, B_HEADS * B_V_DIM), bf16),
        scratch_shapes=[pltpu.VMEM((2, LANES, TQ), bf16),
                        pltpu.VMEM((2, LANES, TQ), bf16),
                        pltpu.VMEM((B_SLOTS, 2, TK, TQ), bf16),
                        pltpu.VMEM((B_SLOTS, 2, 1, TQ), jnp.float32),
                        pltpu.VMEM((2, 1, TQ), jnp.float32),
                        pltpu.VMEM((2, nchunks, TQ), jnp.float32),
                        pltpu.VMEM((2, 1, TQ), jnp.float32),
                        pltpu.VMEM((2, B_V_DIM, TQ), jnp.float32),
                        pltpu.VMEM((2, 1, TQ), jnp.float32)],
        compiler_params=pltpu.CompilerParams(dimension_semantics=("arbitrary", "arbitrary"),
                                             vmem_limit_bytes=VMEM_LIMIT),
        name="diff_attn",
    )(lamv, p["diff_subln_g"].reshape(1, B_V_DIM), qb, kb, vt, bias_b)

    w_out = p["w_out"].astype(bf16)
    a_width = A_WIDTH
    assert w_out.shape[0] == A_WIDTH + B_WIDTH
    d_ff = p["w_gate"].shape[1]
    assert d_ff % FFN_CHUNK == 0 and S % FFN_ROW_TILE == 0
    R = FFN_ROW_TILE
    out = pl.pallas_call(
        _out_ffn_kernel,
        grid=(S // R,),
        in_specs=[pl.BlockSpec((R, A_WIDTH), lambda i: (i, 0)),
                  pl.BlockSpec((R, B_WIDTH), lambda i: (i, 0)),
                  pl.BlockSpec((R, D), lambda i: (i, 0)),
                  _resident((a_width, D)),
                  _resident((w_out.shape[0] - a_width, D)),
                  _resident((1, D)),
                  _resident((1, D)),
                  _resident((D, d_ff)),
                  _resident((D, d_ff)),
                  _resident((d_ff, D)),
                  _resident((1, D))],
        out_specs=pl.BlockSpec((R, D), lambda i: (i, 0)),
        out_shape=jax.ShapeDtypeStruct((S, D), jnp.float32),
        scratch_shapes=[pltpu.VMEM((R, D), jnp.float32)],
        compiler_params=pltpu.CompilerParams(dimension_semantics=("arbitrary",),
                                             vmem_limit_bytes=VMEM_LIMIT),
        name="out_ffn",
    )(ya, yb, h, w_out[:a_width], w_out[a_width:], p["attn_post_g"].reshape(1, D),
      p["ffn_pre_g"].reshape(1, D), p["w_gate"].astype(bf16), p["w_up"].astype(bf16),
      p["w_down"].astype(bf16), p["ffn_post_g"].reshape(1, D))
    return out


def kernel(x, attn_pre_g, w_in, a_sink, lambda_q1, lambda_k1, lambda_q2, lambda_k2, diff_subln_g,
           rel_bias, w_out, attn_post_g, ffn_pre_g, w_gate, w_up, w_down, ffn_post_g):
    batch, S, D = x.shape
    depth = w_in.shape[0]
    assert S % B_TILE == 0 and S % A_BLOCK == 0 and S // A_BLOCK >= 2

    tab_a = _toeplitz_bias(rel_bias[:, :A_Q_HEADS], A_BLOCK, 3 * A_BLOCK, lambda t: -t - A_BLOCK)
    qi = np.arange(A_BLOCK)[:, None]
    kj = np.arange(3 * A_BLOCK)[None, :]
    in_window = np.abs(kj - A_BLOCK - qi) <= A_BLOCK
    valid = np.stack([in_window & (kj >= A_BLOCK), in_window, in_window & (kj < 2 * A_BLOCK)])
    bias_a = jnp.where(valid[:, None], tab_a[None] * LOG2E, MASK_VALUE)

    nb = B_BIAS_BLOCK
    assert len(set(_t5_bucket_np(np.arange(nb + 1, 4 * nb)))) == 1
    tab_b = rel_bias[:, A_Q_HEADS:]
    near = [_toeplitz_bias(tab_b, nb, nb, lambda t, d=d: d * nb + t) for d in (-1, 0, 1)]
    far = tab_b[_t5_bucket_np(np.array([-2 * nb, 2 * nb]))].astype(jnp.float32)
    const = [jnp.broadcast_to(far[side][:, None, None], (B_HEADS, nb, nb)) for side in (0, 1)]
    bias_b = jnp.stack([const[0]] + near + [const[1]], axis=1) * LOG2E

    outs = []
    for b in range(batch):
        h = x[b]
        for l in range(depth):
            p = dict(attn_pre_g=attn_pre_g[l], w_in=w_in[l], a_sink=a_sink[l], lambda_q1=lambda_q1[l],
                     lambda_k1=lambda_k1[l], lambda_q2=lambda_q2[l], lambda_k2=lambda_k2[l],
                     diff_subln_g=diff_subln_g[l], w_out=w_out[l], attn_post_g=attn_post_g[l],
                     ffn_pre_g=ffn_pre_g[l], w_gate=w_gate[l], w_up=w_up[l], w_down=w_down[l],
                     ffn_post_g=ffn_post_g[l])
            h = _layer(h, l, p, bias_a, bias_b)
        outs.append(h)
    return jnp.stack(outs)
```

```python
import functools
import math

import numpy as np
import jax
import jax.numpy as jnp
from jax import lax
from jax.experimental import pallas as pl
from jax.experimental.pallas import tpu as pltpu

HEAD_DIM = 64
A_Q_HEADS = 8
A_KV_HEADS = 2
A_BLOCK = 128
A_STEP_BLOCKS = 8
B_HEADS = 4
B_V_DIM = 2 * HEAD_DIM
NUM_BUCKETS = 32
MAX_DISTANCE = 128
EPS = 1e-6
MASK_VALUE = -1e30
LOG2E = math.log2(math.e)
ROW_SUM_LIMIT = 2.0 ** 40

A_WIDTH = A_Q_HEADS * HEAD_DIM
A_KV_WIDTH = 2 * A_KV_HEADS * HEAD_DIM
B_WIDTH = B_HEADS * B_V_DIM
COL_KA = A_WIDTH
COL_VA = COL_KA + A_KV_WIDTH
COL_QB = COL_VA + A_KV_WIDTH
COL_KB = COL_QB + B_WIDTH
COL_VB = COL_KB + B_WIDTH
PROJ_COLS = COL_VB + B_WIDTH

LANES = 128
ROW_TILE = 1024
FFN_ROW_TILE = 1024
FFN_CHUNK = 1408
B_TILE = 512
B_QTILE = 4096
B_REF_LANES = 3
B_UNROLL = 2
B_AHEAD = 1
B_SLOTS = 2
B_PROBE = 128
B_BIAS_BLOCK = 128
VMEM_LIMIT = 56 * 1024 * 1024

_NT = (((1,), (1,)), ((), ()))


def _t5_bucket_np(rel):
    nb = NUM_BUCKETS // 2
    max_exact = nb // 2
    ret = np.where(rel > 0, nb, 0)
    n = np.abs(rel)
    nf = np.maximum(n, 1).astype(np.float32)
    large = max_exact + (np.log(nf / np.float32(max_exact)) / np.float32(math.log(MAX_DISTANCE / max_exact))
                         * np.float32(nb - max_exact)).astype(np.int32)
    large = np.minimum(large, nb - 1)
    return (ret + np.where(n < max_exact, n, large)).astype(np.int32)


def _toeplitz_bias(table, rows, cols, rel_of):
    length = rows + cols
    u = np.arange(length)
    t = np.where(u < cols, -u, length - u)
    w = table[_t5_bucket_np(rel_of(t))].astype(jnp.float32).T
    x = jnp.tile(w, (1, rows))[:, :rows * (length - 1)].reshape(w.shape[0], rows, length - 1)
    return x[:, :, :cols]


def _rms(xf, g):
    return xf * lax.rsqrt(jnp.mean(xf * xf, axis=-1, keepdims=True) + EPS) * g


def _in_proj_kernel(x_ref, g_ref, w_ref, kones_ref, qa_ref, ka_ref, va_ref, qb_ref, kb_ref, vt_ref):
    u = _rms(x_ref[...], g_ref[...]).astype(jnp.bfloat16)
    proj = jnp.dot(u, w_ref[...], preferred_element_type=jnp.float32)
    qa_ref[...] = proj[:, 0:COL_KA].astype(jnp.bfloat16)
    ka_ref[...] = proj[:, COL_KA:COL_VA].astype(jnp.bfloat16)
    va_ref[...] = proj[:, COL_VA:COL_QB].astype(jnp.bfloat16)
    qb_ref[...] = proj[:, COL_QB:COL_KB].astype(jnp.bfloat16)
    lane = lax.broadcasted_iota(jnp.int32, (proj.shape[0], LANES), 1)
    keep = (lane < HEAD_DIM, lane >= HEAD_DIM)
    for h in range(B_HEADS):
        kpair = proj[:, COL_KB + h * LANES:COL_KB + (h + 1) * LANES]
        for c in range(2):
            col = (2 * h + c) * LANES
            kb_ref[:, col:col + LANES] = jnp.where(keep[c], kpair, kones_ref[:, col:col + LANES]).astype(jnp.bfloat16)
        vt = proj[:, COL_VB + h * B_V_DIM:COL_VB + (h + 1) * B_V_DIM].T.astype(jnp.bfloat16)
        for t in range(vt_ref.shape[1]):
            vt_ref[h, t] = vt[:, t * B_TILE:(t + 1) * B_TILE]


def _win_attn_kernel(sink_ref, q_ref, k_ref, v_ref, bias_ref, rel_bias_ref, o_ref, *, nblocks):
    f32 = jnp.float32
    bf16 = jnp.bfloat16
    n = pl.program_id(0)
    rows = A_STEP_BLOCKS * A_BLOCK
    seq = k_ref.shape[0]
    group = A_Q_HEADS // A_KV_HEADS
    start = pl.multiple_of(n * rows, rows)
    prev = pl.multiple_of(jnp.maximum(start - A_BLOCK, 0), A_BLOCK)
    nxt = pl.multiple_of(jnp.minimum(start + rows, seq - A_BLOCK), A_BLOCK)
    kw = jnp.concatenate([k_ref[pl.ds(prev, A_BLOCK), :], k_ref[pl.ds(start, rows), :],
                          k_ref[pl.ds(nxt, A_BLOCK), :]], axis=0)
    vw = jnp.concatenate([v_ref[pl.ds(prev, A_BLOCK), :], v_ref[pl.ds(start, rows), :],
                          v_ref[pl.ds(nxt, A_BLOCK), :]], axis=0)
    lane = lax.broadcasted_iota(jnp.int32, (A_BLOCK, LANES), 1)
    low = lane < HEAD_DIM
    ones = jnp.ones((3 * A_BLOCK, LANES), bf16)

    def attend(use_row_max):
        over = jnp.zeros((group * A_BLOCK, LANES), f32)
        for b in range(A_STEP_BLOCKS):
            blk = n * A_STEP_BLOCKS + b
            if b == 0:
                variant = jnp.where(blk == 0, 0, 1)
            elif b == A_STEP_BLOCKS - 1:
                variant = jnp.where(blk == nblocks - 1, 2, 1)
            else:
                variant = 1
            q = q_ref[b * A_BLOCK:(b + 1) * A_BLOCK, :]
            for g in range(A_KV_HEADS):
                heads = range(g * group, (g + 1) * group)
                qs = []
                for h in heads:
                    q2 = q[:, (h // 2) * LANES:(h // 2 + 1) * LANES]
                    qs.append(jnp.where(low if h % 2 == 0 else jnp.logical_not(low), q2, jnp.zeros_like(q2)))
                q4 = jnp.concatenate(qs, axis=0)
                kg = kw[b * A_BLOCK:(b + 3) * A_BLOCK, g * LANES:(g + 1) * LANES]
                vg = jnp.concatenate([vw[b * A_BLOCK:(b + 3) * A_BLOCK, g * LANES:(g + 1) * LANES], ones],
                                     axis=1)
                s = lax.dot_general(q4, kg, _NT, preferred_element_type=f32)
                ps, sink_terms = [], []
                for t, h in enumerate(heads):
                    sh = s[t * A_BLOCK:(t + 1) * A_BLOCK]
                    if use_row_max:
                        sh = sh + bias_ref[variant, h]
                        snk = sink_ref[h] * LOG2E
                        m = jnp.maximum(jnp.max(sh, axis=-1, keepdims=True), snk)
                        ps.append(jnp.exp2(sh - m).astype(bf16))
                        sink_terms.append(jnp.exp2(snk - m))
                    else:
                        ps.append(jnp.exp2(sh + rel_bias_ref[variant, h]).astype(bf16))
                p = jnp.concatenate(ps, axis=0)
                res = jnp.dot(p, vg, preferred_element_type=f32)
                row_sum = res[:, LANES:2 * LANES]
                sink_term = jnp.concatenate(sink_terms, axis=0) if use_row_max else 1.0
                on = res[:, 0:LANES] / (row_sum + sink_term)
                over = jnp.maximum(over, jnp.where(row_sum <= ROW_SUM_LIMIT, 0.0, 1.0))
                for t in range(group // 2):
                    even = on[(2 * t) * A_BLOCK:(2 * t + 1) * A_BLOCK]
                    odd = on[(2 * t + 1) * A_BLOCK:(2 * t + 2) * A_BLOCK]
                    pair = (g * group) // 2 + t
                    o_ref[b * A_BLOCK:(b + 1) * A_BLOCK, pair * LANES:(pair + 1) * LANES] = (
                        jnp.where(low, even, odd).astype(o_ref.dtype))
        return over

    over = attend(use_row_max=False)

    @pl.when(jnp.max(over) > 0.0)
    def _():
        attend(use_row_max=True)


def _diff_attn_kernel(lamv_ref, g_ref, q_ref, k_ref, vt_ref, bias_ref, o_ref,
                      qz_ref, qp_ref, p_ref, psum_ref, r_ref, flag_ref, any_ref, acc_ref, den_ref,
                      *, nchunks, lambda_init):
    TQ = B_QTILE
    TK = B_TILE
    i = pl.program_id(1)
    f32 = jnp.float32
    bf16 = jnp.bfloat16

    qt = q_ref[...].astype(f32).T
    row = lax.broadcasted_iota(jnp.int32, (LANES, TQ), 0)
    is_q = (row < HEAD_DIM, row >= HEAD_DIM)
    ref_row = (HEAD_DIM, 0)

    ksub = TK // B_BIAS_BLOCK
    qsub = TQ // B_BIAS_BLOCK

    def bias_tile(j, rows=ksub):
        base = j * ksub - i * qsub
        by_dist = {d: bias_ref[0, jnp.clip(base + d, -2, 2) + 2] for d in range(1 - qsub, rows)}
        return jnp.concatenate(
            [jnp.concatenate([by_dist[a - b] for b in range(qsub)], axis=1) for a in range(rows)], axis=0)

    def k_chunk(j, c):
        return k_ref[pl.ds(pl.multiple_of(j * TK, TK), TK), c * LANES:(c + 1) * LANES]

    def set_reference(c, r_row):
        hi = r_row.astype(bf16).astype(f32)
        rem = r_row - hi
        mid = rem.astype(bf16).astype(f32)
        low = (rem - mid).astype(bf16).astype(f32)
        first = ref_row[c]
        ext = jnp.where(row == first, -hi, jnp.where(row == first + 1, -mid,
                                                     jnp.where(row == first + 2, -low, 0.0)))
        qp_ref[c] = jnp.where(is_q[c], qt, ext).astype(bf16)

    def exact_chunk(j, c, flagged):
        s = jnp.dot(k_chunk(j, c), qz_ref[c], preferred_element_type=f32) + bias_tile(j)
        r_old = r_ref[c]
        r_new = jnp.where(flagged, jnp.maximum(r_old, jnp.max(s, axis=0, keepdims=True)), r_old)
        p = jnp.where(flagged, jnp.exp2(s - r_new), 0.0)
        pv = jnp.dot(vt_ref[0, j], p.astype(bf16), preferred_element_type=f32)
        alpha = jnp.exp2(r_old - r_new)
        acc_ref[c] = acc_ref[c] * alpha + pv
        den_ref[c] = den_ref[c] * alpha + jnp.sum(p, axis=0, keepdims=True)
        r_ref[c] = r_new

    def stage_a(j, slot):
        bias = bias_tile(j)
        for c in range(2):
            s = jnp.dot(k_chunk(j, c), qp_ref[c], preferred_element_type=f32) + bias
            p = jnp.exp2(s)
            psum_ref[slot, c] = jnp.sum(p, axis=0, keepdims=True)
            p_ref[slot, c] = p.astype(bf16)

    def stage_c(j, slot):
        vt = vt_ref[0, j]
        for c in range(2):
            pv = jnp.dot(vt, p_ref[slot, c], preferred_element_type=f32)
            psum = psum_ref[slot, c]
            ok = psum <= ROW_SUM_LIMIT
            acc_ref[c] = acc_ref[c] + jnp.where(ok, pv, 0.0)
            den_ref[c] = den_ref[c] + jnp.where(ok, psum, 0.0)
            flag = jnp.where(ok, 0.0, 1.0)
            flag_ref[c, pl.ds(j, 1), :] = flag
            any_ref[c] = jnp.maximum(any_ref[c], flag)

    for c in range(2):
        qz_ref[c] = jnp.where(is_q[c], qt, 0.0).astype(bf16)
    acc_ref[...] = jnp.zeros(acc_ref.shape, f32)
    den_ref[...] = jnp.zeros(den_ref.shape, f32)
    any_ref[...] = jnp.zeros(any_ref.shape, f32)
    diag = i * (TQ // TK)
    probe_bias = bias_tile(diag, rows=B_PROBE // B_BIAS_BLOCK)
    for c in range(2):
        kp = k_ref[pl.ds(pl.multiple_of(diag * TK, TK), B_PROBE), c * LANES:(c + 1) * LANES]
        s = jnp.dot(kp, qz_ref[c], preferred_element_type=f32) + probe_bias
        r = jnp.max(s, axis=0, keepdims=True)
        r_ref[c] = r
        set_reference(c, r)

    def run(j0, has_next):
        for u in range(B_UNROLL):
            stage_c(j0 + u, u % B_SLOTS)
            if u + B_AHEAD < B_UNROLL or has_next:
                stage_a(j0 + u + B_AHEAD, (u + B_AHEAD) % B_SLOTS)

    nbody = nchunks // B_UNROLL
    for u in range(B_AHEAD):
        stage_a(u, u)

    def body(b, carry):
        run(b * B_UNROLL, True)
        return carry

    lax.fori_loop(0, nbody - 1, body, 0)
    run((nbody - 1) * B_UNROLL, False)

    @pl.when(jnp.max(jnp.maximum(any_ref[0], any_ref[1])) > 0.0)
    def _():
        def redo(j, carry):
            @pl.when(jnp.max(jnp.maximum(flag_ref[0, pl.ds(j, 1), :], flag_ref[1, pl.ds(j, 1), :])) > 0.0)
            def _():
                for c in range(2):
                    exact_chunk(j, c, flag_ref[c, pl.ds(j, 1), :] > 0.0)
            return carry

        lax.fori_loop(0, nchunks, redo, 0)

    lv = lamv_ref[...]
    lam = (jnp.exp(jnp.sum(lv[0:1] * lv[1:2], axis=-1, keepdims=True))
           - jnp.exp(jnp.sum(lv[2:3] * lv[3:4], axis=-1, keepdims=True)) + lambda_init)
    ot = acc_ref[0] / den_ref[0] - lam * (acc_ref[1] / den_ref[1])
    o = ot.T
    o_ref[...] = (_rms(o, g_ref[...]) * (1.0 - lambda_init)).astype(o_ref.dtype)


def _out_ffn_kernel(ya_ref, yb_ref, x_ref, woa_ref, wob_ref, gpost_ref, gpre_ref,
                    wg_ref, wu_ref, wd_ref, gfpost_ref, o_ref, f_ref):
    y = (jnp.dot(ya_ref[...], woa_ref[...], preferred_element_type=jnp.float32)
         + jnp.dot(yb_ref[...], wob_ref[...], preferred_element_type=jnp.float32))
    h1 = x_ref[...] + _rms(y, gpost_ref[...])
    u = _rms(h1, gpre_ref[...]).astype(jnp.bfloat16)
    d_ff = wg_ref.shape[1]
    for c in range(d_ff // FFN_CHUNK):
        sl = slice(c * FFN_CHUNK, (c + 1) * FFN_CHUNK)
        gate = jnp.dot(u, wg_ref[:, sl], preferred_element_type=jnp.float32)
        up = jnp.dot(u, wu_ref[:, sl], preferred_element_type=jnp.float32)
        act = (gate * jax.nn.sigmoid(gate) * up).astype(jnp.bfloat16)
        part = jnp.dot(act, wd_ref[sl, :], preferred_element_type=jnp.float32)
        if c == 0:
            f_ref[...] = part
        else:
            f_ref[...] += part
    o_ref[...] = h1 + _rms(f_ref[...], gfpost_ref[...])


def _resident(shape):
    zeros = (0,) * len(shape)
    return pl.BlockSpec(shape, lambda *_: zeros, pipeline_mode=pl.Buffered(1))


def _layer(h, l, p, bias_a, bias_b):
    S, D = h.shape
    bf16 = jnp.bfloat16
    lambda_init = 0.8 - 0.6 * math.exp(-0.3 * l)
    scale = HEAD_DIM ** -0.5

    w = p["w_in"]
    a_q, a_kv, b_qk = A_Q_HEADS * HEAD_DIM, A_KV_HEADS * HEAD_DIM, B_HEADS * 2 * HEAD_DIM
    c0 = a_q
    c1 = c0 + a_kv
    c2 = c1 + a_kv
    c3 = c2 + b_qk

    def dup(cols):
        parts = []
        for g in range(A_KV_HEADS):
            blk = cols[:, g * HEAD_DIM:(g + 1) * HEAD_DIM]
            parts += [blk, blk]
        return jnp.concatenate(parts, axis=1)

    kones = np.zeros((B_HEADS, 2, LANES), np.float32)
    kones[:, 0, HEAD_DIM:HEAD_DIM + B_REF_LANES] = 1.0
    kones[:, 1, 0:B_REF_LANES] = 1.0
    kones = jnp.asarray(kones.reshape(1, B_HEADS * 2 * LANES))

    w_cat = jnp.concatenate([w[:, :c0] * (scale * LOG2E), dup(w[:, c0:c1]), dup(w[:, c1:c2]),
                             w[:, c2:c3] * (scale * LOG2E), w[:, c3:]],
                            axis=1).astype(bf16)
    ncols = w_cat.shape[1]
    assert ncols == PROJ_COLS and w.shape[1] == c3 + 2 * B_WIDTH
    nrow = S // ROW_TILE
    nchunks = S // B_TILE
    assert ROW_TILE % B_TILE == 0 and S % ROW_TILE == 0 and nchunks % B_UNROLL == 0
    assert B_UNROLL % B_SLOTS == 0 and B_AHEAD < B_SLOTS and B_AHEAD <= B_UNROLL
    kb_cols = B_HEADS * 2 * LANES

    qa, ka, va, qb, kb, vt = pl.pallas_call(
        _in_proj_kernel,
        grid=(nrow,),
        in_specs=[pl.BlockSpec((ROW_TILE, D), lambda i: (i, 0)),
                  _resident((1, D)),
                  _resident((D, ncols)),
                  _resident((1, kb_cols))],
        out_specs=[pl.BlockSpec((ROW_TILE, A_WIDTH), lambda i: (i, 0)),
                   pl.BlockSpec((ROW_TILE, A_KV_WIDTH), lambda i: (i, 0)),
                   pl.BlockSpec((ROW_TILE, A_KV_WIDTH), lambda i: (i, 0)),
                   pl.BlockSpec((ROW_TILE, B_WIDTH), lambda i: (i, 0)),
                   pl.BlockSpec((ROW_TILE, kb_cols), lambda i: (i, 0)),
                   pl.BlockSpec((B_HEADS, ROW_TILE // B_TILE, B_V_DIM, B_TILE), lambda i: (0, i, 0, 0))],
        out_shape=[jax.ShapeDtypeStruct((S, A_WIDTH), bf16),
                   jax.ShapeDtypeStruct((S, A_KV_WIDTH), bf16),
                   jax.ShapeDtypeStruct((S, A_KV_WIDTH), bf16),
                   jax.ShapeDtypeStruct((S, B_WIDTH), bf16),
                   jax.ShapeDtypeStruct((S, kb_cols), bf16),
                   jax.ShapeDtypeStruct((B_HEADS, nchunks, B_V_DIM, B_TILE), bf16)],
        compiler_params=pltpu.CompilerParams(dimension_semantics=("arbitrary",),
                                             vmem_limit_bytes=VMEM_LIMIT),
        name="in_proj",
    )(h, p["attn_pre_g"].reshape(1, D), w_cat, kones)

    nblocks = S // A_BLOCK
    a_rows = A_STEP_BLOCKS * A_BLOCK
    assert nblocks % A_STEP_BLOCKS == 0 and A_STEP_BLOCKS >= 2
    rel_bias_a = bias_a - (p["a_sink"].astype(jnp.float32) * LOG2E)[None, :, None, None]
    ya = pl.pallas_call(
        functools.partial(_win_attn_kernel, nblocks=nblocks),
        grid=(nblocks // A_STEP_BLOCKS,),
        in_specs=[pl.BlockSpec(memory_space=pltpu.SMEM),
                  pl.BlockSpec((a_rows, A_WIDTH), lambda n: (n, 0)),
                  _resident(ka.shape),
                  _resident(va.shape),
                  _resident(bias_a.shape),
                  _resident(bias_a.shape)],
        out_specs=pl.BlockSpec((a_rows, A_WIDTH), lambda n: (n, 0)),
        out_shape=jax.ShapeDtypeStruct((S, A_WIDTH), bf16),
        compiler_params=pltpu.CompilerParams(dimension_semantics=("arbitrary",),
                                             vmem_limit_bytes=VMEM_LIMIT),
        name="win_attn",
    )(p["a_sink"], qa, ka, va, bias_a, rel_bias_a)

    lamv = jnp.stack([p["lambda_q1"], p["lambda_k1"], p["lambda_q2"], p["lambda_k2"]])
    TQ, TK = B_QTILE, B_TILE
    assert S % TQ == 0 and TQ % TK == 0
    yb = pl.pallas_call(
        functools.partial(_diff_attn_kernel, nchunks=nchunks, lambda_init=lambda_init),
        grid=(B_HEADS, S // TQ),
        in_specs=[pl.BlockSpec((4, HEAD_DIM), lambda hh, i: (0, 0)),
                  pl.BlockSpec((1, B_V_DIM), lambda hh, i: (0, 0)),
                  pl.BlockSpec((TQ, LANES), lambda hh, i: (i, hh)),
                  pl.BlockSpec((S, 2 * LANES), lambda hh, i: (0, hh), pipeline_mode=pl.Buffered(1)),
                  pl.BlockSpec((1, nchunks, B_V_DIM, TK), lambda hh, i: (hh, 0, 0, 0),
                               pipeline_mode=pl.Buffered(1)),
                  pl.BlockSpec((1, 5, B_BIAS_BLOCK, B_BIAS_BLOCK), lambda hh, i: (hh, 0, 0, 0))],
        out_specs=pl.BlockSpec((TQ, LANES), lambda hh, i: (i, hh)),
        out_shape=jax.ShapeDtypeStruct((S, B_HEADS * B_V_DIM), bf16),
        scratch_shapes=[pltpu.VMEM((2, LANES, TQ), bf16),
                        pltpu.VMEM((2, LANES, TQ), bf16),
                        pltpu.VMEM((B_SLOTS, 2, TK, TQ), bf16),
                        pltpu.VMEM((B_SLOTS, 2, 1, TQ), jnp.float32),
                        pltpu.VMEM((2, 1, TQ), jnp.float32),
                        pltpu.VMEM((2, nchunks, TQ), jnp.float32),
                        pltpu.VMEM((2, 1, TQ), jnp.float32),
                        pltpu.VMEM((2, B_V_DIM, TQ), jnp.float32),
                        pltpu.VMEM((2, 1, TQ), jnp.float32)],
        compiler_params=pltpu.CompilerParams(dimension_semantics=("arbitrary", "arbitrary"),
                                             vmem_limit_bytes=VMEM_LIMIT),
        name="diff_attn",
    )(lamv, p["diff_subln_g"].reshape(1, B_V_DIM), qb, kb, vt, bias_b)

    w_out = p["w_out"].astype(bf16)
    a_width = A_WIDTH
    assert w_out.shape[0] == A_WIDTH + B_WIDTH
    d_ff = p["w_gate"].shape[1]
    assert d_ff % FFN_CHUNK == 0 and S % FFN_ROW_TILE == 0
    R = FFN_ROW_TILE
    out = pl.pallas_call(
        _out_ffn_kernel,
        grid=(S // R,),
        in_specs=[pl.BlockSpec((R, A_WIDTH), lambda i: (i, 0)),
                  pl.BlockSpec((R, B_WIDTH), lambda i: (i, 0)),
                  pl.BlockSpec((R, D), lambda i: (i, 0)),
                  _resident((a_width, D)),
                  _resident((w_out.shape[0] - a_width, D)),
                  _resident((1, D)),
                  _resident((1, D)),
                  _resident((D, d_ff)),
                  _resident((D, d_ff)),
                  _resident((d_ff, D)),
                  _resident((1, D))],
        out_specs=pl.BlockSpec((R, D), lambda i: (i, 0)),
        out_shape=jax.ShapeDtypeStruct((S, D), jnp.float32),
        scratch_shapes=[pltpu.VMEM((R, D), jnp.float32)],
        compiler_params=pltpu.CompilerParams(dimension_semantics=("arbitrary",),
                                             vmem_limit_bytes=VMEM_LIMIT),
        name="out_ffn",
    )(ya, yb, h, w_out[:a_width], w_out[a_width:], p["attn_post_g"].reshape(1, D),
      p["ffn_pre_g"].reshape(1, D), p["w_gate"].astype(bf16), p["w_up"].astype(bf16),
      p["w_down"].astype(bf16), p["ffn_post_g"].reshape(1, D))
    return out


def kernel(x, attn_pre_g, w_in, a_sink, lambda_q1, lambda_k1, lambda_q2, lambda_k2, diff_subln_g,
           rel_bias, w_out, attn_post_g, ffn_pre_g, w_gate, w_up, w_down, ffn_post_g):
    batch, S, D = x.shape
    depth = w_in.shape[0]
    assert S % B_TILE == 0 and S % A_BLOCK == 0 and S // A_BLOCK >= 2

    tab_a = _toeplitz_bias(rel_bias[:, :A_Q_HEADS], A_BLOCK, 3 * A_BLOCK, lambda t: -t - A_BLOCK)
    qi = np.arange(A_BLOCK)[:, None]
    kj = np.arange(3 * A_BLOCK)[None, :]
    in_window = np.abs(kj - A_BLOCK - qi) <= A_BLOCK
    valid = np.stack([in_window & (kj >= A_BLOCK), in_window, in_window & (kj < 2 * A_BLOCK)])
    bias_a = jnp.where(valid[:, None], tab_a[None] * LOG2E, MASK_VALUE)

    nb = B_BIAS_BLOCK
    assert len(set(_t5_bucket_np(np.arange(nb + 1, 4 * nb)))) == 1
    tab_b = rel_bias[:, A_Q_HEADS:]
    near = [_toeplitz_bias(tab_b, nb, nb, lambda t, d=d: d * nb + t) for d in (-1, 0, 1)]
    far = tab_b[_t5_bucket_np(np.array([-2 * nb, 2 * nb]))].astype(jnp.float32)
    const = [jnp.broadcast_to(far[side][:, None, None], (B_HEADS, nb, nb)) for side in (0, 1)]
    bias_b = jnp.stack([const[0]] + near + [const[1]], axis=1) * LOG2E

    outs = []
    for b in range(batch):
        h = x[b]
        for l in range(depth):
            p = dict(attn_pre_g=attn_pre_g[l], w_in=w_in[l], a_sink=a_sink[l], lambda_q1=lambda_q1[l],
                     lambda_k1=lambda_k1[l], lambda_q2=lambda_q2[l], lambda_k2=lambda_k2[l],
                     diff_subln_g=diff_subln_g[l], w_out=w_out[l], attn_post_g=attn_post_g[l],
                     ffn_pre_g=ffn_pre_g[l], w_gate=w_gate[l], w_up=w_up[l], w_down=w_down[l],
                     ffn_post_g=ffn_post_g[l])
            h = _layer(h, l, p, bias_a, bias_b)
        outs.append(h)
    return jnp.stack(outs)
```

```python
import functools
import math

import numpy as np
import jax
import jax.numpy as jnp
from jax import lax
from jax.experimental import pallas as pl
from jax.experimental.pallas import tpu as pltpu

HEAD_DIM = 64
A_Q_HEADS = 8
A_KV_HEADS = 2
A_BLOCK = 128
A_STEP_BLOCKS = 8
B_HEADS = 4
B_V_DIM = 2 * HEAD_DIM
NUM_BUCKETS = 32
MAX_DISTANCE = 128
EPS = 1e-6
MASK_VALUE = -1e30
LOG2E = math.log2(math.e)
ROW_SUM_LIMIT = 2.0 ** 40

A_WIDTH = A_Q_HEADS * HEAD_DIM
A_KV_WIDTH = 2 * A_KV_HEADS * HEAD_DIM
B_WIDTH = B_HEADS * B_V_DIM
COL_KA = A_WIDTH
COL_VA = COL_KA + A_KV_WIDTH
COL_QB = COL_VA + A_KV_WIDTH
COL_KB = COL_QB + B_WIDTH
COL_VB = COL_KB + B_WIDTH
PROJ_COLS = COL_VB + B_WIDTH

LANES = 128
ROW_TILE = 1024
FFN_ROW_TILE = 1024
FFN_CHUNK = 256
B_TILE = 256
B_QTILE = 4096
B_REF_LANES = 3
B_UNROLL = 4
B_AHEAD = 2
B_SLOTS = 4
B_PROBE = 128
B_BIAS_BLOCK = 128
VMEM_LIMIT = 56 * 1024 * 1024

_NT = (((1,), (1,)), ((), ()))


def _t5_bucket_np(rel):
    nb = NUM_BUCKETS // 2
    max_exact = nb // 2
    ret = np.where(rel > 0, nb, 0)
    n = np.abs(rel)
    nf = np.maximum(n, 1).astype(np.float32)
    large = max_exact + (np.log(nf / np.float32(max_exact)) / np.float32(math.log(MAX_DISTANCE / max_exact))
                         * np.float32(nb - max_exact)).astype(np.int32)
    large = np.minimum(large, nb - 1)
    return (ret + np.where(n < max_exact, n, large)).astype(np.int32)


def _toeplitz_bias(table, rows, cols, rel_of):
    length = rows + cols
    u = np.arange(length)
    t = np.where(u < cols, -u, length - u)
    w = table[_t5_bucket_np(rel_of(t))].astype(jnp.float32).T
    x = jnp.tile(w, (1, rows))[:, :rows * (length - 1)].reshape(w.shape[0], rows, length - 1)
    return x[:, :, :cols]


def _rms(xf, g):
    return xf * lax.rsqrt(jnp.mean(xf * xf, axis=-1, keepdims=True) + EPS) * g


def _in_proj_kernel(x_ref, g_ref, w_ref, kones_ref, qa_ref, ka_ref, va_ref, qb_ref, kb_ref, vt_ref):
    u = _rms(x_ref[...], g_ref[...]).astype(jnp.bfloat16)
    proj = jnp.dot(u, w_ref[...], preferred_element_type=jnp.float32)
    qa_ref[...] = proj[:, 0:COL_KA].astype(jnp.bfloat16)
    ka_ref[...] = proj[:, COL_KA:COL_VA].astype(jnp.bfloat16)
    va_ref[...] = proj[:, COL_VA:COL_QB].astype(jnp.bfloat16)
    qb_ref[...] = proj[:, COL_QB:COL_KB].astype(jnp.bfloat16)
    lane = lax.broadcasted_iota(jnp.int32, (proj.shape[0], LANES), 1)
    keep = (lane < HEAD_DIM, lane >= HEAD_DIM)
    for h in range(B_HEADS):
        kpair = proj[:, COL_KB + h * LANES:COL_KB + (h + 1) * LANES]
        for c in range(2):
            col = (2 * h + c) * LANES
            kb_ref[:, col:col + LANES] = jnp.where(keep[c], kpair, kones_ref[:, col:col + LANES]).astype(jnp.bfloat16)
        vt = proj[:, COL_VB + h * B_V_DIM:COL_VB + (h + 1) * B_V_DIM].T.astype(jnp.bfloat16)
        for t in range(vt_ref.shape[1]):
            vt_ref[h, t] = vt[:, t * B_TILE:(t + 1) * B_TILE]


def _win_attn_kernel(sink_ref, q_ref, k_ref, v_ref, bias_ref, rel_bias_ref, o_ref, *, nblocks):
    f32 = jnp.float32
    bf16 = jnp.bfloat16
    n = pl.program_id(0)
    rows = A_STEP_BLOCKS * A_BLOCK
    seq = k_ref.shape[0]
    group = A_Q_HEADS // A_KV_HEADS
    start = pl.multiple_of(n * rows, rows)
    prev = pl.multiple_of(jnp.maximum(start - A_BLOCK, 0), A_BLOCK)
    nxt = pl.multiple_of(jnp.minimum(start + rows, seq - A_BLOCK), A_BLOCK)
    kw = jnp.concatenate([k_ref[pl.ds(prev, A_BLOCK), :], k_ref[pl.ds(start, rows), :],
                          k_ref[pl.ds(nxt, A_BLOCK), :]], axis=0)
    vw = jnp.concatenate([v_ref[pl.ds(prev, A_BLOCK), :], v_ref[pl.ds(start, rows), :],
                          v_ref[pl.ds(nxt, A_BLOCK), :]], axis=0)
    lane = lax.broadcasted_iota(jnp.int32, (A_BLOCK, LANES), 1)
    low = lane < HEAD_DIM
    ones = jnp.ones((3 * A_BLOCK, LANES), bf16)

    def attend(use_row_max):
        over = jnp.zeros((group * A_BLOCK, LANES), f32)
        for b in range(A_STEP_BLOCKS):
            blk = n * A_STEP_BLOCKS + b
            if b == 0:
                variant = jnp.where(blk == 0, 0, 1)
            elif b == A_STEP_BLOCKS - 1:
                variant = jnp.where(blk == nblocks - 1, 2, 1)
            else:
                variant = 1
            q = q_ref[b * A_BLOCK:(b + 1) * A_BLOCK, :]
            for g in range(A_KV_HEADS):
                heads = range(g * group, (g + 1) * group)
                qs = []
                for h in heads:
                    q2 = q[:, (h // 2) * LANES:(h // 2 + 1) * LANES]
                    qs.append(jnp.where(low if h % 2 == 0 else jnp.logical_not(low), q2, jnp.zeros_like(q2)))
                q4 = jnp.concatenate(qs, axis=0)
                kg = kw[b * A_BLOCK:(b + 3) * A_BLOCK, g * LANES:(g + 1) * LANES]
                vg = jnp.concatenate([vw[b * A_BLOCK:(b + 3) * A_BLOCK, g * LANES:(g + 1) * LANES], ones],
                                     axis=1)
                s = lax.dot_general(q4, kg, _NT, preferred_element_type=f32)
                ps, sink_terms = [], []
                for t, h in enumerate(heads):
                    sh = s[t * A_BLOCK:(t + 1) * A_BLOCK]
                    if use_row_max:
                        sh = sh + bias_ref[variant, h]
                        snk = sink_ref[h] * LOG2E
                        m = jnp.maximum(jnp.max(sh, axis=-1, keepdims=True), snk)
                        ps.append(jnp.exp2(sh - m).astype(bf16))
                        sink_terms.append(jnp.exp2(snk - m))
                    else:
                        ps.append(jnp.exp2(sh + rel_bias_ref[variant, h]).astype(bf16))
                p = jnp.concatenate(ps, axis=0)
                res = jnp.dot(p, vg, preferred_element_type=f32)
                row_sum = res[:, LANES:2 * LANES]
                sink_term = jnp.concatenate(sink_terms, axis=0) if use_row_max else 1.0
                on = res[:, 0:LANES] / (row_sum + sink_term)
                over = jnp.maximum(over, jnp.where(row_sum <= ROW_SUM_LIMIT, 0.0, 1.0))
                for t in range(group // 2):
                    even = on[(2 * t) * A_BLOCK:(2 * t + 1) * A_BLOCK]
                    odd = on[(2 * t + 1) * A_BLOCK:(2 * t + 2) * A_BLOCK]
                    pair = (g * group) // 2 + t
                    o_ref[b * A_BLOCK:(b + 1) * A_BLOCK, pair * LANES:(pair + 1) * LANES] = (
                        jnp.where(low, even, odd).astype(o_ref.dtype))
        return over

    over = attend(use_row_max=False)

    @pl.when(jnp.max(over) > 0.0)
    def _():
        attend(use_row_max=True)


def _diff_attn_kernel(lamv_ref, g_ref, q_ref, k_ref, vt_ref, bias_ref, o_ref,
                      qz_ref, qp_ref, p_ref, psum_ref, r_ref, flag_ref, any_ref, acc_ref, den_ref,
                      *, nchunks, lambda_init):
    TQ = B_QTILE
    TK = B_TILE
    i = pl.program_id(1)
    f32 = jnp.float32
    bf16 = jnp.bfloat16

    qt = q_ref[...].astype(f32).T
    row = lax.broadcasted_iota(jnp.int32, (LANES, TQ), 0)
    is_q = (row < HEAD_DIM, row >= HEAD_DIM)
    ref_row = (HEAD_DIM, 0)

    ksub = TK // B_BIAS_BLOCK
    qsub = TQ // B_BIAS_BLOCK

    def bias_tile(j, rows=ksub):
        base = j * ksub - i * qsub
        by_dist = {d: bias_ref[0, jnp.clip(base + d, -2, 2) + 2] for d in range(1 - qsub, rows)}
        return jnp.concatenate(
            [jnp.concatenate([by_dist[a - b] for b in range(qsub)], axis=1) for a in range(rows)], axis=0)

    def k_chunk(j, c):
        return k_ref[pl.ds(pl.multiple_of(j * TK, TK), TK), c * LANES:(c + 1) * LANES]

    def set_reference(c, r_row):
        hi = r_row.astype(bf16).astype(f32)
        rem = r_row - hi
        mid = rem.astype(bf16).astype(f32)
        low = (rem - mid).astype(bf16).astype(f32)
        first = ref_row[c]
        ext = jnp.where(row == first, -hi, jnp.where(row == first + 1, -mid,
                                                     jnp.where(row == first + 2, -low, 0.0)))
        qp_ref[c] = jnp.where(is_q[c], qt, ext).astype(bf16)

    def exact_chunk(j, c, flagged):
        s = jnp.dot(k_chunk(j, c), qz_ref[c], preferred_element_type=f32) + bias_tile(j)
        r_old = r_ref[c]
        r_new = jnp.where(flagged, jnp.maximum(r_old, jnp.max(s, axis=0, keepdims=True)), r_old)
        p = jnp.where(flagged, jnp.exp2(s - r_new), 0.0)
        pv = jnp.dot(vt_ref[0, j], p.astype(bf16), preferred_element_type=f32)
        alpha = jnp.exp2(r_old - r_new)
        acc_ref[c] = acc_ref[c] * alpha + pv
        den_ref[c] = den_ref[c] * alpha + jnp.sum(p, axis=0, keepdims=True)
        r_ref[c] = r_new

    def stage_a(j, slot):
        bias = bias_tile(j)
        for c in range(2):
            s = jnp.dot(k_chunk(j, c), qp_ref[c], preferred_element_type=f32) + bias
            p = jnp.exp2(s)
            psum_ref[slot, c] = jnp.sum(p, axis=0, keepdims=True)
            p_ref[slot, c] = p.astype(bf16)

    def stage_c(j, slot):
        vt = vt_ref[0, j]
        for c in range(2):
            pv = jnp.dot(vt, p_ref[slot, c], preferred_element_type=f32)
            psum = psum_ref[slot, c]
            ok = psum <= ROW_SUM_LIMIT
            acc_ref[c] = acc_ref[c] + jnp.where(ok, pv, 0.0)
            den_ref[c] = den_ref[c] + jnp.where(ok, psum, 0.0)
            flag = jnp.where(ok, 0.0, 1.0)
            flag_ref[c, pl.ds(j, 1), :] = flag
            any_ref[c] = jnp.maximum(any_ref[c], flag)

    for c in range(2):
        qz_ref[c] = jnp.where(is_q[c], qt, 0.0).astype(bf16)
    acc_ref[...] = jnp.zeros(acc_ref.shape, f32)
    den_ref[...] = jnp.zeros(den_ref.shape, f32)
    any_ref[...] = jnp.zeros(any_ref.shape, f32)
    diag = i * (TQ // TK)
    probe_bias = bias_tile(diag, rows=B_PROBE // B_BIAS_BLOCK)
    for c in range(2):
        kp = k_ref[pl.ds(pl.multiple_of(diag * TK, TK), B_PROBE), c * LANES:(c + 1) * LANES]
        s = jnp.dot(kp, qz_ref[c], preferred_element_type=f32) + probe_bias
        r = jnp.max(s, axis=0, keepdims=True)
        r_ref[c] = r
        set_reference(c, r)

    def run(j0, has_next):
        for u in range(B_UNROLL):
            stage_c(j0 + u, u % B_SLOTS)
            if u + B_AHEAD < B_UNROLL or has_next:
                stage_a(j0 + u + B_AHEAD, (u + B_AHEAD) % B_SLOTS)

    nbody = nchunks // B_UNROLL
    for u in range(B_AHEAD):
        stage_a(u, u)

    def body(b, carry):
        run(b * B_UNROLL, True)
        return carry

    lax.fori_loop(0, nbody - 1, body, 0)
    run((nbody - 1) * B_UNROLL, False)

    @pl.when(jnp.max(jnp.maximum(any_ref[0], any_ref[1])) > 0.0)
    def _():
        def redo(j, carry):
            @pl.when(jnp.max(jnp.maximum(flag_ref[0, pl.ds(j, 1), :], flag_ref[1, pl.ds(j, 1), :])) > 0.0)
            def _():
                for c in range(2):
                    exact_chunk(j, c, flag_ref[c, pl.ds(j, 1), :] > 0.0)
            return carry

        lax.fori_loop(0, nchunks, redo, 0)

    lv = lamv_ref[...]
    lam = (jnp.exp(jnp.sum(lv[0:1] * lv[1:2], axis=-1, keepdims=True))
           - jnp.exp(jnp.sum(lv[2:3] * lv[3:4], axis=-1, keepdims=True)) + lambda_init)
    ot = acc_ref[0] / den_ref[0] - lam * (acc_ref[1] / den_ref[1])
    o = ot.T
    o_ref[...] = (_rms(o, g_ref[...]) * (1.0 - lambda_init)).astype(o_ref.dtype)


def _out_ffn_kernel(ya_ref, yb_ref, x_ref, woa_ref, wob_ref, gpost_ref, gpre_ref,
                    wg_ref, wu_ref, wd_ref, gfpost_ref, o_ref, f_ref):
    y = (jnp.dot(ya_ref[...], woa_ref[...], preferred_element_type=jnp.float32)
         + jnp.dot(yb_ref[...], wob_ref[...], preferred_element_type=jnp.float32))
    h1 = x_ref[...] + _rms(y, gpost_ref[...])
    u = _rms(h1, gpre_ref[...]).astype(jnp.bfloat16)
    d_ff = wg_ref.shape[1]
    for c in range(d_ff // FFN_CHUNK):
        sl = slice(c * FFN_CHUNK, (c + 1) * FFN_CHUNK)
        gate = jnp.dot(u, wg_ref[:, sl], preferred_element_type=jnp.float32)
        up = jnp.dot(u, wu_ref[:, sl], preferred_element_type=jnp.float32)
        act = (gate * jax.nn.sigmoid(gate) * up).astype(jnp.bfloat16)
        part = jnp.dot(act, wd_ref[sl, :], preferred_element_type=jnp.float32)
        if c == 0:
            f_ref[...] = part
        else:
            f_ref[...] += part
    o_ref[...] = h1 + _rms(f_ref[...], gfpost_ref[...])


def _resident(shape):
    zeros = (0,) * len(shape)
    return pl.BlockSpec(shape, lambda *_: zeros, pipeline_mode=pl.Buffered(1))


def _layer(h, l, p, bias_a, bias_b):
    S, D = h.shape
    bf16 = jnp.bfloat16
    lambda_init = 0.8 - 0.6 * math.exp(-0.3 * l)
    scale = HEAD_DIM ** -0.5

    w = p["w_in"]
    a_q, a_kv, b_qk = A_Q_HEADS * HEAD_DIM, A_KV_HEADS * HEAD_DIM, B_HEADS * 2 * HEAD_DIM
    c0 = a_q
    c1 = c0 + a_kv
    c2 = c1 + a_kv
    c3 = c2 + b_qk

    def dup(cols):
        parts = []
        for g in range(A_KV_HEADS):
            blk = cols[:, g * HEAD_DIM:(g + 1) * HEAD_DIM]
            parts += [blk, blk]
        return jnp.concatenate(parts, axis=1)

    kones = np.zeros((B_HEADS, 2, LANES), np.float32)
    kones[:, 0, HEAD_DIM:HEAD_DIM + B_REF_LANES] = 1.0
    kones[:, 1, 0:B_REF_LANES] = 1.0
    kones = jnp.asarray(kones.reshape(1, B_HEADS * 2 * LANES))

    w_cat = jnp.concatenate([w[:, :c0] * (scale * LOG2E), dup(w[:, c0:c1]), dup(w[:, c1:c2]),
                             w[:, c2:c3] * (scale * LOG2E), w[:, c3:]],
                            axis=1).astype(bf16)
    ncols = w_cat.shape[1]
    assert ncols == PROJ_COLS and w.shape[1] == c3 + 2 * B_WIDTH
    nrow = S // ROW_TILE
    nchunks = S // B_TILE
    assert ROW_TILE % B_TILE == 0 and S % ROW_TILE == 0 and nchunks % B_UNROLL == 0
    assert B_UNROLL % B_SLOTS == 0 and B_AHEAD < B_SLOTS and B_AHEAD <= B_UNROLL
    kb_cols = B_HEADS * 2 * LANES

    qa, ka, va, qb, kb, vt = pl.pallas_call(
        _in_proj_kernel,
        grid=(nrow,),
        in_specs=[pl.BlockSpec((ROW_TILE, D), lambda i: (i, 0)),
                  _resident((1, D)),
                  _resident((D, ncols)),
                  _resident((1, kb_cols))],
        out_specs=[pl.BlockSpec((ROW_TILE, A_WIDTH), lambda i: (i, 0)),
                   pl.BlockSpec((ROW_TILE, A_KV_WIDTH), lambda i: (i, 0)),
                   pl.BlockSpec((ROW_TILE, A_KV_WIDTH), lambda i: (i, 0)),
                   pl.BlockSpec((ROW_TILE, B_WIDTH), lambda i: (i, 0)),
                   pl.BlockSpec((ROW_TILE, kb_cols), lambda i: (i, 0)),
                   pl.BlockSpec((B_HEADS, ROW_TILE // B_TILE, B_V_DIM, B_TILE), lambda i: (0, i, 0, 0))],
        out_shape=[jax.ShapeDtypeStruct((S, A_WIDTH), bf16),
                   jax.ShapeDtypeStruct((S, A_KV_WIDTH), bf16),
                   jax.ShapeDtypeStruct((S, A_KV_WIDTH), bf16),
                   jax.ShapeDtypeStruct((S, B_WIDTH), bf16),
                   jax.ShapeDtypeStruct((S, kb_cols), bf16),
                   jax.ShapeDtypeStruct((B_HEADS, nchunks, B_V_DIM, B_TILE), bf16)],
        compiler_params=pltpu.CompilerParams(dimension_semantics=("arbitrary",),
                                             vmem_limit_bytes=VMEM_LIMIT),
        name="in_proj",
    )(h, p["attn_pre_g"].reshape(1, D), w_cat, kones)

    nblocks = S // A_BLOCK
    a_rows = A_STEP_BLOCKS * A_BLOCK
    assert nblocks % A_STEP_BLOCKS == 0 and A_STEP_BLOCKS >= 2
    rel_bias_a = bias_a - (p["a_sink"].astype(jnp.float32) * LOG2E)[None, :, None, None]
    ya = pl.pallas_call(
        functools.partial(_win_attn_kernel, nblocks=nblocks),
        grid=(nblocks // A_STEP_BLOCKS,),
        in_specs=[pl.BlockSpec(memory_space=pltpu.SMEM),
                  pl.BlockSpec((a_rows, A_WIDTH), lambda n: (n, 0)),
                  _resident(ka.shape),
                  _resident(va.shape),
                  _resident(bias_a.shape),
                  _resident(bias_a.shape)],
        out_specs=pl.BlockSpec((a_rows, A_WIDTH), lambda n: (n, 0)),
        out_shape=jax.ShapeDtypeStruct((S, A_WIDTH), bf16),
        compiler_params=pltpu.CompilerParams(dimension_semantics=("arbitrary",),
                                             vmem_limit_bytes=VMEM_LIMIT),
        name="win_attn",
    )(p["a_sink"], qa, ka, va, bias_a, rel_bias_a)

    lamv = jnp.stack([p["lambda_q1"], p["lambda_k1"], p["lambda_q2"], p["lambda_k2"]])
    TQ, TK = B_QTILE, B_TILE
    assert S % TQ == 0 and TQ % TK == 0
    yb = pl.pallas_call(
        functools.partial(_diff_attn_kernel, nchunks=nchunks, lambda_init=lambda_init),
        grid=(B_HEADS, S // TQ),
        in_specs=[pl.BlockSpec((4, HEAD_DIM), lambda hh, i: (0, 0)),
                  pl.BlockSpec((1, B_V_DIM), lambda hh, i: (0, 0)),
                  pl.BlockSpec((TQ, LANES), lambda hh, i: (i, hh)),
                  pl.BlockSpec((S, 2 * LANES), lambda hh, i: (0, hh), pipeline_mode=pl.Buffered(1)),
                  pl.BlockSpec((1, nchunks, B_V_DIM, TK), lambda hh, i: (hh, 0, 0, 0),
                               pipeline_mode=pl.Buffered(1)),
                  pl.BlockSpec((1, 5, B_BIAS_BLOCK, B_BIAS_BLOCK), lambda hh, i: (hh, 0, 0, 0))],
        out_specs=pl.BlockSpec((TQ, LANES), lambda hh, i: (i, hh)),
        out_shape=jax.ShapeDtypeStruct((S, B_HEADS * B_V_DIM), bf16),
        scratch_shapes=[pltpu.VMEM((2, LANES, TQ), bf16),
                        pltpu.VMEM((2, LANES, TQ), bf16),
                        pltpu.VMEM((B_SLOTS, 2, TK, TQ), bf16),
                        pltpu.VMEM((B_SLOTS, 2, 1, TQ), jnp.float32),
                        pltpu.VMEM((2, 1, TQ), jnp.float32),
                        pltpu.VMEM((2, nchunks, TQ), jnp.float32),
                        pltpu.VMEM((2, 1, TQ), jnp.float32),
                        pltpu.VMEM((2, B_V_DIM, TQ), jnp.float32),
                        pltpu.VMEM((2, 1, TQ), jnp.float32)],
        compiler_params=pltpu.CompilerParams(dimension_semantics=("arbitrary", "arbitrary"),
                                             vmem_limit_bytes=VMEM_LIMIT),
        name="diff_attn",
    )(lamv, p["diff_subln_g"].reshape(1, B_V_DIM), qb, kb, vt, bias_b)

    w_out = p["w_out"].astype(bf16)
    a_width = A_WIDTH
    assert w_out.shape[0] == A_WIDTH + B_WIDTH
    d_ff = p["w_gate"].shape[1]
    assert d_ff % FFN_CHUNK == 0 and S % FFN_ROW_TILE == 0
    R = FFN_ROW_TILE
    out = pl.pallas_call(
        _out_ffn_kernel,
        grid=(S // R,),
        in_specs=[pl.BlockSpec((R, A_WIDTH), lambda i: (i, 0)),
                  pl.BlockSpec((R, B_WIDTH), lambda i: (i, 0)),
                  pl.BlockSpec((R, D), lambda i: (i, 0)),
                  _resident((a_width, D)),
                  _resident((w_out.shape[0] - a_width, D)),
                  _resident((1, D)),
                  _resident((1, D)),
                  _resident((D, d_ff)),
                  _resident((D, d_ff)),
                  _resident((d_ff, D)),
                  _resident((1, D))],
        out_specs=pl.BlockSpec((R, D), lambda i: (i, 0)),
        out_shape=jax.ShapeDtypeStruct((S, D), jnp.float32),
        scratch_shapes=[pltpu.VMEM((R, D), jnp.float32)],
        compiler_params=pltpu.CompilerParams(dimension_semantics=("arbitrary",),
                                             vmem_limit_bytes=VMEM_LIMIT),
        name="out_ffn",
    )(ya, yb, h, w_out[:a_width], w_out[a_width:], p["attn_post_g"].reshape(1, D),
      p["ffn_pre_g"].reshape(1, D), p["w_gate"].astype(bf16), p["w_up"].astype(bf16),
      p["w_down"].astype(bf16), p["ffn_post_g"].reshape(1, D))
    return out


def kernel(x, attn_pre_g, w_in, a_sink, lambda_q1, lambda_k1, lambda_q2, lambda_k2, diff_subln_g,
           rel_bias, w_out, attn_post_g, ffn_pre_g, w_gate, w_up, w_down, ffn_post_g):
    batch, S, D = x.shape
    depth = w_in.shape[0]
    assert S % B_TILE == 0 and S % A_BLOCK == 0 and S // A_BLOCK >= 2

    tab_a = _toeplitz_bias(rel_bias[:, :A_Q_HEADS], A_BLOCK, 3 * A_BLOCK, lambda t: -t - A_BLOCK)
    qi = np.arange(A_BLOCK)[:, None]
    kj = np.arange(3 * A_BLOCK)[None, :]
    in_window = np.abs(kj - A_BLOCK - qi) <= A_BLOCK
    valid = np.stack([in_window & (kj >= A_BLOCK), in_window, in_window & (kj < 2 * A_BLOCK)])
    bias_a = jnp.where(valid[:, None], tab_a[None] * LOG2E, MASK_VALUE)

    nb = B_BIAS_BLOCK
    assert len(set(_t5_bucket_np(np.arange(nb + 1, 4 * nb)))) == 1
    tab_b = rel_bias[:, A_Q_HEADS:]
    near = [_toeplitz_bias(tab_b, nb, nb, lambda t, d=d: d * nb + t) for d in (-1, 0, 1)]
    far = tab_b[_t5_bucket_np(np.array([-2 * nb, 2 * nb]))].astype(jnp.float32)
    const = [jnp.broadcast_to(far[side][:, None, None], (B_HEADS, nb, nb)) for side in (0, 1)]
    bias_b = jnp.stack([const[0]] + near + [const[1]], axis=1) * LOG2E

    outs = []
    for b in range(batch):
        h = x[b]
        for l in range(depth):
            p = dict(attn_pre_g=attn_pre_g[l], w_in=w_in[l], a_sink=a_sink[l], lambda_q1=lambda_q1[l],
                     lambda_k1=lambda_k1[l], lambda_q2=lambda_q2[l], lambda_k2=lambda_k2[l],
                     diff_subln_g=diff_subln_g[l], w_out=w_out[l], attn_post_g=attn_post_g[l],
                     ffn_pre_g=ffn_pre_g[l], w_gate=w_gate[l], w_up=w_up[l], w_down=w_down[l],
                     ffn_post_g=ffn_post_g[l])
            h = _layer(h, l, p, bias_a, bias_b)
        outs.append(h)
    return jnp.stack(outs)
```

```python
import functools
import math

import numpy as np
import jax
import jax.numpy as jnp
from jax import lax
from jax.experimental import pallas as pl
from jax.experimental.pallas import tpu as pltpu

HEAD_DIM = 64
A_Q_HEADS = 8
A_KV_HEADS = 2
A_BLOCK = 128
A_STEP_BLOCKS = 8
B_HEADS = 4
B_V_DIM = 2 * HEAD_DIM
NUM_BUCKETS = 32
MAX_DISTANCE = 128
EPS = 1e-6
MASK_VALUE = -1e30
LOG2E = math.log2(math.e)
ROW_SUM_LIMIT = 2.0 ** 40

A_WIDTH = A_Q_HEADS * HEAD_DIM
A_KV_WIDTH = 2 * A_KV_HEADS * HEAD_DIM
B_WIDTH = B_HEADS * B_V_DIM
COL_KA = A_WIDTH
COL_VA = COL_KA + A_KV_WIDTH
COL_QB = COL_VA + A_KV_WIDTH
COL_KB = COL_QB + B_WIDTH
COL_VB = COL_KB + B_WIDTH
PROJ_COLS = COL_VB + B_WIDTH

LANES = 128
ROW_TILE = 1024
FFN_ROW_TILE = 1024
FFN_CHUNK = 256
B_TILE = 1024
B_QTILE = 2048
B_REF_LANES = 3
B_UNROLL = 2
B_AHEAD = 1
B_SLOTS = 2
B_PROBE = 128
B_BIAS_BLOCK = 128
VMEM_LIMIT = 56 * 1024 * 1024

_NT = (((1,), (1,)), ((), ()))


def _t5_bucket_np(rel):
    nb = NUM_BUCKETS // 2
    max_exact = nb // 2
    ret = np.where(rel > 0, nb, 0)
    n = np.abs(rel)
    nf = np.maximum(n, 1).astype(np.float32)
    large = max_exact + (np.log(nf / np.float32(max_exact)) / np.float32(math.log(MAX_DISTANCE / max_exact))
                         * np.float32(nb - max_exact)).astype(np.int32)
    large = np.minimum(large, nb - 1)
    return (ret + np.where(n < max_exact, n, large)).astype(np.int32)


def _toeplitz_bias(table, rows, cols, rel_of):
    length = rows + cols
    u = np.arange(length)
    t = np.where(u < cols, -u, length - u)
    w = table[_t5_bucket_np(rel_of(t))].astype(jnp.float32).T
    x = jnp.tile(w, (1, rows))[:, :rows * (length - 1)].reshape(w.shape[0], rows, length - 1)
    return x[:, :, :cols]


def _rms(xf, g):
    return xf * lax.rsqrt(jnp.mean(xf * xf, axis=-1, keepdims=True) + EPS) * g


def _in_proj_kernel(x_ref, g_ref, w_ref, kones_ref, qa_ref, ka_ref, va_ref, qb_ref, kb_ref, vt_ref):
    u = _rms(x_ref[...], g_ref[...]).astype(jnp.bfloat16)
    proj = jnp.dot(u, w_ref[...], preferred_element_type=jnp.float32)
    qa_ref[...] = proj[:, 0:COL_KA].astype(jnp.bfloat16)
    ka_ref[...] = proj[:, COL_KA:COL_VA].astype(jnp.bfloat16)
    va_ref[...] = proj[:, COL_VA:COL_QB].astype(jnp.bfloat16)
    qb_ref[...] = proj[:, COL_QB:COL_KB].astype(jnp.bfloat16)
    lane = lax.broadcasted_iota(jnp.int32, (proj.shape[0], LANES), 1)
    keep = (lane < HEAD_DIM, lane >= HEAD_DIM)
    for h in range(B_HEADS):
        kpair = proj[:, COL_KB + h * LANES:COL_KB + (h + 1) * LANES]
        for c in range(2):
            col = (2 * h + c) * LANES
            kb_ref[:, col:col + LANES] = jnp.where(keep[c], kpair, kones_ref[:, col:col + LANES]).astype(jnp.bfloat16)
        vt = proj[:, COL_VB + h * B_V_DIM:COL_VB + (h + 1) * B_V_DIM].T.astype(jnp.bfloat16)
        for t in range(vt_ref.shape[1]):
            vt_ref[h, t] = vt[:, t * B_TILE:(t + 1) * B_TILE]


def _win_attn_kernel(sink_ref, q_ref, k_ref, v_ref, bias_ref, rel_bias_ref, o_ref, *, nblocks):
    f32 = jnp.float32
    bf16 = jnp.bfloat16
    n = pl.program_id(0)
    rows = A_STEP_BLOCKS * A_BLOCK
    seq = k_ref.shape[0]
    group = A_Q_HEADS // A_KV_HEADS
    start = pl.multiple_of(n * rows, rows)
    prev = pl.multiple_of(jnp.maximum(start - A_BLOCK, 0), A_BLOCK)
    nxt = pl.multiple_of(jnp.minimum(start + rows, seq - A_BLOCK), A_BLOCK)
    kw = jnp.concatenate([k_ref[pl.ds(prev, A_BLOCK), :], k_ref[pl.ds(start, rows), :],
                          k_ref[pl.ds(nxt, A_BLOCK), :]], axis=0)
    vw = jnp.concatenate([v_ref[pl.ds(prev, A_BLOCK), :], v_ref[pl.ds(start, rows), :],
                          v_ref[pl.ds(nxt, A_BLOCK), :]], axis=0)
    lane = lax.broadcasted_iota(jnp.int32, (A_BLOCK, LANES), 1)
    low = lane < HEAD_DIM
    ones = jnp.ones((3 * A_BLOCK, LANES), bf16)

    def attend(use_row_max):
        over = jnp.zeros((group * A_BLOCK, LANES), f32)
        for b in range(A_STEP_BLOCKS):
            blk = n * A_STEP_BLOCKS + b
            if b == 0:
                variant = jnp.where(blk == 0, 0, 1)
            elif b == A_STEP_BLOCKS - 1:
                variant = jnp.where(blk == nblocks - 1, 2, 1)
            else:
                variant = 1
            q = q_ref[b * A_BLOCK:(b + 1) * A_BLOCK, :]
            for g in range(A_KV_HEADS):
                heads = range(g * group, (g + 1) * group)
                qs = []
                for h in heads:
                    q2 = q[:, (h // 2) * LANES:(h // 2 + 1) * LANES]
                    qs.append(jnp.where(low if h % 2 == 0 else jnp.logical_not(low), q2, jnp.zeros_like(q2)))
                q4 = jnp.concatenate(qs, axis=0)
                kg = kw[b * A_BLOCK:(b + 3) * A_BLOCK, g * LANES:(g + 1) * LANES]
                vg = jnp.concatenate([vw[b * A_BLOCK:(b + 3) * A_BLOCK, g * LANES:(g + 1) * LANES], ones],
                                     axis=1)
                s = lax.dot_general(q4, kg, _NT, preferred_element_type=f32)
                ps, sink_terms = [], []
                for t, h in enumerate(heads):
                    sh = s[t * A_BLOCK:(t + 1) * A_BLOCK]
                    if use_row_max:
                        sh = sh + bias_ref[variant, h]
                        snk = sink_ref[h] * LOG2E
                        m = jnp.maximum(jnp.max(sh, axis=-1, keepdims=True), snk)
                        ps.append(jnp.exp2(sh - m).astype(bf16))
                        sink_terms.append(jnp.exp2(snk - m))
                    else:
                        ps.append(jnp.exp2(sh + rel_bias_ref[variant, h]).astype(bf16))
                p = jnp.concatenate(ps, axis=0)
                res = jnp.dot(p, vg, preferred_element_type=f32)
                row_sum = res[:, LANES:2 * LANES]
                sink_term = jnp.concatenate(sink_terms, axis=0) if use_row_max else 1.0
                on = res[:, 0:LANES] / (row_sum + sink_term)
                over = jnp.maximum(over, jnp.where(row_sum <= ROW_SUM_LIMIT, 0.0, 1.0))
                for t in range(group // 2):
                    even = on[(2 * t) * A_BLOCK:(2 * t + 1) * A_BLOCK]
                    odd = on[(2 * t + 1) * A_BLOCK:(2 * t + 2) * A_BLOCK]
                    pair = (g * group) // 2 + t
                    o_ref[b * A_BLOCK:(b + 1) * A_BLOCK, pair * LANES:(pair + 1) * LANES] = (
                        jnp.where(low, even, odd).astype(o_ref.dtype))
        return over

    over = attend(use_row_max=False)

    @pl.when(jnp.max(over) > 0.0)
    def _():
        attend(use_row_max=True)


def _diff_attn_kernel(lamv_ref, g_ref, q_ref, k_ref, vt_ref, bias_ref, o_ref,
                      qz_ref, qp_ref, p_ref, psum_ref, r_ref, flag_ref, any_ref, acc_ref, den_ref,
                      *, nchunks, lambda_init):
    TQ = B_QTILE
    TK = B_TILE
    i = pl.program_id(1)
    f32 = jnp.float32
    bf16 = jnp.bfloat16

    qt = q_ref[...].astype(f32).T
    row = lax.broadcasted_iota(jnp.int32, (LANES, TQ), 0)
    is_q = (row < HEAD_DIM, row >= HEAD_DIM)
    ref_row = (HEAD_DIM, 0)

    ksub = TK // B_BIAS_BLOCK
    qsub = TQ // B_BIAS_BLOCK

    def bias_tile(j, rows=ksub):
        base = j * ksub - i * qsub
        by_dist = {d: bias_ref[0, jnp.clip(base + d, -2, 2) + 2] for d in range(1 - qsub, rows)}
        return jnp.concatenate(
            [jnp.concatenate([by_dist[a - b] for b in range(qsub)], axis=1) for a in range(rows)], axis=0)

    def k_chunk(j, c):
        return k_ref[pl.ds(pl.multiple_of(j * TK, TK), TK), c * LANES:(c + 1) * LANES]

    def set_reference(c, r_row):
        hi = r_row.astype(bf16).astype(f32)
        rem = r_row - hi
        mid = rem.astype(bf16).astype(f32)
        low = (rem - mid).astype(bf16).astype(f32)
        first = ref_row[c]
        ext = jnp.where(row == first, -hi, jnp.where(row == first + 1, -mid,
                                                     jnp.where(row == first + 2, -low, 0.0)))
        qp_ref[c] = jnp.where(is_q[c], qt, ext).astype(bf16)

    def exact_chunk(j, c, flagged):
        s = jnp.dot(k_chunk(j, c), qz_ref[c], preferred_element_type=f32) + bias_tile(j)
        r_old = r_ref[c]
        r_new = jnp.where(flagged, jnp.maximum(r_old, jnp.max(s, axis=0, keepdims=True)), r_old)
        p = jnp.where(flagged, jnp.exp2(s - r_new), 0.0)
        pv = jnp.dot(vt_ref[0, j], p.astype(bf16), preferred_element_type=f32)
        alpha = jnp.exp2(r_old - r_new)
        acc_ref[c] = acc_ref[c] * alpha + pv
        den_ref[c] = den_ref[c] * alpha + jnp.sum(p, axis=0, keepdims=True)
        r_ref[c] = r_new

    def stage_a(j, slot):
        bias = bias_tile(j)
        for c in range(2):
            s = jnp.dot(k_chunk(j, c), qp_ref[c], preferred_element_type=f32) + bias
            p = jnp.exp2(s)
            psum_ref[slot, c] = jnp.sum(p, axis=0, keepdims=True)
            p_ref[slot, c] = p.astype(bf16)

    def stage_c(j, slot):
        vt = vt_ref[0, j]
        for c in range(2):
            pv = jnp.dot(vt, p_ref[slot, c], preferred_element_type=f32)
            psum = psum_ref[slot, c]
            ok = psum <= ROW_SUM_LIMIT
            acc_ref[c] = acc_ref[c] + jnp.where(ok, pv, 0.0)
            den_ref[c] = den_ref[c] + jnp.where(ok, psum, 0.0)
            flag = jnp.where(ok, 0.0, 1.0)
            flag_ref[c, pl.ds(j, 1), :] = flag
            any_ref[c] = jnp.maximum(any_ref[c], flag)

    for c in range(2):
        qz_ref[c] = jnp.where(is_q[c], qt, 0.0).astype(bf16)
    acc_ref[...] = jnp.zeros(acc_ref.shape, f32)
    den_ref[...] = jnp.zeros(den_ref.shape, f32)
    any_ref[...] = jnp.zeros(any_ref.shape, f32)
    diag = i * (TQ // TK)
    probe_bias = bias_tile(diag, rows=B_PROBE // B_BIAS_BLOCK)
    for c in range(2):
        kp = k_ref[pl.ds(pl.multiple_of(diag * TK, TK), B_PROBE), c * LANES:(c + 1) * LANES]
        s = jnp.dot(kp, qz_ref[c], preferred_element_type=f32) + probe_bias
        r = jnp.max(s, axis=0, keepdims=True)
        r_ref[c] = r
        set_reference(c, r)

    def run(j0, has_next):
        for u in range(B_UNROLL):
            stage_c(j0 + u, u % B_SLOTS)
            if u + B_AHEAD < B_UNROLL or has_next:
                stage_a(j0 + u + B_AHEAD, (u + B_AHEAD) % B_SLOTS)

    nbody = nchunks // B_UNROLL
    for u in range(B_AHEAD):
        stage_a(u, u)

    def body(b, carry):
        run(b * B_UNROLL, True)
        return carry

    lax.fori_loop(0, nbody - 1, body, 0)
    run((nbody - 1) * B_UNROLL, False)

    @pl.when(jnp.max(jnp.maximum(any_ref[0], any_ref[1])) > 0.0)
    def _():
        def redo(j, carry):
            @pl.when(jnp.max(jnp.maximum(flag_ref[0, pl.ds(j, 1), :], flag_ref[1, pl.ds(j, 1), :])) > 0.0)
            def _():
                for c in range(2):
                    exact_chunk(j, c, flag_ref[c, pl.ds(j, 1), :] > 0.0)
            return carry

        lax.fori_loop(0, nchunks, redo, 0)

    lv = lamv_ref[...]
    lam = (jnp.exp(jnp.sum(lv[0:1] * lv[1:2], axis=-1, keepdims=True))
           - jnp.exp(jnp.sum(lv[2:3] * lv[3:4], axis=-1, keepdims=True)) + lambda_init)
    ot = acc_ref[0] / den_ref[0] - lam * (acc_ref[1] / den_ref[1])
    o = ot.T
    o_ref[...] = (_rms(o, g_ref[...]) * (1.0 - lambda_init)).astype(o_ref.dtype)


def _out_ffn_kernel(ya_ref, yb_ref, x_ref, woa_ref, wob_ref, gpost_ref, gpre_ref,
                    wg_ref, wu_ref, wd_ref, gfpost_ref, o_ref, f_ref):
    y = (jnp.dot(ya_ref[...], woa_ref[...], preferred_element_type=jnp.float32)
         + jnp.dot(yb_ref[...], wob_ref[...], preferred_element_type=jnp.float32))
    h1 = x_ref[...] + _rms(y, gpost_ref[...])
    u = _rms(h1, gpre_ref[...]).astype(jnp.bfloat16)
    d_ff = wg_ref.shape[1]
    for c in range(d_ff // FFN_CHUNK):
        sl = slice(c * FFN_CHUNK, (c + 1) * FFN_CHUNK)
        gate = jnp.dot(u, wg_ref[:, sl], preferred_element_type=jnp.float32)
        up = jnp.dot(u, wu_ref[:, sl], preferred_element_type=jnp.float32)
        act = (gate * jax.nn.sigmoid(gate) * up).astype(jnp.bfloat16)
        part = jnp.dot(act, wd_ref[sl, :], preferred_element_type=jnp.float32)
        if c == 0:
            f_ref[...] = part
        else:
            f_ref[...] += part
    o_ref[...] = h1 + _rms(f_ref[...], gfpost_ref[...])


def _resident(shape):
    zeros = (0,) * len(shape)
    return pl.BlockSpec(shape, lambda *_: zeros, pipeline_mode=pl.Buffered(1))


def _layer(h, l, p, bias_a, bias_b):
    S, D = h.shape
    bf16 = jnp.bfloat16
    lambda_init = 0.8 - 0.6 * math.exp(-0.3 * l)
    scale = HEAD_DIM ** -0.5

    w = p["w_in"]
    a_q, a_kv, b_qk = A_Q_HEADS * HEAD_DIM, A_KV_HEADS * HEAD_DIM, B_HEADS * 2 * HEAD_DIM
    c0 = a_q
    c1 = c0 + a_kv
    c2 = c1 + a_kv
    c3 = c2 + b_qk

    def dup(cols):
        parts = []
        for g in range(A_KV_HEADS):
            blk = cols[:, g * HEAD_DIM:(g + 1) * HEAD_DIM]
            parts += [blk, blk]
        return jnp.concatenate(parts, axis=1)

    kones = np.zeros((B_HEADS, 2, LANES), np.float32)
    kones[:, 0, HEAD_DIM:HEAD_DIM + B_REF_LANES] = 1.0
    kones[:, 1, 0:B_REF_LANES] = 1.0
    kones = jnp.asarray(kones.reshape(1, B_HEADS * 2 * LANES))

    w_cat = jnp.concatenate([w[:, :c0] * (scale * LOG2E), dup(w[:, c0:c1]), dup(w[:, c1:c2]),
                             w[:, c2:c3] * (scale * LOG2E), w[:, c3:]],
                            axis=1).astype(bf16)
    ncols = w_cat.shape[1]
    assert ncols == PROJ_COLS and w.shape[1] == c3 + 2 * B_WIDTH
    nrow = S // ROW_TILE
    nchunks = S // B_TILE
    assert ROW_TILE % B_TILE == 0 and S % ROW_TILE == 0 and nchunks % B_UNROLL == 0
    assert B_UNROLL % B_SLOTS == 0 and B_AHEAD < B_SLOTS and B_AHEAD <= B_UNROLL
    kb_cols = B_HEADS * 2 * LANES

    qa, ka, va, qb, kb, vt = pl.pallas_call(
        _in_proj_kernel,
        grid=(nrow,),
        in_specs=[pl.BlockSpec((ROW_TILE, D), lambda i: (i, 0)),
                  _resident((1, D)),
                  _resident((D, ncols)),
                  _resident((1, kb_cols))],
        out_specs=[pl.BlockSpec((ROW_TILE, A_WIDTH), lambda i: (i, 0)),
                   pl.BlockSpec((ROW_TILE, A_KV_WIDTH), lambda i: (i, 0)),
                   pl.BlockSpec((ROW_TILE, A_KV_WIDTH), lambda i: (i, 0)),
                   pl.BlockSpec((ROW_TILE, B_WIDTH), lambda i: (i, 0)),
                   pl.BlockSpec((ROW_TILE, kb_cols), lambda i: (i, 0)),
                   pl.BlockSpec((B_HEADS, ROW_TILE // B_TILE, B_V_DIM, B_TILE), lambda i: (0, i, 0, 0))],
        out_shape=[jax.ShapeDtypeStruct((S, A_WIDTH), bf16),
                   jax.ShapeDtypeStruct((S, A_KV_WIDTH), bf16),
                   jax.ShapeDtypeStruct((S, A_KV_WIDTH), bf16),
                   jax.ShapeDtypeStruct((S, B_WIDTH), bf16),
                   jax.ShapeDtypeStruct((S, kb_cols), bf16),
                   jax.ShapeDtypeStruct((B_HEADS, nchunks, B_V_DIM, B_TILE), bf16)],
        compiler_params=pltpu.CompilerParams(dimension_semantics=("arbitrary",),
                                             vmem_limit_bytes=VMEM_LIMIT),
        name="in_proj",
    )(h, p["attn_pre_g"].reshape(1, D), w_cat, kones)

    nblocks = S // A_BLOCK
    a_rows = A_STEP_BLOCKS * A_BLOCK
    assert nblocks % A_STEP_BLOCKS == 0 and A_STEP_BLOCKS >= 2
    rel_bias_a = bias_a - (p["a_sink"].astype(jnp.float32) * LOG2E)[None, :, None, None]
    ya = pl.pallas_call(
        functools.partial(_win_attn_kernel, nblocks=nblocks),
        grid=(nblocks // A_STEP_BLOCKS,),
        in_specs=[pl.BlockSpec(memory_space=pltpu.SMEM),
                  pl.BlockSpec((a_rows, A_WIDTH), lambda n: (n, 0)),
                  _resident(ka.shape),
                  _resident(va.shape),
                  _resident(bias_a.shape),
                  _resident(bias_a.shape)],
        out_specs=pl.BlockSpec((a_rows, A_WIDTH), lambda n: (n, 0)),
        out_shape=jax.ShapeDtypeStruct((S, A_WIDTH), bf16),
        compiler_params=pltpu.CompilerParams(dimension_semantics=("arbitrary",),
                                             vmem_limit_bytes=VMEM_LIMIT),
        name="win_attn",
    )(p["a_sink"], qa, ka, va, bias_a, rel_bias_a)

    lamv = jnp.stack([p["lambda_q1"], p["lambda_k1"], p["lambda_q2"], p["lambda_k2"]])
    TQ, TK = B_QTILE, B_TILE
    assert S % TQ == 0 and TQ % TK == 0
    yb = pl.pallas_call(
        functools.partial(_diff_attn_kernel, nchunks=nchunks, lambda_init=lambda_init),
        grid=(B_HEADS, S // TQ),
        in_specs=[pl.BlockSpec((4, HEAD_DIM), lambda hh, i: (0, 0)),
                  pl.BlockSpec((1, B_V_DIM), lambda hh, i: (0, 0)),
                  pl.BlockSpec((TQ, LANES), lambda hh, i: (i, hh)),
                  pl.BlockSpec((S, 2 * LANES), lambda hh, i: (0, hh), pipeline_mode=pl.Buffered(1)),
                  pl.BlockSpec((1, nchunks, B_V_DIM, TK), lambda hh, i: (hh, 0, 0, 0),
                               pipeline_mode=pl.Buffered(1)),
                  pl.BlockSpec((1, 5, B_BIAS_BLOCK, B_BIAS_BLOCK), lambda hh, i: (hh, 0, 0, 0))],
        out_specs=pl.BlockSpec((TQ, LANES), lambda hh, i: (i, hh)),
        out_shape=jax.ShapeDtypeStruct((S, B_HEADS * B_V_DIM), bf16),
        scratch_shapes=[pltpu.VMEM((2, LANES, TQ), bf16),
                        pltpu.VMEM((2, LANES, TQ), bf16),
                        pltpu.VMEM((B_SLOTS, 2, TK, TQ), bf16),
                        pltpu.VMEM((B_SLOTS, 2, 1, TQ), jnp.float32),
                        pltpu.VMEM((2, 1, TQ), jnp.float32),
                        pltpu.VMEM((2, nchunks, TQ), jnp.float32),
                        pltpu.VMEM((2, 1, TQ), jnp.float32),
                        pltpu.VMEM((2, B_V_DIM, TQ), jnp.float32),
                        pltpu.VMEM((2, 1, TQ), jnp.float32)],
        compiler_params=pltpu.CompilerParams(dimension_semantics=("arbitrary", "arbitrary"),
                                             vmem_limit_bytes=VMEM_LIMIT),
        name="diff_attn",
    )(lamv, p["diff_subln_g"].reshape(1, B_V_DIM), qb, kb, vt, bias_b)

    w_out = p["w_out"].astype(bf16)
    a_width = A_WIDTH
    assert w_out.shape[0] == A_WIDTH + B_WIDTH
    d_ff = p["w_gate"].shape[1]
    assert d_ff % FFN_CHUNK == 0 and S % FFN_ROW_TILE == 0
    R = FFN_ROW_TILE
    out = pl.pallas_call(
        _out_ffn_kernel,
        grid=(S // R,),
        in_specs=[pl.BlockSpec((R, A_WIDTH), lambda i: (i, 0)),
                  pl.BlockSpec((R, B_WIDTH), lambda i: (i, 0)),
                  pl.BlockSpec((R, D), lambda i: (i, 0)),
                  _resident((a_width, D)),
                  _resident((w_out.shape[0] - a_width, D)),
                  _resident((1, D)),
                  _resident((1, D)),
                  _resident((D, d_ff)),
                  _resident((D, d_ff)),
                  _resident((d_ff, D)),
                  _resident((1, D))],
        out_specs=pl.BlockSpec((R, D), lambda i: (i, 0)),
        out_shape=jax.ShapeDtypeStruct((S, D), jnp.float32),
        scratch_shapes=[pltpu.VMEM((R, D), jnp.float32)],
        compiler_params=pltpu.CompilerParams(dimension_semantics=("arbitrary",),
                                             vmem_limit_bytes=VMEM_LIMIT),
        name="out_ffn",
    )(ya, yb, h, w_out[:a_width], w_out[a_width:], p["attn_post_g"].reshape(1, D),
      p["ffn_pre_g"].reshape(1, D), p["w_gate"].astype(bf16), p["w_up"].astype(bf16),
      p["w_down"].astype(bf16), p["ffn_post_g"].reshape(1, D))
    return out


def kernel(x, attn_pre_g, w_in, a_sink, lambda_q1, lambda_k1, lambda_q2, lambda_k2, diff_subln_g,
           rel_bias, w_out, attn_post_g, ffn_pre_g, w_gate, w_up, w_down, ffn_post_g):
    batch, S, D = x.shape
    depth = w_in.shape[0]
    assert S % B_TILE == 0 and S % A_BLOCK == 0 and S // A_BLOCK >= 2

    tab_a = _toeplitz_bias(rel_bias[:, :A_Q_HEADS], A_BLOCK, 3 * A_BLOCK, lambda t: -t - A_BLOCK)
    qi = np.arange(A_BLOCK)[:, None]
    kj = np.arange(3 * A_BLOCK)[None, :]
    in_window = np.abs(kj - A_BLOCK - qi) <= A_BLOCK
    valid = np.stack([in_window & (kj >= A_BLOCK), in_window, in_window & (kj < 2 * A_BLOCK)])
    bias_a = jnp.where(valid[:, None], tab_a[None] * LOG2E, MASK_VALUE)

    nb = B_BIAS_BLOCK
    assert len(set(_t5_bucket_np(np.arange(nb + 1, 4 * nb)))) == 1
    tab_b = rel_bias[:, A_Q_HEADS:]
    near = [_toeplitz_bias(tab_b, nb, nb, lambda t, d=d: d * nb + t) for d in (-1, 0, 1)]
    far = tab_b[_t5_bucket_np(np.array([-2 * nb, 2 * nb]))].astype(jnp.float32)
    const = [jnp.broadcast_to(far[side][:, None, None], (B_HEADS, nb, nb)) for side in (0, 1)]
    bias_b = jnp.stack([const[0]] + near + [const[1]], axis=1) * LOG2E

    outs = []
    for b in range(batch):
        h = x[b]
        for l in range(depth):
            p = dict(attn_pre_g=attn_pre_g[l], w_in=w_in[l], a_sink=a_sink[l], lambda_q1=lambda_q1[l],
                     lambda_k1=lambda_k1[l], lambda_q2=lambda_q2[l], lambda_k2=lambda_k2[l],
                     diff_subln_g=diff_subln_g[l], w_out=w_out[l], attn_post_g=attn_post_g[l],
                     ffn_pre_g=ffn_pre_g[l], w_gate=w_gate[l], w_up=w_up[l], w_down=w_down[l],
                     ffn_post_g=ffn_post_g[l])
            h = _layer(h, l, p, bias_a, bias_b)
        outs.append(h)
    return jnp.stack(outs)
```

```python
import functools
import math

import numpy as np
import jax
import jax.numpy as jnp
from jax import lax
from jax.experimental import pallas as pl
from jax.experimental.pallas import tpu as pltpu

HEAD_DIM = 64
A_Q_HEADS = 8
A_KV_HEADS = 2
A_BLOCK = 128
A_STEP_BLOCKS = 8
B_HEADS = 4
B_V_DIM = 2 * HEAD_DIM
NUM_BUCKETS = 32
MAX_DISTANCE = 128
EPS = 1e-6
MASK_VALUE = -1e30
LOG2E = math.log2(math.e)
ROW_SUM_LIMIT = 2.0 ** 40

A_WIDTH = A_Q_HEADS * HEAD_DIM
A_KV_WIDTH = 2 * A_KV_HEADS * HEAD_DIM
B_WIDTH = B_HEADS * B_V_DIM
COL_KA = A_WIDTH
COL_VA = COL_KA + A_KV_WIDTH
COL_QB = COL_VA + A_KV_WIDTH
COL_KB = COL_QB + B_WIDTH
COL_VB = COL_KB + B_WIDTH
PROJ_COLS = COL_VB + B_WIDTH

LANES = 128
ROW_TILE = 1024
FFN_ROW_TILE = 1024
FFN_CHUNK = 256
B_TILE = 1024
B_QTILE = 2048
B_REF_LANES = 3
B_UNROLL = 2
B_AHEAD = 1
B_SLOTS = 2
B_PROBE = 128
B_BIAS_BLOCK = 128
VMEM_LIMIT = 56 * 1024 * 1024

_NT = (((1,), (1,)), ((), ()))


def _t5_bucket_np(rel):
    nb = NUM_BUCKETS // 2
    max_exact = nb // 2
    ret = np.where(rel > 0, nb, 0)
    n = np.abs(rel)
    nf = np.maximum(n, 1).astype(np.float32)
    large = max_exact + (np.log(nf / np.float32(max_exact)) / np.float32(math.log(MAX_DISTANCE / max_exact))
                         * np.float32(nb - max_exact)).astype(np.int32)
    large = np.minimum(large, nb - 1)
    return (ret + np.where(n < max_exact, n, large)).astype(np.int32)


def _toeplitz_bias(table, rows, cols, rel_of):
    length = rows + cols
    u = np.arange(length)
    t = np.where(u < cols, -u, length - u)
    w = table[_t5_bucket_np(rel_of(t))].astype(jnp.float32).T
    x = jnp.tile(w, (1, rows))[:, :rows * (length - 1)].reshape(w.shape[0], rows, length - 1)
    return x[:, :, :cols]


def _rms(xf, g):
    return xf * lax.rsqrt(jnp.mean(xf * xf, axis=-1, keepdims=True) + EPS) * g


def _in_proj_kernel(x_ref, g_ref, w_ref, kones_ref, qa_ref, ka_ref, va_ref, qb_ref, kb_ref, vt_ref):
    u = _rms(x_ref[...], g_ref[...]).astype(jnp.bfloat16)
    proj = jnp.dot(u, w_ref[...], preferred_element_type=jnp.float32)
    qa_ref[...] = proj[:, 0:COL_KA].astype(jnp.bfloat16)
    ka_ref[...] = proj[:, COL_KA:COL_VA].astype(jnp.bfloat16)
    va_ref[...] = proj[:, COL_VA:COL_QB].astype(jnp.bfloat16)
    qb_ref[...] = proj[:, COL_QB:COL_KB].astype(jnp.bfloat16)
    lane = lax.broadcasted_iota(jnp.int32, (proj.shape[0], LANES), 1)
    keep = (lane < HEAD_DIM, lane >= HEAD_DIM)
    for h in range(B_HEADS):
        kpair = proj[:, COL_KB + h * LANES:COL_KB + (h + 1) * LANES]
        for c in range(2):
            col = (2 * h + c) * LANES
            kb_ref[:, col:col + LANES] = jnp.where(keep[c], kpair, kones_ref[:, col:col + LANES]).astype(jnp.bfloat16)
        vt = proj[:, COL_VB + h * B_V_DIM:COL_VB + (h + 1) * B_V_DIM].T.astype(jnp.bfloat16)
        for t in range(vt_ref.shape[1]):
            vt_ref[h, t] = vt[:, t * B_TILE:(t + 1) * B_TILE]


def _win_attn_kernel(sink_ref, q_ref, k_ref, v_ref, bias_ref, rel_bias_ref, o_ref, *, nblocks):
    f32 = jnp.float32
    bf16 = jnp.bfloat16
    n = pl.program_id(0)
    rows = A_STEP_BLOCKS * A_BLOCK
    seq = k_ref.shape[0]
    group = A_Q_HEADS // A_KV_HEADS
    start = pl.multiple_of(n * rows, rows)
    prev = pl.multiple_of(jnp.maximum(start - A_BLOCK, 0), A_BLOCK)
    nxt = pl.multiple_of(jnp.minimum(start + rows, seq - A_BLOCK), A_BLOCK)
    kw = jnp.concatenate([k_ref[pl.ds(prev, A_BLOCK), :], k_ref[pl.ds(start, rows), :],
                          k_ref[pl.ds(nxt, A_BLOCK), :]], axis=0)
    vw = jnp.concatenate([v_ref[pl.ds(prev, A_BLOCK), :], v_ref[pl.ds(start, rows), :],
                          v_ref[pl.ds(nxt, A_BLOCK), :]], axis=0)
    lane = lax.broadcasted_iota(jnp.int32, (A_BLOCK, LANES), 1)
    low = lane < HEAD_DIM
    ones = jnp.ones((3 * A_BLOCK, LANES), bf16)

    def attend(use_row_max):
        over = jnp.zeros((group * A_BLOCK, LANES), f32)
        for b in range(A_STEP_BLOCKS):
            blk = n * A_STEP_BLOCKS + b
            if b == 0:
                variant = jnp.where(blk == 0, 0, 1)
            elif b == A_STEP_BLOCKS - 1:
                variant = jnp.where(blk == nblocks - 1, 2, 1)
            else:
                variant = 1
            q = q_ref[b * A_BLOCK:(b + 1) * A_BLOCK, :]
            for g in range(A_KV_HEADS):
                heads = range(g * group, (g + 1) * group)
                qs = []
                for h in heads:
                    q2 = q[:, (h // 2) * LANES:(h // 2 + 1) * LANES]
                    qs.append(jnp.where(low if h % 2 == 0 else jnp.logical_not(low), q2, jnp.zeros_like(q2)))
                q4 = jnp.concatenate(qs, axis=0)
                kg = kw[b * A_BLOCK:(b + 3) * A_BLOCK, g * LANES:(g + 1) * LANES]
                vg = jnp.concatenate([vw[b * A_BLOCK:(b + 3) * A_BLOCK, g * LANES:(g + 1) * LANES], ones],
                                     axis=1)
                s = lax.dot_general(q4, kg, _NT, preferred_element_type=f32)
                ps, sink_terms = [], []
                for t, h in enumerate(heads):
                    sh = s[t * A_BLOCK:(t + 1) * A_BLOCK]
                    if use_row_max:
                        sh = sh + bias_ref[variant, h]
                        snk = sink_ref[h] * LOG2E
                        m = jnp.maximum(jnp.max(sh, axis=-1, keepdims=True), snk)
                        ps.append(jnp.exp2(sh - m).astype(bf16))
                        sink_terms.append(jnp.exp2(snk - m))
                    else:
                        ps.append(jnp.exp2(sh + rel_bias_ref[variant, h]).astype(bf16))
                p = jnp.concatenate(ps, axis=0)
                res = jnp.dot(p, vg, preferred_element_type=f32)
                row_sum = res[:, LANES:2 * LANES]
                sink_term = jnp.concatenate(sink_terms, axis=0) if use_row_max else 1.0
                on = res[:, 0:LANES] / (row_sum + sink_term)
                over = jnp.maximum(over, jnp.where(row_sum <= ROW_SUM_LIMIT, 0.0, 1.0))
                for t in range(group // 2):
                    even = on[(2 * t) * A_BLOCK:(2 * t + 1) * A_BLOCK]
                    odd = on[(2 * t + 1) * A_BLOCK:(2 * t + 2) * A_BLOCK]
                    pair = (g * group) // 2 + t
                    o_ref[b * A_BLOCK:(b + 1) * A_BLOCK, pair * LANES:(pair + 1) * LANES] = (
                        jnp.where(low, even, odd).astype(o_ref.dtype))
        return over

    over = attend(use_row_max=False)

    @pl.when(jnp.max(over) > 0.0)
    def _():
        attend(use_row_max=True)


def _diff_attn_kernel(lamv_ref, g_ref, q_ref, k_ref, vt_ref, bias_ref, o_ref,
                      qz_ref, qp_ref, p_ref, psum_ref, r_ref, flag_ref, any_ref, acc_ref, den_ref,
                      *, nchunks, lambda_init):
    TQ = B_QTILE
    TK = B_TILE
    i = pl.program_id(1)
    f32 = jnp.float32
    bf16 = jnp.bfloat16

    qt = q_ref[...].astype(f32).T
    row = lax.broadcasted_iota(jnp.int32, (LANES, TQ), 0)
    is_q = (row < HEAD_DIM, row >= HEAD_DIM)
    ref_row = (HEAD_DIM, 0)

    ksub = TK // B_BIAS_BLOCK
    qsub = TQ // B_BIAS_BLOCK

    def bias_tile(j, rows=ksub):
        base = j * ksub - i * qsub
        by_dist = {d: bias_ref[0, jnp.clip(base + d, -2, 2) + 2] for d in range(1 - qsub, rows)}
        return jnp.concatenate(
            [jnp.concatenate([by_dist[a - b] for b in range(qsub)], axis=1) for a in range(rows)], axis=0)

    def k_chunk(j, c):
        return k_ref[pl.ds(pl.multiple_of(j * TK, TK), TK), c * LANES:(c + 1) * LANES]

    def set_reference(c, r_row):
        hi = r_row.astype(bf16).astype(f32)
        rem = r_row - hi
        mid = rem.astype(bf16).astype(f32)
        low = (rem - mid).astype(bf16).astype(f32)
        first = ref_row[c]
        ext = jnp.where(row == first, -hi, jnp.where(row == first + 1, -mid,
                                                     jnp.where(row == first + 2, -low, 0.0)))
        qp_ref[c] = jnp.where(is_q[c], qt, ext).astype(bf16)

    def exact_chunk(j, c, flagged):
        s = jnp.dot(k_chunk(j, c), qz_ref[c], preferred_element_type=f32) + bias_tile(j)
        r_old = r_ref[c]
        r_new = jnp.where(flagged, jnp.maximum(r_old, jnp.max(s, axis=0, keepdims=True)), r_old)
        p = jnp.where(flagged, jnp.exp2(s - r_new), 0.0)
        pv = jnp.dot(vt_ref[0, j], p.astype(bf16), preferred_element_type=f32)
        alpha = jnp.exp2(r_old - r_new)
        acc_ref[c] = acc_ref[c] * alpha + pv
        den_ref[c] = den_ref[c] * alpha + jnp.sum(p, axis=0, keepdims=True)
        r_ref[c] = r_new

    def stage_a(j, slot):
        bias = bias_tile(j)
        for c in range(2):
            s = jnp.dot(k_chunk(j, c), qp_ref[c], preferred_element_type=f32) + bias
            p = jnp.exp2(s)
            psum_ref[slot, c] = jnp.sum(p, axis=0, keepdims=True)
            p_ref[slot, c] = p.astype(bf16)

    def stage_c(j, slot):
        vt = vt_ref[0, j]
        for c in range(2):
            pv = jnp.dot(vt, p_ref[slot, c], preferred_element_type=f32)
            psum = psum_ref[slot, c]
            ok = psum <= ROW_SUM_LIMIT
            acc_ref[c] = acc_ref[c] + jnp.where(ok, pv, 0.0)
            den_ref[c] = den_ref[c] + jnp.where(ok, psum, 0.0)
            flag = jnp.where(ok, 0.0, 1.0)
            flag_ref[c, pl.ds(j, 1), :] = flag
            any_ref[c] = jnp.maximum(any_ref[c], flag)

    for c in range(2):
        qz_ref[c] = jnp.where(is_q[c], qt, 0.0).astype(bf16)
    acc_ref[...] = jnp.zeros(acc_ref.shape, f32)
    den_ref[...] = jnp.zeros(den_ref.shape, f32)
    any_ref[...] = jnp.zeros(any_ref.shape, f32)
    diag = i * (TQ // TK)
    probe_bias = bias_tile(diag, rows=B_PROBE // B_BIAS_BLOCK)
    for c in range(2):
        kp = k_ref[pl.ds(pl.multiple_of(diag * TK, TK), B_PROBE), c * LANES:(c + 1) * LANES]
        s = jnp.dot(kp, qz_ref[c], preferred_element_type=f32) + probe_bias
        r = jnp.max(s, axis=0, keepdims=True)
        r_ref[c] = r
        set_reference(c, r)

    def run(j0, has_next):
        for u in range(B_UNROLL):
            stage_c(j0 + u, u % B_SLOTS)
            if u + B_AHEAD < B_UNROLL or has_next:
                stage_a(j0 + u + B_AHEAD, (u + B_AHEAD) % B_SLOTS)

    nbody = nchunks // B_UNROLL
    for u in range(B_AHEAD):
        stage_a(u, u)

    def body(b, carry):
        run(b * B_UNROLL, True)
        return carry

    lax.fori_loop(0, nbody - 1, body, 0)
    run((nbody - 1) * B_UNROLL, False)

    @pl.when(jnp.max(jnp.maximum(any_ref[0], any_ref[1])) > 0.0)
    def _():
        def redo(j, carry):
            @pl.when(jnp.max(jnp.maximum(flag_ref[0, pl.ds(j, 1), :], flag_ref[1, pl.ds(j, 1), :])) > 0.0)
            def _():
                for c in range(2):
                    exact_chunk(j, c, flag_ref[c, pl.ds(j, 1), :] > 0.0)
            return carry

        lax.fori_loop(0, nchunks, redo, 0)

    lv = lamv_ref[...]
    lam = (jnp.exp(jnp.sum(lv[0:1] * lv[1:2], axis=-1, keepdims=True))
           - jnp.exp(jnp.sum(lv[2:3] * lv[3:4], axis=-1, keepdims=True)) + lambda_init)
    ot = acc_ref[0] / den_ref[0] - lam * (acc_ref[1] / den_ref[1])
    o = ot.T
    o_ref[...] = (_rms(o, g_ref[...]) * (1.0 - lambda_init)).astype(o_ref.dtype)


def _out_ffn_kernel(ya_ref, yb_ref, x_ref, woa_ref, wob_ref, gpost_ref, gpre_ref,
                    wg_ref, wu_ref, wd_ref, gfpost_ref, o_ref, f_ref):
    y = (jnp.dot(ya_ref[...], woa_ref[...], preferred_element_type=jnp.float32)
         + jnp.dot(yb_ref[...], wob_ref[...], preferred_element_type=jnp.float32))
    h1 = x_ref[...] + _rms(y, gpost_ref[...])
    u = _rms(h1, gpre_ref[...]).astype(jnp.bfloat16)
    d_ff = wg_ref.shape[1]
    for c in range(d_ff // FFN_CHUNK):
        sl = slice(c * FFN_CHUNK, (c + 1) * FFN_CHUNK)
        gate = jnp.dot(u, wg_ref[:, sl], preferred_element_type=jnp.float32)
        up = jnp.dot(u, wu_ref[:, sl], preferred_element_type=jnp.float32)
        act = (gate * jax.nn.sigmoid(gate) * up).astype(jnp.bfloat16)
        part = jnp.dot(act, wd_ref[sl, :], preferred_element_type=jnp.float32)
        if c == 0:
            f_ref[...] = part
        else:
            f_ref[...] += part
    o_ref[...] = h1 + _rms(f_ref[...], gfpost_ref[...])


def _resident(shape):
    zeros = (0,) * len(shape)
    return pl.BlockSpec(shape, lambda *_: zeros, pipeline_mode=pl.Buffered(1))


def _layer(h, l, p, bias_a, bias_b):
    S, D = h.shape
    bf16 = jnp.bfloat16
    lambda_init = 0.8 - 0.6 * math.exp(-0.3 * l)
    scale = HEAD_DIM ** -0.5

    w = p["w_in"]
    a_q, a_kv, b_qk = A_Q_HEADS * HEAD_DIM, A_KV_HEADS * HEAD_DIM, B_HEADS * 2 * HEAD_DIM
    c0 = a_q
    c1 = c0 + a_kv
    c2 = c1 + a_kv
    c3 = c2 + b_qk

    def dup(cols):
        parts = []
        for g in range(A_KV_HEADS):
            blk = cols[:, g * HEAD_DIM:(g + 1) * HEAD_DIM]
            parts += [blk, blk]
        return jnp.concatenate(parts, axis=1)

    kones = np.zeros((B_HEADS, 2, LANES), np.float32)
    kones[:, 0, HEAD_DIM:HEAD_DIM + B_REF_LANES] = 1.0
    kones[:, 1, 0:B_REF_LANES] = 1.0
    kones = jnp.asarray(kones.reshape(1, B_HEADS * 2 * LANES))

    w_cat = jnp.concatenate([w[:, :c0] * (scale * LOG2E), dup(w[:, c0:c1]), dup(w[:, c1:c2]),
                             w[:, c2:c3] * (scale * LOG2E), w[:, c3:]],
                            axis=1).astype(bf16)
    ncols = w_cat.shape[1]
    assert ncols == PROJ_COLS and w.shape[1] == c3 + 2 * B_WIDTH
    nrow = S // ROW_TILE
    nchunks = S // B_TILE
    assert ROW_TILE % B_TILE == 0 and S % ROW_TILE == 0 and nchunks % B_UNROLL == 0
    assert B_UNROLL % B_SLOTS == 0 and B_AHEAD < B_SLOTS and B_AHEAD <= B_UNROLL
    kb_cols = B_HEADS * 2 * LANES

    qa, ka, va, qb, kb, vt = pl.pallas_call(
        _in_proj_kernel,
        grid=(nrow,),
        in_specs=[pl.BlockSpec((ROW_TILE, D), lambda i: (i, 0)),
                  _resident((1, D)),
                  _resident((D, ncols)),
                  _resident((1, kb_cols))],
        out_specs=[pl.BlockSpec((ROW_TILE, A_WIDTH), lambda i: (i, 0)),
                   pl.BlockSpec((ROW_TILE, A_KV_WIDTH), lambda i: (i, 0)),
                   pl.BlockSpec((ROW_TILE, A_KV_WIDTH), lambda i: (i, 0)),
                   pl.BlockSpec((ROW_TILE, B_WIDTH), lambda i: (i, 0)),
                   pl.BlockSpec((ROW_TILE, kb_cols), lambda i: (i, 0)),
                   pl.BlockSpec((B_HEADS, ROW_TILE // B_TILE, B_V_DIM, B_TILE), lambda i: (0, i, 0, 0))],
        out_shape=[jax.ShapeDtypeStruct((S, A_WIDTH), bf16),
                   jax.ShapeDtypeStruct((S, A_KV_WIDTH), bf16),
                   jax.ShapeDtypeStruct((S, A_KV_WIDTH), bf16),
                   jax.ShapeDtypeStruct((S, B_WIDTH), bf16),
                   jax.ShapeDtypeStruct((S, kb_cols), bf16),
                   jax.ShapeDtypeStruct((B_HEADS, nchunks, B_V_DIM, B_TILE), bf16)],
        compiler_params=pltpu.CompilerParams(dimension_semantics=("arbitrary",),
                                             vmem_limit_bytes=VMEM_LIMIT),
        name="in_proj",
    )(h, p["attn_pre_g"].reshape(1, D), w_cat, kones)

    nblocks = S // A_BLOCK
    a_rows = A_STEP_BLOCKS * A_BLOCK
    assert nblocks % A_STEP_BLOCKS == 0 and A_STEP_BLOCKS >= 2
    rel_bias_a = bias_a - (p["a_sink"].astype(jnp.float32) * LOG2E)[None, :, None, None]
    ya = pl.pallas_call(
        functools.partial(_win_attn_kernel, nblocks=nblocks),
        grid=(nblocks // A_STEP_BLOCKS,),
        in_specs=[pl.BlockSpec(memory_space=pltpu.SMEM),
                  pl.BlockSpec((a_rows, A_WIDTH), lambda n: (n, 0)),
                  _resident(ka.shape),
                  _resident(va.shape),
                  _resident(bias_a.shape),
                  _resident(bias_a.shape)],
        out_specs=pl.BlockSpec((a_rows, A_WIDTH), lambda n: (n, 0)),
        out_shape=jax.ShapeDtypeStruct((S, A_WIDTH), bf16),
        compiler_params=pltpu.CompilerParams(dimension_semantics=("arbitrary",),
                                             vmem_limit_bytes=VMEM_LIMIT),
        name="win_attn",
    )(p["a_sink"], qa, ka, va, bias_a, rel_bias_a)

    lamv = jnp.stack([p["lambda_q1"], p["lambda_k1"], p["lambda_q2"], p["lambda_k2"]])
    TQ, TK = B_QTILE, B_TILE
    assert S % TQ == 0 and TQ % TK == 0
    yb = pl.pallas_call(
        functools.partial(_diff_attn_kernel, nchunks=nchunks, lambda_init=lambda_init),
        grid=(B_HEADS, S // TQ),
        in_specs=[pl.BlockSpec((4, HEAD_DIM), lambda hh, i: (0, 0)),
                  pl.BlockSpec((1, B_V_DIM), lambda hh, i: (0, 0)),
                  pl.BlockSpec((TQ, LANES), lambda hh, i: (i, hh)),
                  pl.BlockSpec((S, 2 * LANES), lambda hh, i: (0, hh), pipeline_mode=pl.Buffered(1)),
                  pl.BlockSpec((1, nchunks, B_V_DIM, TK), lambda hh, i: (hh, 0, 0, 0)),
                  pl.BlockSpec((1, 5, B_BIAS_BLOCK, B_BIAS_BLOCK), lambda hh, i: (hh, 0, 0, 0))],
        out_specs=pl.BlockSpec((TQ, LANES), lambda hh, i: (i, hh)),
        out_shape=jax.ShapeDtypeStruct((S, B_HEADS * B_V_DIM), bf16),
        scratch_shapes=[pltpu.VMEM((2, LANES, TQ), bf16),
                        pltpu.VMEM((2, LANES, TQ), bf16),
                        pltpu.VMEM((B_SLOTS, 2, TK, TQ), bf16),
                        pltpu.VMEM((B_SLOTS, 2, 1, TQ), jnp.float32),
                        pltpu.VMEM((2, 1, TQ), jnp.float32),
                        pltpu.VMEM((2, nchunks, TQ), jnp.float32),
                        pltpu.VMEM((2, 1, TQ), jnp.float32),
                        pltpu.VMEM((2, B_V_DIM, TQ), jnp.float32),
                        pltpu.VMEM((2, 1, TQ), jnp.float32)],
        compiler_params=pltpu.CompilerParams(dimension_semantics=("arbitrary", "arbitrary"),
                                             vmem_limit_bytes=VMEM_LIMIT),
        name="diff_attn",
    )(lamv, p["diff_subln_g"].reshape(1, B_V_DIM), qb, kb, vt, bias_b)

    w_out = p["w_out"].astype(bf16)
    a_width = A_WIDTH
    assert w_out.shape[0] == A_WIDTH + B_WIDTH
    d_ff = p["w_gate"].shape[1]
    assert d_ff % FFN_CHUNK == 0 and S % FFN_ROW_TILE == 0
    R = FFN_ROW_TILE
    out = pl.pallas_call(
        _out_ffn_kernel,
        grid=(S // R,),
        in_specs=[pl.BlockSpec((R, A_WIDTH), lambda i: (i, 0)),
                  pl.BlockSpec((R, B_WIDTH), lambda i: (i, 0)),
                  pl.BlockSpec((R, D), lambda i: (i, 0)),
                  _resident((a_width, D)),
                  _resident((w_out.shape[0] - a_width, D)),
                  _resident((1, D)),
                  _resident((1, D)),
                  _resident((D, d_ff)),
                  _resident((D, d_ff)),
                  _resident((d_ff, D)),
                  _resident((1, D))],
        out_specs=pl.BlockSpec((R, D), lambda i: (i, 0)),
        out_shape=jax.ShapeDtypeStruct((S, D), jnp.float32),
        scratch_shapes=[pltpu.VMEM((R, D), jnp.float32)],
        compiler_params=pltpu.CompilerParams(dimension_semantics=("arbitrary",),
                                             vmem_limit_bytes=VMEM_LIMIT),
        name="out_ffn",
    )(ya, yb, h, w_out[:a_width], w_out[a_width:], p["attn_post_g"].reshape(1, D),
      p["ffn_pre_g"].reshape(1, D), p["w_gate"].astype(bf16), p["w_up"].astype(bf16),
      p["w_down"].astype(bf16), p["ffn_post_g"].reshape(1, D))
    return out


def kernel(x, attn_pre_g, w_in, a_sink, lambda_q1, lambda_k1, lambda_q2, lambda_k2, diff_subln_g,
           rel_bias, w_out, attn_post_g, ffn_pre_g, w_gate, w_up, w_down, ffn_post_g):
    batch, S, D = x.shape
    depth = w_in.shape[0]
    assert S % B_TILE == 0 and S % A_BLOCK == 0 and S // A_BLOCK >= 2

    tab_a = _toeplitz_bias(rel_bias[:, :A_Q_HEADS], A_BLOCK, 3 * A_BLOCK, lambda t: -t - A_BLOCK)
    qi = np.arange(A_BLOCK)[:, None]
    kj = np.arange(3 * A_BLOCK)[None, :]
    in_window = np.abs(kj - A_BLOCK - qi) <= A_BLOCK
    valid = np.stack([in_window & (kj >= A_BLOCK), in_window, in_window & (kj < 2 * A_BLOCK)])
    bias_a = jnp.where(valid[:, None], tab_a[None] * LOG2E, MASK_VALUE)

    nb = B_BIAS_BLOCK
    assert len(set(_t5_bucket_np(np.arange(nb + 1, 4 * nb)))) == 1
    tab_b = rel_bias[:, A_Q_HEADS:]
    near = [_toeplitz_bias(tab_b, nb, nb, lambda t, d=d: d * nb + t) for d in (-1, 0, 1)]
    far = tab_b[_t5_bucket_np(np.array([-2 * nb, 2 * nb]))].astype(jnp.float32)
    const = [jnp.broadcast_to(far[side][:, None, None], (B_HEADS, nb, nb)) for side in (0, 1)]
    bias_b = jnp.stack([const[0]] + near + [const[1]], axis=1) * LOG2E

    outs = []
    for b in range(batch):
        h = x[b]
        for l in range(depth):
            p = dict(attn_pre_g=attn_pre_g[l], w_in=w_in[l], a_sink=a_sink[l], lambda_q1=lambda_q1[l],
                     lambda_k1=lambda_k1[l], lambda_q2=lambda_q2[l], lambda_k2=lambda_k2[l],
                     diff_subln_g=diff_subln_g[l], w_out=w_out[l], attn_post_g=attn_post_g[l],
                     ffn_pre_g=ffn_pre_g[l], w_gate=w_gate[l], w_up=w_up[l], w_down=w_down[l],
                     ffn_post_g=ffn_post_g[l])
            h = _layer(h, l, p, bias_a, bias_b)
        outs.append(h)
    return jnp.stack(outs)
```

```python
import functools
import math

import numpy as np
import jax
import jax.numpy as jnp
from jax import lax
from jax.experimental import pallas as pl
from jax.experimental.pallas import tpu as pltpu

HEAD_DIM = 64
A_Q_HEADS = 8
A_KV_HEADS = 2
A_BLOCK = 128
A_STEP_BLOCKS = 8
B_HEADS = 4
B_V_DIM = 2 * HEAD_DIM
NUM_BUCKETS = 32
MAX_DISTANCE = 128
EPS = 1e-6
MASK_VALUE = -1e30
LOG2E = math.log2(math.e)
ROW_SUM_LIMIT = 2.0 ** 40

A_WIDTH = A_Q_HEADS * HEAD_DIM
A_KV_WIDTH = 2 * A_KV_HEADS * HEAD_DIM
B_WIDTH = B_HEADS * B_V_DIM
COL_KA = A_WIDTH
COL_VA = COL_KA + A_KV_WIDTH
COL_QB = COL_VA + A_KV_WIDTH
COL_KB = COL_QB + B_WIDTH
COL_VB = COL_KB + B_WIDTH
PROJ_COLS = COL_VB + B_WIDTH

LANES = 128
ROW_TILE = 1024
FFN_ROW_TILE = 1024
FFN_CHUNK = 256
B_TILE = 1024
B_QTILE = 2048
B_QGROUP = 512
B_REF_LANES = 3
B_UNROLL = 2
B_AHEAD = 1
B_SLOTS = 2
B_PROBE = 128
B_BIAS_BLOCK = 128
VMEM_LIMIT = 56 * 1024 * 1024

_NT = (((1,), (1,)), ((), ()))


def _t5_bucket_np(rel):
    nb = NUM_BUCKETS // 2
    max_exact = nb // 2
    ret = np.where(rel > 0, nb, 0)
    n = np.abs(rel)
    nf = np.maximum(n, 1).astype(np.float32)
    large = max_exact + (np.log(nf / np.float32(max_exact)) / np.float32(math.log(MAX_DISTANCE / max_exact))
                         * np.float32(nb - max_exact)).astype(np.int32)
    large = np.minimum(large, nb - 1)
    return (ret + np.where(n < max_exact, n, large)).astype(np.int32)


def _toeplitz_bias(table, rows, cols, rel_of):
    length = rows + cols
    u = np.arange(length)
    t = np.where(u < cols, -u, length - u)
    w = table[_t5_bucket_np(rel_of(t))].astype(jnp.float32).T
    x = jnp.tile(w, (1, rows))[:, :rows * (length - 1)].reshape(w.shape[0], rows, length - 1)
    return x[:, :, :cols]


def _rms(xf, g):
    return xf * lax.rsqrt(jnp.mean(xf * xf, axis=-1, keepdims=True) + EPS) * g


def _in_proj_kernel(x_ref, g_ref, w_ref, kones_ref, qa_ref, ka_ref, va_ref, qb_ref, kb_ref, vt_ref):
    u = _rms(x_ref[...], g_ref[...]).astype(jnp.bfloat16)
    proj = jnp.dot(u, w_ref[...], preferred_element_type=jnp.float32)
    qa_ref[...] = proj[:, 0:COL_KA].astype(jnp.bfloat16)
    ka_ref[...] = proj[:, COL_KA:COL_VA].astype(jnp.bfloat16)
    va_ref[...] = proj[:, COL_VA:COL_QB].astype(jnp.bfloat16)
    qb_ref[...] = proj[:, COL_QB:COL_KB].astype(jnp.bfloat16)
    lane = lax.broadcasted_iota(jnp.int32, (proj.shape[0], LANES), 1)
    keep = (lane < HEAD_DIM, lane >= HEAD_DIM)
    for h in range(B_HEADS):
        kpair = proj[:, COL_KB + h * LANES:COL_KB + (h + 1) * LANES]
        for c in range(2):
            col = (2 * h + c) * LANES
            kb_ref[:, col:col + LANES] = jnp.where(keep[c], kpair, kones_ref[:, col:col + LANES]).astype(jnp.bfloat16)
        vt = proj[:, COL_VB + h * B_V_DIM:COL_VB + (h + 1) * B_V_DIM].T.astype(jnp.bfloat16)
        for t in range(vt_ref.shape[1]):
            vt_ref[h, t] = vt[:, t * B_TILE:(t + 1) * B_TILE]


def _win_attn_kernel(sink_ref, q_ref, k_ref, v_ref, bias_ref, rel_bias_ref, o_ref, *, nblocks):
    f32 = jnp.float32
    bf16 = jnp.bfloat16
    n = pl.program_id(0)
    rows = A_STEP_BLOCKS * A_BLOCK
    seq = k_ref.shape[0]
    group = A_Q_HEADS // A_KV_HEADS
    start = pl.multiple_of(n * rows, rows)
    prev = pl.multiple_of(jnp.maximum(start - A_BLOCK, 0), A_BLOCK)
    nxt = pl.multiple_of(jnp.minimum(start + rows, seq - A_BLOCK), A_BLOCK)
    kw = jnp.concatenate([k_ref[pl.ds(prev, A_BLOCK), :], k_ref[pl.ds(start, rows), :],
                          k_ref[pl.ds(nxt, A_BLOCK), :]], axis=0)
    vw = jnp.concatenate([v_ref[pl.ds(prev, A_BLOCK), :], v_ref[pl.ds(start, rows), :],
                          v_ref[pl.ds(nxt, A_BLOCK), :]], axis=0)
    lane = lax.broadcasted_iota(jnp.int32, (A_BLOCK, LANES), 1)
    low = lane < HEAD_DIM
    ones = jnp.ones((3 * A_BLOCK, LANES), bf16)

    def attend(use_row_max):
        over = jnp.zeros((group * A_BLOCK, LANES), f32)
        for b in range(A_STEP_BLOCKS):
            blk = n * A_STEP_BLOCKS + b
            if b == 0:
                variant = jnp.where(blk == 0, 0, 1)
            elif b == A_STEP_BLOCKS - 1:
                variant = jnp.where(blk == nblocks - 1, 2, 1)
            else:
                variant = 1
            q = q_ref[b * A_BLOCK:(b + 1) * A_BLOCK, :]
            for g in range(A_KV_HEADS):
                heads = range(g * group, (g + 1) * group)
                qs = []
                for h in heads:
                    q2 = q[:, (h // 2) * LANES:(h // 2 + 1) * LANES]
                    qs.append(jnp.where(low if h % 2 == 0 else jnp.logical_not(low), q2, jnp.zeros_like(q2)))
                q4 = jnp.concatenate(qs, axis=0)
                kg = kw[b * A_BLOCK:(b + 3) * A_BLOCK, g * LANES:(g + 1) * LANES]
                vg = jnp.concatenate([vw[b * A_BLOCK:(b + 3) * A_BLOCK, g * LANES:(g + 1) * LANES], ones],
                                     axis=1)
                s = lax.dot_general(q4, kg, _NT, preferred_element_type=f32)
                ps, sink_terms = [], []
                for t, h in enumerate(heads):
                    sh = s[t * A_BLOCK:(t + 1) * A_BLOCK]
                    if use_row_max:
                        sh = sh + bias_ref[variant, h]
                        snk = sink_ref[h] * LOG2E
                        m = jnp.maximum(jnp.max(sh, axis=-1, keepdims=True), snk)
                        ps.append(jnp.exp2(sh - m).astype(bf16))
                        sink_terms.append(jnp.exp2(snk - m))
                    else:
                        ps.append(jnp.exp2(sh + rel_bias_ref[variant, h]).astype(bf16))
                p = jnp.concatenate(ps, axis=0)
                res = jnp.dot(p, vg, preferred_element_type=f32)
                row_sum = res[:, LANES:2 * LANES]
                sink_term = jnp.concatenate(sink_terms, axis=0) if use_row_max else 1.0
                on = res[:, 0:LANES] / (row_sum + sink_term)
                over = jnp.maximum(over, jnp.where(row_sum <= ROW_SUM_LIMIT, 0.0, 1.0))
                for t in range(group // 2):
                    even = on[(2 * t) * A_BLOCK:(2 * t + 1) * A_BLOCK]
                    odd = on[(2 * t + 1) * A_BLOCK:(2 * t + 2) * A_BLOCK]
                    pair = (g * group) // 2 + t
                    o_ref[b * A_BLOCK:(b + 1) * A_BLOCK, pair * LANES:(pair + 1) * LANES] = (
                        jnp.where(low, even, odd).astype(o_ref.dtype))
        return over

    over = attend(use_row_max=False)

    @pl.when(jnp.max(over) > 0.0)
    def _():
        attend(use_row_max=True)


def _diff_attn_kernel(lamv_ref, g_ref, q_ref, k_ref, vt_ref, bias_ref, o_ref,
                      qz_ref, qp_ref, p_ref, psum_ref, r_ref, flag_ref, any_ref, acc_ref, den_ref,
                      *, nchunks, lambda_init):
    TQ = B_QTILE
    TK = B_TILE
    i = pl.program_id(1)
    f32 = jnp.float32
    bf16 = jnp.bfloat16

    qt = q_ref[...].astype(f32).T
    row = lax.broadcasted_iota(jnp.int32, (LANES, TQ), 0)
    is_q = (row < HEAD_DIM, row >= HEAD_DIM)
    ref_row = (HEAD_DIM, 0)

    ksub = TK // B_BIAS_BLOCK
    qsub = TQ // B_BIAS_BLOCK

    def bias_tile(j, rows=ksub):
        base = j * ksub - i * qsub
        by_dist = {d: bias_ref[0, jnp.clip(base + d, -2, 2) + 2] for d in range(1 - qsub, rows)}
        return jnp.concatenate(
            [jnp.concatenate([by_dist[a - b] for b in range(qsub)], axis=1) for a in range(rows)], axis=0)

    def k_chunk(j, c):
        return k_ref[pl.ds(pl.multiple_of(j * TK, TK), TK), c * LANES:(c + 1) * LANES]

    def set_reference(c, r_row):
        hi = r_row.astype(bf16).astype(f32)
        rem = r_row - hi
        mid = rem.astype(bf16).astype(f32)
        low = (rem - mid).astype(bf16).astype(f32)
        first = ref_row[c]
        ext = jnp.where(row == first, -hi, jnp.where(row == first + 1, -mid,
                                                     jnp.where(row == first + 2, -low, 0.0)))
        qp_ref[c] = jnp.where(is_q[c], qt, ext).astype(bf16)

    def exact_chunk(j, c, flagged):
        s = jnp.dot(k_chunk(j, c), qz_ref[c], preferred_element_type=f32) + bias_tile(j)
        r_old = r_ref[c]
        r_new = jnp.where(flagged, jnp.maximum(r_old, jnp.max(s, axis=0, keepdims=True)), r_old)
        p = jnp.where(flagged, jnp.exp2(s - r_new), 0.0)
        pv = jnp.dot(vt_ref[0, j], p.astype(bf16), preferred_element_type=f32)
        alpha = jnp.exp2(r_old - r_new)
        acc_ref[c] = acc_ref[c] * alpha + pv
        den_ref[c] = den_ref[c] * alpha + jnp.sum(p, axis=0, keepdims=True)
        r_ref[c] = r_new

    def stage_a(j, slot):
        bias = bias_tile(j)
        for c in range(2):
            kc = k_chunk(j, c)
            for g in range(TQ // B_QGROUP):
                cols = slice(g * B_QGROUP, (g + 1) * B_QGROUP)
                s = jnp.dot(kc, qp_ref[c, :, cols], preferred_element_type=f32) + bias[:, cols]
                p = jnp.exp2(s)
                psum_ref[slot, c, :, cols] = jnp.sum(p, axis=0, keepdims=True)
                p_ref[slot, c, :, cols] = p.astype(bf16)

    def stage_c(j, slot):
        vt = vt_ref[0, j]
        for c in range(2):
            pv = jnp.dot(vt, p_ref[slot, c], preferred_element_type=f32)
            psum = psum_ref[slot, c]
            ok = psum <= ROW_SUM_LIMIT
            acc_ref[c] = acc_ref[c] + jnp.where(ok, pv, 0.0)
            den_ref[c] = den_ref[c] + jnp.where(ok, psum, 0.0)
            flag = jnp.where(ok, 0.0, 1.0)
            flag_ref[c, pl.ds(j, 1), :] = flag
            any_ref[c] = jnp.maximum(any_ref[c], flag)

    for c in range(2):
        qz_ref[c] = jnp.where(is_q[c], qt, 0.0).astype(bf16)
    acc_ref[...] = jnp.zeros(acc_ref.shape, f32)
    den_ref[...] = jnp.zeros(den_ref.shape, f32)
    any_ref[...] = jnp.zeros(any_ref.shape, f32)
    diag = i * (TQ // TK)
    probe_bias = bias_tile(diag, rows=B_PROBE // B_BIAS_BLOCK)
    for c in range(2):
        kp = k_ref[pl.ds(pl.multiple_of(diag * TK, TK), B_PROBE), c * LANES:(c + 1) * LANES]
        s = jnp.dot(kp, qz_ref[c], preferred_element_type=f32) + probe_bias
        r = jnp.max(s, axis=0, keepdims=True)
        r_ref[c] = r
        set_reference(c, r)

    def run(j0, has_next):
        for u in range(B_UNROLL):
            stage_c(j0 + u, u % B_SLOTS)
            if u + B_AHEAD < B_UNROLL or has_next:
                stage_a(j0 + u + B_AHEAD, (u + B_AHEAD) % B_SLOTS)

    nbody = nchunks // B_UNROLL
    for u in range(B_AHEAD):
        stage_a(u, u)

    def body(b, carry):
        run(b * B_UNROLL, True)
        return carry

    lax.fori_loop(0, nbody - 1, body, 0)
    run((nbody - 1) * B_UNROLL, False)

    @pl.when(jnp.max(jnp.maximum(any_ref[0], any_ref[1])) > 0.0)
    def _():
        def redo(j, carry):
            @pl.when(jnp.max(jnp.maximum(flag_ref[0, pl.ds(j, 1), :], flag_ref[1, pl.ds(j, 1), :])) > 0.0)
            def _():
                for c in range(2):
                    exact_chunk(j, c, flag_ref[c, pl.ds(j, 1), :] > 0.0)
            return carry

        lax.fori_loop(0, nchunks, redo, 0)

    lv = lamv_ref[...]
    lam = (jnp.exp(jnp.sum(lv[0:1] * lv[1:2], axis=-1, keepdims=True))
           - jnp.exp(jnp.sum(lv[2:3] * lv[3:4], axis=-1, keepdims=True)) + lambda_init)
    ot = acc_ref[0] / den_ref[0] - lam * (acc_ref[1] / den_ref[1])
    o = ot.T
    o_ref[...] = (_rms(o, g_ref[...]) * (1.0 - lambda_init)).astype(o_ref.dtype)


def _out_ffn_kernel(ya_ref, yb_ref, x_ref, woa_ref, wob_ref, gpost_ref, gpre_ref,
                    wg_ref, wu_ref, wd_ref, gfpost_ref, o_ref, f_ref):
    y = (jnp.dot(ya_ref[...], woa_ref[...], preferred_element_type=jnp.float32)
         + jnp.dot(yb_ref[...], wob_ref[...], preferred_element_type=jnp.float32))
    h1 = x_ref[...] + _rms(y, gpost_ref[...])
    u = _rms(h1, gpre_ref[...]).astype(jnp.bfloat16)
    d_ff = wg_ref.shape[1]
    for c in range(d_ff // FFN_CHUNK):
        sl = slice(c * FFN_CHUNK, (c + 1) * FFN_CHUNK)
        gate = jnp.dot(u, wg_ref[:, sl], preferred_element_type=jnp.float32)
        up = jnp.dot(u, wu_ref[:, sl], preferred_element_type=jnp.float32)
        act = (gate * jax.nn.sigmoid(gate) * up).astype(jnp.bfloat16)
        part = jnp.dot(act, wd_ref[sl, :], preferred_element_type=jnp.float32)
        if c == 0:
            f_ref[...] = part
        else:
            f_ref[...] += part
    o_ref[...] = h1 + _rms(f_ref[...], gfpost_ref[...])


def _resident(shape):
    zeros = (0,) * len(shape)
    return pl.BlockSpec(shape, lambda *_: zeros, pipeline_mode=pl.Buffered(1))


def _layer(h, l, p, bias_a, bias_b):
    S, D = h.shape
    bf16 = jnp.bfloat16
    lambda_init = 0.8 - 0.6 * math.exp(-0.3 * l)
    scale = HEAD_DIM ** -0.5

    w = p["w_in"]
    a_q, a_kv, b_qk = A_Q_HEADS * HEAD_DIM, A_KV_HEADS * HEAD_DIM, B_HEADS * 2 * HEAD_DIM
    c0 = a_q
    c1 = c0 + a_kv
    c2 = c1 + a_kv
    c3 = c2 + b_qk

    def dup(cols):
        parts = []
        for g in range(A_KV_HEADS):
            blk = cols[:, g * HEAD_DIM:(g + 1) * HEAD_DIM]
            parts += [blk, blk]
        return jnp.concatenate(parts, axis=1)

    kones = np.zeros((B_HEADS, 2, LANES), np.float32)
    kones[:, 0, HEAD_DIM:HEAD_DIM + B_REF_LANES] = 1.0
    kones[:, 1, 0:B_REF_LANES] = 1.0
    kones = jnp.asarray(kones.reshape(1, B_HEADS * 2 * LANES))

    w_cat = jnp.concatenate([w[:, :c0] * (scale * LOG2E), dup(w[:, c0:c1]), dup(w[:, c1:c2]),
                             w[:, c2:c3] * (scale * LOG2E), w[:, c3:]],
                            axis=1).astype(bf16)
    ncols = w_cat.shape[1]
    assert ncols == PROJ_COLS and w.shape[1] == c3 + 2 * B_WIDTH
    nrow = S // ROW_TILE
    nchunks = S // B_TILE
    assert ROW_TILE % B_TILE == 0 and S % ROW_TILE == 0 and nchunks % B_UNROLL == 0
    assert B_UNROLL % B_SLOTS == 0 and B_AHEAD < B_SLOTS and B_AHEAD <= B_UNROLL
    kb_cols = B_HEADS * 2 * LANES

    qa, ka, va, qb, kb, vt = pl.pallas_call(
        _in_proj_kernel,
        grid=(nrow,),
        in_specs=[pl.BlockSpec((ROW_TILE, D), lambda i: (i, 0)),
                  _resident((1, D)),
                  _resident((D, ncols)),
                  _resident((1, kb_cols))],
        out_specs=[pl.BlockSpec((ROW_TILE, A_WIDTH), lambda i: (i, 0)),
                   pl.BlockSpec((ROW_TILE, A_KV_WIDTH), lambda i: (i, 0)),
                   pl.BlockSpec((ROW_TILE, A_KV_WIDTH), lambda i: (i, 0)),
                   pl.BlockSpec((ROW_TILE, B_WIDTH), lambda i: (i, 0)),
                   pl.BlockSpec((ROW_TILE, kb_cols), lambda i: (i, 0)),
                   pl.BlockSpec((B_HEADS, ROW_TILE // B_TILE, B_V_DIM, B_TILE), lambda i: (0, i, 0, 0))],
        out_shape=[jax.ShapeDtypeStruct((S, A_WIDTH), bf16),
                   jax.ShapeDtypeStruct((S, A_KV_WIDTH), bf16),
                   jax.ShapeDtypeStruct((S, A_KV_WIDTH), bf16),
                   jax.ShapeDtypeStruct((S, B_WIDTH), bf16),
                   jax.ShapeDtypeStruct((S, kb_cols), bf16),
                   jax.ShapeDtypeStruct((B_HEADS, nchunks, B_V_DIM, B_TILE), bf16)],
        compiler_params=pltpu.CompilerParams(dimension_semantics=("arbitrary",),
                                             vmem_limit_bytes=VMEM_LIMIT),
        name="in_proj",
    )(h, p["attn_pre_g"].reshape(1, D), w_cat, kones)

    nblocks = S // A_BLOCK
    a_rows = A_STEP_BLOCKS * A_BLOCK
    assert nblocks % A_STEP_BLOCKS == 0 and A_STEP_BLOCKS >= 2
    rel_bias_a = bias_a - (p["a_sink"].astype(jnp.float32) * LOG2E)[None, :, None, None]
    ya = pl.pallas_call(
        functools.partial(_win_attn_kernel, nblocks=nblocks),
        grid=(nblocks // A_STEP_BLOCKS,),
        in_specs=[pl.BlockSpec(memory_space=pltpu.SMEM),
                  pl.BlockSpec((a_rows, A_WIDTH), lambda n: (n, 0)),
                  _resident(ka.shape),
                  _resident(va.shape),
                  _resident(bias_a.shape),
                  _resident(bias_a.shape)],
        out_specs=pl.BlockSpec((a_rows, A_WIDTH), lambda n: (n, 0)),
        out_shape=jax.ShapeDtypeStruct((S, A_WIDTH), bf16),
        compiler_params=pltpu.CompilerParams(dimension_semantics=("arbitrary",),
                                             vmem_limit_bytes=VMEM_LIMIT),
        name="win_attn",
    )(p["a_sink"], qa, ka, va, bias_a, rel_bias_a)

    lamv = jnp.stack([p["lambda_q1"], p["lambda_k1"], p["lambda_q2"], p["lambda_k2"]])
    TQ, TK = B_QTILE, B_TILE
    assert S % TQ == 0 and TQ % TK == 0
    yb = pl.pallas_call(
        functools.partial(_diff_attn_kernel, nchunks=nchunks, lambda_init=lambda_init),
        grid=(B_HEADS, S // TQ),
        in_specs=[pl.BlockSpec((4, HEAD_DIM), lambda hh, i: (0, 0)),
                  pl.BlockSpec((1, B_V_DIM), lambda hh, i: (0, 0)),
                  pl.BlockSpec((TQ, LANES), lambda hh, i: (i, hh)),
                  pl.BlockSpec((S, 2 * LANES), lambda hh, i: (0, hh), pipeline_mode=pl.Buffered(1)),
                  pl.BlockSpec((1, nchunks, B_V_DIM, TK), lambda hh, i: (hh, 0, 0, 0),
                               pipeline_mode=pl.Buffered(1)),
                  pl.BlockSpec((1, 5, B_BIAS_BLOCK, B_BIAS_BLOCK), lambda hh, i: (hh, 0, 0, 0))],
        out_specs=pl.BlockSpec((TQ, LANES), lambda hh, i: (i, hh)),
        out_shape=jax.ShapeDtypeStruct((S, B_HEADS * B_V_DIM), bf16),
        scratch_shapes=[pltpu.VMEM((2, LANES, TQ), bf16),
                        pltpu.VMEM((2, LANES, TQ), bf16),
                        pltpu.VMEM((B_SLOTS, 2, TK, TQ), bf16),
                        pltpu.VMEM((B_SLOTS, 2, 1, TQ), jnp.float32),
                        pltpu.VMEM((2, 1, TQ), jnp.float32),
                        pltpu.VMEM((2, nchunks, TQ), jnp.float32),
                        pltpu.VMEM((2, 1, TQ), jnp.float32),
                        pltpu.VMEM((2, B_V_DIM, TQ), jnp.float32),
                        pltpu.VMEM((2, 1, TQ), jnp.float32)],
        compiler_params=pltpu.CompilerParams(dimension_semantics=("arbitrary", "arbitrary"),
                                             vmem_limit_bytes=VMEM_LIMIT),
        name="diff_attn",
    )(lamv, p["diff_subln_g"].reshape(1, B_V_DIM), qb, kb, vt, bias_b)

    w_out = p["w_out"].astype(bf16)
    a_width = A_WIDTH
    assert w_out.shape[0] == A_WIDTH + B_WIDTH
    d_ff = p["w_gate"].shape[1]
    assert d_ff % FFN_CHUNK == 0 and S % FFN_ROW_TILE == 0
    R = FFN_ROW_TILE
    out = pl.pallas_call(
        _out_ffn_kernel,
        grid=(S // R,),
        in_specs=[pl.BlockSpec((R, A_WIDTH), lambda i: (i, 0)),
                  pl.BlockSpec((R, B_WIDTH), lambda i: (i, 0)),
                  pl.BlockSpec((R, D), lambda i: (i, 0)),
                  _resident((a_width, D)),
                  _resident((w_out.shape[0] - a_width, D)),
                  _resident((1, D)),
                  _resident((1, D)),
                  _resident((D, d_ff)),
                  _resident((D, d_ff)),
                  _resident((d_ff, D)),
                  _resident((1, D))],
        out_specs=pl.BlockSpec((R, D), lambda i: (i, 0)),
        out_shape=jax.ShapeDtypeStruct((S, D), jnp.float32),
        scratch_shapes=[pltpu.VMEM((R, D), jnp.float32)],
        compiler_params=pltpu.CompilerParams(dimension_semantics=("arbitrary",),
                                             vmem_limit_bytes=VMEM_LIMIT),
        name="out_ffn",
    )(ya, yb, h, w_out[:a_width], w_out[a_width:], p["attn_post_g"].reshape(1, D),
      p["ffn_pre_g"].reshape(1, D), p["w_gate"].astype(bf16), p["w_up"].astype(bf16),
      p["w_down"].astype(bf16), p["ffn_post_g"].reshape(1, D))
    return out


def kernel(x, attn_pre_g, w_in, a_sink, lambda_q1, lambda_k1, lambda_q2, lambda_k2, diff_subln_g,
           rel_bias, w_out, attn_post_g, ffn_pre_g, w_gate, w_up, w_down, ffn_post_g):
    batch, S, D = x.shape
    depth = w_in.shape[0]
    assert S % B_TILE == 0 and S % A_BLOCK == 0 and S // A_BLOCK >= 2

    tab_a = _toeplitz_bias(rel_bias[:, :A_Q_HEADS], A_BLOCK, 3 * A_BLOCK, lambda t: -t - A_BLOCK)
    qi = np.arange(A_BLOCK)[:, None]
    kj = np.arange(3 * A_BLOCK)[None, :]
    in_window = np.abs(kj - A_BLOCK - qi) <= A_BLOCK
    valid = np.stack([in_window & (kj >= A_BLOCK), in_window, in_window & (kj < 2 * A_BLOCK)])
    bias_a = jnp.where(valid[:, None], tab_a[None] * LOG2E, MASK_VALUE)

    nb = B_BIAS_BLOCK
    assert len(set(_t5_bucket_np(np.arange(nb + 1, 4 * nb)))) == 1
    tab_b = rel_bias[:, A_Q_HEADS:]
    near = [_toeplitz_bias(tab_b, nb, nb, lambda t, d=d: d * nb + t) for d in (-1, 0, 1)]
    far = tab_b[_t5_bucket_np(np.array([-2 * nb, 2 * nb]))].astype(jnp.float32)
    const = [jnp.broadcast_to(far[side][:, None, None], (B_HEADS, nb, nb)) for side in (0, 1)]
    bias_b = jnp.stack([const[0]] + near + [const[1]], axis=1) * LOG2E

    outs = []
    for b in range(batch):
        h = x[b]
        for l in range(depth):
            p = dict(attn_pre_g=attn_pre_g[l], w_in=w_in[l], a_sink=a_sink[l], lambda_q1=lambda_q1[l],
                     lambda_k1=lambda_k1[l], lambda_q2=lambda_q2[l], lambda_k2=lambda_k2[l],
                     diff_subln_g=diff_subln_g[l], w_out=w_out[l], attn_post_g=attn_post_g[l],
                     ffn_pre_g=ffn_pre_g[l], w_gate=w_gate[l], w_up=w_up[l], w_down=w_down[l],
                     ffn_post_g=ffn_post_g[l])
            h = _layer(h, l, p, bias_a, bias_b)
        outs.append(h)
    return jnp.stack(outs)
```
